```python
import math
import jax
import jax.numpy as jnp
from jax import lax
import numpy as np

D_MODEL = 1024
BATCH = 2
SEQ = 16384
DEPTH = 4
DEC_BATCH = 8
DEC_SEQ = 8192
PAST_LEN = 128

N_META = 16
GRID_W = 64
NA_ROWS = 8
NA_COLS = 16
N_ATT_HEADS = 16
ATT_HEAD_DIM = 64
ATT_W = N_ATT_HEADS * ATT_HEAD_DIM
N_SSM_HEADS = 16
SSM_HEAD_DIM = 64
D_SSM = N_SSM_HEADS * SSM_HEAD_DIM
SSM_GROUPS = 2
SSM_STATE = 128
CONV_K = 5
CONV_CH = D_SSM + 2 * SSM_GROUPS * SSM_STATE
CHUNK = 128
IN_COLS = 3 * ATT_W + D_SSM + CONV_CH + 2 * N_SSM_HEADS
IN_SPLITS = (ATT_W, 2 * ATT_W, 3 * ATT_W, 3 * ATT_W + D_SSM, 3 * ATT_W + D_SSM + CONV_CH)
MIX_W = ATT_W + D_SSM
D_FF = 2816
N_EXPERTS = 8
TOP_K = 2
D_FF_EXPERT = 3584
MOE_BLOCK = 256
RMS_EPS = 1e-6
NEG_INF = -1e30

kernel_name = "hymba_natten_ssd_encoder"


def _rms(x):
    xf = x.astype(jnp.float32)
    return xf * lax.rsqrt(jnp.mean(xf * xf, axis=-1, keepdims=True) + RMS_EPS)


def _rms_norm(x, w):
    return (_rms(x) * w.astype(jnp.float32)).astype(x.dtype)


def _swiglu(h, w_gate, w_up, w_down):
    return (jax.nn.silu(h @ w_gate) * (h @ w_up)) @ w_down


def _window_tables():
    ncb = GRID_W // NA_COLS
    qc = np.arange(ncb)[:, None] * NA_COLS + np.arange(NA_COLS)[None, :]
    cs = np.clip(np.arange(ncb) * NA_COLS - NA_COLS // 2, 0, GRID_W - 2 * NA_COLS)
    kc = cs[:, None] + np.arange(2 * NA_COLS)[None, :]
    st = np.clip(qc - NA_COLS // 2, 0, GRID_W - NA_COLS)
    valid = (kc[:, None, :] >= st[:, :, None]) & (kc[:, None, :] < st[:, :, None] + NA_COLS)
    dc = np.clip(kc[:, None, :] - qc[:, :, None] + NA_COLS - 1, 0, 2 * NA_COLS - 2)
    return kc.astype(np.int32), valid, dc.astype(np.int32)


def _neighbourhood_attention(q, k, v, rpb):
    bsz, L, H, Dh = q.shape
    S = L - N_META
    rows = S // GRID_W
    kh = min(NA_ROWS, rows)
    ncb = GRID_W // NA_COLS
    scale = Dh ** -0.5
    col_idx, col_valid, dc_idx = _window_tables()

    qm, km, vm = q[:, :N_META], k[:, :N_META], v[:, :N_META]
    sm = jnp.einsum('bqhd,bkhd->bhqk', qm, km).astype(jnp.float32) * scale
    pm = jax.nn.softmax(sm, axis=-1).astype(v.dtype)
    out_meta = jnp.einsum('bhqk,bkhd->bqhd', pm, vm)

    qg = q[:, N_META:].reshape(bsz, rows, ncb, NA_COLS, H, Dh)
    qg = jnp.moveaxis(qg, 1, 0)
    kg = k[:, N_META:].reshape(bsz, rows, GRID_W, H, Dh)
    vg = v[:, N_META:].reshape(bsz, rows, GRID_W, H, Dh)
    n_win = kh * 2 * NA_COLS

    def row_block(args):
        q_row, r = args
        r0 = jnp.clip(r - kh // 2, 0, rows - kh)
        k_rows = lax.dynamic_slice_in_dim(kg, r0, kh, axis=1)
        v_rows = lax.dynamic_slice_in_dim(vg, r0, kh, axis=1)
        kb = k_rows[:, :, col_idx]
        vb = v_rows[:, :, col_idx]
        dr_idx = r0 + jnp.arange(kh, dtype=jnp.int32) - r + (NA_ROWS - 1)
        bias = rpb[:, dr_idx[None, None, :, None], dc_idx[:, :, None, :]]
        s_win = jnp.einsum('bjqhd,bkjmhd->bhjqkm', q_row, kb).astype(jnp.float32) * scale
        s_win = s_win + bias.astype(jnp.float32)[None]
        s_win = jnp.where(col_valid[:, :, None, :], s_win, NEG_INF)
        s_meta = jnp.einsum('bjqhd,bmhd->bhjqm', q_row, km).astype(jnp.float32) * scale
        logits = jnp.concatenate([s_win.reshape(bsz, H, ncb, NA_COLS, n_win), s_meta], axis=-1)
        p = jax.nn.softmax(logits, axis=-1).astype(v.dtype)
        p_win = p[..., :n_win].reshape(bsz, H, ncb, NA_COLS, kh, 2 * NA_COLS)
        p_meta = p[..., n_win:]
        return (jnp.einsum('bhjqkm,bkjmhd->bjqhd', p_win, vb)
                + jnp.einsum('bhjqm,bmhd->bjqhd', p_meta, vm))

    out_grid = lax.map(row_block, (qg, jnp.arange(rows, dtype=jnp.int32)))
    out_grid = jnp.moveaxis(out_grid, 0, 1).reshape(bsz, S, H, Dh)
    return jnp.concatenate([out_meta, out_grid], axis=1).reshape(bsz, L, H * Dh)


def _centred_dwconv(u, w, b):
    half = w.shape[-1] // 2
    L = u.shape[1]
    up = jnp.pad(u, ((0, 0), (half, half), (0, 0)))
    out = b
    for i in range(w.shape[-1]):
        out = out + up[:, i:i + L] * w[:, i]
    return out


def _ssd(X, dt, A, B, C):
    bsz, T, h, p = X.shape
    g, n = B.shape[2], B.shape[3]
    hg = h // g
    c = T // CHUNK
    Q = CHUNK
    Xdt = (X * dt[..., None]).reshape(bsz, c, Q, g, hg, p)
    Bc = B.reshape(bsz, c, Q, g, n)
    Cc = C.reshape(bsz, c, Q, g, n)
    a_cum = jnp.cumsum((dt * A).reshape(bsz, c, Q, h), axis=2)
    seg = a_cum[:, :, :, None, :] - a_cum[:, :, None, :, :]
    lower = np.tril(np.ones((Q, Q), dtype=bool))[:, :, None]
    Lmat = jnp.exp(jnp.where(lower, seg, -jnp.inf)).reshape(bsz, c, Q, Q, g, hg)
    CB = jnp.einsum('bclgn,bcsgn->bclsg', Cc, Bc)
    y_diag = jnp.einsum('bclsgk,bcsgkp->bclgkp', CB[..., None] * Lmat, Xdt)
    decay_states = jnp.exp(a_cum[:, :, -1:, :] - a_cum).reshape(bsz, c, Q, g, hg)
    states = jnp.einsum('bcsgn,bcsgkp->bcgkpn', Bc, Xdt * decay_states[..., None])
    chunk_decay = jnp.exp(a_cum[:, :, -1, :]).reshape(bsz, c, g, hg)

    def step(hstate, inp):
        dec, st = inp
        return hstate * dec[..., None, None] + st, hstate

    init = jnp.zeros((bsz, g, hg, p, n), jnp.float32)
    _, prev = lax.scan(step, init, (jnp.moveaxis(chunk_decay, 1, 0), jnp.moveaxis(states, 1, 0)))
    prev = jnp.moveaxis(prev, 0, 1)
    y_off = jnp.einsum('bclgn,bcgkpn->bclgkp', Cc, prev)
    y_off = y_off * jnp.exp(a_cum).reshape(bsz, c, Q, g, hg)[..., None]
    return (y_diag + y_off).reshape(bsz, T, h, p)


def _bidirectional_ssd(xs, dt_f, dt_b, a_f, a_b, b_ssm, c_ssm, d_skip):
    pad = CHUNK - N_META
    X = xs.astype(jnp.float32)
    pad4 = ((0, 0), (pad, 0), (0, 0), (0, 0))
    pad3 = ((0, 0), (pad, 0), (0, 0))
    Xp = jnp.pad(X, pad4)
    Bp = jnp.pad(b_ssm.astype(jnp.float32), pad4)
    Cp = jnp.pad(c_ssm.astype(jnp.float32), pad4)
    dfp = jnp.pad(dt_f, pad3)
    dbp = jnp.pad(dt_b, pad3)
    y_f = _ssd(Xp, dfp, a_f, Bp, Cp)
    fl = lambda t: jnp.flip(t, axis=1)
    y_b = fl(_ssd(fl(Xp), fl(dbp), a_b, fl(Bp), fl(Cp)))
    return (y_f + y_b)[:, pad:] + d_skip.astype(jnp.float32)[:, None] * X


def _moe(h, w_router, w_gate, w_up, w_down):
    shp = h.shape
    xt = h.reshape(-1, shp[-1])
    n_tok = xt.shape[0]
    logits = (xt @ w_router).astype(jnp.float32)
    top_v, top_e = lax.top_k(logits, TOP_K)
    top_w = jax.nn.softmax(top_v, axis=-1)
    flat_e = top_e.reshape(-1)
    flat_w = top_w.reshape(-1)
    flat_tok = jnp.arange(n_tok * TOP_K, dtype=jnp.int32) // TOP_K
    order = jnp.argsort(flat_e)
    e_s, tok_s, w_s = flat_e[order], flat_tok[order], flat_w[order]
    counts = jnp.bincount(flat_e, length=N_EXPERTS)
    starts = jnp.cumsum(counts) - counts
    padded = (counts + MOE_BLOCK - 1) // MOE_BLOCK * MOE_BLOCK
    pends = jnp.cumsum(padded)
    pstarts = pends - padded
    dest = pstarts[e_s] + (jnp.arange(n_tok * TOP_K, dtype=jnp.int32) - starts[e_s])
    n_blocks = (n_tok * TOP_K + MOE_BLOCK - 1) // MOE_BLOCK + N_EXPERTS
    buf = jnp.zeros((n_blocks * MOE_BLOCK, xt.shape[-1]), xt.dtype).at[dest].set(xt[tok_s])
    block_e = jnp.searchsorted(pends, jnp.arange(n_blocks, dtype=jnp.int32) * MOE_BLOCK, side='right')
    block_e = jnp.minimum(block_e, N_EXPERTS - 1)

    def expert_block(args):
        xb, e = args
        return _swiglu(xb, w_gate[e], w_up[e], w_down[e])

    out = lax.map(expert_block, (buf.reshape(n_blocks, MOE_BLOCK, -1), block_e))
    out = out.reshape(n_blocks * MOE_BLOCK, -1)[dest] * w_s[:, None].astype(out.dtype)
    y = jnp.zeros_like(xt).at[tok_s].add(out.astype(xt.dtype))
    return y.reshape(shp)


def _layer(x, l, p):
    bsz, L, _ = x.shape
    h = _rms_norm(x, p['norm1_w'][l])
    proj = h @ p['w_in'][l]
    q, k, v, z, xbc, dt_raw = jnp.split(proj, IN_SPLITS, axis=-1)
    q = _rms_norm(q.reshape(bsz, L, N_ATT_HEADS, ATT_HEAD_DIM), p['q_norm_w'][l])
    k = _rms_norm(k.reshape(bsz, L, N_ATT_HEADS, ATT_HEAD_DIM), p['k_norm_w'][l])
    v = v.reshape(bsz, L, N_ATT_HEADS, ATT_HEAD_DIM)
    attn = _neighbourhood_attention(q, k, v, p['rpb'][l])
    attn = _rms_norm(attn, p['attn_out_norm_w'][l])
    xbc = jax.nn.silu(_centred_dwconv(xbc, p['conv_w'][l], p['conv_b'][l]))
    xs, b_ssm, c_ssm = jnp.split(xbc, (D_SSM, D_SSM + SSM_GROUPS * SSM_STATE), axis=-1)
    xs = xs.reshape(bsz, L, N_SSM_HEADS, SSM_HEAD_DIM)
    b_ssm = b_ssm.reshape(bsz, L, SSM_GROUPS, SSM_STATE)
    c_ssm = c_ssm.reshape(bsz, L, SSM_GROUPS, SSM_STATE)
    dt = jax.nn.softplus(dt_raw.astype(jnp.float32).reshape(bsz, L, 2, N_SSM_HEADS)
                         + p['dt_bias'][l].astype(jnp.float32))
    A = -jnp.exp(p['a_log'][l].astype(jnp.float32))
    y = _bidirectional_ssd(xs, dt[:, :, 0], dt[:, :, 1], A[0], A[1], b_ssm, c_ssm, p['d_skip'][l])
    y = y.reshape(bsz, L, D_SSM) * jax.nn.silu(z.astype(jnp.float32))
    y = _rms(y.reshape(bsz, L, SSM_GROUPS, D_SSM // SSM_GROUPS)).reshape(bsz, L, D_SSM)
    y = (y * p['ssm_norm_w'][l].astype(jnp.float32)).astype(x.dtype)
    x = x + jnp.concatenate([attn, y], axis=-1) @ p['w_out'][l]
    h2 = _rms_norm(x, p['norm2_w'][l])
    if l % 2 == 0:
        j = l // 2
        f = _swiglu(h2, p['ffn_w_gate'][j], p['ffn_w_up'][j], p['ffn_w_down'][j])
    else:
        j = l // 2
        f = _moe(h2, p['moe_router'][j], p['moe_w_gate'][j], p['moe_w_up'][j], p['moe_w_down'][j])
    return x + f


def _trunk(x_in, p):
    bsz = x_in.shape[0]
    meta = jnp.broadcast_to(p['meta_tokens'][None].astype(x_in.dtype), (bsz, N_META, x_in.shape[-1]))
    x = jnp.concatenate([meta, x_in], axis=1)
    for l in range(DEPTH):
        x = _layer(x, l, p)
    return x[:, N_META:]


def setup_inputs(seed: int = 0) -> dict:
    key = jax.random.key(seed)
    ks = iter(jax.random.split(key, 32))
    f32 = jnp.float32

    def nrm(shape, scale):
        return jax.random.normal(next(ks), shape, f32) * scale

    n_dense = (DEPTH + 1) // 2
    n_moe = DEPTH // 2
    u_dt = jax.random.uniform(next(ks), (DEPTH, 2, N_SSM_HEADS), f32)
    dt0 = jnp.exp(u_dt * (math.log(0.1) - math.log(0.001)) + math.log(0.001))
    dt_bias = dt0 + jnp.log(-jnp.expm1(-dt0))
    a_log = jnp.log(jax.random.uniform(next(ks), (DEPTH, 2, N_SSM_HEADS), f32, minval=1.0, maxval=16.0))
    return {
        'x_prompt': nrm((BATCH, SEQ, D_MODEL), 1.0),
        'x_sample': nrm((DEC_BATCH, DEC_SEQ, D_MODEL), 1.0),
        'meta_tokens': nrm((N_META, D_MODEL), 1.0),
        'norm1_w': 1.0 + nrm((DEPTH, D_MODEL), 0.02),
        'w_in': nrm((DEPTH, D_MODEL, IN_COLS), D_MODEL ** -0.5),
        'q_norm_w': 1.0 + nrm((DEPTH, ATT_HEAD_DIM), 0.02),
        'k_norm_w': 1.0 + nrm((DEPTH, ATT_HEAD_DIM), 0.02),
        'rpb': nrm((DEPTH, N_ATT_HEADS, 2 * NA_ROWS - 1, 2 * NA_COLS - 1), 0.02),
        'attn_out_norm_w': 1.0 + nrm((DEPTH, ATT_W), 0.02),
        'conv_w': nrm((DEPTH, CONV_CH, CONV_K), CONV_K ** -0.5),
        'conv_b': nrm((DEPTH, CONV_CH), 0.02),
        'dt_bias': dt_bias,
        'a_log': a_log,
        'd_skip': 1.0 + nrm((DEPTH, N_SSM_HEADS), 0.1),
        'ssm_norm_w': 1.0 + nrm((DEPTH, D_SSM), 0.02),
        'w_out': nrm((DEPTH, MIX_W, D_MODEL), MIX_W ** -0.5),
        'norm2_w': 1.0 + nrm((DEPTH, D_MODEL), 0.02),
        'ffn_w_gate': nrm((n_dense, D_MODEL, D_FF), D_MODEL ** -0.5),
        'ffn_w_up': nrm((n_dense, D_MODEL, D_FF), D_MODEL ** -0.5),
        'ffn_w_down': nrm((n_dense, D_FF, D_MODEL), D_FF ** -0.5),
        'moe_router': nrm((n_moe, D_MODEL, N_EXPERTS), D_MODEL ** -0.5),
        'moe_w_gate': nrm((n_moe, N_EXPERTS, D_MODEL, D_FF_EXPERT), D_MODEL ** -0.5),
        'moe_w_up': nrm((n_moe, N_EXPERTS, D_MODEL, D_FF_EXPERT), D_MODEL ** -0.5),
        'moe_w_down': nrm((n_moe, N_EXPERTS, D_FF_EXPERT, D_MODEL), D_FF_EXPERT ** -0.5),
    }


def reference(x_prompt, x_sample, meta_tokens, norm1_w, w_in, q_norm_w, k_norm_w, rpb, attn_out_norm_w,
              conv_w, conv_b, dt_bias, a_log, d_skip, ssm_norm_w, w_out, norm2_w,
              ffn_w_gate, ffn_w_up, ffn_w_down, moe_router, moe_w_gate, moe_w_up, moe_w_down):
    p = {
        'meta_tokens': meta_tokens, 'norm1_w': norm1_w, 'w_in': w_in, 'q_norm_w': q_norm_w,
        'k_norm_w': k_norm_w, 'rpb': rpb, 'attn_out_norm_w': attn_out_norm_w, 'conv_w': conv_w,
        'conv_b': conv_b, 'dt_bias': dt_bias, 'a_log': a_log, 'd_skip': d_skip, 'ssm_norm_w': ssm_norm_w,
        'w_out': w_out, 'norm2_w': norm2_w, 'ffn_w_gate': ffn_w_gate, 'ffn_w_up': ffn_w_up,
        'ffn_w_down': ffn_w_down, 'moe_router': moe_router, 'moe_w_gate': moe_w_gate,
        'moe_w_up': moe_w_up, 'moe_w_down': moe_w_down,
    }
    y_prompt = _trunk(x_prompt, p)
    y_sample = _trunk(x_sample, p)
    return (y_prompt, y_sample)
```

```python
import functools

import numpy as np
import jax
import jax.numpy as jnp
from jax import lax
from jax.experimental import pallas as pl
from jax.experimental.pallas import tpu as pltpu

F32 = jnp.float32
BF16 = jnp.bfloat16

D_MODEL = 1024
N_META = 16
GRID_W = 64
NA_ROWS = 8
NA_COLS = 16
N_HEADS = 16
HEAD_DIM = 64
N_PAIRS = N_HEADS // 2
ATT_W = N_HEADS * HEAD_DIM
D_SSM = 1024
SSM_GROUPS = 2
SSM_STATE = 128
GROUP_W = D_SSM // SSM_GROUPS
CONV_K = 5
CONV_CH = D_SSM + 2 * SSM_GROUPS * SSM_STATE
D_FF = 2816
N_EXPERTS = 8
D_FF_EXPERT = 3584
RMS_EPS = 1e-6
NEG_INF = -1e30

LANES = 128
ROW_TILE = D_MODEL // LANES
CHUNK = 128
META_OFF = CHUNK - N_META
TM = 512
BAND_ROWS = 8
BAND = BAND_ROWS * GRID_W
WIN = NA_ROWS * GRID_W
FF_CHUNK = 256
TME = 512
TF = 512
RB = 2 * TM
TC = 256
VMEM_LIMIT = 56 * 1024 * 1024


def _cparams(sem):
    return pltpu.CompilerParams(dimension_semantics=sem, vmem_limit_bytes=VMEM_LIMIT)


def _round_up(a, b):
    return (a + b - 1) // b * b


def _const_spec(shape):
    nd = len(shape)
    return pl.BlockSpec(shape, lambda *_: (0,) * nd)


class _Layout:
    def __init__(self, seq_lens):
        self.seq_lens = tuple(seq_lens)
        self.n_seq = len(seq_lens)
        self.n_grid = sum(seq_lens)
        self.starts = np.concatenate([[0], np.cumsum(seq_lens)[:-1]]).astype(np.int64)
        self.n_tok = _round_up(self.n_grid + CHUNK * self.n_seq, TM)
        assert all(s % BAND == 0 and s // GRID_W >= NA_ROWS for s in seq_lens)

    def meta_block(self, s):
        return self.n_grid // CHUNK + s

    def attn_tables(self):
        prev, nxt, flags, mblk = [], [], [], []
        for s, (st, ln) in enumerate(zip(self.starts, self.seq_lens)):
            lo, hi = st // BAND, (st + ln) // BAND
            for b in range(lo, hi):
                prev.append(max(b - 1, lo))
                nxt.append(min(b + 1, hi - 1))
                flags.append((1 if b == lo else 0) | (2 if b == hi - 1 else 0))
                mblk.append((self.n_grid + CHUNK * s + META_OFF) // N_META)
        return [np.asarray(a, np.int32) for a in (prev, nxt, flags, mblk)]

    def ssd_tables(self, reverse):
        blk, prev, nxt, flags = [], [], [], []
        for s, (st, ln) in enumerate(zip(self.starts, self.seq_lens)):
            nc = ln // CHUNK
            b0 = st // CHUNK
            meta16 = (self.n_grid + CHUNK * s + META_OFF) // N_META
            steps = []
            steps.append((self.meta_block(s), 0, b0 * 8, 2 | 4))
            for c in range(nc):
                b = b0 + c
                p16 = meta16 if c == 0 else b * 8 - 1
                n16 = (b + 1) * 8 if c < nc - 1 else 0
                steps.append((b, p16, n16, 1 | (2 if c < nc - 1 else 0)))
            if reverse:
                steps = steps[::-1]
            for i, (b, p, n, f) in enumerate(steps):
                blk.append(b); prev.append(p); nxt.append(n)
                flags.append(f | (8 if i == 0 else 0))
        for b in range(self.n_grid // CHUNK + self.n_seq, self.n_tok // CHUNK):
            blk.append(b); prev.append(0); nxt.append(0)
            flags.append(8 | 16)
        return [np.asarray(a, np.int32) for a in (blk, prev, nxt, flags)]


def _rms_rows(x, w):
    ms = jnp.mean(x * x, axis=-1, keepdims=True)
    return x * lax.rsqrt(ms + RMS_EPS) * w


def _norm_kernel(x_ref, w_ref, o_ref):
    o_ref[...] = _rms_rows(x_ref[...], w_ref[...]).astype(BF16)


def _norm(x, w):
    n = x.shape[0]
    return pl.pallas_call(
        _norm_kernel,
        grid=(n // TM,),
        in_specs=[pl.BlockSpec((TM, D_MODEL), lambda i: (i, 0)), _const_spec((1, D_MODEL))],
        out_specs=pl.BlockSpec((TM, D_MODEL), lambda i: (i, 0)),
        out_shape=jax.ShapeDtypeStruct((n, D_MODEL), BF16),
        compiler_params=_cparams(("parallel",)),
    )(x, w.reshape(1, D_MODEL))


def _inproj_kernel(h_ref, wqk_ref, wvzx_ref, wdt_ref, g_ref, nw_ref,
                   qk_ref, v_ref, z_ref, xbc_ref, dt_ref):
    h = h_ref[...]
    g = g_ref[...]
    for c in range(2 * ATT_W // FF_CHUNK):
        sl = slice(c * FF_CHUNK, (c + 1) * FF_CHUNK)
        y = jnp.dot(h, wqk_ref[:, sl], preferred_element_type=F32)
        ss = jnp.dot((y * y).astype(BF16), g, preferred_element_type=F32)
        inv = lax.rsqrt(ss * (1.0 / HEAD_DIM) + RMS_EPS)
        qk_ref[:, sl] = (y * inv * nw_ref[:, sl]).astype(BF16)
    for c in range(ATT_W // FF_CHUNK):
        sl = slice(c * FF_CHUNK, (c + 1) * FF_CHUNK)
        v_ref[:, sl] = jnp.dot(h, wvzx_ref[:, sl], preferred_element_type=F32).astype(BF16)
    for c in range(D_SSM // FF_CHUNK):
        sl = slice(c * FF_CHUNK, (c + 1) * FF_CHUNK)
        src = slice(ATT_W + c * FF_CHUNK, ATT_W + (c + 1) * FF_CHUNK)
        z_ref[:, sl] = jnp.dot(h, wvzx_ref[:, src], preferred_element_type=F32).astype(BF16)
    for c in range(CONV_CH // FF_CHUNK):
        sl = slice(c * FF_CHUNK, (c + 1) * FF_CHUNK)
        src = slice(ATT_W + D_SSM + c * FF_CHUNK, ATT_W + D_SSM + (c + 1) * FF_CHUNK)
        xbc_ref[:, sl] = jnp.dot(h, wvzx_ref[:, src], preferred_element_type=F32).astype(BF16)
    dt_ref[...] = jnp.dot(h, wdt_ref[...], preferred_element_type=F32)


def _head_sum_matrix():
    i = np.arange(FF_CHUNK)
    return jnp.asarray((i[:, None] // HEAD_DIM == i[None, :] // HEAD_DIM).astype(np.float32), BF16)


def _inproj(h, wqk, wvzx, wdt, nw):
    n = h.shape[0]
    row = lambda w: pl.BlockSpec((TM, w), lambda i: (i, 0))
    return pl.pallas_call(
        _inproj_kernel,
        grid=(n // TM,),
        in_specs=[row(D_MODEL), _const_spec(wqk.shape), _const_spec(wvzx.shape), _const_spec(wdt.shape),
                  _const_spec((FF_CHUNK, FF_CHUNK)), _const_spec((1, 2 * ATT_W))],
        out_specs=[row(2 * ATT_W), row(ATT_W), row(D_SSM), row(CONV_CH), row(LANES)],
        out_shape=[jax.ShapeDtypeStruct((n, 2 * ATT_W), BF16), jax.ShapeDtypeStruct((n, ATT_W), BF16),
                   jax.ShapeDtypeStruct((n, D_SSM), BF16), jax.ShapeDtypeStruct((n, CONV_CH), BF16),
                   jax.ShapeDtypeStruct((n, LANES), F32)],
        compiler_params=_cparams(("parallel",)),
    )(h, wqk, wvzx, wdt, _head_sum_matrix(), nw)


def _split_bf16(x):
    hi = x.astype(BF16)
    lo = (x - hi.astype(F32)).astype(BF16)
    return hi, lo


def _outproj_kernel(with_router, attn_ref, y_ref, x_ref, w_ref, aw_ref, nw_ref, *rest):
    if with_router:
        wrh_ref, wrl_ref, xo_ref, h_ref, route_ref = rest
    else:
        xo_ref, h_ref = rest
    a = _rms_rows(attn_ref[...].astype(F32), aw_ref[...]).astype(BF16)
    acc = x_ref[...]
    acc = acc + jnp.dot(a, w_ref[0:ATT_W, :], preferred_element_type=F32)
    acc = acc + jnp.dot(y_ref[...], w_ref[ATT_W:ATT_W + D_SSM, :], preferred_element_type=F32)
    xo_ref[...] = acc
    h2 = _rms_rows(acc, nw_ref[...])
    h_ref[...] = h2.astype(BF16)
    if with_router:
        hh, hl = _split_bf16(h2)
        wh = wrh_ref[...]
        logits = (jnp.dot(hh, wh, preferred_element_type=F32)
                  + jnp.dot(hl, wh, preferred_element_type=F32)
                  + jnp.dot(hh, wrl_ref[...], preferred_element_type=F32))
        lane = lax.broadcasted_iota(jnp.int32, logits.shape, 1)
        logits = jnp.where(lane < N_EXPERTS, logits, NEG_INF)
        m1 = jnp.max(logits, axis=-1, keepdims=True)
        i1 = jnp.min(jnp.where(logits == m1, lane, LANES), axis=-1, keepdims=True)
        rest_l = jnp.where(lane == i1, NEG_INF, logits)
        m2 = jnp.max(rest_l, axis=-1, keepdims=True)
        i2 = jnp.min(jnp.where(rest_l == m2, lane, LANES), axis=-1, keepdims=True)
        e = jnp.exp(m2 - m1)
        w1 = 1.0 / (1.0 + e)
        w2 = e * w1
        route = jnp.where(lane == 0, i1.astype(F32),
                          jnp.where(lane == 1, i2.astype(F32),
                                    jnp.where(lane == 2, w1, jnp.where(lane == 3, w2, 0.0))))
        route_ref[...] = route


def _outproj(attn, y, x, w_out, aw, nw, router=None):
    n = x.shape[0]
    row = lambda w: pl.BlockSpec((TM, w), lambda i: (i, 0))
    in_specs = [row(ATT_W), row(D_SSM), row(D_MODEL), _const_spec(w_out.shape),
                _const_spec((1, ATT_W)), _const_spec((1, D_MODEL))]
    out_specs = [row(D_MODEL), row(D_MODEL)]
    out_shape = [jax.ShapeDtypeStruct((n, D_MODEL), F32), jax.ShapeDtypeStruct((n, D_MODEL), BF16)]
    args = [attn, y, x, w_out, aw, nw]
    if router is not None:
        in_specs += [_const_spec((D_MODEL, LANES))] * 2
        out_specs.append(row(LANES))
        out_shape.append(jax.ShapeDtypeStruct((n, LANES), F32))
        args += list(router)
    return pl.pallas_call(
        functools.partial(_outproj_kernel, router is not None),
        grid=(n // TM,), in_specs=in_specs, out_specs=out_specs, out_shape=out_shape,
        compiler_params=_cparams(("parallel",)),
    )(*args)


def _silu(x):
    return x * (1.0 / (1.0 + jnp.exp(-x)))


def _ffn_kernel(h_ref, x_ref, wg_ref, wu_ref, wd_ref, nw_ref, xo_ref, ho_ref, act_ref):
    h = h_ref[...]
    for c in range(D_FF // FF_CHUNK):
        sl = slice(c * FF_CHUNK, (c + 1) * FF_CHUNK)
        g = jnp.dot(h, wg_ref[:, sl], preferred_element_type=F32)
        u = jnp.dot(h, wu_ref[:, sl], preferred_element_type=F32)
        act_ref[:, sl] = (_silu(g) * u).astype(BF16)
    xn = x_ref[...] + jnp.dot(act_ref[...], wd_ref[...], preferred_element_type=F32)
    xo_ref[...] = xn
    ho_ref[...] = _rms_rows(xn, nw_ref[...]).astype(BF16)


def _ffn(h, x, wg, wu, wd, nw_next):
    n = x.shape[0]
    row = lambda w: pl.BlockSpec((TM, w), lambda i: (i, 0))
    return pl.pallas_call(
        _ffn_kernel,
        grid=(n // TM,),
        in_specs=[row(D_MODEL), row(D_MODEL), _const_spec(wg.shape), _const_spec(wu.shape),
                  _const_spec(wd.shape), _const_spec((1, D_MODEL))],
        out_specs=[row(D_MODEL), row(D_MODEL)],
        out_shape=[jax.ShapeDtypeStruct((n, D_MODEL), F32), jax.ShapeDtypeStruct((n, D_MODEL), BF16)],
        scratch_shapes=[pltpu.VMEM((TM, D_FF), BF16)],
        compiler_params=_cparams(("parallel",)),
    )(h, x, wg, wu, wd, nw_next)


def _attn_bias_table(rpb):
    qc = np.arange(GRID_W)
    kc = np.arange(GRID_W)
    st = np.clip(qc - NA_COLS // 2, 0, GRID_W - NA_COLS)
    valid = (kc[None, :] >= st[:, None]) & (kc[None, :] < st[:, None] + NA_COLS)
    dc = np.clip(kc[None, :] - qc[:, None] + NA_COLS - 1, 0, 2 * NA_COLS - 2)
    delta = np.arange(NA_ROWS)
    i = np.arange(NA_ROWS)
    dr = np.clip(i[None, :] - delta[:, None] + NA_ROWS - 1, 0, 2 * NA_ROWS - 2)
    b = rpb.astype(F32)[:, dr][:, :, :, dc]
    b = jnp.where(valid[None, None, None], b, NEG_INF)
    b = jnp.transpose(b, (0, 1, 3, 2, 4)).reshape(N_PAIRS, 2, NA_ROWS, GRID_W, WIN)
    return jnp.transpose(b, (0, 2, 1, 3, 4)).reshape(N_PAIRS, NA_ROWS, 2 * GRID_W, WIN)


def _pair_queries(q):
    lo = lax.broadcasted_iota(jnp.int32, q.shape, 1) < HEAD_DIM
    zero = jnp.zeros_like(q)
    return jnp.concatenate([jnp.where(lo, q, zero), jnp.where(lo, zero, q)], axis=0)


def _nt_dot(a, b):
    return lax.dot_general(a, b, (((1,), (1,)), ((), ())), preferred_element_type=F32)


def _attn_kernel(prev_ref, next_ref, flag_ref, mblk_ref,
                 q_ref, kp_ref, kc_ref, kn_ref, vp_ref, vc_ref, vn_ref, km_ref, vm_ref,
                 bias_ref, mbias_ref, o_ref, kbuf, vbuf, kmbuf, vmbuf):
    b = pl.program_id(1)
    kbuf[0:BAND, :] = kp_ref[...]
    kbuf[BAND:2 * BAND, :] = kc_ref[...]
    kbuf[2 * BAND:3 * BAND, :] = kn_ref[...]
    vbuf[0:BAND, :] = vp_ref[...]
    vbuf[BAND:2 * BAND, :] = vc_ref[...]
    vbuf[2 * BAND:3 * BAND, :] = vn_ref[...]
    kmbuf[...] = jnp.zeros_like(kmbuf)
    vmbuf[...] = jnp.zeros_like(vmbuf)
    kmbuf[0:N_META, :] = km_ref[...]
    vmbuf[0:N_META, :] = vm_ref[...]
    flags = flag_ref[b]
    is_first = (flags & 1) != 0
    is_last = (flags & 2) != 0
    kmeta = kmbuf[...]
    vmeta = vmbuf[...]
    mbias = mbias_ref[...]
    for ri in range(BAND_ROWS):
        br = jnp.int32(BAND_ROWS // 2 + ri)
        br = jnp.where(is_first, jnp.maximum(br, BAND_ROWS), br)
        br = jnp.where(is_last, jnp.minimum(br, BAND_ROWS), br)
        delta = BAND_ROWS + ri - br
        start = pl.multiple_of(br * GRID_W, GRID_W)
        kwin = kbuf[pl.ds(start, WIN), :]
        vwin = vbuf[pl.ds(start, WIN), :]
        qs = _pair_queries(q_ref[ri * GRID_W:(ri + 1) * GRID_W, :])
        s = _nt_dot(qs, kwin) + bias_ref[delta]
        s2 = _nt_dot(qs, kmeta) + mbias
        m = jnp.maximum(jnp.max(s, axis=-1, keepdims=True), jnp.max(s2, axis=-1, keepdims=True))
        p = jnp.exp(s - m)
        p2 = jnp.exp(s2 - m)
        l = jnp.sum(p, axis=-1, keepdims=True) + jnp.sum(p2, axis=-1, keepdims=True)
        o = (jnp.dot(p.astype(BF16), vwin, preferred_element_type=F32)
             + jnp.dot(p2.astype(BF16), vmeta, preferred_element_type=F32))
        o = o / l
        lo = lax.broadcasted_iota(jnp.int32, (GRID_W, LANES), 1) < HEAD_DIM
        o_ref[ri * GRID_W:(ri + 1) * GRID_W, :] = jnp.where(lo, o[0:GRID_W], o[GRID_W:]).astype(BF16)


def _attn_meta_kernel(n_seq, qk_ref, v_ref, mbias_ref, alias_ref, o_ref):
    del alias_ref
    s = pl.program_id(0)
    o_ref[...] = jnp.zeros_like(o_ref)

    @pl.when(s < n_seq)
    def _():
        mbias = mbias_ref[...]
        for p in range(N_PAIRS):
            c = slice(p * LANES, (p + 1) * LANES)
            q = qk_ref[META_OFF:CHUNK, c]
            k = qk_ref[:, ATT_W + p * LANES:ATT_W + (p + 1) * LANES]
            qs = _pair_queries(q)
            sc = _nt_dot(qs, k) + mbias
            m = jnp.max(sc, axis=-1, keepdims=True)
            e = jnp.exp(sc - m)
            l = jnp.sum(e, axis=-1, keepdims=True)
            o = jnp.dot(e.astype(BF16), v_ref[:, c], preferred_element_type=F32) / l
            lo = lax.broadcasted_iota(jnp.int32, (N_META, LANES), 1) < HEAD_DIM
            o_ref[META_OFF:CHUNK, c] = jnp.where(lo, o[0:N_META], o[N_META:]).astype(BF16)


def _attention(lay, qk, v, rpb):
    n = qk.shape[0]
    prev, nxt, flags, mblk = lay.attn_tables()
    n_bands = lay.n_grid // BAND
    bias = _attn_bias_table(rpb)
    lane = np.arange(LANES)
    mbias_main = jnp.asarray(np.where(lane < N_META, 0.0, NEG_INF)[None, :], F32)
    mbias_meta = jnp.asarray(np.where(lane >= META_OFF, 0.0, NEG_INF)[None, :], F32)
    kcol = ATT_W // LANES
    grid_spec = pltpu.PrefetchScalarGridSpec(
        num_scalar_prefetch=4,
        grid=(N_PAIRS, n_bands),
        in_specs=[
            pl.BlockSpec((BAND, LANES), lambda p, b, *_: (b, p)),
            pl.BlockSpec((BAND, LANES), lambda p, b, pv, nx, fl, mb: (pv[b], kcol + p)),
            pl.BlockSpec((BAND, LANES), lambda p, b, pv, nx, fl, mb: (b, kcol + p)),
            pl.BlockSpec((BAND, LANES), lambda p, b, pv, nx, fl, mb: (nx[b], kcol + p)),
            pl.BlockSpec((BAND, LANES), lambda p, b, pv, nx, fl, mb: (pv[b], p)),
            pl.BlockSpec((BAND, LANES), lambda p, b, pv, nx, fl, mb: (b, p)),
            pl.BlockSpec((BAND, LANES), lambda p, b, pv, nx, fl, mb: (nx[b], p)),
            pl.BlockSpec((N_META, LANES), lambda p, b, pv, nx, fl, mb: (mb[b], kcol + p)),
            pl.BlockSpec((N_META, LANES), lambda p, b, pv, nx, fl, mb: (mb[b], p)),
            pl.BlockSpec((None, NA_ROWS, 2 * GRID_W, WIN), lambda p, b, *_: (p, 0, 0, 0)),
            pl.BlockSpec((1, LANES), lambda p, b, *_: (0, 0)),
        ],
        out_specs=pl.BlockSpec((BAND, LANES), lambda p, b, *_: (b, p)),
        scratch_shapes=[pltpu.VMEM((3 * BAND, LANES), BF16), pltpu.VMEM((3 * BAND, LANES), BF16),
                        pltpu.VMEM((LANES, LANES), BF16), pltpu.VMEM((LANES, LANES), BF16)],
    )
    attn = pl.pallas_call(
        _attn_kernel, grid_spec=grid_spec,
        out_shape=jax.ShapeDtypeStruct((n, ATT_W), BF16),
        compiler_params=_cparams(("arbitrary", "arbitrary")),
    )(jnp.asarray(prev), jnp.asarray(nxt), jnp.asarray(flags), jnp.asarray(mblk),
      qk, qk, qk, qk, v, v, v, qk, v, bias, mbias_main)
    mb0 = lay.n_grid // CHUNK
    n_tail = (n - lay.n_grid) // CHUNK
    return pl.pallas_call(
        functools.partial(_attn_meta_kernel, lay.n_seq),
        grid=(n_tail,),
        in_specs=[pl.BlockSpec((CHUNK, 2 * ATT_W), lambda s: (mb0 + s, 0)),
                  pl.BlockSpec((CHUNK, ATT_W), lambda s: (mb0 + s, 0)),
                  _const_spec((1, LANES)),
                  pl.BlockSpec(memory_space=pl.ANY)],
        out_specs=pl.BlockSpec((CHUNK, ATT_W), lambda s: (mb0 + s, 0)),
        out_shape=jax.ShapeDtypeStruct((n, ATT_W), BF16),
        input_output_aliases={3: 0},
        compiler_params=_cparams(("arbitrary",)),
    )(qk, v, mbias_meta, attn)


def _softplus(x):
    return jnp.maximum(x, 0.0) + jnp.log(1.0 + jnp.exp(-jnp.abs(x)))


def _split3_bf16(x):
    hi = x.astype(BF16)
    r = x - hi.astype(F32)
    mid = r.astype(BF16)
    lo = (r - mid.astype(F32)).astype(BF16)
    return hi, mid, lo


def _expand_heads(w, e):
    hi, lo = _split_bf16(w)
    return jnp.dot(hi, e, preferred_element_type=F32) + jnp.dot(lo, e, preferred_element_type=F32)


def _ssd_kernel(reverse, blk_ref, prev_ref, next_ref, flag_ref,
                xbc_ref, xp_ref, xn_ref, dt_ref, cw_ref, dtb_ref, arow_ref, e_ref, tri_ref, *rest):
    if reverse:
        yf_ref, z_ref, dsk_ref, nw_ref, o_ref, xe, state, ybuf = rest
    else:
        o_ref, xe, state = rest
        ybuf = o_ref
    del blk_ref, prev_ref, next_ref
    flags = flag_ref[pl.program_id(0)]
    has_prev = (flags & 1) != 0
    has_next = (flags & 2) != 0
    is_meta = (flags & 4) != 0

    @pl.when((flags & 8) != 0)
    def _():
        state[...] = jnp.zeros_like(state)

    rid = lax.broadcasted_iota(jnp.int32, (CHUNK, 1), 0)
    valid = rid >= jnp.where(is_meta, META_OFF, 0)
    xe[8:8 + CHUNK, :] = jnp.where(valid, xbc_ref[...].astype(F32), 0.0)
    xe[0:8, :] = jnp.where(has_prev, xp_ref[...].astype(F32)[8:16], 0.0)
    xe[8 + CHUNK:16 + CHUNK, :] = jnp.where(has_next, xn_ref[...].astype(F32)[0:8], 0.0)
    half = CONV_K // 2
    u = cw_ref[CONV_K:CONV_K + 1, :]
    for k in range(CONV_K):
        u = u + cw_ref[k:k + 1, :] * xe[8 - half + k:8 - half + k + CHUNK, :]
    u = jnp.where(valid, _silu(u), 0.0)
    xs = u[:, 0:D_SSM]
    x_bf = xs.astype(BF16)

    dt = jnp.where(valid, _softplus(dt_ref[...] + dtb_ref[...]), 0.0)
    a = dt * arow_ref[...]
    tri = tri_ref[...]
    ah, am, al = _split3_bf16(a)
    acum = (jnp.dot(tri, ah, preferred_element_type=F32) + jnp.dot(tri, am, preferred_element_type=F32)
            + jnp.dot(tri, al, preferred_element_type=F32))
    acum_t = acum.T
    dt_t = dt.T
    edge = 0 if reverse else CHUNK - 1
    a_tot = acum[edge:edge + 1, :]
    e = e_ref[...]
    w1e = _expand_heads(dt * jnp.exp(a_tot - acum), e)
    w2e = _expand_heads(jnp.exp(acum), e)
    decay_row = w2e[edge:edge + 1, :]
    xw = (xs * w1e).astype(BF16)

    li = lax.broadcasted_iota(jnp.int32, (CHUNK, CHUNK), 0)
    si = lax.broadcasted_iota(jnp.int32, (CHUNK, CHUNK), 1)
    causal = (si >= li) if reverse else (li >= si)
    lane_lo = si < HEAD_DIM
    ho = N_HEADS if reverse else 0
    heads_per_group = N_HEADS // SSM_GROUPS
    for g in range(SSM_GROUPS):
        gsl = slice(g * GROUP_W, (g + 1) * GROUP_W)
        bg = u[:, D_SSM + g * SSM_STATE:D_SSM + (g + 1) * SSM_STATE]
        cg = u[:, D_SSM + (SSM_GROUPS + g) * SSM_STATE:D_SSM + (SSM_GROUPS + g + 1) * SSM_STATE].astype(BF16)
        cb = _nt_dot(cg, bg.astype(BF16))
        st_old = state[:, gsl]
        y_off = jnp.dot(cg, st_old.astype(BF16), preferred_element_type=F32) * w2e[:, gsl]
        st_new = jnp.dot(bg.T.astype(BF16), xw[:, gsl], preferred_element_type=F32)
        state[:, gsl] = st_old * decay_row[:, gsl] + st_new
        for j in range(heads_per_group // 2):
            col = g * GROUP_W + j * LANES
            acc = y_off[:, j * LANES:(j + 1) * LANES]
            xpair = x_bf[:, col:col + LANES]
            for hh in range(2):
                hi = ho + g * heads_per_group + 2 * j + hh
                seg = acum[:, hi:hi + 1] - acum_t[hi:hi + 1, :]
                lm = jnp.exp(jnp.where(causal, seg, NEG_INF))
                mh = (cb * lm * dt_t[hi:hi + 1, :]).astype(BF16)
                keep = lane_lo if hh == 0 else jnp.logical_not(lane_lo)
                acc = acc + jnp.dot(mh, jnp.where(keep, xpair, jnp.zeros_like(xpair)),
                                    preferred_element_type=F32)
            ybuf[:, col:col + LANES] = acc
    if reverse:
        for g in range(SSM_GROUPS):
            gsl = slice(g * GROUP_W, (g + 1) * GROUP_W)
            yg = ybuf[:, gsl] + yf_ref[:, gsl] + dsk_ref[:, gsl] * xs[:, gsl]
            yg = yg * _silu(z_ref[:, gsl].astype(F32))
            ms = jnp.mean(yg * yg, axis=-1, keepdims=True)
            o_ref[:, gsl] = (yg * lax.rsqrt(ms + RMS_EPS) * nw_ref[:, gsl]).astype(BF16)

    @pl.when((flags & 16) != 0)
    def _():
        o_ref[...] = jnp.zeros_like(o_ref)


def _ssd_constants(reverse):
    ho = N_HEADS if reverse else 0
    e = np.zeros((LANES, D_SSM), np.float32)
    for h in range(N_HEADS):
        e[ho + h, h * HEAD_DIM:(h + 1) * HEAD_DIM] = 1.0
    i = np.arange(CHUNK)
    tri = (i[None, :] >= i[:, None]) if reverse else (i[:, None] >= i[None, :])
    return jnp.asarray(e, BF16), jnp.asarray(tri.astype(np.float32), BF16)


def _ssd_pass(lay, reverse, xbc, dt, cw, dtb, arow, extra=()):
    n = xbc.shape[0]
    blk, prev, nxt, flags = lay.ssd_tables(reverse)
    e, tri = _ssd_constants(reverse)
    cur = lambda w: pl.BlockSpec((CHUNK, w), lambda i, bk, pv, nx, fl: (bk[i], 0))
    in_specs = [cur(CONV_CH),
                pl.BlockSpec((N_META, CONV_CH), lambda i, bk, pv, nx, fl: (pv[i], 0)),
                pl.BlockSpec((N_META, CONV_CH), lambda i, bk, pv, nx, fl: (nx[i], 0)),
                cur(LANES), _const_spec((8, CONV_CH)), _const_spec((1, LANES)), _const_spec((1, LANES)),
                _const_spec((LANES, D_SSM)), _const_spec((CHUNK, CHUNK))]
    scratch = [pltpu.VMEM((CHUNK + 16, CONV_CH), F32), pltpu.VMEM((SSM_STATE, D_SSM), F32)]
    if reverse:
        in_specs += [cur(D_SSM), cur(D_SSM), _const_spec((1, D_SSM)), _const_spec((1, D_SSM))]
        scratch.append(pltpu.VMEM((CHUNK, D_SSM), F32))
    grid_spec = pltpu.PrefetchScalarGridSpec(
        num_scalar_prefetch=4, grid=(len(blk),), in_specs=in_specs,
        out_specs=cur(D_SSM), scratch_shapes=scratch)
    return pl.pallas_call(
        functools.partial(_ssd_kernel, reverse), grid_spec=grid_spec,
        out_shape=jax.ShapeDtypeStruct((n, D_SSM), BF16 if reverse else F32),
        compiler_params=_cparams(("arbitrary",)),
    )(jnp.asarray(blk), jnp.asarray(prev), jnp.asarray(nxt), jnp.asarray(flags),
      xbc, xbc, xbc, dt, cw, dtb, arow, e, tri, *extra)


def _moe_plan(route):
    n = route.shape[0]
    flat_e = route[:, 0:2].astype(jnp.int32).reshape(-1)
    onehot = (flat_e[:, None] == jnp.arange(N_EXPERTS, dtype=jnp.int32)[None, :]).astype(jnp.int32)
    csum = jnp.cumsum(onehot, axis=0)
    counts = csum[-1]
    rank = jnp.sum(csum * onehot, axis=1) - 1
    padded = (counts + TME - 1) // TME * TME
    pends = jnp.cumsum(padded)
    pstarts = pends - padded
    dest = jnp.sum(onehot * pstarts[None, :], axis=1) + rank
    n_rows = _round_up(2 * n + N_EXPERTS * TME, TME)
    n_blocks = n_rows // TME
    block_e = jnp.sum((jnp.arange(n_blocks, dtype=jnp.int32)[:, None] * TME >= pends[None, :]).astype(jnp.int32),
                      axis=1)
    block_e = jnp.minimum(block_e, N_EXPERTS - 1)
    n_used = (pends[-1] // TME).astype(jnp.int32).reshape(1)
    return dest.astype(jnp.int32), block_e, n_used, n_rows


def _row_tile(r):
    return pl.multiple_of(r * ROW_TILE, ROW_TILE)


def _dispatch_kernel(dest_hbm, h_ref, zero_hbm, xs_hbm, idx, stage, sem_i, sem):
    del zero_hbm
    cp = pltpu.make_async_copy(dest_hbm.at[pl.program_id(0)], idx, sem_i)
    cp.start()
    hf = h_ref[...].astype(F32)
    for j in range(ROW_TILE):
        stage[pl.ds(j, TM, stride=ROW_TILE), :] = hf[:, j * LANES:(j + 1) * LANES]
    cp.wait()

    def row_copy(src_row, dst_row):
        return pltpu.make_async_copy(stage.at[pl.ds(_row_tile(src_row), ROW_TILE)],
                                     xs_hbm.at[pl.ds(_row_tile(dst_row), ROW_TILE)], sem)

    def issue(r, c):
        row_copy(lax.shift_right_logical(r, 1), idx[r]).start()
        return c

    lax.fori_loop(0, RB, issue, 0)

    def drain(r, c):
        row_copy(0, 0).wait()
        return c

    lax.fori_loop(0, RB, drain, 0)


def _dispatch(h, dest, n_rows):
    n = h.shape[0]
    steps = n // TM
    return pl.pallas_call(
        _dispatch_kernel,
        grid=(steps,),
        in_specs=[pl.BlockSpec(memory_space=pl.ANY), pl.BlockSpec((TM, D_MODEL), lambda i: (i, 0)),
                  pl.BlockSpec(memory_space=pl.ANY)],
        out_specs=pl.BlockSpec(memory_space=pl.ANY),
        out_shape=jax.ShapeDtypeStruct((n_rows * ROW_TILE, LANES), F32),
        scratch_shapes=[pltpu.SMEM((RB,), jnp.int32), pltpu.VMEM((TM * ROW_TILE, LANES), F32),
                        pltpu.SemaphoreType.DMA, pltpu.SemaphoreType.DMA],
        input_output_aliases={2: 0},
        compiler_params=_cparams(("arbitrary",)),
    )(dest.reshape(steps, RB), h, jnp.zeros((n_rows * ROW_TILE, LANES), F32))


def _expert_kernel(be_ref, nu_ref, x_ref, wg_ref, wu_ref, wd_ref, o_ref, xbf, acc):
    del be_ref
    i = pl.program_id(0)
    f = pl.program_id(1)
    last = pl.num_programs(1) - 1
    used = i < nu_ref[0]

    @pl.when(used)
    def _():
        @pl.when(f == 0)
        def _():
            for j in range(ROW_TILE):
                xbf[:, j * LANES:(j + 1) * LANES] = x_ref[pl.ds(j, TME, stride=ROW_TILE), :].astype(BF16)

        x = xbf[...]
        g = jnp.dot(x, wg_ref[...], preferred_element_type=F32)
        u = jnp.dot(x, wu_ref[...], preferred_element_type=F32)
        part = jnp.dot((_silu(g) * u).astype(BF16), wd_ref[...], preferred_element_type=F32)

        @pl.when(f == 0)
        def _():
            acc[...] = part

        @pl.when(f > 0)
        def _():
            acc[...] += part

        @pl.when(f == last)
        def _():
            for j in range(ROW_TILE):
                o_ref[pl.ds(j, TME, stride=ROW_TILE), :] = acc[:, j * LANES:(j + 1) * LANES]

    @pl.when(jnp.logical_and(jnp.logical_not(used), f == last))
    def _():
        o_ref[...] = jnp.zeros_like(o_ref)


def _experts(xs, block_e, n_used, wg, wu, wd):
    n_blocks = xs.shape[0] // (TME * ROW_TILE)
    n_f = D_FF_EXPERT // TF
    ftile = lambda i, f, nu: jnp.where(i < nu[0], f, n_f - 1)
    rows = pl.BlockSpec((TME * ROW_TILE, LANES), lambda i, f, be, nu: (i, 0))
    grid_spec = pltpu.PrefetchScalarGridSpec(
        num_scalar_prefetch=2, grid=(n_blocks, n_f),
        in_specs=[rows,
                  pl.BlockSpec((None, D_MODEL, TF), lambda i, f, be, nu: (be[i], 0, ftile(i, f, nu))),
                  pl.BlockSpec((None, D_MODEL, TF), lambda i, f, be, nu: (be[i], 0, ftile(i, f, nu))),
                  pl.BlockSpec((None, TF, D_MODEL), lambda i, f, be, nu: (be[i], ftile(i, f, nu), 0))],
        out_specs=rows,
        scratch_shapes=[pltpu.VMEM((TME, D_MODEL), BF16), pltpu.VMEM((TME, D_MODEL), F32)])
    return pl.pallas_call(
        _expert_kernel, grid_spec=grid_spec,
        out_shape=jax.ShapeDtypeStruct(xs.shape, F32),
        compiler_params=_cparams(("arbitrary", "arbitrary")),
    )(block_e, n_used, xs, wg, wu, wd)


def _combine_kernel(dest_hbm, eo_hbm, x_ref, route_ref, nw_ref, xo_ref, ho_ref, idx, gbuf, sem_i, sem):
    cp = pltpu.make_async_copy(dest_hbm.at[pl.program_id(0)], idx, sem_i)
    cp.start()
    cp.wait()

    def row_copy(src_row, k, t):
        return pltpu.make_async_copy(eo_hbm.at[pl.ds(_row_tile(src_row), ROW_TILE)],
                                     gbuf.at[k, pl.ds(_row_tile(t), ROW_TILE)], sem)

    def issue(r, c):
        row_copy(idx[r], r & 1, lax.shift_right_logical(r, 1)).start()
        return c

    lax.fori_loop(0, 2 * TC, issue, 0)

    def drain(r, c):
        row_copy(0, 0, 0).wait()
        return c

    lax.fori_loop(0, 2 * TC, drain, 0)
    w1 = route_ref[:, 2:3]
    w2 = route_ref[:, 3:4]
    for j in range(ROW_TILE):
        sl = slice(j * LANES, (j + 1) * LANES)
        xo_ref[:, sl] = x_ref[:, sl] + (gbuf[0, pl.ds(j, TC, stride=ROW_TILE), :] * w1
                                        + gbuf[1, pl.ds(j, TC, stride=ROW_TILE), :] * w2)
    ho_ref[...] = _rms_rows(xo_ref[...], nw_ref[...]).astype(BF16)


def _combine(eo, dest, x, route, nw_next):
    n = x.shape[0]
    steps = n // TC
    row = lambda w: pl.BlockSpec((TC, w), lambda i: (i, 0))
    return pl.pallas_call(
        _combine_kernel,
        grid=(steps,),
        in_specs=[pl.BlockSpec(memory_space=pl.ANY), pl.BlockSpec(memory_space=pl.ANY),
                  row(D_MODEL), row(LANES), _const_spec((1, D_MODEL))],
        out_specs=[row(D_MODEL), row(D_MODEL)],
        out_shape=[jax.ShapeDtypeStruct((n, D_MODEL), F32), jax.ShapeDtypeStruct((n, D_MODEL), BF16)],
        scratch_shapes=[pltpu.SMEM((2 * TC,), jnp.int32), pltpu.VMEM((2, TC * ROW_TILE, LANES), F32),
                        pltpu.SemaphoreType.DMA, pltpu.SemaphoreType.DMA],
        compiler_params=_cparams(("arbitrary",)),
    )(dest.reshape(steps, 2 * TC), eo, x, route, nw_next)


def _trunk(lay, x, p):
    depth = p['norm1_w'].shape[0]
    row = lambda v: v.reshape(1, -1).astype(F32)
    h = _norm(x, p['norm1_w'][0])
    for l in range(depth):
        w_in = p['w_in'][l]
        wqk = w_in[:, 0:2 * ATT_W].astype(BF16)
        wvzx = w_in[:, 2 * ATT_W:3 * ATT_W + D_SSM + CONV_CH].astype(BF16)
        wdt = jnp.pad(w_in[:, 3 * ATT_W + D_SSM + CONV_CH:], ((0, 0), (0, LANES - 2 * N_HEADS))).astype(BF16)
        nw_qk = jnp.concatenate([jnp.tile(p['q_norm_w'][l].astype(F32), N_HEADS) * (HEAD_DIM ** -0.5),
                                 jnp.tile(p['k_norm_w'][l].astype(F32), N_HEADS)]).reshape(1, -1)
        qk, v, z, xbc, dt_raw = _inproj(h, wqk, wvzx, wdt, nw_qk)

        attn = _attention(lay, qk, v, p['rpb'][l])

        cw = jnp.concatenate([p['conv_w'][l].astype(F32).T, p['conv_b'][l].astype(F32)[None],
                              jnp.zeros((8 - CONV_K - 1, CONV_CH), F32)], axis=0)
        pad = jnp.zeros((LANES - 2 * N_HEADS,), F32)
        dtb = jnp.concatenate([p['dt_bias'][l].astype(F32).reshape(-1), pad]).reshape(1, -1)
        arow = jnp.concatenate([-jnp.exp(p['a_log'][l].astype(F32)).reshape(-1), pad]).reshape(1, -1)
        y_f = _ssd_pass(lay, False, xbc, dt_raw, cw, dtb, arow)
        dsk = jnp.repeat(p['d_skip'][l].astype(F32), HEAD_DIM).reshape(1, -1)
        y = _ssd_pass(lay, True, xbc, dt_raw, cw, dtb, arow,
                      extra=(y_f, z, dsk, row(p['ssm_norm_w'][l])))

        w_out = p['w_out'][l].astype(BF16)
        nw_next = row(p['norm1_w'][l + 1]) if l + 1 < depth else jnp.ones((1, D_MODEL), F32)
        j = l // 2
        if l % 2 == 0:
            x, h2 = _outproj(attn, y, x, w_out, row(p['attn_out_norm_w'][l]), row(p['norm2_w'][l]))
            x, h = _ffn(h2, x, p['ffn_w_gate'][j].astype(BF16), p['ffn_w_up'][j].astype(BF16),
                        p['ffn_w_down'][j].astype(BF16), nw_next)
        else:
            wr = jnp.pad(p['moe_router'][j].astype(F32), ((0, 0), (0, LANES - N_EXPERTS)))
            x, h2, route = _outproj(attn, y, x, w_out, row(p['attn_out_norm_w'][l]), row(p['norm2_w'][l]),
                                    router=_split_bf16(wr))
            dest, block_e, n_used, n_rows = _moe_plan(route)
            xs = _dispatch(h2, dest, n_rows)
            eo = _experts(xs, block_e, n_used, p['moe_w_gate'][j].astype(BF16),
                          p['moe_w_up'][j].astype(BF16), p['moe_w_down'][j].astype(BF16))
            x, h = _combine(eo, dest, x, route, nw_next)
    return x


def kernel(x_prompt, x_sample, meta_tokens, norm1_w, w_in, q_norm_w, k_norm_w, rpb, attn_out_norm_w,
           conv_w, conv_b, dt_bias, a_log, d_skip, ssm_norm_w, w_out, norm2_w,
           ffn_w_gate, ffn_w_up, ffn_w_down, moe_router, moe_w_gate, moe_w_up, moe_w_down):
    p = dict(norm1_w=norm1_w, w_in=w_in, q_norm_w=q_norm_w, k_norm_w=k_norm_w, rpb=rpb,
             attn_out_norm_w=attn_out_norm_w, conv_w=conv_w, conv_b=conv_b, dt_bias=dt_bias, a_log=a_log,
             d_skip=d_skip, ssm_norm_w=ssm_norm_w, w_out=w_out, norm2_w=norm2_w, ffn_w_gate=ffn_w_gate,
             ffn_w_up=ffn_w_up, ffn_w_down=ffn_w_down, moe_router=moe_router, moe_w_gate=moe_w_gate,
             moe_w_up=moe_w_up, moe_w_down=moe_w_down)
    groups = (x_prompt, x_sample)
    lay = _Layout([g.shape[1] for g in groups for _ in range(g.shape[0])])
    meta_block = jnp.concatenate([jnp.zeros((META_OFF, D_MODEL), F32), meta_tokens.astype(F32)], axis=0)
    tail = lay.n_tok - lay.n_grid - CHUNK * lay.n_seq
    x = jnp.concatenate([g.reshape(-1, D_MODEL).astype(F32) for g in groups]
                        + [jnp.tile(meta_block, (lay.n_seq, 1)), jnp.zeros((tail, D_MODEL), F32)], axis=0)
    x = _trunk(lay, x, p)
    outs, off = [], 0
    for g in groups:
        cnt = g.shape[0] * g.shape[1]
        outs.append(x[off:off + cnt].reshape(g.shape).astype(g.dtype))
        off += cnt
    return tuple(outs)
```

```python
import functools

import numpy as np
import jax
import jax.numpy as jnp
from jax import lax
from jax.experimental import pallas as pl
from jax.experimental.pallas import tpu as pltpu

F32 = jnp.float32
BF16 = jnp.bfloat16

D_MODEL = 1024
N_META = 16
GRID_W = 64
NA_ROWS = 8
NA_COLS = 16
N_HEADS = 16
HEAD_DIM = 64
N_PAIRS = N_HEADS // 2
ATT_W = N_HEADS * HEAD_DIM
D_SSM = 1024
SSM_GROUPS = 2
SSM_STATE = 128
GROUP_W = D_SSM // SSM_GROUPS
CONV_K = 5
CONV_CH = D_SSM + 2 * SSM_GROUPS * SSM_STATE
D_FF = 2816
N_EXPERTS = 8
D_FF_EXPERT = 3584
RMS_EPS = 1e-6
NEG_INF = -1e30

LANES = 128
ROW_TILE = D_MODEL // LANES
CHUNK = 128
META_OFF = CHUNK - N_META
TM = 512
BAND_ROWS = 8
BAND = BAND_ROWS * GRID_W
WIN = NA_ROWS * GRID_W
FF_CHUNK = 256
TME = 512
TF = 512
RB = 2 * TM
TC = 256
DMA_UNROLL = 8
VMEM_LIMIT = 56 * 1024 * 1024


def _cparams(sem):
    return pltpu.CompilerParams(dimension_semantics=sem, vmem_limit_bytes=VMEM_LIMIT)


def _round_up(a, b):
    return (a + b - 1) // b * b


def _const_spec(shape):
    nd = len(shape)
    return pl.BlockSpec(shape, lambda *_: (0,) * nd)


class _Layout:
    def __init__(self, seq_lens):
        self.seq_lens = tuple(seq_lens)
        self.n_seq = len(seq_lens)
        self.n_grid = sum(seq_lens)
        self.starts = np.concatenate([[0], np.cumsum(seq_lens)[:-1]]).astype(np.int64)
        self.n_tok = _round_up(self.n_grid + CHUNK * self.n_seq, TM)
        assert all(s % BAND == 0 and s // GRID_W >= NA_ROWS for s in seq_lens)

    def meta_block(self, s):
        return self.n_grid // CHUNK + s

    def attn_tables(self):
        prev, nxt, flags, mblk = [], [], [], []
        for s, (st, ln) in enumerate(zip(self.starts, self.seq_lens)):
            lo, hi = st // BAND, (st + ln) // BAND
            for b in range(lo, hi):
                prev.append(max(b - 1, lo))
                nxt.append(min(b + 1, hi - 1))
                flags.append((1 if b == lo else 0) | (2 if b == hi - 1 else 0))
                mblk.append((self.n_grid + CHUNK * s + META_OFF) // N_META)
        return [np.asarray(a, np.int32) for a in (prev, nxt, flags, mblk)]

    def ssd_tables(self, reverse):
        blk, prev, nxt, flags = [], [], [], []
        for s, (st, ln) in enumerate(zip(self.starts, self.seq_lens)):
            nc = ln // CHUNK
            b0 = st // CHUNK
            meta16 = (self.n_grid + CHUNK * s + META_OFF) // N_META
            steps = []
            steps.append((self.meta_block(s), 0, b0 * 8, 2 | 4))
            for c in range(nc):
                b = b0 + c
                p16 = meta16 if c == 0 else b * 8 - 1
                n16 = (b + 1) * 8 if c < nc - 1 else 0
                steps.append((b, p16, n16, 1 | (2 if c < nc - 1 else 0)))
            if reverse:
                steps = steps[::-1]
            for i, (b, p, n, f) in enumerate(steps):
                blk.append(b); prev.append(p); nxt.append(n)
                flags.append(f | (8 if i == 0 else 0))
        for b in range(self.n_grid // CHUNK + self.n_seq, self.n_tok // CHUNK):
            blk.append(b); prev.append(0); nxt.append(0)
            flags.append(8 | 16)
        return [np.asarray(a, np.int32) for a in (blk, prev, nxt, flags)]


def _rms_rows(x, w):
    ms = jnp.mean(x * x, axis=-1, keepdims=True)
    return x * lax.rsqrt(ms + RMS_EPS) * w


def _norm_kernel(x_ref, w_ref, o_ref):
    o_ref[...] = _rms_rows(x_ref[...], w_ref[...]).astype(BF16)


def _norm(x, w):
    n = x.shape[0]
    return pl.pallas_call(
        _norm_kernel,
        grid=(n // TM,),
        in_specs=[pl.BlockSpec((TM, D_MODEL), lambda i: (i, 0)), _const_spec((1, D_MODEL))],
        out_specs=pl.BlockSpec((TM, D_MODEL), lambda i: (i, 0)),
        out_shape=jax.ShapeDtypeStruct((n, D_MODEL), BF16),
        compiler_params=_cparams(("parallel",)),
    )(x, w.reshape(1, D_MODEL))


def _inproj_kernel(h_ref, wqk_ref, wvzx_ref, wdt_ref, g_ref, nw_ref,
                   qk_ref, v_ref, z_ref, xbc_ref, dt_ref):
    h = h_ref[...]
    g = g_ref[...]
    for c in range(2 * ATT_W // FF_CHUNK):
        sl = slice(c * FF_CHUNK, (c + 1) * FF_CHUNK)
        y = jnp.dot(h, wqk_ref[:, sl], preferred_element_type=F32)
        ss = jnp.dot((y * y).astype(BF16), g, preferred_element_type=F32)
        inv = lax.rsqrt(ss * (1.0 / HEAD_DIM) + RMS_EPS)
        qk_ref[:, sl] = (y * inv * nw_ref[:, sl]).astype(BF16)
    for c in range(ATT_W // FF_CHUNK):
        sl = slice(c * FF_CHUNK, (c + 1) * FF_CHUNK)
        v_ref[:, sl] = jnp.dot(h, wvzx_ref[:, sl], preferred_element_type=F32).astype(BF16)
    for c in range(D_SSM // FF_CHUNK):
        sl = slice(c * FF_CHUNK, (c + 1) * FF_CHUNK)
        src = slice(ATT_W + c * FF_CHUNK, ATT_W + (c + 1) * FF_CHUNK)
        z_ref[:, sl] = jnp.dot(h, wvzx_ref[:, src], preferred_element_type=F32).astype(BF16)
    for c in range(CONV_CH // FF_CHUNK):
        sl = slice(c * FF_CHUNK, (c + 1) * FF_CHUNK)
        src = slice(ATT_W + D_SSM + c * FF_CHUNK, ATT_W + D_SSM + (c + 1) * FF_CHUNK)
        xbc_ref[:, sl] = jnp.dot(h, wvzx_ref[:, src], preferred_element_type=F32).astype(BF16)
    dt_ref[...] = jnp.dot(h, wdt_ref[...], preferred_element_type=F32)


def _head_sum_matrix():
    i = np.arange(FF_CHUNK)
    return jnp.asarray((i[:, None] // HEAD_DIM == i[None, :] // HEAD_DIM).astype(np.float32), BF16)


def _inproj(h, wqk, wvzx, wdt, nw):
    n = h.shape[0]
    row = lambda w: pl.BlockSpec((TM, w), lambda i: (i, 0))
    return pl.pallas_call(
        _inproj_kernel,
        grid=(n // TM,),
        in_specs=[row(D_MODEL), _const_spec(wqk.shape), _const_spec(wvzx.shape), _const_spec(wdt.shape),
                  _const_spec((FF_CHUNK, FF_CHUNK)), _const_spec((1, 2 * ATT_W))],
        out_specs=[row(2 * ATT_W), row(ATT_W), row(D_SSM), row(CONV_CH), row(LANES)],
        out_shape=[jax.ShapeDtypeStruct((n, 2 * ATT_W), BF16), jax.ShapeDtypeStruct((n, ATT_W), BF16),
                   jax.ShapeDtypeStruct((n, D_SSM), BF16), jax.ShapeDtypeStruct((n, CONV_CH), BF16),
                   jax.ShapeDtypeStruct((n, LANES), F32)],
        compiler_params=_cparams(("parallel",)),
    )(h, wqk, wvzx, wdt, _head_sum_matrix(), nw)


def _split_bf16(x):
    hi = x.astype(BF16)
    lo = (x - hi.astype(F32)).astype(BF16)
    return hi, lo


def _outproj_kernel(with_router, attn_ref, y_ref, x_ref, w_ref, aw_ref, nw_ref, *rest):
    if with_router:
        wrh_ref, wrl_ref, xo_ref, h_ref, route_ref = rest
    else:
        xo_ref, h_ref = rest
    a = _rms_rows(attn_ref[...].astype(F32), aw_ref[...]).astype(BF16)
    acc = x_ref[...]
    acc = acc + jnp.dot(a, w_ref[0:ATT_W, :], preferred_element_type=F32)
    acc = acc + jnp.dot(y_ref[...], w_ref[ATT_W:ATT_W + D_SSM, :], preferred_element_type=F32)
    xo_ref[...] = acc
    h2 = _rms_rows(acc, nw_ref[...])
    h_ref[...] = h2.astype(BF16)
    if with_router:
        hh, hl = _split_bf16(h2)
        wh = wrh_ref[...]
        logits = (jnp.dot(hh, wh, preferred_element_type=F32)
                  + jnp.dot(hl, wh, preferred_element_type=F32)
                  + jnp.dot(hh, wrl_ref[...], preferred_element_type=F32))
        lane = lax.broadcasted_iota(jnp.int32, logits.shape, 1)
        logits = jnp.where(lane < N_EXPERTS, logits, NEG_INF)
        m1 = jnp.max(logits, axis=-1, keepdims=True)
        i1 = jnp.min(jnp.where(logits == m1, lane, LANES), axis=-1, keepdims=True)
        rest_l = jnp.where(lane == i1, NEG_INF, logits)
        m2 = jnp.max(rest_l, axis=-1, keepdims=True)
        i2 = jnp.min(jnp.where(rest_l == m2, lane, LANES), axis=-1, keepdims=True)
        e = jnp.exp(m2 - m1)
        w1 = 1.0 / (1.0 + e)
        w2 = e * w1
        route = jnp.where(lane == 0, i1.astype(F32),
                          jnp.where(lane == 1, i2.astype(F32),
                                    jnp.where(lane == 2, w1, jnp.where(lane == 3, w2, 0.0))))
        route_ref[...] = route


def _outproj(attn, y, x, w_out, aw, nw, router=None):
    n = x.shape[0]
    row = lambda w: pl.BlockSpec((TM, w), lambda i: (i, 0))
    in_specs = [row(ATT_W), row(D_SSM), row(D_MODEL), _const_spec(w_out.shape),
                _const_spec((1, ATT_W)), _const_spec((1, D_MODEL))]
    out_specs = [row(D_MODEL), row(D_MODEL)]
    out_shape = [jax.ShapeDtypeStruct((n, D_MODEL), F32), jax.ShapeDtypeStruct((n, D_MODEL), BF16)]
    args = [attn, y, x, w_out, aw, nw]
    if router is not None:
        in_specs += [_const_spec((D_MODEL, LANES))] * 2
        out_specs.append(row(LANES))
        out_shape.append(jax.ShapeDtypeStruct((n, LANES), F32))
        args += list(router)
    return pl.pallas_call(
        functools.partial(_outproj_kernel, router is not None),
        grid=(n // TM,), in_specs=in_specs, out_specs=out_specs, out_shape=out_shape,
        compiler_params=_cparams(("parallel",)),
    )(*args)


def _silu(x):
    return x * (1.0 / (1.0 + jnp.exp(-x)))


def _ffn_kernel(h_ref, x_ref, wg_ref, wu_ref, wd_ref, nw_ref, xo_ref, ho_ref, act_ref):
    h = h_ref[...]
    for c in range(D_FF // FF_CHUNK):
        sl = slice(c * FF_CHUNK, (c + 1) * FF_CHUNK)
        g = jnp.dot(h, wg_ref[:, sl], preferred_element_type=F32)
        u = jnp.dot(h, wu_ref[:, sl], preferred_element_type=F32)
        act_ref[:, sl] = (_silu(g) * u).astype(BF16)
    xn = x_ref[...] + jnp.dot(act_ref[...], wd_ref[...], preferred_element_type=F32)
    xo_ref[...] = xn
    ho_ref[...] = _rms_rows(xn, nw_ref[...]).astype(BF16)


def _ffn(h, x, wg, wu, wd, nw_next):
    n = x.shape[0]
    row = lambda w: pl.BlockSpec((TM, w), lambda i: (i, 0))
    return pl.pallas_call(
        _ffn_kernel,
        grid=(n // TM,),
        in_specs=[row(D_MODEL), row(D_MODEL), _const_spec(wg.shape), _const_spec(wu.shape),
                  _const_spec(wd.shape), _const_spec((1, D_MODEL))],
        out_specs=[row(D_MODEL), row(D_MODEL)],
        out_shape=[jax.ShapeDtypeStruct((n, D_MODEL), F32), jax.ShapeDtypeStruct((n, D_MODEL), BF16)],
        scratch_shapes=[pltpu.VMEM((TM, D_FF), BF16)],
        compiler_params=_cparams(("parallel",)),
    )(h, x, wg, wu, wd, nw_next)


def _attn_bias_table(rpb):
    qc = np.arange(GRID_W)
    kc = np.arange(GRID_W)
    st = np.clip(qc - NA_COLS // 2, 0, GRID_W - NA_COLS)
    valid = (kc[None, :] >= st[:, None]) & (kc[None, :] < st[:, None] + NA_COLS)
    dc = np.clip(kc[None, :] - qc[:, None] + NA_COLS - 1, 0, 2 * NA_COLS - 2)
    delta = np.arange(NA_ROWS)
    i = np.arange(NA_ROWS)
    dr = np.clip(i[None, :] - delta[:, None] + NA_ROWS - 1, 0, 2 * NA_ROWS - 2)
    b = rpb.astype(F32)[:, dr][:, :, :, dc]
    b = jnp.where(valid[None, None, None], b, NEG_INF)
    b = jnp.transpose(b, (0, 1, 3, 2, 4)).reshape(N_PAIRS, 2, NA_ROWS, GRID_W, WIN)
    return jnp.transpose(b, (0, 2, 1, 3, 4)).reshape(N_PAIRS, NA_ROWS, 2 * GRID_W, WIN)


def _pair_queries(q):
    lo = lax.broadcasted_iota(jnp.int32, q.shape, 1) < HEAD_DIM
    zero = jnp.zeros_like(q)
    return jnp.concatenate([jnp.where(lo, q, zero), jnp.where(lo, zero, q)], axis=0)


def _nt_dot(a, b):
    return lax.dot_general(a, b, (((1,), (1,)), ((), ())), preferred_element_type=F32)


def _attn_kernel(prev_ref, next_ref, flag_ref, mblk_ref,
                 q_ref, kp_ref, kc_ref, kn_ref, vp_ref, vc_ref, vn_ref, km_ref, vm_ref,
                 bias_ref, mbias_ref, o_ref, kbuf, vbuf, kmbuf, vmbuf):
    b = pl.program_id(1)
    kbuf[0:BAND, :] = kp_ref[...]
    kbuf[BAND:2 * BAND, :] = kc_ref[...]
    kbuf[2 * BAND:3 * BAND, :] = kn_ref[...]
    vbuf[0:BAND, :] = vp_ref[...]
    vbuf[BAND:2 * BAND, :] = vc_ref[...]
    vbuf[2 * BAND:3 * BAND, :] = vn_ref[...]
    kmbuf[...] = jnp.zeros_like(kmbuf)
    vmbuf[...] = jnp.zeros_like(vmbuf)
    kmbuf[0:N_META, :] = km_ref[...]
    vmbuf[0:N_META, :] = vm_ref[...]
    flags = flag_ref[b]
    is_first = (flags & 1) != 0
    is_last = (flags & 2) != 0
    kmeta = kmbuf[...]
    vmeta = vmbuf[...]
    mbias = mbias_ref[...]
    for ri in range(BAND_ROWS):
        br = jnp.int32(BAND_ROWS // 2 + ri)
        br = jnp.where(is_first, jnp.maximum(br, BAND_ROWS), br)
        br = jnp.where(is_last, jnp.minimum(br, BAND_ROWS), br)
        delta = BAND_ROWS + ri - br
        start = pl.multiple_of(br * GRID_W, GRID_W)
        kwin = kbuf[pl.ds(start, WIN), :]
        vwin = vbuf[pl.ds(start, WIN), :]
        qs = _pair_queries(q_ref[ri * GRID_W:(ri + 1) * GRID_W, :])
        s = _nt_dot(qs, kwin) + bias_ref[delta]
        s2 = _nt_dot(qs, kmeta) + mbias
        m = jnp.maximum(jnp.max(s, axis=-1, keepdims=True), jnp.max(s2, axis=-1, keepdims=True))
        p = jnp.exp(s - m)
        p2 = jnp.exp(s2 - m)
        l = jnp.sum(p, axis=-1, keepdims=True) + jnp.sum(p2, axis=-1, keepdims=True)
        o = (jnp.dot(p.astype(BF16), vwin, preferred_element_type=F32)
             + jnp.dot(p2.astype(BF16), vmeta, preferred_element_type=F32))
        o = o / l
        lo = lax.broadcasted_iota(jnp.int32, (GRID_W, LANES), 1) < HEAD_DIM
        o_ref[ri * GRID_W:(ri + 1) * GRID_W, :] = jnp.where(lo, o[0:GRID_W], o[GRID_W:]).astype(BF16)


def _attn_meta_kernel(n_seq, qk_ref, v_ref, mbias_ref, alias_ref, o_ref):
    del alias_ref
    s = pl.program_id(0)
    o_ref[...] = jnp.zeros_like(o_ref)

    @pl.when(s < n_seq)
    def _():
        mbias = mbias_ref[...]
        for p in range(N_PAIRS):
            c = slice(p * LANES, (p + 1) * LANES)
            q = qk_ref[META_OFF:CHUNK, c]
            k = qk_ref[:, ATT_W + p * LANES:ATT_W + (p + 1) * LANES]
            qs = _pair_queries(q)
            sc = _nt_dot(qs, k) + mbias
            m = jnp.max(sc, axis=-1, keepdims=True)
            e = jnp.exp(sc - m)
            l = jnp.sum(e, axis=-1, keepdims=True)
            o = jnp.dot(e.astype(BF16), v_ref[:, c], preferred_element_type=F32) / l
            lo = lax.broadcasted_iota(jnp.int32, (N_META, LANES), 1) < HEAD_DIM
            o_ref[META_OFF:CHUNK, c] = jnp.where(lo, o[0:N_META], o[N_META:]).astype(BF16)


def _attention(lay, qk, v, rpb):
    n = qk.shape[0]
    prev, nxt, flags, mblk = lay.attn_tables()
    n_bands = lay.n_grid // BAND
    bias = _attn_bias_table(rpb)
    lane = np.arange(LANES)
    mbias_main = jnp.asarray(np.where(lane < N_META, 0.0, NEG_INF)[None, :], F32)
    mbias_meta = jnp.asarray(np.where(lane >= META_OFF, 0.0, NEG_INF)[None, :], F32)
    kcol = ATT_W // LANES
    grid_spec = pltpu.PrefetchScalarGridSpec(
        num_scalar_prefetch=4,
        grid=(N_PAIRS, n_bands),
        in_specs=[
            pl.BlockSpec((BAND, LANES), lambda p, b, *_: (b, p)),
            pl.BlockSpec((BAND, LANES), lambda p, b, pv, nx, fl, mb: (pv[b], kcol + p)),
            pl.BlockSpec((BAND, LANES), lambda p, b, pv, nx, fl, mb: (b, kcol + p)),
            pl.BlockSpec((BAND, LANES), lambda p, b, pv, nx, fl, mb: (nx[b], kcol + p)),
            pl.BlockSpec((BAND, LANES), lambda p, b, pv, nx, fl, mb: (pv[b], p)),
            pl.BlockSpec((BAND, LANES), lambda p, b, pv, nx, fl, mb: (b, p)),
            pl.BlockSpec((BAND, LANES), lambda p, b, pv, nx, fl, mb: (nx[b], p)),
            pl.BlockSpec((N_META, LANES), lambda p, b, pv, nx, fl, mb: (mb[b], kcol + p)),
            pl.BlockSpec((N_META, LANES), lambda p, b, pv, nx, fl, mb: (mb[b], p)),
            pl.BlockSpec((None, NA_ROWS, 2 * GRID_W, WIN), lambda p, b, *_: (p, 0, 0, 0)),
            pl.BlockSpec((1, LANES), lambda p, b, *_: (0, 0)),
        ],
        out_specs=pl.BlockSpec((BAND, LANES), lambda p, b, *_: (b, p)),
        scratch_shapes=[pltpu.VMEM((3 * BAND, LANES), BF16), pltpu.VMEM((3 * BAND, LANES), BF16),
                        pltpu.VMEM((LANES, LANES), BF16), pltpu.VMEM((LANES, LANES), BF16)],
    )
    attn = pl.pallas_call(
        _attn_kernel, grid_spec=grid_spec,
        out_shape=jax.ShapeDtypeStruct((n, ATT_W), BF16),
        compiler_params=_cparams(("arbitrary", "arbitrary")),
    )(jnp.asarray(prev), jnp.asarray(nxt), jnp.asarray(flags), jnp.asarray(mblk),
      qk, qk, qk, qk, v, v, v, qk, v, bias, mbias_main)
    mb0 = lay.n_grid // CHUNK
    n_tail = (n - lay.n_grid) // CHUNK
    return pl.pallas_call(
        functools.partial(_attn_meta_kernel, lay.n_seq),
        grid=(n_tail,),
        in_specs=[pl.BlockSpec((CHUNK, 2 * ATT_W), lambda s: (mb0 + s, 0)),
                  pl.BlockSpec((CHUNK, ATT_W), lambda s: (mb0 + s, 0)),
                  _const_spec((1, LANES)),
                  pl.BlockSpec(memory_space=pl.ANY)],
        out_specs=pl.BlockSpec((CHUNK, ATT_W), lambda s: (mb0 + s, 0)),
        out_shape=jax.ShapeDtypeStruct((n, ATT_W), BF16),
        input_output_aliases={3: 0},
        compiler_params=_cparams(("arbitrary",)),
    )(qk, v, mbias_meta, attn)


def _softplus(x):
    return jnp.maximum(x, 0.0) + jnp.log(1.0 + jnp.exp(-jnp.abs(x)))


def _split3_bf16(x):
    hi = x.astype(BF16)
    r = x - hi.astype(F32)
    mid = r.astype(BF16)
    lo = (r - mid.astype(F32)).astype(BF16)
    return hi, mid, lo


def _expand_heads(w, e):
    hi, lo = _split_bf16(w)
    return jnp.dot(hi, e, preferred_element_type=F32) + jnp.dot(lo, e, preferred_element_type=F32)


def _ssd_kernel(reverse, blk_ref, prev_ref, next_ref, flag_ref,
                xbc_ref, xp_ref, xn_ref, dt_ref, cw_ref, dtb_ref, arow_ref, e_ref, tri_ref, *rest):
    if reverse:
        yf_ref, z_ref, dsk_ref, nw_ref, o_ref, xe, state, ybuf = rest
    else:
        o_ref, xe, state = rest
        ybuf = o_ref
    del blk_ref, prev_ref, next_ref
    flags = flag_ref[pl.program_id(0)]
    has_prev = (flags & 1) != 0
    has_next = (flags & 2) != 0
    is_meta = (flags & 4) != 0

    @pl.when((flags & 8) != 0)
    def _():
        state[...] = jnp.zeros_like(state)

    rid = lax.broadcasted_iota(jnp.int32, (CHUNK, 1), 0)
    valid = rid >= jnp.where(is_meta, META_OFF, 0)
    xe[8:8 + CHUNK, :] = jnp.where(valid, xbc_ref[...].astype(F32), 0.0)
    xe[0:8, :] = jnp.where(has_prev, xp_ref[...].astype(F32)[8:16], 0.0)
    xe[8 + CHUNK:16 + CHUNK, :] = jnp.where(has_next, xn_ref[...].astype(F32)[0:8], 0.0)
    half = CONV_K // 2
    u = cw_ref[CONV_K:CONV_K + 1, :]
    for k in range(CONV_K):
        u = u + cw_ref[k:k + 1, :] * xe[8 - half + k:8 - half + k + CHUNK, :]
    u = jnp.where(valid, _silu(u), 0.0)
    xs = u[:, 0:D_SSM]
    x_bf = xs.astype(BF16)

    dt = jnp.where(valid, _softplus(dt_ref[...] + dtb_ref[...]), 0.0)
    a = dt * arow_ref[...]
    tri = tri_ref[...]
    ah, am, al = _split3_bf16(a)
    acum = (jnp.dot(tri, ah, preferred_element_type=F32) + jnp.dot(tri, am, preferred_element_type=F32)
            + jnp.dot(tri, al, preferred_element_type=F32))
    acum_t = acum.T
    dt_t = dt.T
    edge = 0 if reverse else CHUNK - 1
    a_tot = acum[edge:edge + 1, :]
    e = e_ref[...]
    w1e = _expand_heads(dt * jnp.exp(a_tot - acum), e)
    w2e = _expand_heads(jnp.exp(acum), e)
    decay_row = w2e[edge:edge + 1, :]
    xw = (xs * w1e).astype(BF16)

    li = lax.broadcasted_iota(jnp.int32, (CHUNK, CHUNK), 0)
    si = lax.broadcasted_iota(jnp.int32, (CHUNK, CHUNK), 1)
    causal = (si >= li) if reverse else (li >= si)
    lane_lo = si < HEAD_DIM
    ho = N_HEADS if reverse else 0
    heads_per_group = N_HEADS // SSM_GROUPS
    for g in range(SSM_GROUPS):
        gsl = slice(g * GROUP_W, (g + 1) * GROUP_W)
        bg = u[:, D_SSM + g * SSM_STATE:D_SSM + (g + 1) * SSM_STATE]
        cg = u[:, D_SSM + (SSM_GROUPS + g) * SSM_STATE:D_SSM + (SSM_GROUPS + g + 1) * SSM_STATE].astype(BF16)
        cb = _nt_dot(cg, bg.astype(BF16))
        st_old = state[:, gsl]
        y_off = jnp.dot(cg, st_old.astype(BF16), preferred_element_type=F32) * w2e[:, gsl]
        st_new = jnp.dot(bg.T.astype(BF16), xw[:, gsl], preferred_element_type=F32)
        state[:, gsl] = st_old * decay_row[:, gsl] + st_new
        for j in range(heads_per_group // 2):
            col = g * GROUP_W + j * LANES
            acc = y_off[:, j * LANES:(j + 1) * LANES]
            xpair = x_bf[:, col:col + LANES]
            for hh in range(2):
                hi = ho + g * heads_per_group + 2 * j + hh
                seg = acum[:, hi:hi + 1] - acum_t[hi:hi + 1, :]
                lm = jnp.exp(jnp.where(causal, seg, NEG_INF))
                mh = (cb * lm * dt_t[hi:hi + 1, :]).astype(BF16)
                keep = lane_lo if hh == 0 else jnp.logical_not(lane_lo)
                acc = acc + jnp.dot(mh, jnp.where(keep, xpair, jnp.zeros_like(xpair)),
                                    preferred_element_type=F32)
            ybuf[:, col:col + LANES] = acc
    if reverse:
        for g in range(SSM_GROUPS):
            gsl = slice(g * GROUP_W, (g + 1) * GROUP_W)
            yg = ybuf[:, gsl] + yf_ref[:, gsl] + dsk_ref[:, gsl] * xs[:, gsl]
            yg = yg * _silu(z_ref[:, gsl].astype(F32))
            ms = jnp.mean(yg * yg, axis=-1, keepdims=True)
            o_ref[:, gsl] = (yg * lax.rsqrt(ms + RMS_EPS) * nw_ref[:, gsl]).astype(BF16)

    @pl.when((flags & 16) != 0)
    def _():
        o_ref[...] = jnp.zeros_like(o_ref)


def _ssd_constants(reverse):
    ho = N_HEADS if reverse else 0
    e = np.zeros((LANES, D_SSM), np.float32)
    for h in range(N_HEADS):
        e[ho + h, h * HEAD_DIM:(h + 1) * HEAD_DIM] = 1.0
    i = np.arange(CHUNK)
    tri = (i[None, :] >= i[:, None]) if reverse else (i[:, None] >= i[None, :])
    return jnp.asarray(e, BF16), jnp.asarray(tri.astype(np.float32), BF16)


def _ssd_pass(lay, reverse, xbc, dt, cw, dtb, arow, extra=()):
    n = xbc.shape[0]
    blk, prev, nxt, flags = lay.ssd_tables(reverse)
    e, tri = _ssd_constants(reverse)
    cur = lambda w: pl.BlockSpec((CHUNK, w), lambda i, bk, pv, nx, fl: (bk[i], 0))
    in_specs = [cur(CONV_CH),
                pl.BlockSpec((N_META, CONV_CH), lambda i, bk, pv, nx, fl: (pv[i], 0)),
                pl.BlockSpec((N_META, CONV_CH), lambda i, bk, pv, nx, fl: (nx[i], 0)),
                cur(LANES), _const_spec((8, CONV_CH)), _const_spec((1, LANES)), _const_spec((1, LANES)),
                _const_spec((LANES, D_SSM)), _const_spec((CHUNK, CHUNK))]
    scratch = [pltpu.VMEM((CHUNK + 16, CONV_CH), F32), pltpu.VMEM((SSM_STATE, D_SSM), F32)]
    if reverse:
        in_specs += [cur(D_SSM), cur(D_SSM), _const_spec((1, D_SSM)), _const_spec((1, D_SSM))]
        scratch.append(pltpu.VMEM((CHUNK, D_SSM), F32))
    grid_spec = pltpu.PrefetchScalarGridSpec(
        num_scalar_prefetch=4, grid=(len(blk),), in_specs=in_specs,
        out_specs=cur(D_SSM), scratch_shapes=scratch)
    return pl.pallas_call(
        functools.partial(_ssd_kernel, reverse), grid_spec=grid_spec,
        out_shape=jax.ShapeDtypeStruct((n, D_SSM), BF16 if reverse else F32),
        compiler_params=_cparams(("arbitrary",)),
    )(jnp.asarray(blk), jnp.asarray(prev), jnp.asarray(nxt), jnp.asarray(flags),
      xbc, xbc, xbc, dt, cw, dtb, arow, e, tri, *extra)


def _moe_plan(route):
    n = route.shape[0]
    flat_e = route[:, 0:2].astype(jnp.int32).reshape(-1)
    onehot = (flat_e[:, None] == jnp.arange(N_EXPERTS, dtype=jnp.int32)[None, :]).astype(jnp.int32)
    csum = jnp.cumsum(onehot, axis=0)
    counts = csum[-1]
    rank = jnp.sum(csum * onehot, axis=1) - 1
    padded = (counts + TME - 1) // TME * TME
    pends = jnp.cumsum(padded)
    pstarts = pends - padded
    dest = jnp.sum(onehot * pstarts[None, :], axis=1) + rank
    n_rows = _round_up(2 * n + N_EXPERTS * TME, TME)
    n_blocks = n_rows // TME
    block_e = jnp.sum((jnp.arange(n_blocks, dtype=jnp.int32)[:, None] * TME >= pends[None, :]).astype(jnp.int32),
                      axis=1)
    block_e = jnp.minimum(block_e, N_EXPERTS - 1)
    n_used = (pends[-1] // TME).astype(jnp.int32).reshape(1)
    return dest.astype(jnp.int32), block_e, n_used, n_rows


def _row_tile(r):
    return pl.multiple_of(r * ROW_TILE, ROW_TILE)


def _dispatch_kernel(dest_hbm, h_ref, zero_hbm, xs_hbm, idx, stage, sem_i, sem):
    del zero_hbm
    cp = pltpu.make_async_copy(dest_hbm.at[pl.program_id(0)], idx, sem_i)
    cp.start()
    hf = h_ref[...].astype(F32)
    for j in range(ROW_TILE):
        stage[pl.ds(j, TM, stride=ROW_TILE), :] = hf[:, j * LANES:(j + 1) * LANES]
    cp.wait()

    def row_copy(src_row, dst_row):
        return pltpu.make_async_copy(stage.at[pl.ds(_row_tile(src_row), ROW_TILE)],
                                     xs_hbm.at[pl.ds(_row_tile(dst_row), ROW_TILE)], sem)

    def issue(b, c):
        for u in range(DMA_UNROLL):
            r = b * DMA_UNROLL + u
            row_copy(lax.shift_right_logical(r, 1), idx[r]).start(priority=u % 2)
        return c

    lax.fori_loop(0, RB // DMA_UNROLL, issue, 0)

    def drain(r, c):
        row_copy(0, 0).wait()
        return c

    lax.fori_loop(0, RB, drain, 0, unroll=DMA_UNROLL)


def _dispatch(h, dest, n_rows):
    n = h.shape[0]
    steps = n // TM
    return pl.pallas_call(
        _dispatch_kernel,
        grid=(steps,),
        in_specs=[pl.BlockSpec(memory_space=pl.ANY), pl.BlockSpec((TM, D_MODEL), lambda i: (i, 0)),
                  pl.BlockSpec(memory_space=pl.ANY)],
        out_specs=pl.BlockSpec(memory_space=pl.ANY),
        out_shape=jax.ShapeDtypeStruct((n_rows * ROW_TILE, LANES), F32),
        scratch_shapes=[pltpu.SMEM((RB,), jnp.int32), pltpu.VMEM((TM * ROW_TILE, LANES), F32),
                        pltpu.SemaphoreType.DMA, pltpu.SemaphoreType.DMA],
        input_output_aliases={2: 0},
        compiler_params=_cparams(("arbitrary",)),
    )(dest.reshape(steps, RB), h, jnp.zeros((n_rows * ROW_TILE, LANES), F32))


def _expert_kernel(be_ref, nu_ref, x_ref, wg_ref, wu_ref, wd_ref, o_ref, xbf, act):
    del be_ref
    used = pl.program_id(0) < nu_ref[0]

    @pl.when(used)
    def _():
        for j in range(ROW_TILE):
            xbf[:, j * LANES:(j + 1) * LANES] = x_ref[pl.ds(j, TME, stride=ROW_TILE), :].astype(BF16)
        x = xbf[...]
        for c in range(D_FF_EXPERT // FF_CHUNK):
            sl = slice(c * FF_CHUNK, (c + 1) * FF_CHUNK)
            g = jnp.dot(x, wg_ref[:, sl], preferred_element_type=F32)
            u = jnp.dot(x, wu_ref[:, sl], preferred_element_type=F32)
            act[:, sl] = (_silu(g) * u).astype(BF16)
        out = jnp.dot(act[...], wd_ref[...], preferred_element_type=F32)
        for j in range(ROW_TILE):
            o_ref[pl.ds(j, TME, stride=ROW_TILE), :] = out[:, j * LANES:(j + 1) * LANES]

    @pl.when(jnp.logical_not(used))
    def _():
        o_ref[...] = jnp.zeros_like(o_ref)


def _experts(xs, block_e, n_used, wg, wu, wd):
    n_blocks = xs.shape[0] // (TME * ROW_TILE)
    rows = pl.BlockSpec((TME * ROW_TILE, LANES), lambda i, be, nu: (i, 0))
    resident = lambda shape: pl.BlockSpec((None,) + shape, lambda i, be, nu: (be[i], 0, 0),
                                          pipeline_mode=pl.Buffered(1))
    grid_spec = pltpu.PrefetchScalarGridSpec(
        num_scalar_prefetch=2, grid=(n_blocks,),
        in_specs=[rows, resident((D_MODEL, D_FF_EXPERT)), resident((D_MODEL, D_FF_EXPERT)),
                  resident((D_FF_EXPERT, D_MODEL))],
        out_specs=rows,
        scratch_shapes=[pltpu.VMEM((TME, D_MODEL), BF16), pltpu.VMEM((TME, D_FF_EXPERT), BF16)])
    return pl.pallas_call(
        _expert_kernel, grid_spec=grid_spec,
        out_shape=jax.ShapeDtypeStruct(xs.shape, F32),
        compiler_params=_cparams(("arbitrary",)),
    )(block_e, n_used, xs, wg, wu, wd)


def _combine_kernel(dest_hbm, eo_hbm, x_ref, route_ref, nw_ref, xo_ref, ho_ref, idx, gbuf, sem_i, sem):
    cp = pltpu.make_async_copy(dest_hbm.at[pl.program_id(0)], idx, sem_i)
    cp.start()
    cp.wait()

    def row_copy(src_row, k, t):
        return pltpu.make_async_copy(eo_hbm.at[pl.ds(_row_tile(src_row), ROW_TILE)],
                                     gbuf.at[k, pl.ds(_row_tile(t), ROW_TILE)], sem)

    def issue(b, c):
        for u in range(DMA_UNROLL):
            r = b * DMA_UNROLL + u
            row_copy(idx[r], u % 2, b * (DMA_UNROLL // 2) + u // 2).start(priority=u % 2)
        return c

    lax.fori_loop(0, 2 * TC // DMA_UNROLL, issue, 0)

    def drain(r, c):
        row_copy(0, 0, 0).wait()
        return c

    lax.fori_loop(0, 2 * TC, drain, 0, unroll=DMA_UNROLL)
    w1 = route_ref[:, 2:3]
    w2 = route_ref[:, 3:4]
    for j in range(ROW_TILE):
        sl = slice(j * LANES, (j + 1) * LANES)
        xo_ref[:, sl] = x_ref[:, sl] + (gbuf[0, pl.ds(j, TC, stride=ROW_TILE), :] * w1
                                        + gbuf[1, pl.ds(j, TC, stride=ROW_TILE), :] * w2)
    ho_ref[...] = _rms_rows(xo_ref[...], nw_ref[...]).astype(BF16)


def _combine(eo, dest, x, route, nw_next):
    n = x.shape[0]
    steps = n // TC
    row = lambda w: pl.BlockSpec((TC, w), lambda i: (i, 0))
    return pl.pallas_call(
        _combine_kernel,
        grid=(steps,),
        in_specs=[pl.BlockSpec(memory_space=pl.ANY), pl.BlockSpec(memory_space=pl.ANY),
                  row(D_MODEL), row(LANES), _const_spec((1, D_MODEL))],
        out_specs=[row(D_MODEL), row(D_MODEL)],
        out_shape=[jax.ShapeDtypeStruct((n, D_MODEL), F32), jax.ShapeDtypeStruct((n, D_MODEL), BF16)],
        scratch_shapes=[pltpu.SMEM((2 * TC,), jnp.int32), pltpu.VMEM((2, TC * ROW_TILE, LANES), F32),
                        pltpu.SemaphoreType.DMA, pltpu.SemaphoreType.DMA],
        compiler_params=_cparams(("arbitrary",)),
    )(dest.reshape(steps, 2 * TC), eo, x, route, nw_next)


def _trunk(lay, x, p):
    depth = p['norm1_w'].shape[0]
    row = lambda v: v.reshape(1, -1).astype(F32)
    h = _norm(x, p['norm1_w'][0])
    for l in range(depth):
        w_in = p['w_in'][l]
        wqk = w_in[:, 0:2 * ATT_W].astype(BF16)
        wvzx = w_in[:, 2 * ATT_W:3 * ATT_W + D_SSM + CONV_CH].astype(BF16)
        wdt = jnp.pad(w_in[:, 3 * ATT_W + D_SSM + CONV_CH:], ((0, 0), (0, LANES - 2 * N_HEADS))).astype(BF16)
        nw_qk = jnp.concatenate([jnp.tile(p['q_norm_w'][l].astype(F32), N_HEADS) * (HEAD_DIM ** -0.5),
                                 jnp.tile(p['k_norm_w'][l].astype(F32), N_HEADS)]).reshape(1, -1)
        qk, v, z, xbc, dt_raw = _inproj(h, wqk, wvzx, wdt, nw_qk)

        attn = _attention(lay, qk, v, p['rpb'][l])

        cw = jnp.concatenate([p['conv_w'][l].astype(F32).T, p['conv_b'][l].astype(F32)[None],
                              jnp.zeros((8 - CONV_K - 1, CONV_CH), F32)], axis=0)
        pad = jnp.zeros((LANES - 2 * N_HEADS,), F32)
        dtb = jnp.concatenate([p['dt_bias'][l].astype(F32).reshape(-1), pad]).reshape(1, -1)
        arow = jnp.concatenate([-jnp.exp(p['a_log'][l].astype(F32)).reshape(-1), pad]).reshape(1, -1)
        y_f = _ssd_pass(lay, False, xbc, dt_raw, cw, dtb, arow)
        dsk = jnp.repeat(p['d_skip'][l].astype(F32), HEAD_DIM).reshape(1, -1)
        y = _ssd_pass(lay, True, xbc, dt_raw, cw, dtb, arow,
                      extra=(y_f, z, dsk, row(p['ssm_norm_w'][l])))

        w_out = p['w_out'][l].astype(BF16)
        nw_next = row(p['norm1_w'][l + 1]) if l + 1 < depth else jnp.ones((1, D_MODEL), F32)
        j = l // 2
        if l % 2 == 0:
            x, h2 = _outproj(attn, y, x, w_out, row(p['attn_out_norm_w'][l]), row(p['norm2_w'][l]))
            x, h = _ffn(h2, x, p['ffn_w_gate'][j].astype(BF16), p['ffn_w_up'][j].astype(BF16),
                        p['ffn_w_down'][j].astype(BF16), nw_next)
        else:
            wr = jnp.pad(p['moe_router'][j].astype(F32), ((0, 0), (0, LANES - N_EXPERTS)))
            x, h2, route = _outproj(attn, y, x, w_out, row(p['attn_out_norm_w'][l]), row(p['norm2_w'][l]),
                                    router=_split_bf16(wr))
            dest, block_e, n_used, n_rows = _moe_plan(route)
            xs = _dispatch(h2, dest, n_rows)
            eo = _experts(xs, block_e, n_used, p['moe_w_gate'][j].astype(BF16),
                          p['moe_w_up'][j].astype(BF16), p['moe_w_down'][j].astype(BF16))
            x, h = _combine(eo, dest, x, route, nw_next)
    return x


def kernel(x_prompt, x_sample, meta_tokens, norm1_w, w_in, q_norm_w, k_norm_w, rpb, attn_out_norm_w,
           conv_w, conv_b, dt_bias, a_log, d_skip, ssm_norm_w, w_out, norm2_w,
           ffn_w_gate, ffn_w_up, ffn_w_down, moe_router, moe_w_gate, moe_w_up, moe_w_down):
    p = dict(norm1_w=norm1_w, w_in=w_in, q_norm_w=q_norm_w, k_norm_w=k_norm_w, rpb=rpb,
             attn_out_norm_w=attn_out_norm_w, conv_w=conv_w, conv_b=conv_b, dt_bias=dt_bias, a_log=a_log,
             d_skip=d_skip, ssm_norm_w=ssm_norm_w, w_out=w_out, norm2_w=norm2_w, ffn_w_gate=ffn_w_gate,
             ffn_w_up=ffn_w_up, ffn_w_down=ffn_w_down, moe_router=moe_router, moe_w_gate=moe_w_gate,
             moe_w_up=moe_w_up, moe_w_down=moe_w_down)
    groups = (x_prompt, x_sample)
    lay = _Layout([g.shape[1] for g in groups for _ in range(g.shape[0])])
    meta_block = jnp.concatenate([jnp.zeros((META_OFF, D_MODEL), F32), meta_tokens.astype(F32)], axis=0)
    tail = lay.n_tok - lay.n_grid - CHUNK * lay.n_seq
    x = jnp.concatenate([g.reshape(-1, D_MODEL).astype(F32) for g in groups]
                        + [jnp.tile(meta_block, (lay.n_seq, 1)), jnp.zeros((tail, D_MODEL), F32)], axis=0)
    x = _trunk(lay, x, p)
    outs, off = [], 0
    for g in groups:
        cnt = g.shape[0] * g.shape[1]
        outs.append(x[off:off + cnt].reshape(g.shape).astype(g.dtype))
        off += cnt
    return tuple(outs)
```

```python
import functools

import numpy as np
import jax
import jax.numpy as jnp
from jax import lax
from jax.experimental import pallas as pl
from jax.experimental.pallas import tpu as pltpu

F32 = jnp.float32
BF16 = jnp.bfloat16

D_MODEL = 1024
N_META = 16
GRID_W = 64
NA_ROWS = 8
NA_COLS = 16
N_HEADS = 16
HEAD_DIM = 64
N_PAIRS = N_HEADS // 2
ATT_W = N_HEADS * HEAD_DIM
D_SSM = 1024
SSM_GROUPS = 2
SSM_STATE = 128
GROUP_W = D_SSM // SSM_GROUPS
CONV_K = 5
CONV_CH = D_SSM + 2 * SSM_GROUPS * SSM_STATE
D_FF = 2816
N_EXPERTS = 8
D_FF_EXPERT = 3584
RMS_EPS = 1e-6
NEG_INF = -1e30

LANES = 128
ROW_TILE = D_MODEL // LANES
CHUNK = 128
META_OFF = CHUNK - N_META
TM = 512
BAND_ROWS = 8
BAND = BAND_ROWS * GRID_W
HALO_ROWS = NA_ROWS // 2
BUF_ROWS = BAND_ROWS + NA_ROWS - 1
FF_CHUNK = 256
TME = 512
TF = 512
RB = 2 * TM
TC = 256
DMA_UNROLL = 8
VMEM_LIMIT = 56 * 1024 * 1024


def _cparams(sem):
    return pltpu.CompilerParams(dimension_semantics=sem, vmem_limit_bytes=VMEM_LIMIT)


def _round_up(a, b):
    return (a + b - 1) // b * b


def _const_spec(shape):
    nd = len(shape)
    return pl.BlockSpec(shape, lambda *_: (0,) * nd)


class _Layout:
    def __init__(self, seq_lens):
        self.seq_lens = tuple(seq_lens)
        self.n_seq = len(seq_lens)
        self.n_grid = sum(seq_lens)
        self.starts = np.concatenate([[0], np.cumsum(seq_lens)[:-1]]).astype(np.int64)
        self.n_tok = _round_up(self.n_grid + CHUNK * self.n_seq, TM)
        assert all(s % BAND == 0 and s // GRID_W >= NA_ROWS for s in seq_lens)

    def meta_block(self, s):
        return self.n_grid // CHUNK + s

    def attn_tables(self):
        prev, nxt, flags, mblk = [], [], [], []
        for s, (st, ln) in enumerate(zip(self.starts, self.seq_lens)):
            lo, hi = st // BAND, (st + ln) // BAND
            for b in range(lo, hi):
                per = BAND_ROWS // HALO_ROWS
                prev.append(max(b * per - 1, lo * per))
                nxt.append(min((b + 1) * per, hi * per - 1))
                flags.append((1 if b == lo else 0) | (2 if b == hi - 1 else 0))
                mblk.append((self.n_grid + CHUNK * s + META_OFF) // N_META)
        return [np.asarray(a, np.int32) for a in (prev, nxt, flags, mblk)]

    def ssd_tables(self, reverse):
        blk, prev, nxt, flags = [], [], [], []
        for s, (st, ln) in enumerate(zip(self.starts, self.seq_lens)):
            nc = ln // CHUNK
            b0 = st // CHUNK
            meta16 = (self.n_grid + CHUNK * s + META_OFF) // N_META
            steps = []
            steps.append((self.meta_block(s), 0, b0 * 8, 2 | 4))
            for c in range(nc):
                b = b0 + c
                p16 = meta16 if c == 0 else b * 8 - 1
                n16 = (b + 1) * 8 if c < nc - 1 else 0
                steps.append((b, p16, n16, 1 | (2 if c < nc - 1 else 0)))
            if reverse:
                steps = steps[::-1]
            for i, (b, p, n, f) in enumerate(steps):
                blk.append(b); prev.append(p); nxt.append(n)
                flags.append(f | (8 if i == 0 else 0))
        for b in range(self.n_grid // CHUNK + self.n_seq, self.n_tok // CHUNK):
            blk.append(b); prev.append(0); nxt.append(0)
            flags.append(8 | 16)
        return [np.asarray(a, np.int32) for a in (blk, prev, nxt, flags)]


def _rms_rows(x, w):
    ms = jnp.mean(x * x, axis=-1, keepdims=True)
    return x * lax.rsqrt(ms + RMS_EPS) * w


def _norm_kernel(x_ref, w_ref, o_ref):
    o_ref[...] = _rms_rows(x_ref[...], w_ref[...]).astype(BF16)


def _norm(x, w):
    n = x.shape[0]
    return pl.pallas_call(
        _norm_kernel,
        grid=(n // TM,),
        in_specs=[pl.BlockSpec((TM, D_MODEL), lambda i: (i, 0)), _const_spec((1, D_MODEL))],
        out_specs=pl.BlockSpec((TM, D_MODEL), lambda i: (i, 0)),
        out_shape=jax.ShapeDtypeStruct((n, D_MODEL), BF16),
        compiler_params=_cparams(("parallel",)),
    )(x, w.reshape(1, D_MODEL))


def _inproj_kernel(h_ref, wqk_ref, wvzx_ref, wdt_ref, g_ref, nw_ref,
                   qk_ref, v_ref, z_ref, xbc_ref, dt_ref):
    h = h_ref[...]
    g = g_ref[...]
    for c in range(2 * ATT_W // FF_CHUNK):
        sl = slice(c * FF_CHUNK, (c + 1) * FF_CHUNK)
        y = jnp.dot(h, wqk_ref[:, sl], preferred_element_type=F32)
        ss = jnp.dot((y * y).astype(BF16), g, preferred_element_type=F32)
        inv = lax.rsqrt(ss * (1.0 / HEAD_DIM) + RMS_EPS)
        qk_ref[:, sl] = (y * inv * nw_ref[:, sl]).astype(BF16)
    for c in range(ATT_W // FF_CHUNK):
        sl = slice(c * FF_CHUNK, (c + 1) * FF_CHUNK)
        v_ref[:, sl] = jnp.dot(h, wvzx_ref[:, sl], preferred_element_type=F32).astype(BF16)
    for c in range(D_SSM // FF_CHUNK):
        sl = slice(c * FF_CHUNK, (c + 1) * FF_CHUNK)
        src = slice(ATT_W + c * FF_CHUNK, ATT_W + (c + 1) * FF_CHUNK)
        z_ref[:, sl] = jnp.dot(h, wvzx_ref[:, src], preferred_element_type=F32).astype(BF16)
    for c in range(CONV_CH // FF_CHUNK):
        sl = slice(c * FF_CHUNK, (c + 1) * FF_CHUNK)
        src = slice(ATT_W + D_SSM + c * FF_CHUNK, ATT_W + D_SSM + (c + 1) * FF_CHUNK)
        xbc_ref[:, sl] = jnp.dot(h, wvzx_ref[:, src], preferred_element_type=F32).astype(BF16)
    dt_ref[...] = jnp.dot(h, wdt_ref[...], preferred_element_type=F32)


def _head_sum_matrix():
    i = np.arange(FF_CHUNK)
    return jnp.asarray((i[:, None] // HEAD_DIM == i[None, :] // HEAD_DIM).astype(np.float32), BF16)


def _inproj(h, wqk, wvzx, wdt, nw):
    n = h.shape[0]
    row = lambda w: pl.BlockSpec((TM, w), lambda i: (i, 0))
    return pl.pallas_call(
        _inproj_kernel,
        grid=(n // TM,),
        in_specs=[row(D_MODEL), _const_spec(wqk.shape), _const_spec(wvzx.shape), _const_spec(wdt.shape),
                  _const_spec((FF_CHUNK, FF_CHUNK)), _const_spec((1, 2 * ATT_W))],
        out_specs=[row(2 * ATT_W), row(ATT_W), row(D_SSM), row(CONV_CH), row(LANES)],
        out_shape=[jax.ShapeDtypeStruct((n, 2 * ATT_W), BF16), jax.ShapeDtypeStruct((n, ATT_W), BF16),
                   jax.ShapeDtypeStruct((n, D_SSM), BF16), jax.ShapeDtypeStruct((n, CONV_CH), BF16),
                   jax.ShapeDtypeStruct((n, LANES), F32)],
        compiler_params=_cparams(("parallel",)),
    )(h, wqk, wvzx, wdt, _head_sum_matrix(), nw)


def _split_bf16(x):
    hi = x.astype(BF16)
    lo = (x - hi.astype(F32)).astype(BF16)
    return hi, lo


def _outproj_kernel(with_router, attn_ref, y_ref, x_ref, w_ref, aw_ref, nw_ref, *rest):
    if with_router:
        wrh_ref, wrl_ref, xo_ref, h_ref, route_ref = rest
    else:
        xo_ref, h_ref = rest
    a = _rms_rows(attn_ref[...].astype(F32), aw_ref[...]).astype(BF16)
    acc = x_ref[...]
    acc = acc + jnp.dot(a, w_ref[0:ATT_W, :], preferred_element_type=F32)
    acc = acc + jnp.dot(y_ref[...], w_ref[ATT_W:ATT_W + D_SSM, :], preferred_element_type=F32)
    xo_ref[...] = acc
    h2 = _rms_rows(acc, nw_ref[...])
    h_ref[...] = h2.astype(BF16)
    if with_router:
        hh, hl = _split_bf16(h2)
        wh = wrh_ref[...]
        logits = (jnp.dot(hh, wh, preferred_element_type=F32)
                  + jnp.dot(hl, wh, preferred_element_type=F32)
                  + jnp.dot(hh, wrl_ref[...], preferred_element_type=F32))
        lane = lax.broadcasted_iota(jnp.int32, logits.shape, 1)
        logits = jnp.where(lane < N_EXPERTS, logits, NEG_INF)
        m1 = jnp.max(logits, axis=-1, keepdims=True)
        i1 = jnp.min(jnp.where(logits == m1, lane, LANES), axis=-1, keepdims=True)
        rest_l = jnp.where(lane == i1, NEG_INF, logits)
        m2 = jnp.max(rest_l, axis=-1, keepdims=True)
        i2 = jnp.min(jnp.where(rest_l == m2, lane, LANES), axis=-1, keepdims=True)
        e = jnp.exp(m2 - m1)
        w1 = 1.0 / (1.0 + e)
        w2 = e * w1
        route = jnp.where(lane == 0, i1.astype(F32),
                          jnp.where(lane == 1, i2.astype(F32),
                                    jnp.where(lane == 2, w1, jnp.where(lane == 3, w2, 0.0))))
        route_ref[...] = route


def _outproj(attn, y, x, w_out, aw, nw, router=None):
    n = x.shape[0]
    row = lambda w: pl.BlockSpec((TM, w), lambda i: (i, 0))
    in_specs = [row(ATT_W), row(D_SSM), row(D_MODEL), _const_spec(w_out.shape),
                _const_spec((1, ATT_W)), _const_spec((1, D_MODEL))]
    out_specs = [row(D_MODEL), row(D_MODEL)]
    out_shape = [jax.ShapeDtypeStruct((n, D_MODEL), F32), jax.ShapeDtypeStruct((n, D_MODEL), BF16)]
    args = [attn, y, x, w_out, aw, nw]
    if router is not None:
        in_specs += [_const_spec((D_MODEL, LANES))] * 2
        out_specs.append(row(LANES))
        out_shape.append(jax.ShapeDtypeStruct((n, LANES), F32))
        args += list(router)
    return pl.pallas_call(
        functools.partial(_outproj_kernel, router is not None),
        grid=(n // TM,), in_specs=in_specs, out_specs=out_specs, out_shape=out_shape,
        compiler_params=_cparams(("parallel",)),
    )(*args)


def _silu(x):
    return x * (1.0 / (1.0 + jnp.exp(-x)))


def _ffn_kernel(h_ref, x_ref, wg_ref, wu_ref, wd_ref, nw_ref, xo_ref, ho_ref, act_ref):
    h = h_ref[...]
    for c in range(D_FF // FF_CHUNK):
        sl = slice(c * FF_CHUNK, (c + 1) * FF_CHUNK)
        g = jnp.dot(h, wg_ref[:, sl], preferred_element_type=F32)
        u = jnp.dot(h, wu_ref[:, sl], preferred_element_type=F32)
        act_ref[:, sl] = (_silu(g) * u).astype(BF16)
    xn = x_ref[...] + jnp.dot(act_ref[...], wd_ref[...], preferred_element_type=F32)
    xo_ref[...] = xn
    ho_ref[...] = _rms_rows(xn, nw_ref[...]).astype(BF16)


def _ffn(h, x, wg, wu, wd, nw_next):
    n = x.shape[0]
    row = lambda w: pl.BlockSpec((TM, w), lambda i: (i, 0))
    return pl.pallas_call(
        _ffn_kernel,
        grid=(n // TM,),
        in_specs=[row(D_MODEL), row(D_MODEL), _const_spec(wg.shape), _const_spec(wu.shape),
                  _const_spec(wd.shape), _const_spec((1, D_MODEL))],
        out_specs=[row(D_MODEL), row(D_MODEL)],
        out_shape=[jax.ShapeDtypeStruct((n, D_MODEL), F32), jax.ShapeDtypeStruct((n, D_MODEL), BF16)],
        scratch_shapes=[pltpu.VMEM((TM, D_FF), BF16)],
        compiler_params=_cparams(("parallel",)),
    )(h, x, wg, wu, wd, nw_next)


N_CB = GRID_W // NA_COLS
SPAN_START = (0, 0, 16, 32)
SPAN_W = (32, 48, 48, 32)
META_SLOT = (24, 0, 0, 0)
META_PER_ROW = N_META // 2
KTILE = 16
ATTN_DEPTH = 16


def _attn_bias_tables(rpb):
    i = np.arange(NA_ROWS)
    dr = np.clip(i[None, :] - i[:, None] + NA_ROWS - 1, 0, 2 * NA_ROWS - 2)
    tables = {}
    for j in range(N_CB):
        span = SPAN_W[j]
        qc = NA_COLS * j + np.arange(NA_COLS)
        kc = SPAN_START[j] + np.arange(span)
        st = np.clip(qc - NA_COLS // 2, 0, GRID_W - NA_COLS)
        valid = (kc[None, :] >= st[:, None]) & (kc[None, :] < st[:, None] + NA_COLS)
        dc = np.clip(kc[None, :] - qc[:, None] + NA_COLS - 1, 0, 2 * NA_COLS - 2)
        slot = (np.arange(span) >= META_SLOT[j]) & (np.arange(span) < META_SLOT[j] + META_PER_ROW)
        assert not valid[:, slot].any()
        b = rpb.astype(F32)[:, dr][:, :, :, dc]
        b = jnp.where(valid[None, None, None], b, NEG_INF)
        meta_vis = np.where(i[:, None] < 2, 0.0, NEG_INF) * np.ones((1, span))
        b = jnp.where(slot[None, None, None, None, :], jnp.asarray(meta_vis, F32)[None, None, :, None, :], b)
        b = b.reshape(N_PAIRS, 2, NA_ROWS, NA_ROWS, NA_COLS, span)
        b = jnp.transpose(b, (0, 2, 1, 4, 3, 5)).reshape(N_PAIRS, NA_ROWS, 2 * NA_COLS, NA_ROWS * span)
        tables.setdefault(span, []).append(b)
    return [jnp.stack(t, axis=1) for _, t in sorted(tables.items())]


def _pair_queries(q):
    lo = lax.broadcasted_iota(jnp.int32, q.shape, 1) < HEAD_DIM
    zero = jnp.zeros_like(q)
    return jnp.concatenate([jnp.where(lo, q, zero), jnp.where(lo, zero, q)], axis=0)


def _nt_dot(a, b):
    return lax.dot_general(a, b, (((1,), (1,)), ((), ())), preferred_element_type=F32)


def _attn_kernel(prev_ref, next_ref, flag_ref, mblk_ref,
                 q_ref, kp_ref, kc_ref, kn_ref, vp_ref, vc_ref, vn_ref, km_ref, vm_ref,
                 b32_ref, b48_ref, o_ref, *spans):
    del prev_ref, next_ref, mblk_ref
    flags = flag_ref[pl.program_id(1)]
    is_first = (flags & 1) != 0
    is_last = (flags & 2) != 0
    low = lax.broadcasted_iota(jnp.int32, (KTILE, LANES), 0) < META_PER_ROW

    for refs, bufs, m_ref in (((kp_ref, kc_ref, kn_ref), spans[:N_CB], km_ref),
                              ((vp_ref, vc_ref, vn_ref), spans[N_CB:], vm_ref)):
        m = m_ref[...]
        mf = m.astype(F32)
        m_swapped = jnp.concatenate([mf[META_PER_ROW:], mf[:META_PER_ROW]], axis=0).astype(BF16)
        for rr in range(BUF_ROWS):
            if rr < HALO_ROWS:
                ref, row = refs[0], rr
            elif rr < HALO_ROWS + BAND_ROWS:
                ref, row = refs[1], rr - HALO_ROWS
            else:
                ref, row = refs[2], rr - HALO_ROWS - BAND_ROWS
            for j in range(N_CB):
                span = SPAN_W[j]
                for t in range(span // KTILE):
                    off = row * GRID_W + SPAN_START[j] + t * KTILE
                    tile = ref[off:off + KTILE, :]
                    if t == META_SLOT[j] // KTILE:
                        if META_SLOT[j] % KTILE == 0:
                            tile = jnp.where(low, m if rr % 2 == 0 else m_swapped, tile)
                        else:
                            tile = jnp.where(low, tile, m_swapped if rr % 2 == 0 else m)
                    bufs[j][rr * span + t * KTILE:rr * span + (t + 1) * KTILE, :] = tile

    lane_lo = lax.broadcasted_iota(jnp.int32, (NA_COLS, LANES), 1) < HEAD_DIM
    starts = []
    for ri in range(BAND_ROWS):
        ls = jnp.int32(ri)
        ls = jnp.where(is_first, jnp.maximum(ls, HALO_ROWS), ls)
        ls = jnp.where(is_last, jnp.minimum(ls, HALO_ROWS), ls)
        starts.append((ls, HALO_ROWS + ri - ls))

    def scores(u):
        ri, j = divmod(u, N_CB)
        ls, delta = starts[ri]
        span = SPAN_W[j]
        kwin = spans[j][pl.ds(pl.multiple_of(ls * span, KTILE), NA_ROWS * span), :]
        q0 = ri * GRID_W + j * NA_COLS
        bias_ref = b32_ref if span == SPAN_W[0] else b48_ref
        return _nt_dot(_pair_queries(q_ref[q0:q0 + NA_COLS, :]), kwin) + bias_ref[SPAN_W[:j].count(span), delta]

    def finish(u, s):
        ri, j = divmod(u, N_CB)
        span = SPAN_W[j]
        vwin = spans[N_CB + j][pl.ds(pl.multiple_of(starts[ri][0] * span, KTILE), NA_ROWS * span), :]
        p = jnp.exp(s - jnp.max(s, axis=-1, keepdims=True))
        l = jnp.sum(p, axis=-1, keepdims=True)
        o = jnp.dot(p.astype(BF16), vwin, preferred_element_type=F32) / l
        q0 = ri * GRID_W + j * NA_COLS
        o_ref[q0:q0 + NA_COLS, :] = jnp.where(lane_lo, o[:NA_COLS], o[NA_COLS:]).astype(BF16)

    n_units = BAND_ROWS * N_CB
    pending = {}
    for t in range(n_units + ATTN_DEPTH):
        if t < n_units:
            pending[t] = scores(t)
        if t >= ATTN_DEPTH:
            finish(t - ATTN_DEPTH, pending.pop(t - ATTN_DEPTH))


def _attn_meta_kernel(n_seq, qk_ref, v_ref, mbias_ref, alias_ref, o_ref):
    del alias_ref
    s = pl.program_id(0)
    o_ref[...] = jnp.zeros_like(o_ref)

    @pl.when(s < n_seq)
    def _():
        mbias = mbias_ref[...]
        for p in range(N_PAIRS):
            c = slice(p * LANES, (p + 1) * LANES)
            q = qk_ref[META_OFF:CHUNK, c]
            k = qk_ref[:, ATT_W + p * LANES:ATT_W + (p + 1) * LANES]
            qs = _pair_queries(q)
            sc = _nt_dot(qs, k) + mbias
            m = jnp.max(sc, axis=-1, keepdims=True)
            e = jnp.exp(sc - m)
            l = jnp.sum(e, axis=-1, keepdims=True)
            o = jnp.dot(e.astype(BF16), v_ref[:, c], preferred_element_type=F32) / l
            lo = lax.broadcasted_iota(jnp.int32, (N_META, LANES), 1) < HEAD_DIM
            o_ref[META_OFF:CHUNK, c] = jnp.where(lo, o[0:N_META], o[N_META:]).astype(BF16)


def _attention(lay, qk, v, rpb):
    n = qk.shape[0]
    prev, nxt, flags, mblk = lay.attn_tables()
    n_bands = lay.n_grid // BAND
    b32, b48 = _attn_bias_tables(rpb)
    lane = np.arange(LANES)
    mbias_meta = jnp.asarray(np.where(lane >= META_OFF, 0.0, NEG_INF)[None, :], F32)
    kcol = ATT_W // LANES
    halo = HALO_ROWS * GRID_W
    bias_spec = lambda a: pl.BlockSpec((None,) + a.shape[1:], lambda p, b, *_: (p, 0, 0, 0, 0))
    grid_spec = pltpu.PrefetchScalarGridSpec(
        num_scalar_prefetch=4,
        grid=(N_PAIRS, n_bands),
        in_specs=[
            pl.BlockSpec((BAND, LANES), lambda p, b, *_: (b, p)),
            pl.BlockSpec((halo, LANES), lambda p, b, pv, nx, fl, mb: (pv[b], kcol + p)),
            pl.BlockSpec((BAND, LANES), lambda p, b, pv, nx, fl, mb: (b, kcol + p)),
            pl.BlockSpec((halo, LANES), lambda p, b, pv, nx, fl, mb: (nx[b], kcol + p)),
            pl.BlockSpec((halo, LANES), lambda p, b, pv, nx, fl, mb: (pv[b], p)),
            pl.BlockSpec((BAND, LANES), lambda p, b, pv, nx, fl, mb: (b, p)),
            pl.BlockSpec((halo, LANES), lambda p, b, pv, nx, fl, mb: (nx[b], p)),
            pl.BlockSpec((N_META, LANES), lambda p, b, pv, nx, fl, mb: (mb[b], kcol + p)),
            pl.BlockSpec((N_META, LANES), lambda p, b, pv, nx, fl, mb: (mb[b], p)),
            bias_spec(b32), bias_spec(b48),
        ],
        out_specs=pl.BlockSpec((BAND, LANES), lambda p, b, *_: (b, p)),
        scratch_shapes=[pltpu.VMEM((BUF_ROWS * w, LANES), BF16) for w in SPAN_W] * 2,
    )
    attn = pl.pallas_call(
        _attn_kernel, grid_spec=grid_spec,
        out_shape=jax.ShapeDtypeStruct((n, ATT_W), BF16),
        compiler_params=_cparams(("arbitrary", "arbitrary")),
    )(jnp.asarray(prev), jnp.asarray(nxt), jnp.asarray(flags), jnp.asarray(mblk),
      qk, qk, qk, qk, v, v, v, qk, v, b32, b48)
    mb0 = lay.n_grid // CHUNK
    n_tail = (n - lay.n_grid) // CHUNK
    return pl.pallas_call(
        functools.partial(_attn_meta_kernel, lay.n_seq),
        grid=(n_tail,),
        in_specs=[pl.BlockSpec((CHUNK, 2 * ATT_W), lambda s: (mb0 + s, 0)),
                  pl.BlockSpec((CHUNK, ATT_W), lambda s: (mb0 + s, 0)),
                  _const_spec((1, LANES)),
                  pl.BlockSpec(memory_space=pl.ANY)],
        out_specs=pl.BlockSpec((CHUNK, ATT_W), lambda s: (mb0 + s, 0)),
        out_shape=jax.ShapeDtypeStruct((n, ATT_W), BF16),
        input_output_aliases={3: 0},
        compiler_params=_cparams(("arbitrary",)),
    )(qk, v, mbias_meta, attn)


def _softplus(x):
    return jnp.maximum(x, 0.0) + jnp.log(1.0 + jnp.exp(-jnp.abs(x)))


def _split3_bf16(x):
    hi = x.astype(BF16)
    r = x - hi.astype(F32)
    mid = r.astype(BF16)
    lo = (r - mid.astype(F32)).astype(BF16)
    return hi, mid, lo


def _expand_heads(w, e):
    hi, lo = _split_bf16(w)
    return jnp.dot(hi, e, preferred_element_type=F32) + jnp.dot(lo, e, preferred_element_type=F32)


def _ssd_kernel(reverse, blk_ref, prev_ref, next_ref, flag_ref, *refs):
    if reverse:
        (u_ref, dt_ref, dtb_ref, arow_ref, e_ref, tri_ref, yf_ref, z_ref, dsk_ref, nw_ref,
         o_ref, state, ybuf) = refs
    else:
        (xbc_ref, xp_ref, xn_ref, shift_ref, cw_ref, dt_ref, dtb_ref, arow_ref, e_ref, tri_ref,
         o_ref, u_ref, xe, state) = refs
        ybuf = o_ref
    del blk_ref, prev_ref, next_ref
    flags = flag_ref[pl.program_id(0)]
    is_meta = (flags & 4) != 0

    @pl.when((flags & 8) != 0)
    def _():
        state[...] = jnp.zeros_like(state)

    if reverse:
        u = u_ref[...].astype(F32)
    else:
        zero = jnp.zeros((N_META, CONV_CH), BF16)
        xe[0:N_META, :] = jnp.where((flags & 1) != 0, xp_ref[...], zero)
        xe[N_META:N_META + CHUNK, :] = xbc_ref[...]
        xe[N_META + CHUNK:, :] = jnp.where((flags & 2) != 0, xn_ref[...], zero)
        shifted = jnp.dot(shift_ref[jnp.where(is_meta, 1, 0)], xe[...], preferred_element_type=F32)
        u = cw_ref[CONV_K:CONV_K + 1, :]
        for k in range(CONV_K):
            u = u + cw_ref[k:k + 1, :] * shifted[k * CHUNK:(k + 1) * CHUNK, :]
        u = _silu(u)
        u_ref[...] = u.astype(BF16)
    xs = u[:, 0:D_SSM]
    x_bf = xs.astype(BF16)

    rid = lax.broadcasted_iota(jnp.int32, (CHUNK, 1), 0)
    valid = rid >= jnp.where(is_meta, META_OFF, 0)
    dt = jnp.where(valid, _softplus(dt_ref[...] + dtb_ref[...]), 0.0)
    a = dt * arow_ref[...]
    tri = tri_ref[...]
    ah, am, al = _split3_bf16(a)
    acum = (jnp.dot(tri, ah, preferred_element_type=F32) + jnp.dot(tri, am, preferred_element_type=F32)
            + jnp.dot(tri, al, preferred_element_type=F32))
    acum_t = acum.T
    dt_t = dt.T
    edge = 0 if reverse else CHUNK - 1
    a_tot = acum[edge:edge + 1, :]
    e = e_ref[...]
    w1e = _expand_heads(dt * jnp.exp(a_tot - acum), e)
    w2e = _expand_heads(jnp.exp(acum), e)
    decay_row = w2e[edge:edge + 1, :]
    xw = (xs * w1e).astype(BF16)

    li = lax.broadcasted_iota(jnp.int32, (CHUNK, CHUNK), 0)
    si = lax.broadcasted_iota(jnp.int32, (CHUNK, CHUNK), 1)
    causal = (si >= li) if reverse else (li >= si)
    lane_lo = si < HEAD_DIM
    ho = N_HEADS if reverse else 0
    heads_per_group = N_HEADS // SSM_GROUPS
    for g in range(SSM_GROUPS):
        gsl = slice(g * GROUP_W, (g + 1) * GROUP_W)
        bg = u[:, D_SSM + g * SSM_STATE:D_SSM + (g + 1) * SSM_STATE]
        cg = u[:, D_SSM + (SSM_GROUPS + g) * SSM_STATE:D_SSM + (SSM_GROUPS + g + 1) * SSM_STATE].astype(BF16)
        cb = _nt_dot(cg, bg.astype(BF16))
        st_old = state[:, gsl]
        y_off = jnp.dot(cg, st_old.astype(BF16), preferred_element_type=F32) * w2e[:, gsl]
        st_new = jnp.dot(bg.T.astype(BF16), xw[:, gsl], preferred_element_type=F32)
        state[:, gsl] = st_old * decay_row[:, gsl] + st_new
        for j in range(heads_per_group // 2):
            col = g * GROUP_W + j * LANES
            acc = y_off[:, j * LANES:(j + 1) * LANES]
            xpair = x_bf[:, col:col + LANES]
            for hh in range(2):
                hi = ho + g * heads_per_group + 2 * j + hh
                seg = acum[:, hi:hi + 1] - acum_t[hi:hi + 1, :]
                lm = jnp.exp(jnp.where(causal, seg, NEG_INF))
                mh = (cb * lm * dt_t[hi:hi + 1, :]).astype(BF16)
                keep = lane_lo if hh == 0 else jnp.logical_not(lane_lo)
                acc = acc + jnp.dot(mh, jnp.where(keep, xpair, jnp.zeros_like(xpair)),
                                    preferred_element_type=F32)
            ybuf[:, col:col + LANES] = acc
    if reverse:
        for g in range(SSM_GROUPS):
            gsl = slice(g * GROUP_W, (g + 1) * GROUP_W)
            yg = ybuf[:, gsl] + yf_ref[:, gsl] + dsk_ref[:, gsl] * xs[:, gsl]
            yg = yg * _silu(z_ref[:, gsl].astype(F32))
            ms = jnp.mean(yg * yg, axis=-1, keepdims=True)
            o_ref[:, gsl] = (yg * lax.rsqrt(ms + RMS_EPS) * nw_ref[:, gsl]).astype(BF16)

    @pl.when((flags & 16) != 0)
    def _():
        o_ref[...] = jnp.zeros_like(o_ref)


def _ssd_constants(reverse):
    ho = N_HEADS if reverse else 0
    e = np.zeros((LANES, D_SSM), np.float32)
    for h in range(N_HEADS):
        e[ho + h, h * HEAD_DIM:(h + 1) * HEAD_DIM] = 1.0
    i = np.arange(CHUNK)
    tri = (i[None, :] >= i[:, None]) if reverse else (i[:, None] >= i[None, :])
    return jnp.asarray(e, BF16), jnp.asarray(tri.astype(np.float32), BF16)


def _shift_matrices():
    s = np.zeros((2, CONV_K * CHUNK, CHUNK + 2 * N_META), np.float32)
    i = np.arange(CHUNK)
    for k in range(CONV_K):
        s[:, k * CHUNK + i, N_META + i + k - CONV_K // 2] = 1.0
    s[1, :, :N_META + META_OFF] = 0.0
    return jnp.asarray(s, BF16)


def _ssd_pass(lay, reverse, dt, dtb, arow, fwd_in=(), rev_in=()):
    n = dt.shape[0]
    blk, prev, nxt, flags = lay.ssd_tables(reverse)
    e, tri = _ssd_constants(reverse)
    cur = lambda w: pl.BlockSpec((CHUNK, w), lambda i, bk, pv, nx, fl: (bk[i], 0))
    common = [cur(LANES), _const_spec((1, LANES)), _const_spec((1, LANES)),
              _const_spec((LANES, D_SSM)), _const_spec((CHUNK, CHUNK))]
    state = pltpu.VMEM((SSM_STATE, D_SSM), F32)
    if reverse:
        u, y_f, z, dsk, nw = rev_in
        args = (u, dt, dtb, arow, e, tri, y_f, z, dsk, nw)
        in_specs = [cur(CONV_CH)] + common + [cur(D_SSM), cur(D_SSM), _const_spec((1, D_SSM)),
                                              _const_spec((1, D_SSM))]
        out_specs = cur(D_SSM)
        out_shape = jax.ShapeDtypeStruct((n, D_SSM), BF16)
        scratch = [state, pltpu.VMEM((CHUNK, D_SSM), F32)]
    else:
        xbc, cw = fwd_in
        shift = _shift_matrices()
        args = (xbc, xbc, xbc, shift, cw, dt, dtb, arow, e, tri)
        in_specs = [cur(CONV_CH),
                    pl.BlockSpec((N_META, CONV_CH), lambda i, bk, pv, nx, fl: (pv[i], 0)),
                    pl.BlockSpec((N_META, CONV_CH), lambda i, bk, pv, nx, fl: (nx[i], 0)),
                    _const_spec(shift.shape), _const_spec((8, CONV_CH))] + common
        out_specs = [cur(D_SSM), cur(CONV_CH)]
        out_shape = [jax.ShapeDtypeStruct((n, D_SSM), F32), jax.ShapeDtypeStruct((n, CONV_CH), BF16)]
        scratch = [pltpu.VMEM((CHUNK + 2 * N_META, CONV_CH), BF16), state]
    grid_spec = pltpu.PrefetchScalarGridSpec(
        num_scalar_prefetch=4, grid=(len(blk),), in_specs=in_specs,
        out_specs=out_specs, scratch_shapes=scratch)
    return pl.pallas_call(
        functools.partial(_ssd_kernel, reverse), grid_spec=grid_spec, out_shape=out_shape,
        compiler_params=_cparams(("arbitrary",)),
    )(jnp.asarray(blk), jnp.asarray(prev), jnp.asarray(nxt), jnp.asarray(flags), *args)


def _moe_plan(route):
    n = route.shape[0]
    flat_e = route[:, 0:2].astype(jnp.int32).reshape(-1)
    onehot = (flat_e[:, None] == jnp.arange(N_EXPERTS, dtype=jnp.int32)[None, :]).astype(jnp.int32)
    csum = jnp.cumsum(onehot, axis=0)
    counts = csum[-1]
    rank = jnp.sum(csum * onehot, axis=1) - 1
    padded = (counts + TME - 1) // TME * TME
    pends = jnp.cumsum(padded)
    pstarts = pends - padded
    dest = jnp.sum(onehot * pstarts[None, :], axis=1) + rank
    n_rows = _round_up(2 * n + N_EXPERTS * TME, TME)
    n_blocks = n_rows // TME
    block_e = jnp.sum((jnp.arange(n_blocks, dtype=jnp.int32)[:, None] * TME >= pends[None, :]).astype(jnp.int32),
                      axis=1)
    block_e = jnp.minimum(block_e, N_EXPERTS - 1)
    n_used = (pends[-1] // TME).astype(jnp.int32).reshape(1)
    return dest.astype(jnp.int32), block_e, n_used, n_rows


def _row_tile(r):
    return pl.multiple_of(r * ROW_TILE, ROW_TILE)


def _dispatch_kernel(dest_hbm, h_ref, zero_hbm, xs_hbm, idx, stage, sem_i, sem):
    del zero_hbm
    cp = pltpu.make_async_copy(dest_hbm.at[pl.program_id(0)], idx, sem_i)
    cp.start()
    hf = h_ref[...].astype(F32)
    for j in range(ROW_TILE):
        stage[pl.ds(j, TM, stride=ROW_TILE), :] = hf[:, j * LANES:(j + 1) * LANES]
    cp.wait()

    def row_copy(src_row, dst_row):
        return pltpu.make_async_copy(stage.at[pl.ds(_row_tile(src_row), ROW_TILE)],
                                     xs_hbm.at[pl.ds(_row_tile(dst_row), ROW_TILE)], sem)

    def issue(b, c):
        for u in range(DMA_UNROLL):
            r = b * DMA_UNROLL + u
            row_copy(lax.shift_right_logical(r, 1), idx[r]).start(priority=u % 2)
        return c

    lax.fori_loop(0, RB // DMA_UNROLL, issue, 0)

    def drain(r, c):
        row_copy(0, 0).wait()
        return c

    lax.fori_loop(0, RB, drain, 0, unroll=DMA_UNROLL)


def _dispatch(h, dest, n_rows):
    n = h.shape[0]
    steps = n // TM
    return pl.pallas_call(
        _dispatch_kernel,
        grid=(steps,),
        in_specs=[pl.BlockSpec(memory_space=pl.ANY), pl.BlockSpec((TM, D_MODEL), lambda i: (i, 0)),
                  pl.BlockSpec(memory_space=pl.ANY)],
        out_specs=pl.BlockSpec(memory_space=pl.ANY),
        out_shape=jax.ShapeDtypeStruct((n_rows * ROW_TILE, LANES), F32),
        scratch_shapes=[pltpu.SMEM((RB,), jnp.int32), pltpu.VMEM((TM * ROW_TILE, LANES), F32),
                        pltpu.SemaphoreType.DMA, pltpu.SemaphoreType.DMA],
        input_output_aliases={2: 0},
        compiler_params=_cparams(("arbitrary",)),
    )(dest.reshape(steps, RB), h, jnp.zeros((n_rows * ROW_TILE, LANES), F32))


def _expert_kernel(be_ref, nu_ref, x_ref, wg_ref, wu_ref, wd_ref, o_ref, xbf, act):
    del be_ref
    used = pl.program_id(0) < nu_ref[0]

    @pl.when(used)
    def _():
        for j in range(ROW_TILE):
            xbf[:, j * LANES:(j + 1) * LANES] = x_ref[pl.ds(j, TME, stride=ROW_TILE), :].astype(BF16)
        x = xbf[...]
        for c in range(D_FF_EXPERT // FF_CHUNK):
            sl = slice(c * FF_CHUNK, (c + 1) * FF_CHUNK)
            g = jnp.dot(x, wg_ref[:, sl], preferred_element_type=F32)
            u = jnp.dot(x, wu_ref[:, sl], preferred_element_type=F32)
            act[:, sl] = (_silu(g) * u).astype(BF16)
        out = jnp.dot(act[...], wd_ref[...], preferred_element_type=F32)
        for j in range(ROW_TILE):
            o_ref[pl.ds(j, TME, stride=ROW_TILE), :] = out[:, j * LANES:(j + 1) * LANES]

    @pl.when(jnp.logical_not(used))
    def _():
        o_ref[...] = jnp.zeros_like(o_ref)


def _experts(xs, block_e, n_used, wg, wu, wd):
    n_blocks = xs.shape[0] // (TME * ROW_TILE)
    rows = pl.BlockSpec((TME * ROW_TILE, LANES), lambda i, be, nu: (i, 0))
    resident = lambda shape: pl.BlockSpec((None,) + shape, lambda i, be, nu: (be[i], 0, 0),
                                          pipeline_mode=pl.Buffered(1))
    grid_spec = pltpu.PrefetchScalarGridSpec(
        num_scalar_prefetch=2, grid=(n_blocks,),
        in_specs=[rows, resident((D_MODEL, D_FF_EXPERT)), resident((D_MODEL, D_FF_EXPERT)),
                  resident((D_FF_EXPERT, D_MODEL))],
        out_specs=rows,
        scratch_shapes=[pltpu.VMEM((TME, D_MODEL), BF16), pltpu.VMEM((TME, D_FF_EXPERT), BF16)])
    return pl.pallas_call(
        _expert_kernel, grid_spec=grid_spec,
        out_shape=jax.ShapeDtypeStruct(xs.shape, F32),
        compiler_params=_cparams(("arbitrary",)),
    )(block_e, n_used, xs, wg, wu, wd)


def _combine_kernel(dest_hbm, eo_hbm, x_ref, route_ref, nw_ref, xo_ref, ho_ref, idx, gbuf, sem_i, sem):
    cp = pltpu.make_async_copy(dest_hbm.at[pl.program_id(0)], idx, sem_i)
    cp.start()
    cp.wait()

    def row_copy(src_row, k, t):
        return pltpu.make_async_copy(eo_hbm.at[pl.ds(_row_tile(src_row), ROW_TILE)],
                                     gbuf.at[k, pl.ds(_row_tile(t), ROW_TILE)], sem)

    def issue(b, c):
        for u in range(DMA_UNROLL):
            r = b * DMA_UNROLL + u
            row_copy(idx[r], u % 2, b * (DMA_UNROLL // 2) + u // 2).start(priority=u % 2)
        return c

    lax.fori_loop(0, 2 * TC // DMA_UNROLL, issue, 0)

    def drain(r, c):
        row_copy(0, 0, 0).wait()
        return c

    lax.fori_loop(0, 2 * TC, drain, 0, unroll=DMA_UNROLL)
    w1 = route_ref[:, 2:3]
    w2 = route_ref[:, 3:4]
    for j in range(ROW_TILE):
        sl = slice(j * LANES, (j + 1) * LANES)
        xo_ref[:, sl] = x_ref[:, sl] + (gbuf[0, pl.ds(j, TC, stride=ROW_TILE), :] * w1
                                        + gbuf[1, pl.ds(j, TC, stride=ROW_TILE), :] * w2)
    ho_ref[...] = _rms_rows(xo_ref[...], nw_ref[...]).astype(BF16)


def _combine(eo, dest, x, route, nw_next):
    n = x.shape[0]
    steps = n // TC
    row = lambda w: pl.BlockSpec((TC, w), lambda i: (i, 0))
    return pl.pallas_call(
        _combine_kernel,
        grid=(steps,),
        in_specs=[pl.BlockSpec(memory_space=pl.ANY), pl.BlockSpec(memory_space=pl.ANY),
                  row(D_MODEL), row(LANES), _const_spec((1, D_MODEL))],
        out_specs=[row(D_MODEL), row(D_MODEL)],
        out_shape=[jax.ShapeDtypeStruct((n, D_MODEL), F32), jax.ShapeDtypeStruct((n, D_MODEL), BF16)],
        scratch_shapes=[pltpu.SMEM((2 * TC,), jnp.int32), pltpu.VMEM((2, TC * ROW_TILE, LANES), F32),
                        pltpu.SemaphoreType.DMA, pltpu.SemaphoreType.DMA],
        compiler_params=_cparams(("arbitrary",)),
    )(dest.reshape(steps, 2 * TC), eo, x, route, nw_next)


def _trunk(lay, x, p):
    depth = p['norm1_w'].shape[0]
    row = lambda v: v.reshape(1, -1).astype(F32)
    h = _norm(x, p['norm1_w'][0])
    for l in range(depth):
        w_in = p['w_in'][l]
        wqk = w_in[:, 0:2 * ATT_W].astype(BF16)
        wvzx = w_in[:, 2 * ATT_W:3 * ATT_W + D_SSM + CONV_CH].astype(BF16)
        wdt = jnp.pad(w_in[:, 3 * ATT_W + D_SSM + CONV_CH:], ((0, 0), (0, LANES - 2 * N_HEADS))).astype(BF16)
        nw_qk = jnp.concatenate([jnp.tile(p['q_norm_w'][l].astype(F32), N_HEADS) * (HEAD_DIM ** -0.5),
                                 jnp.tile(p['k_norm_w'][l].astype(F32), N_HEADS)]).reshape(1, -1)
        qk, v, z, xbc, dt_raw = _inproj(h, wqk, wvzx, wdt, nw_qk)

        attn = _attention(lay, qk, v, p['rpb'][l])

        cw = jnp.concatenate([p['conv_w'][l].astype(F32).T, p['conv_b'][l].astype(F32)[None],
                              jnp.zeros((8 - CONV_K - 1, CONV_CH), F32)], axis=0)
        pad = jnp.zeros((LANES - 2 * N_HEADS,), F32)
        dtb = jnp.concatenate([p['dt_bias'][l].astype(F32).reshape(-1), pad]).reshape(1, -1)
        arow = jnp.concatenate([-jnp.exp(p['a_log'][l].astype(F32)).reshape(-1), pad]).reshape(1, -1)
        y_f, u = _ssd_pass(lay, False, dt_raw, dtb, arow, fwd_in=(xbc, cw))
        dsk = jnp.repeat(p['d_skip'][l].astype(F32), HEAD_DIM).reshape(1, -1)
        y = _ssd_pass(lay, True, dt_raw, dtb, arow, rev_in=(u, y_f, z, dsk, row(p['ssm_norm_w'][l])))

        w_out = p['w_out'][l].astype(BF16)
        nw_next = row(p['norm1_w'][l + 1]) if l + 1 < depth else jnp.ones((1, D_MODEL), F32)
        j = l // 2
        if l % 2 == 0:
            x, h2 = _outproj(attn, y, x, w_out, row(p['attn_out_norm_w'][l]), row(p['norm2_w'][l]))
            x, h = _ffn(h2, x, p['ffn_w_gate'][j].astype(BF16), p['ffn_w_up'][j].astype(BF16),
                        p['ffn_w_down'][j].astype(BF16), nw_next)
        else:
            wr = jnp.pad(p['moe_router'][j].astype(F32), ((0, 0), (0, LANES - N_EXPERTS)))
            x, h2, route = _outproj(attn, y, x, w_out, row(p['attn_out_norm_w'][l]), row(p['norm2_w'][l]),
                                    router=_split_bf16(wr))
            dest, block_e, n_used, n_rows = _moe_plan(route)
            xs = _dispatch(h2, dest, n_rows)
            eo = _experts(xs, block_e, n_used, p['moe_w_gate'][j].astype(BF16),
                          p['moe_w_up'][j].astype(BF16), p['moe_w_down'][j].astype(BF16))
            x, h = _combine(eo, dest, x, route, nw_next)
    return x


def kernel(x_prompt, x_sample, meta_tokens, norm1_w, w_in, q_norm_w, k_norm_w, rpb, attn_out_norm_w,
           conv_w, conv_b, dt_bias, a_log, d_skip, ssm_norm_w, w_out, norm2_w,
           ffn_w_gate, ffn_w_up, ffn_w_down, moe_router, moe_w_gate, moe_w_up, moe_w_down):
    p = dict(norm1_w=norm1_w, w_in=w_in, q_norm_w=q_norm_w, k_norm_w=k_norm_w, rpb=rpb,
             attn_out_norm_w=attn_out_norm_w, conv_w=conv_w, conv_b=conv_b, dt_bias=dt_bias, a_log=a_log,
             d_skip=d_skip, ssm_norm_w=ssm_norm_w, w_out=w_out, norm2_w=norm2_w, ffn_w_gate=ffn_w_gate,
             ffn_w_up=ffn_w_up, ffn_w_down=ffn_w_down, moe_router=moe_router, moe_w_gate=moe_w_gate,
             moe_w_up=moe_w_up, moe_w_down=moe_w_down)
    groups = (x_prompt, x_sample)
    lay = _Layout([g.shape[1] for g in groups for _ in range(g.shape[0])])
    meta_block = jnp.concatenate([jnp.zeros((META_OFF, D_MODEL), F32), meta_tokens.astype(F32)], axis=0)
    tail = lay.n_tok - lay.n_grid - CHUNK * lay.n_seq
    x = jnp.concatenate([g.reshape(-1, D_MODEL).astype(F32) for g in groups]
                        + [jnp.tile(meta_block, (lay.n_seq, 1)), jnp.zeros((tail, D_MODEL), F32)], axis=0)
    x = _trunk(lay, x, p)
    outs, off = [], 0
    for g in groups:
        cnt = g.shape[0] * g.shape[1]
        outs.append(x[off:off + cnt].reshape(g.shape).astype(g.dtype))
        off += cnt
    return tuple(outs)
```

```python
import functools

import numpy as np
import jax
import jax.numpy as jnp
from jax import lax
from jax.experimental import pallas as pl
from jax.experimental.pallas import tpu as pltpu

F32 = jnp.float32
BF16 = jnp.bfloat16

D_MODEL = 1024
N_META = 16
GRID_W = 64
NA_ROWS = 8
NA_COLS = 16
N_HEADS = 16
HEAD_DIM = 64
N_PAIRS = N_HEADS // 2
ATT_W = N_HEADS * HEAD_DIM
D_SSM = 1024
SSM_GROUPS = 2
SSM_STATE = 128
GROUP_W = D_SSM // SSM_GROUPS
CONV_K = 5
CONV_CH = D_SSM + 2 * SSM_GROUPS * SSM_STATE
D_FF = 2816
N_EXPERTS = 8
D_FF_EXPERT = 3584
RMS_EPS = 1e-6
NEG_INF = -1e30

LANES = 128
ROW_TILE = D_MODEL // LANES
CHUNK = 128
META_OFF = CHUNK - N_META
TM = 512
BAND_ROWS = 8
BAND = BAND_ROWS * GRID_W
HALO_ROWS = NA_ROWS // 2
BUF_ROWS = BAND_ROWS + NA_ROWS - 1
FF_CHUNK = 256
TME = 512
ROUTER_ROWS = 16
N_PAD_RANGES = 2 * N_EXPERTS + 1
RB = 2 * TM
TC = 256
DMA_UNROLL = 8
VMEM_LIMIT = 56 * 1024 * 1024


def _cparams(sem):
    return pltpu.CompilerParams(dimension_semantics=sem, vmem_limit_bytes=VMEM_LIMIT)


def _round_up(a, b):
    return (a + b - 1) // b * b


def _const_spec(shape):
    nd = len(shape)
    return pl.BlockSpec(shape, lambda *_: (0,) * nd)


class _Layout:
    def __init__(self, seq_lens):
        self.seq_lens = tuple(seq_lens)
        self.n_seq = len(seq_lens)
        self.n_grid = sum(seq_lens)
        self.starts = np.concatenate([[0], np.cumsum(seq_lens)[:-1]]).astype(np.int64)
        self.n_tok = _round_up(self.n_grid + CHUNK * self.n_seq, TM)
        assert all(s % BAND == 0 and s // GRID_W >= NA_ROWS for s in seq_lens)

    def meta_block(self, s):
        return self.n_grid // CHUNK + s

    def attn_tables(self):
        prev, nxt, flags, mblk = [], [], [], []
        for s, (st, ln) in enumerate(zip(self.starts, self.seq_lens)):
            lo, hi = st // BAND, (st + ln) // BAND
            for b in range(lo, hi):
                per = BAND_ROWS // HALO_ROWS
                prev.append(max(b * per - 1, lo * per))
                nxt.append(min((b + 1) * per, hi * per - 1))
                flags.append((1 if b == lo else 0) | (2 if b == hi - 1 else 0))
                mblk.append((self.n_grid + CHUNK * s + META_OFF) // N_META)
        return [np.asarray(a, np.int32) for a in (prev, nxt, flags, mblk)]

    def ssd_tables(self, reverse):
        blk, prev, nxt, flags = [], [], [], []
        for s, (st, ln) in enumerate(zip(self.starts, self.seq_lens)):
            nc = ln // CHUNK
            b0 = st // CHUNK
            meta16 = (self.n_grid + CHUNK * s + META_OFF) // N_META
            steps = []
            steps.append((self.meta_block(s), 0, b0 * 8, 2 | 4))
            for c in range(nc):
                b = b0 + c
                p16 = meta16 if c == 0 else b * 8 - 1
                n16 = (b + 1) * 8 if c < nc - 1 else 0
                steps.append((b, p16, n16, 1 | (2 if c < nc - 1 else 0)))
            if reverse:
                steps = steps[::-1]
            for i, (b, p, n, f) in enumerate(steps):
                blk.append(b); prev.append(p); nxt.append(n)
                flags.append(f | (8 if i == 0 else 0))
        for b in range(self.n_grid // CHUNK + self.n_seq, self.n_tok // CHUNK):
            blk.append(b); prev.append(0); nxt.append(0)
            flags.append(8 | 16)
        return [np.asarray(a, np.int32) for a in (blk, prev, nxt, flags)]


def _rms_rows(x, w):
    ms = jnp.mean(x * x, axis=-1, keepdims=True)
    return x * lax.rsqrt(ms + RMS_EPS) * w


def _norm_kernel(x_ref, w_ref, o_ref):
    o_ref[...] = _rms_rows(x_ref[...], w_ref[...]).astype(BF16)


def _norm(x, w):
    n = x.shape[0]
    return pl.pallas_call(
        _norm_kernel,
        grid=(n // TM,),
        in_specs=[pl.BlockSpec((TM, D_MODEL), lambda i: (i, 0)), _const_spec((1, D_MODEL))],
        out_specs=pl.BlockSpec((TM, D_MODEL), lambda i: (i, 0)),
        out_shape=jax.ShapeDtypeStruct((n, D_MODEL), BF16),
        compiler_params=_cparams(("parallel",)),
    )(x, w.reshape(1, D_MODEL))


def _inproj_kernel(h_ref, wqk_ref, wvzx_ref, wdt_ref, g_ref, nw_ref,
                   qk_ref, v_ref, z_ref, xbc_ref, dt_ref):
    h = h_ref[...]
    g = g_ref[...]
    n_qk = 2 * ATT_W // FF_CHUNK
    col = lambda c: slice(c * FF_CHUNK, (c + 1) * FF_CHUNK)
    y_next = jnp.dot(h, wqk_ref[:, col(0)], preferred_element_type=F32)
    for c in range(n_qk):
        y = y_next
        if c + 1 < n_qk:
            y_next = jnp.dot(h, wqk_ref[:, col(c + 1)], preferred_element_type=F32)
        ss = jnp.dot((y * y).astype(BF16), g, preferred_element_type=F32)
        inv = lax.rsqrt(ss * (1.0 / HEAD_DIM) + RMS_EPS)
        qk_ref[:, col(c)] = (y * inv * nw_ref[:, col(c)]).astype(BF16)
    for c in range(ATT_W // FF_CHUNK):
        sl = slice(c * FF_CHUNK, (c + 1) * FF_CHUNK)
        v_ref[:, sl] = jnp.dot(h, wvzx_ref[:, sl], preferred_element_type=F32).astype(BF16)
    for c in range(D_SSM // FF_CHUNK):
        sl = slice(c * FF_CHUNK, (c + 1) * FF_CHUNK)
        src = slice(ATT_W + c * FF_CHUNK, ATT_W + (c + 1) * FF_CHUNK)
        z_ref[:, sl] = jnp.dot(h, wvzx_ref[:, src], preferred_element_type=F32).astype(BF16)
    for c in range(CONV_CH // FF_CHUNK):
        sl = slice(c * FF_CHUNK, (c + 1) * FF_CHUNK)
        src = slice(ATT_W + D_SSM + c * FF_CHUNK, ATT_W + D_SSM + (c + 1) * FF_CHUNK)
        xbc_ref[:, sl] = jnp.dot(h, wvzx_ref[:, src], preferred_element_type=F32).astype(BF16)
    dt_ref[...] = jnp.dot(h, wdt_ref[...], preferred_element_type=F32)


def _head_sum_matrix():
    i = np.arange(FF_CHUNK)
    return jnp.asarray((i[:, None] // HEAD_DIM == i[None, :] // HEAD_DIM).astype(np.float32), BF16)


def _inproj(h, wqk, wvzx, wdt, nw):
    n = h.shape[0]
    row = lambda w: pl.BlockSpec((TM, w), lambda i: (i, 0))
    return pl.pallas_call(
        _inproj_kernel,
        grid=(n // TM,),
        in_specs=[row(D_MODEL), _const_spec(wqk.shape), _const_spec(wvzx.shape), _const_spec(wdt.shape),
                  _const_spec((FF_CHUNK, FF_CHUNK)), _const_spec((1, 2 * ATT_W))],
        out_specs=[row(2 * ATT_W), row(ATT_W), row(D_SSM), row(CONV_CH), row(LANES)],
        out_shape=[jax.ShapeDtypeStruct((n, 2 * ATT_W), BF16), jax.ShapeDtypeStruct((n, ATT_W), BF16),
                   jax.ShapeDtypeStruct((n, D_SSM), BF16), jax.ShapeDtypeStruct((n, CONV_CH), BF16),
                   jax.ShapeDtypeStruct((n, LANES), F32)],
        compiler_params=_cparams(("parallel",)),
    )(h, wqk, wvzx, wdt, _head_sum_matrix(), nw)


def _split_bf16(x):
    hi = x.astype(BF16)
    lo = (x - hi.astype(F32)).astype(BF16)
    return hi, lo


def _outproj_kernel(with_router, attn_ref, y_ref, x_ref, w_ref, aw_ref, nw_ref, *rest):
    if with_router:
        wr_ref, xo_ref, h_ref, route_ref = rest
    else:
        xo_ref, h_ref = rest
    acc = x_ref[...] + jnp.dot(y_ref[...], w_ref[ATT_W:ATT_W + D_SSM, :], preferred_element_type=F32)
    a = _rms_rows(attn_ref[...].astype(F32), aw_ref[...]).astype(BF16)
    acc = acc + jnp.dot(a, w_ref[0:ATT_W, :], preferred_element_type=F32)
    xo_ref[...] = acc
    h2 = _rms_rows(acc, nw_ref[...])
    h_ref[...] = h2.astype(BF16)
    if with_router:
        hh, hl = _split_bf16(h2)
        wr = wr_ref[...]
        lg = _nt_dot(wr, hh)
        logits = lg[0:ROUTER_ROWS] + lg[ROUTER_ROWS:] + _nt_dot(wr[0:ROUTER_ROWS], hl)
        sub = lax.broadcasted_iota(jnp.int32, logits.shape, 0)
        logits = jnp.where(sub < N_EXPERTS, logits, NEG_INF)
        m1 = jnp.max(logits, axis=0, keepdims=True)
        i1 = jnp.min(jnp.where(logits == m1, sub, ROUTER_ROWS), axis=0, keepdims=True)
        rest_l = jnp.where(sub == i1, NEG_INF, logits)
        m2 = jnp.max(rest_l, axis=0, keepdims=True)
        i2 = jnp.min(jnp.where(rest_l == m2, sub, ROUTER_ROWS), axis=0, keepdims=True)
        e = jnp.exp(m2 - m1)
        w1 = 1.0 / (1.0 + e)
        route_ref[...] = jnp.concatenate(
            [i1.astype(F32), i2.astype(F32), w1, e * w1, jnp.zeros((4, logits.shape[1]), F32)], axis=0)


def _outproj(attn, y, x, w_out, aw, nw, router=None):
    n = x.shape[0]
    row = lambda w: pl.BlockSpec((TM, w), lambda i: (i, 0))
    in_specs = [row(ATT_W), row(D_SSM), row(D_MODEL), _const_spec(w_out.shape),
                _const_spec((1, ATT_W)), _const_spec((1, D_MODEL))]
    out_specs = [row(D_MODEL), row(D_MODEL)]
    out_shape = [jax.ShapeDtypeStruct((n, D_MODEL), F32), jax.ShapeDtypeStruct((n, D_MODEL), BF16)]
    args = [attn, y, x, w_out, aw, nw]
    if router is not None:
        in_specs.append(_const_spec((2 * ROUTER_ROWS, D_MODEL)))
        out_specs.append(pl.BlockSpec((8, TM), lambda i: (0, i)))
        out_shape.append(jax.ShapeDtypeStruct((8, n), F32))
        args.append(router)
    return pl.pallas_call(
        functools.partial(_outproj_kernel, router is not None),
        grid=(n // TM,), in_specs=in_specs, out_specs=out_specs, out_shape=out_shape,
        compiler_params=_cparams(("parallel",)),
    )(*args)


def _silu(x):
    return x * (1.0 / (1.0 + jnp.exp(-x)))


def _ffn_kernel(h_ref, x_ref, wg_ref, wu_ref, wd_ref, nw_ref, xo_ref, ho_ref, act_ref):
    h = h_ref[...]
    for c in range(D_FF // FF_CHUNK):
        sl = slice(c * FF_CHUNK, (c + 1) * FF_CHUNK)
        g = jnp.dot(h, wg_ref[:, sl], preferred_element_type=F32)
        u = jnp.dot(h, wu_ref[:, sl], preferred_element_type=F32)
        act_ref[:, sl] = (_silu(g) * u).astype(BF16)
    xn = x_ref[...] + jnp.dot(act_ref[...], wd_ref[...], preferred_element_type=F32)
    xo_ref[...] = xn
    ho_ref[...] = _rms_rows(xn, nw_ref[...]).astype(BF16)


def _ffn(h, x, wg, wu, wd, nw_next):
    n = x.shape[0]
    row = lambda w: pl.BlockSpec((TM, w), lambda i: (i, 0))
    return pl.pallas_call(
        _ffn_kernel,
        grid=(n // TM,),
        in_specs=[row(D_MODEL), row(D_MODEL), _const_spec(wg.shape), _const_spec(wu.shape),
                  _const_spec(wd.shape), _const_spec((1, D_MODEL))],
        out_specs=[row(D_MODEL), row(D_MODEL)],
        out_shape=[jax.ShapeDtypeStruct((n, D_MODEL), F32), jax.ShapeDtypeStruct((n, D_MODEL), BF16)],
        scratch_shapes=[pltpu.VMEM((TM, D_FF), BF16)],
        compiler_params=_cparams(("parallel",)),
    )(h, x, wg, wu, wd, nw_next)


N_CB = GRID_W // NA_COLS
SPAN_START = (0, 0, 16, 32)
SPAN_W = (32, 48, 48, 32)
META_SLOT = (24, 0, 0, 0)
META_PER_ROW = N_META // 2
KTILE = 16
ATTN_DEPTH = 16


def _attn_bias_tables(rpb):
    i = np.arange(NA_ROWS)
    dr = np.clip(i[None, :] - i[:, None] + NA_ROWS - 1, 0, 2 * NA_ROWS - 2)
    tables = {}
    for j in range(N_CB):
        span = SPAN_W[j]
        qc = NA_COLS * j + np.arange(NA_COLS)
        kc = SPAN_START[j] + np.arange(span)
        st = np.clip(qc - NA_COLS // 2, 0, GRID_W - NA_COLS)
        valid = (kc[None, :] >= st[:, None]) & (kc[None, :] < st[:, None] + NA_COLS)
        dc = np.clip(kc[None, :] - qc[:, None] + NA_COLS - 1, 0, 2 * NA_COLS - 2)
        slot = (np.arange(span) >= META_SLOT[j]) & (np.arange(span) < META_SLOT[j] + META_PER_ROW)
        assert not valid[:, slot].any()
        b = rpb.astype(F32)[:, dr][:, :, :, dc]
        b = jnp.where(valid[None, None, None], b, NEG_INF)
        meta_vis = np.where(i[:, None] < 2, 0.0, NEG_INF) * np.ones((1, span))
        b = jnp.where(slot[None, None, None, None, :], jnp.asarray(meta_vis, F32)[None, None, :, None, :], b)
        b = b.reshape(N_PAIRS, 2, NA_ROWS, NA_ROWS, NA_COLS, span)
        b = jnp.transpose(b, (0, 2, 1, 4, 3, 5)).reshape(N_PAIRS, NA_ROWS, 2 * NA_COLS, NA_ROWS * span)
        tables.setdefault(span, []).append(b)
    return [jnp.stack(t, axis=1) for _, t in sorted(tables.items())]


def _pair_queries(q):
    lo = lax.broadcasted_iota(jnp.int32, q.shape, 1) < HEAD_DIM
    zero = jnp.zeros_like(q)
    return jnp.concatenate([jnp.where(lo, q, zero), jnp.where(lo, zero, q)], axis=0)


def _nt_dot(a, b):
    return lax.dot_general(a, b, (((1,), (1,)), ((), ())), preferred_element_type=F32)


def _attn_kernel(prev_ref, next_ref, flag_ref, mblk_ref,
                 q_ref, kp_ref, kc_ref, kn_ref, vp_ref, vc_ref, vn_ref, km_ref, vm_ref,
                 b32_ref, b48_ref, o_ref, *spans):
    del prev_ref, next_ref, mblk_ref
    flags = flag_ref[pl.program_id(1)]
    is_first = (flags & 1) != 0
    is_last = (flags & 2) != 0
    low = lax.broadcasted_iota(jnp.int32, (KTILE, LANES), 0) < META_PER_ROW

    for refs, bufs, m_ref in (((kp_ref, kc_ref, kn_ref), spans[:N_CB], km_ref),
                              ((vp_ref, vc_ref, vn_ref), spans[N_CB:], vm_ref)):
        m = m_ref[...]
        mf = m.astype(F32)
        m_swapped = jnp.concatenate([mf[META_PER_ROW:], mf[:META_PER_ROW]], axis=0).astype(BF16)
        for rr in range(BUF_ROWS):
            if rr < HALO_ROWS:
                ref, row = refs[0], rr
            elif rr < HALO_ROWS + BAND_ROWS:
                ref, row = refs[1], rr - HALO_ROWS
            else:
                ref, row = refs[2], rr - HALO_ROWS - BAND_ROWS
            for j in range(N_CB):
                span = SPAN_W[j]
                for t in range(span // KTILE):
                    off = row * GRID_W + SPAN_START[j] + t * KTILE
                    tile = ref[off:off + KTILE, :]
                    if t == META_SLOT[j] // KTILE:
                        if META_SLOT[j] % KTILE == 0:
                            tile = jnp.where(low, m if rr % 2 == 0 else m_swapped, tile)
                        else:
                            tile = jnp.where(low, tile, m_swapped if rr % 2 == 0 else m)
                    bufs[j][rr * span + t * KTILE:rr * span + (t + 1) * KTILE, :] = tile

    lane_lo = lax.broadcasted_iota(jnp.int32, (NA_COLS, LANES), 1) < HEAD_DIM
    starts = []
    for ri in range(BAND_ROWS):
        ls = jnp.int32(ri)
        ls = jnp.where(is_first, jnp.maximum(ls, HALO_ROWS), ls)
        ls = jnp.where(is_last, jnp.minimum(ls, HALO_ROWS), ls)
        starts.append((ls, HALO_ROWS + ri - ls))

    def scores(u):
        ri, j = divmod(u, N_CB)
        ls, delta = starts[ri]
        span = SPAN_W[j]
        kwin = spans[j][pl.ds(pl.multiple_of(ls * span, KTILE), NA_ROWS * span), :]
        q0 = ri * GRID_W + j * NA_COLS
        bias_ref = b32_ref if span == SPAN_W[0] else b48_ref
        return _nt_dot(_pair_queries(q_ref[q0:q0 + NA_COLS, :]), kwin) + bias_ref[SPAN_W[:j].count(span), delta]

    def finish(u, s):
        ri, j = divmod(u, N_CB)
        span = SPAN_W[j]
        vwin = spans[N_CB + j][pl.ds(pl.multiple_of(starts[ri][0] * span, KTILE), NA_ROWS * span), :]
        p = jnp.exp(s - jnp.max(s, axis=-1, keepdims=True))
        l = jnp.sum(p, axis=-1, keepdims=True)
        o = jnp.dot(p.astype(BF16), vwin, preferred_element_type=F32) / l
        q0 = ri * GRID_W + j * NA_COLS
        o_ref[q0:q0 + NA_COLS, :] = jnp.where(lane_lo, o[:NA_COLS], o[NA_COLS:]).astype(BF16)

    n_units = BAND_ROWS * N_CB
    pending = {}
    for t in range(n_units + ATTN_DEPTH):
        if t < n_units:
            pending[t] = scores(t)
        if t >= ATTN_DEPTH:
            finish(t - ATTN_DEPTH, pending.pop(t - ATTN_DEPTH))


def _attn_meta_kernel(n_seq, qk_ref, v_ref, mbias_ref, alias_ref, o_ref):
    del alias_ref
    s = pl.program_id(0)
    o_ref[...] = jnp.zeros_like(o_ref)

    @pl.when(s < n_seq)
    def _():
        mbias = mbias_ref[...]
        for p in range(N_PAIRS):
            c = slice(p * LANES, (p + 1) * LANES)
            q = qk_ref[META_OFF:CHUNK, c]
            k = qk_ref[:, ATT_W + p * LANES:ATT_W + (p + 1) * LANES]
            qs = _pair_queries(q)
            sc = _nt_dot(qs, k) + mbias
            m = jnp.max(sc, axis=-1, keepdims=True)
            e = jnp.exp(sc - m)
            l = jnp.sum(e, axis=-1, keepdims=True)
            o = jnp.dot(e.astype(BF16), v_ref[:, c], preferred_element_type=F32) / l
            lo = lax.broadcasted_iota(jnp.int32, (N_META, LANES), 1) < HEAD_DIM
            o_ref[META_OFF:CHUNK, c] = jnp.where(lo, o[0:N_META], o[N_META:]).astype(BF16)


def _attention(lay, qk, v, rpb):
    n = qk.shape[0]
    prev, nxt, flags, mblk = lay.attn_tables()
    n_bands = lay.n_grid // BAND
    b32, b48 = _attn_bias_tables(rpb)
    lane = np.arange(LANES)
    mbias_meta = jnp.asarray(np.where(lane >= META_OFF, 0.0, NEG_INF)[None, :], F32)
    kcol = ATT_W // LANES
    halo = HALO_ROWS * GRID_W
    bias_spec = lambda a: pl.BlockSpec((None,) + a.shape[1:], lambda p, b, *_: (p, 0, 0, 0, 0))
    grid_spec = pltpu.PrefetchScalarGridSpec(
        num_scalar_prefetch=4,
        grid=(N_PAIRS, n_bands),
        in_specs=[
            pl.BlockSpec((BAND, LANES), lambda p, b, *_: (b, p)),
            pl.BlockSpec((halo, LANES), lambda p, b, pv, nx, fl, mb: (pv[b], kcol + p)),
            pl.BlockSpec((BAND, LANES), lambda p, b, pv, nx, fl, mb: (b, kcol + p)),
            pl.BlockSpec((halo, LANES), lambda p, b, pv, nx, fl, mb: (nx[b], kcol + p)),
            pl.BlockSpec((halo, LANES), lambda p, b, pv, nx, fl, mb: (pv[b], p)),
            pl.BlockSpec((BAND, LANES), lambda p, b, pv, nx, fl, mb: (b, p)),
            pl.BlockSpec((halo, LANES), lambda p, b, pv, nx, fl, mb: (nx[b], p)),
            pl.BlockSpec((N_META, LANES), lambda p, b, pv, nx, fl, mb: (mb[b], kcol + p)),
            pl.BlockSpec((N_META, LANES), lambda p, b, pv, nx, fl, mb: (mb[b], p)),
            bias_spec(b32), bias_spec(b48),
        ],
        out_specs=pl.BlockSpec((BAND, LANES), lambda p, b, *_: (b, p)),
        scratch_shapes=[pltpu.VMEM((BUF_ROWS * w, LANES), BF16) for w in SPAN_W] * 2,
    )
    attn = pl.pallas_call(
        _attn_kernel, grid_spec=grid_spec,
        out_shape=jax.ShapeDtypeStruct((n, ATT_W), BF16),
        compiler_params=_cparams(("arbitrary", "arbitrary")),
    )(jnp.asarray(prev), jnp.asarray(nxt), jnp.asarray(flags), jnp.asarray(mblk),
      qk, qk, qk, qk, v, v, v, qk, v, b32, b48)
    mb0 = lay.n_grid // CHUNK
    n_tail = (n - lay.n_grid) // CHUNK
    return pl.pallas_call(
        functools.partial(_attn_meta_kernel, lay.n_seq),
        grid=(n_tail,),
        in_specs=[pl.BlockSpec((CHUNK, 2 * ATT_W), lambda s: (mb0 + s, 0)),
                  pl.BlockSpec((CHUNK, ATT_W), lambda s: (mb0 + s, 0)),
                  _const_spec((1, LANES)),
                  pl.BlockSpec(memory_space=pl.ANY)],
        out_specs=pl.BlockSpec((CHUNK, ATT_W), lambda s: (mb0 + s, 0)),
        out_shape=jax.ShapeDtypeStruct((n, ATT_W), BF16),
        input_output_aliases={3: 0},
        compiler_params=_cparams(("arbitrary",)),
    )(qk, v, mbias_meta, attn)


def _softplus(x):
    return jnp.maximum(x, 0.0) + jnp.log(1.0 + jnp.exp(-jnp.abs(x)))


def _split3_bf16(x):
    hi = x.astype(BF16)
    r = x - hi.astype(F32)
    mid = r.astype(BF16)
    lo = (r - mid.astype(F32)).astype(BF16)
    return hi, mid, lo


def _expand_heads(w, e):
    hi, lo = _split_bf16(w)
    return jnp.dot(hi, e, preferred_element_type=F32) + jnp.dot(lo, e, preferred_element_type=F32)


def _ssd_kernel(reverse, blk_ref, prev_ref, next_ref, flag_ref, *refs):
    if reverse:
        (u_ref, dt_ref, dtb_ref, arow_ref, e_ref, tri_ref, yf_ref, z_ref, dsk_ref, nw_ref,
         o_ref, state, ybuf) = refs
    else:
        (xbc_ref, xp_ref, xn_ref, shift_ref, cw_ref, dt_ref, dtb_ref, arow_ref, e_ref, tri_ref,
         o_ref, u_ref, xe, state) = refs
        ybuf = o_ref
    del blk_ref, prev_ref, next_ref
    flags = flag_ref[pl.program_id(0)]
    is_meta = (flags & 4) != 0

    @pl.when((flags & 8) != 0)
    def _():
        state[...] = jnp.zeros_like(state)

    if reverse:
        u = u_ref[...].astype(F32)
    else:
        zero = jnp.zeros((N_META, CONV_CH), BF16)
        xe[0:N_META, :] = jnp.where((flags & 1) != 0, xp_ref[...], zero)
        xe[N_META:N_META + CHUNK, :] = xbc_ref[...]
        xe[N_META + CHUNK:, :] = jnp.where((flags & 2) != 0, xn_ref[...], zero)
        shifted = jnp.dot(shift_ref[jnp.where(is_meta, 1, 0)], xe[...], preferred_element_type=F32)
        u = cw_ref[CONV_K:CONV_K + 1, :]
        for k in range(CONV_K):
            u = u + cw_ref[k:k + 1, :] * shifted[k * CHUNK:(k + 1) * CHUNK, :]
        u = _silu(u)
        u_ref[...] = u.astype(BF16)
    xs = u[:, 0:D_SSM]
    x_bf = xs.astype(BF16)

    rid = lax.broadcasted_iota(jnp.int32, (CHUNK, 1), 0)
    valid = rid >= jnp.where(is_meta, META_OFF, 0)
    dt = jnp.where(valid, _softplus(dt_ref[...] + dtb_ref[...]), 0.0)
    a = dt * arow_ref[...]
    tri = tri_ref[...]
    ah, am, al = _split3_bf16(a)
    acum = (jnp.dot(tri, ah, preferred_element_type=F32) + jnp.dot(tri, am, preferred_element_type=F32)
            + jnp.dot(tri, al, preferred_element_type=F32))
    acum_t = acum.T
    dt_t = dt.T
    edge = 0 if reverse else CHUNK - 1
    a_tot = acum[edge:edge + 1, :]
    e = e_ref[...]
    w1e = _expand_heads(dt * jnp.exp(a_tot - acum), e)
    w2e = _expand_heads(jnp.exp(acum), e)
    decay_row = w2e[edge:edge + 1, :]
    xw = (xs * w1e).astype(BF16)

    li = lax.broadcasted_iota(jnp.int32, (CHUNK, CHUNK), 0)
    si = lax.broadcasted_iota(jnp.int32, (CHUNK, CHUNK), 1)
    causal = (si >= li) if reverse else (li >= si)
    lane_lo = si < HEAD_DIM
    ho = N_HEADS if reverse else 0
    heads_per_group = N_HEADS // SSM_GROUPS
    for g in range(SSM_GROUPS):
        gsl = slice(g * GROUP_W, (g + 1) * GROUP_W)
        bg = u[:, D_SSM + g * SSM_STATE:D_SSM + (g + 1) * SSM_STATE]
        cg = u[:, D_SSM + (SSM_GROUPS + g) * SSM_STATE:D_SSM + (SSM_GROUPS + g + 1) * SSM_STATE].astype(BF16)
        cb = _nt_dot(cg, bg.astype(BF16))
        st_old = state[:, gsl]
        y_off = jnp.dot(cg, st_old.astype(BF16), preferred_element_type=F32) * w2e[:, gsl]
        st_new = jnp.dot(bg.T.astype(BF16), xw[:, gsl], preferred_element_type=F32)
        state[:, gsl] = st_old * decay_row[:, gsl] + st_new
        for j in range(heads_per_group // 2):
            col = g * GROUP_W + j * LANES
            acc = y_off[:, j * LANES:(j + 1) * LANES]
            xpair = x_bf[:, col:col + LANES]
            for hh in range(2):
                hi = ho + g * heads_per_group + 2 * j + hh
                seg = acum[:, hi:hi + 1] - acum_t[hi:hi + 1, :]
                lm = jnp.exp(jnp.where(causal, seg, NEG_INF))
                mh = (cb * lm * dt_t[hi:hi + 1, :]).astype(BF16)
                keep = lane_lo if hh == 0 else jnp.logical_not(lane_lo)
                acc = acc + jnp.dot(mh, jnp.where(keep, xpair, jnp.zeros_like(xpair)),
                                    preferred_element_type=F32)
            ybuf[:, col:col + LANES] = acc
    if reverse:
        for g in range(SSM_GROUPS):
            gsl = slice(g * GROUP_W, (g + 1) * GROUP_W)
            yg = ybuf[:, gsl] + yf_ref[:, gsl] + dsk_ref[:, gsl] * xs[:, gsl]
            yg = yg * _silu(z_ref[:, gsl].astype(F32))
            ms = jnp.mean(yg * yg, axis=-1, keepdims=True)
            o_ref[:, gsl] = (yg * lax.rsqrt(ms + RMS_EPS) * nw_ref[:, gsl]).astype(BF16)

    @pl.when((flags & 16) != 0)
    def _():
        o_ref[...] = jnp.zeros_like(o_ref)


def _ssd_constants(reverse):
    ho = N_HEADS if reverse else 0
    e = np.zeros((LANES, D_SSM), np.float32)
    for h in range(N_HEADS):
        e[ho + h, h * HEAD_DIM:(h + 1) * HEAD_DIM] = 1.0
    i = np.arange(CHUNK)
    tri = (i[None, :] >= i[:, None]) if reverse else (i[:, None] >= i[None, :])
    return jnp.asarray(e, BF16), jnp.asarray(tri.astype(np.float32), BF16)


def _shift_matrices():
    s = np.zeros((2, CONV_K * CHUNK, CHUNK + 2 * N_META), np.float32)
    i = np.arange(CHUNK)
    for k in range(CONV_K):
        s[:, k * CHUNK + i, N_META + i + k - CONV_K // 2] = 1.0
    s[1, :, :N_META + META_OFF] = 0.0
    return jnp.asarray(s, BF16)


def _ssd_pass(lay, reverse, dt, dtb, arow, fwd_in=(), rev_in=()):
    n = dt.shape[0]
    blk, prev, nxt, flags = lay.ssd_tables(reverse)
    e, tri = _ssd_constants(reverse)
    cur = lambda w: pl.BlockSpec((CHUNK, w), lambda i, bk, pv, nx, fl: (bk[i], 0))
    common = [cur(LANES), _const_spec((1, LANES)), _const_spec((1, LANES)),
              _const_spec((LANES, D_SSM)), _const_spec((CHUNK, CHUNK))]
    state = pltpu.VMEM((SSM_STATE, D_SSM), F32)
    if reverse:
        u, y_f, z, dsk, nw = rev_in
        args = (u, dt, dtb, arow, e, tri, y_f, z, dsk, nw)
        in_specs = [cur(CONV_CH)] + common + [cur(D_SSM), cur(D_SSM), _const_spec((1, D_SSM)),
                                              _const_spec((1, D_SSM))]
        out_specs = cur(D_SSM)
        out_shape = jax.ShapeDtypeStruct((n, D_SSM), BF16)
        scratch = [state, pltpu.VMEM((CHUNK, D_SSM), F32)]
    else:
        xbc, cw = fwd_in
        shift = _shift_matrices()
        args = (xbc, xbc, xbc, shift, cw, dt, dtb, arow, e, tri)
        in_specs = [cur(CONV_CH),
                    pl.BlockSpec((N_META, CONV_CH), lambda i, bk, pv, nx, fl: (pv[i], 0)),
                    pl.BlockSpec((N_META, CONV_CH), lambda i, bk, pv, nx, fl: (nx[i], 0)),
                    _const_spec(shift.shape), _const_spec((8, CONV_CH))] + common
        out_specs = [cur(D_SSM), cur(CONV_CH)]
        out_shape = [jax.ShapeDtypeStruct((n, D_SSM), F32), jax.ShapeDtypeStruct((n, CONV_CH), BF16)]
        scratch = [pltpu.VMEM((CHUNK + 2 * N_META, CONV_CH), BF16), state]
    grid_spec = pltpu.PrefetchScalarGridSpec(
        num_scalar_prefetch=4, grid=(len(blk),), in_specs=in_specs,
        out_specs=out_specs, scratch_shapes=scratch)
    return pl.pallas_call(
        functools.partial(_ssd_kernel, reverse), grid_spec=grid_spec, out_shape=out_shape,
        compiler_params=_cparams(("arbitrary",)),
    )(jnp.asarray(blk), jnp.asarray(prev), jnp.asarray(nxt), jnp.asarray(flags), *args)


def _moe_plan(route_t):
    n = route_t.shape[1]
    flat_e = route_t[0:2].T.astype(jnp.int32).reshape(-1)
    onehot = (flat_e[:, None] == jnp.arange(N_EXPERTS, dtype=jnp.int32)[None, :]).astype(jnp.int32)
    csum = jnp.cumsum(onehot, axis=0)
    counts = csum[-1]
    rank = jnp.sum(csum * onehot, axis=1) - 1
    padded = (counts + TME - 1) // TME * TME
    pends = jnp.cumsum(padded)
    pstarts = pends - padded
    dest = jnp.sum(onehot * pstarts[None, :], axis=1) + rank
    n_rows = _round_up(2 * n + N_EXPERTS * TME, TME)
    n_blocks = n_rows // TME
    block_e = jnp.sum((jnp.arange(n_blocks, dtype=jnp.int32)[:, None] * TME >= pends[None, :]).astype(jnp.int32),
                      axis=1)
    block_e = jnp.minimum(block_e, N_EXPERTS - 1)
    n_used = (pends[-1] // TME).astype(jnp.int32).reshape(1)
    tail = jnp.minimum(pends[-1] + jnp.arange(N_PAD_RANGES - N_EXPERTS + 1, dtype=jnp.int32) * TME, n_rows)
    pad_rows = jnp.concatenate([pstarts + counts, tail[:-1], pends, tail[1:]]).astype(jnp.int32)
    return dest.astype(jnp.int32), block_e, n_used, pad_rows, n_rows


def _row_tile(r):
    return pl.multiple_of(r * ROW_TILE, ROW_TILE)


def _dispatch_kernel(pad_ref, dest_hbm, h_ref, xs_hbm, idx, stage, ztile, sem_i, sem, sem_z):
    @pl.when(pl.program_id(0) == 0)
    def _():
        ztile[...] = jnp.zeros_like(ztile)

        def zero_copy(r):
            return pltpu.make_async_copy(ztile, xs_hbm.at[pl.ds(_row_tile(r), ROW_TILE)], sem_z)

        for e in range(N_PAD_RANGES):
            lo, hi = pad_ref[e], pad_ref[N_PAD_RANGES + e]

            def fill(r, c):
                zero_copy(r).start()
                return c

            def drain_fill(r, c):
                zero_copy(0).wait()
                return c

            lax.fori_loop(lo, hi, fill, 0)
            lax.fori_loop(lo, hi, drain_fill, 0)

    cp = pltpu.make_async_copy(dest_hbm.at[pl.program_id(0)], idx, sem_i)
    cp.start()
    hf = h_ref[...].astype(F32)
    for j in range(ROW_TILE):
        stage[pl.ds(j, TM, stride=ROW_TILE), :] = hf[:, j * LANES:(j + 1) * LANES]
    cp.wait()

    def row_copy(src_row, dst_row):
        return pltpu.make_async_copy(stage.at[pl.ds(_row_tile(src_row), ROW_TILE)],
                                     xs_hbm.at[pl.ds(_row_tile(dst_row), ROW_TILE)], sem)

    def issue(b, c):
        for u in range(DMA_UNROLL):
            r = b * DMA_UNROLL + u
            row_copy(lax.shift_right_logical(r, 1), idx[r]).start(priority=u % 2)
        return c

    lax.fori_loop(0, RB // DMA_UNROLL, issue, 0)

    def drain(r, c):
        row_copy(0, 0).wait()
        return c

    lax.fori_loop(0, RB, drain, 0, unroll=DMA_UNROLL)


def _dispatch(h, dest, pad_rows, n_rows):
    n = h.shape[0]
    steps = n // TM
    grid_spec = pltpu.PrefetchScalarGridSpec(
        num_scalar_prefetch=1, grid=(steps,),
        in_specs=[pl.BlockSpec(memory_space=pl.ANY), pl.BlockSpec((TM, D_MODEL), lambda i, pad: (i, 0))],
        out_specs=pl.BlockSpec(memory_space=pl.ANY),
        scratch_shapes=[pltpu.SMEM((RB,), jnp.int32), pltpu.VMEM((TM * ROW_TILE, LANES), F32),
                        pltpu.VMEM((ROW_TILE, LANES), F32),
                        pltpu.SemaphoreType.DMA, pltpu.SemaphoreType.DMA, pltpu.SemaphoreType.DMA])
    return pl.pallas_call(
        _dispatch_kernel, grid_spec=grid_spec,
        out_shape=jax.ShapeDtypeStruct((n_rows * ROW_TILE, LANES), F32),
        compiler_params=_cparams(("arbitrary",)),
    )(pad_rows, dest.reshape(steps, RB), h)


def _expert_kernel(be_ref, nu_ref, x_ref, wg_ref, wu_ref, wd_ref, o_ref, xbf, act):
    del be_ref
    used = pl.program_id(0) < nu_ref[0]

    @pl.when(used)
    def _():
        for j in range(ROW_TILE):
            xbf[:, j * LANES:(j + 1) * LANES] = x_ref[pl.ds(j, TME, stride=ROW_TILE), :].astype(BF16)
        x = xbf[...]
        for c in range(D_FF_EXPERT // FF_CHUNK):
            sl = slice(c * FF_CHUNK, (c + 1) * FF_CHUNK)
            g = jnp.dot(x, wg_ref[:, sl], preferred_element_type=F32)
            u = jnp.dot(x, wu_ref[:, sl], preferred_element_type=F32)
            act[:, sl] = (_silu(g) * u).astype(BF16)
        out = jnp.dot(act[...], wd_ref[...], preferred_element_type=F32)
        for j in range(ROW_TILE):
            o_ref[pl.ds(j, TME, stride=ROW_TILE), :] = out[:, j * LANES:(j + 1) * LANES]

    @pl.when(jnp.logical_not(used))
    def _():
        o_ref[...] = jnp.zeros_like(o_ref)


def _experts(xs, block_e, n_used, wg, wu, wd):
    n_blocks = xs.shape[0] // (TME * ROW_TILE)
    rows = pl.BlockSpec((TME * ROW_TILE, LANES), lambda i, be, nu: (i, 0))
    resident = lambda shape: pl.BlockSpec((None,) + shape, lambda i, be, nu: (be[i], 0, 0),
                                          pipeline_mode=pl.Buffered(1))
    grid_spec = pltpu.PrefetchScalarGridSpec(
        num_scalar_prefetch=2, grid=(n_blocks,),
        in_specs=[pl.BlockSpec((TME * ROW_TILE, LANES), lambda i, be, nu: (jnp.minimum(i, nu[0] - 1), 0)),
                  resident((D_MODEL, D_FF_EXPERT)), resident((D_MODEL, D_FF_EXPERT)),
                  resident((D_FF_EXPERT, D_MODEL))],
        out_specs=rows,
        scratch_shapes=[pltpu.VMEM((TME, D_MODEL), BF16), pltpu.VMEM((TME, D_FF_EXPERT), BF16)])
    return pl.pallas_call(
        _expert_kernel, grid_spec=grid_spec,
        out_shape=jax.ShapeDtypeStruct(xs.shape, F32),
        compiler_params=_cparams(("arbitrary",)),
    )(block_e, n_used, xs, wg, wu, wd)


def _combine_kernel(dest_hbm, eo_hbm, x_ref, route_ref, nw_ref, xo_ref, ho_ref, idx, gbuf, sem_i, sem):
    cp = pltpu.make_async_copy(dest_hbm.at[pl.program_id(0)], idx, sem_i)
    cp.start()
    cp.wait()

    def row_copy(src_row, k, t):
        return pltpu.make_async_copy(eo_hbm.at[pl.ds(_row_tile(src_row), ROW_TILE)],
                                     gbuf.at[k, pl.ds(_row_tile(t), ROW_TILE)], sem)

    def issue(b, c):
        for u in range(DMA_UNROLL):
            r = b * DMA_UNROLL + u
            row_copy(idx[r], u % 2, b * (DMA_UNROLL // 2) + u // 2).start(priority=u % 2)
        return c

    lax.fori_loop(0, 2 * TC // DMA_UNROLL, issue, 0)

    def drain(r, c):
        row_copy(0, 0, 0).wait()
        return c

    lax.fori_loop(0, 2 * TC, drain, 0, unroll=DMA_UNROLL)
    w1 = route_ref[:, 0:1]
    w2 = route_ref[:, 1:2]
    for j in range(ROW_TILE):
        sl = slice(j * LANES, (j + 1) * LANES)
        xo_ref[:, sl] = x_ref[:, sl] + (gbuf[0, pl.ds(j, TC, stride=ROW_TILE), :] * w1
                                        + gbuf[1, pl.ds(j, TC, stride=ROW_TILE), :] * w2)
    ho_ref[...] = _rms_rows(xo_ref[...], nw_ref[...]).astype(BF16)


def _combine(eo, dest, x, route, nw_next):
    n = x.shape[0]
    steps = n // TC
    row = lambda w: pl.BlockSpec((TC, w), lambda i: (i, 0))
    return pl.pallas_call(
        _combine_kernel,
        grid=(steps,),
        in_specs=[pl.BlockSpec(memory_space=pl.ANY), pl.BlockSpec(memory_space=pl.ANY),
                  row(D_MODEL), row(2), _const_spec((1, D_MODEL))],
        out_specs=[row(D_MODEL), row(D_MODEL)],
        out_shape=[jax.ShapeDtypeStruct((n, D_MODEL), F32), jax.ShapeDtypeStruct((n, D_MODEL), BF16)],
        scratch_shapes=[pltpu.SMEM((2 * TC,), jnp.int32), pltpu.VMEM((2, TC * ROW_TILE, LANES), F32),
                        pltpu.SemaphoreType.DMA, pltpu.SemaphoreType.DMA],
        compiler_params=_cparams(("arbitrary",)),
    )(dest.reshape(steps, 2 * TC), eo, x, route, nw_next)


def _trunk(lay, x, p):
    depth = p['norm1_w'].shape[0]
    row = lambda v: v.reshape(1, -1).astype(F32)
    h = _norm(x, p['norm1_w'][0])
    for l in range(depth):
        w_in = p['w_in'][l]
        wqk = w_in[:, 0:2 * ATT_W].astype(BF16)
        wvzx = w_in[:, 2 * ATT_W:3 * ATT_W + D_SSM + CONV_CH].astype(BF16)
        wdt = jnp.pad(w_in[:, 3 * ATT_W + D_SSM + CONV_CH:], ((0, 0), (0, LANES - 2 * N_HEADS))).astype(BF16)
        nw_qk = jnp.concatenate([jnp.tile(p['q_norm_w'][l].astype(F32), N_HEADS) * (HEAD_DIM ** -0.5),
                                 jnp.tile(p['k_norm_w'][l].astype(F32), N_HEADS)]).reshape(1, -1)
        qk, v, z, xbc, dt_raw = _inproj(h, wqk, wvzx, wdt, nw_qk)

        attn = _attention(lay, qk, v, p['rpb'][l])

        cw = jnp.concatenate([p['conv_w'][l].astype(F32).T, p['conv_b'][l].astype(F32)[None],
                              jnp.zeros((8 - CONV_K - 1, CONV_CH), F32)], axis=0)
        pad = jnp.zeros((LANES - 2 * N_HEADS,), F32)
        dtb = jnp.concatenate([p['dt_bias'][l].astype(F32).reshape(-1), pad]).reshape(1, -1)
        arow = jnp.concatenate([-jnp.exp(p['a_log'][l].astype(F32)).reshape(-1), pad]).reshape(1, -1)
        y_f, u = _ssd_pass(lay, False, dt_raw, dtb, arow, fwd_in=(xbc, cw))
        dsk = jnp.repeat(p['d_skip'][l].astype(F32), HEAD_DIM).reshape(1, -1)
        y = _ssd_pass(lay, True, dt_raw, dtb, arow, rev_in=(u, y_f, z, dsk, row(p['ssm_norm_w'][l])))

        w_out = p['w_out'][l].astype(BF16)
        nw_next = row(p['norm1_w'][l + 1]) if l + 1 < depth else jnp.ones((1, D_MODEL), F32)
        j = l // 2
        if l % 2 == 0:
            x, h2 = _outproj(attn, y, x, w_out, row(p['attn_out_norm_w'][l]), row(p['norm2_w'][l]))
            x, h = _ffn(h2, x, p['ffn_w_gate'][j].astype(BF16), p['ffn_w_up'][j].astype(BF16),
                        p['ffn_w_down'][j].astype(BF16), nw_next)
        else:
            wr = jnp.pad(p['moe_router'][j].astype(F32).T, ((0, ROUTER_ROWS - N_EXPERTS), (0, 0)))
            x, h2, route_t = _outproj(attn, y, x, w_out, row(p['attn_out_norm_w'][l]), row(p['norm2_w'][l]),
                                      router=jnp.concatenate(_split_bf16(wr), axis=0))
            dest, block_e, n_used, pad_rows, n_rows = _moe_plan(route_t)
            xs = _dispatch(h2, dest, pad_rows, n_rows)
            eo = _experts(xs, block_e, n_used, p['moe_w_gate'][j].astype(BF16),
                          p['moe_w_up'][j].astype(BF16), p['moe_w_down'][j].astype(BF16))
            x, h = _combine(eo, dest, x, route_t[2:4].T, nw_next)
    return x


def kernel(x_prompt, x_sample, meta_tokens, norm1_w, w_in, q_norm_w, k_norm_w, rpb, attn_out_norm_w,
           conv_w, conv_b, dt_bias, a_log, d_skip, ssm_norm_w, w_out, norm2_w,
           ffn_w_gate, ffn_w_up, ffn_w_down, moe_router, moe_w_gate, moe_w_up, moe_w_down):
    p = dict(norm1_w=norm1_w, w_in=w_in, q_norm_w=q_norm_w, k_norm_w=k_norm_w, rpb=rpb,
             attn_out_norm_w=attn_out_norm_w, conv_w=conv_w, conv_b=conv_b, dt_bias=dt_bias, a_log=a_log,
             d_skip=d_skip, ssm_norm_w=ssm_norm_w, w_out=w_out, norm2_w=norm2_w, ffn_w_gate=ffn_w_gate,
             ffn_w_up=ffn_w_up, ffn_w_down=ffn_w_down, moe_router=moe_router, moe_w_gate=moe_w_gate,
             moe_w_up=moe_w_up, moe_w_down=moe_w_down)
    groups = (x_prompt, x_sample)
    lay = _Layout([g.shape[1] for g in groups for _ in range(g.shape[0])])
    meta_block = jnp.concatenate([jnp.zeros((META_OFF, D_MODEL), F32), meta_tokens.astype(F32)], axis=0)
    tail = lay.n_tok - lay.n_grid - CHUNK * lay.n_seq
    x = jnp.concatenate([g.reshape(-1, D_MODEL).astype(F32) for g in groups]
                        + [jnp.tile(meta_block, (lay.n_seq, 1)), jnp.zeros((tail, D_MODEL), F32)], axis=0)
    x = _trunk(lay, x, p)
    outs, off = [], 0
    for g in groups:
        cnt = g.shape[0] * g.shape[1]
        outs.append(x[off:off + cnt].reshape(g.shape).astype(g.dtype))
        off += cnt
    return tuple(outs)
```

```python
import functools

import numpy as np
import jax
import jax.numpy as jnp
from jax import lax
from jax.experimental import pallas as pl
from jax.experimental.pallas import tpu as pltpu

F32 = jnp.float32
BF16 = jnp.bfloat16

D_MODEL = 1024
N_META = 16
GRID_W = 64
NA_ROWS = 8
NA_COLS = 16
N_HEADS = 16
HEAD_DIM = 64
N_PAIRS = N_HEADS // 2
ATT_W = N_HEADS * HEAD_DIM
D_SSM = 1024
SSM_GROUPS = 2
SSM_STATE = 128
GROUP_W = D_SSM // SSM_GROUPS
CONV_K = 5
CONV_CH = D_SSM + 2 * SSM_GROUPS * SSM_STATE
D_FF = 2816
N_EXPERTS = 8
D_FF_EXPERT = 3584
RMS_EPS = 1e-6
NEG_INF = -1e30
LOG2E = 1.4426950408889634

LANES = 128
ROW_TILE = D_MODEL // LANES
CHUNK = 128
META_OFF = CHUNK - N_META
TM = 512
BAND_ROWS = 16
BAND = BAND_ROWS * GRID_W
HALO_ROWS = NA_ROWS // 2
BUF_ROWS = BAND_ROWS + NA_ROWS - 1
FF_CHUNK = 256
TME = 512
ROUTER_ROWS = 16
N_PAD_RANGES = 2 * N_EXPERTS + 1
RB = 2 * TM
TC = 256
DMA_UNROLL = 8
VMEM_LIMIT = 56 * 1024 * 1024


def _cparams(sem):
    return pltpu.CompilerParams(dimension_semantics=sem, vmem_limit_bytes=VMEM_LIMIT)


def _round_up(a, b):
    return (a + b - 1) // b * b


def _const_spec(shape):
    nd = len(shape)
    return pl.BlockSpec(shape, lambda *_: (0,) * nd)


class _Layout:
    def __init__(self, seq_lens):
        self.seq_lens = tuple(seq_lens)
        self.n_seq = len(seq_lens)
        self.n_grid = sum(seq_lens)
        self.starts = np.concatenate([[0], np.cumsum(seq_lens)[:-1]]).astype(np.int64)
        self.n_tok = _round_up(self.n_grid + CHUNK * self.n_seq, TM)
        assert all(s % BAND == 0 and s // GRID_W >= NA_ROWS for s in seq_lens)

    def meta_block(self, s):
        return self.n_grid // CHUNK + s

    def attn_tables(self):
        prev, nxt, flags, mblk = [], [], [], []
        for s, (st, ln) in enumerate(zip(self.starts, self.seq_lens)):
            lo, hi = st // BAND, (st + ln) // BAND
            for b in range(lo, hi):
                per = BAND_ROWS // HALO_ROWS
                prev.append(max(b * per - 1, lo * per))
                nxt.append(min((b + 1) * per, hi * per - 1))
                flags.append((1 if b == lo else 0) | (2 if b == hi - 1 else 0))
                mblk.append((self.n_grid + CHUNK * s + META_OFF) // N_META)
        return [np.asarray(a, np.int32) for a in (prev, nxt, flags, mblk)]

    def ssd_tables(self, reverse):
        blk, prev, nxt, flags = [], [], [], []
        for s, (st, ln) in enumerate(zip(self.starts, self.seq_lens)):
            nc = ln // CHUNK
            b0 = st // CHUNK
            meta16 = (self.n_grid + CHUNK * s + META_OFF) // N_META
            steps = []
            steps.append((self.meta_block(s), 0, b0 * 8, 2 | 4))
            for c in range(nc):
                b = b0 + c
                p16 = meta16 if c == 0 else b * 8 - 1
                n16 = (b + 1) * 8 if c < nc - 1 else 0
                steps.append((b, p16, n16, 1 | (2 if c < nc - 1 else 0)))
            if reverse:
                steps = steps[::-1]
            for i, (b, p, n, f) in enumerate(steps):
                blk.append(b); prev.append(p); nxt.append(n)
                flags.append(f | (8 if i == 0 else 0))
        for b in range(self.n_grid // CHUNK + self.n_seq, self.n_tok // CHUNK):
            blk.append(b); prev.append(0); nxt.append(0)
            flags.append(8 | 16)
        return [np.asarray(a, np.int32) for a in (blk, prev, nxt, flags)]


def _rms_rows(x, w):
    ms = jnp.mean(x * x, axis=-1, keepdims=True)
    return x * lax.rsqrt(ms + RMS_EPS) * w


def _norm_kernel(x_ref, w_ref, o_ref):
    o_ref[...] = _rms_rows(x_ref[...], w_ref[...]).astype(BF16)


def _norm(x, w):
    n = x.shape[0]
    return pl.pallas_call(
        _norm_kernel,
        grid=(n // TM,),
        in_specs=[pl.BlockSpec((TM, D_MODEL), lambda i: (i, 0)), _const_spec((1, D_MODEL))],
        out_specs=pl.BlockSpec((TM, D_MODEL), lambda i: (i, 0)),
        out_shape=jax.ShapeDtypeStruct((n, D_MODEL), BF16),
        compiler_params=_cparams(("parallel",)),
    )(x, w.reshape(1, D_MODEL))


def _inproj_kernel(h_ref, wqk_ref, wvzx_ref, wdt_ref, g_ref, nw_ref,
                   qk_ref, v_ref, z_ref, xbc_ref, dt_ref):
    h = h_ref[...]
    g = g_ref[...]
    n_qk = 2 * ATT_W // FF_CHUNK
    col = lambda c: slice(c * FF_CHUNK, (c + 1) * FF_CHUNK)
    y_next = jnp.dot(h, wqk_ref[:, col(0)], preferred_element_type=F32)
    for c in range(n_qk):
        y = y_next
        if c + 1 < n_qk:
            y_next = jnp.dot(h, wqk_ref[:, col(c + 1)], preferred_element_type=F32)
        ss = jnp.dot((y * y).astype(BF16), g, preferred_element_type=F32)
        inv = lax.rsqrt(ss * (1.0 / HEAD_DIM) + RMS_EPS)
        qk_ref[:, col(c)] = (y * inv * nw_ref[:, col(c)]).astype(BF16)
    for c in range(ATT_W // FF_CHUNK):
        sl = slice(c * FF_CHUNK, (c + 1) * FF_CHUNK)
        v_ref[:, sl] = jnp.dot(h, wvzx_ref[:, sl], preferred_element_type=F32).astype(BF16)
    for c in range(D_SSM // FF_CHUNK):
        sl = slice(c * FF_CHUNK, (c + 1) * FF_CHUNK)
        src = slice(ATT_W + c * FF_CHUNK, ATT_W + (c + 1) * FF_CHUNK)
        z_ref[:, sl] = jnp.dot(h, wvzx_ref[:, src], preferred_element_type=F32).astype(BF16)
    for c in range(CONV_CH // FF_CHUNK):
        sl = slice(c * FF_CHUNK, (c + 1) * FF_CHUNK)
        src = slice(ATT_W + D_SSM + c * FF_CHUNK, ATT_W + D_SSM + (c + 1) * FF_CHUNK)
        xbc_ref[:, sl] = jnp.dot(h, wvzx_ref[:, src], preferred_element_type=F32).astype(BF16)
    dt_ref[...] = jnp.dot(h, wdt_ref[...], preferred_element_type=F32)


def _head_sum_matrix():
    i = np.arange(FF_CHUNK)
    return jnp.asarray((i[:, None] // HEAD_DIM == i[None, :] // HEAD_DIM).astype(np.float32), BF16)


def _inproj(h, wqk, wvzx, wdt, nw):
    n = h.shape[0]
    row = lambda w: pl.BlockSpec((TM, w), lambda i: (i, 0))
    return pl.pallas_call(
        _inproj_kernel,
        grid=(n // TM,),
        in_specs=[row(D_MODEL), _const_spec(wqk.shape), _const_spec(wvzx.shape), _const_spec(wdt.shape),
                  _const_spec((FF_CHUNK, FF_CHUNK)), _const_spec((1, 2 * ATT_W))],
        out_specs=[row(2 * ATT_W), row(ATT_W), row(D_SSM), row(CONV_CH), row(LANES)],
        out_shape=[jax.ShapeDtypeStruct((n, 2 * ATT_W), BF16), jax.ShapeDtypeStruct((n, ATT_W), BF16),
                   jax.ShapeDtypeStruct((n, D_SSM), BF16), jax.ShapeDtypeStruct((n, CONV_CH), BF16),
                   jax.ShapeDtypeStruct((n, LANES), F32)],
        compiler_params=_cparams(("parallel",)),
    )(h, wqk, wvzx, wdt, _head_sum_matrix(), nw)


def _split_bf16(x):
    hi = x.astype(BF16)
    lo = (x - hi.astype(F32)).astype(BF16)
    return hi, lo


def _outproj_kernel(with_router, attn_ref, y_ref, x_ref, w_ref, aw_ref, nw_ref, *rest):
    if with_router:
        wr_ref, xo_ref, h_ref, route_ref = rest
    else:
        xo_ref, h_ref = rest
    acc = x_ref[...] + jnp.dot(y_ref[...], w_ref[ATT_W:ATT_W + D_SSM, :], preferred_element_type=F32)
    a = _rms_rows(attn_ref[...].astype(F32), aw_ref[...]).astype(BF16)
    acc = acc + jnp.dot(a, w_ref[0:ATT_W, :], preferred_element_type=F32)
    xo_ref[...] = acc
    h2 = _rms_rows(acc, nw_ref[...])
    h_ref[...] = h2.astype(BF16)
    if with_router:
        hh, hl = _split_bf16(h2)
        wr = wr_ref[...]
        lg = _nt_dot(wr, hh)
        logits = lg[0:ROUTER_ROWS] + lg[ROUTER_ROWS:] + _nt_dot(wr[0:ROUTER_ROWS], hl)
        sub = lax.broadcasted_iota(jnp.int32, logits.shape, 0)
        logits = jnp.where(sub < N_EXPERTS, logits, NEG_INF)
        m1 = jnp.max(logits, axis=0, keepdims=True)
        i1 = jnp.min(jnp.where(logits == m1, sub, ROUTER_ROWS), axis=0, keepdims=True)
        rest_l = jnp.where(sub == i1, NEG_INF, logits)
        m2 = jnp.max(rest_l, axis=0, keepdims=True)
        i2 = jnp.min(jnp.where(rest_l == m2, sub, ROUTER_ROWS), axis=0, keepdims=True)
        e = jnp.exp(m2 - m1)
        w1 = 1.0 / (1.0 + e)
        route_ref[...] = jnp.concatenate(
            [i1.astype(F32), i2.astype(F32), w1, e * w1, jnp.zeros((4, logits.shape[1]), F32)], axis=0)


def _outproj(attn, y, x, w_out, aw, nw, router=None):
    n = x.shape[0]
    row = lambda w: pl.BlockSpec((TM, w), lambda i: (i, 0))
    in_specs = [row(ATT_W), row(D_SSM), row(D_MODEL), _const_spec(w_out.shape),
                _const_spec((1, ATT_W)), _const_spec((1, D_MODEL))]
    out_specs = [row(D_MODEL), row(D_MODEL)]
    out_shape = [jax.ShapeDtypeStruct((n, D_MODEL), F32), jax.ShapeDtypeStruct((n, D_MODEL), BF16)]
    args = [attn, y, x, w_out, aw, nw]
    if router is not None:
        in_specs.append(_const_spec((2 * ROUTER_ROWS, D_MODEL)))
        out_specs.append(pl.BlockSpec((8, TM), lambda i: (0, i)))
        out_shape.append(jax.ShapeDtypeStruct((8, n), F32))
        args.append(router)
    return pl.pallas_call(
        functools.partial(_outproj_kernel, router is not None),
        grid=(n // TM,), in_specs=in_specs, out_specs=out_specs, out_shape=out_shape,
        compiler_params=_cparams(("parallel",)),
    )(*args)


def _silu(x):
    return x * (1.0 / (1.0 + jnp.exp(-x)))


def _ffn_kernel(h_ref, x_ref, wg_ref, wu_ref, wd_ref, nw_ref, xo_ref, ho_ref, act_ref):
    h = h_ref[...]
    for c in range(D_FF // FF_CHUNK):
        sl = slice(c * FF_CHUNK, (c + 1) * FF_CHUNK)
        g = jnp.dot(h, wg_ref[:, sl], preferred_element_type=F32)
        u = jnp.dot(h, wu_ref[:, sl], preferred_element_type=F32)
        act_ref[:, sl] = (_silu(g) * u).astype(BF16)
    xn = x_ref[...] + jnp.dot(act_ref[...], wd_ref[...], preferred_element_type=F32)
    xo_ref[...] = xn
    ho_ref[...] = _rms_rows(xn, nw_ref[...]).astype(BF16)


def _ffn(h, x, wg, wu, wd, nw_next):
    n = x.shape[0]
    row = lambda w: pl.BlockSpec((TM, w), lambda i: (i, 0))
    return pl.pallas_call(
        _ffn_kernel,
        grid=(n // TM,),
        in_specs=[row(D_MODEL), row(D_MODEL), _const_spec(wg.shape), _const_spec(wu.shape),
                  _const_spec(wd.shape), _const_spec((1, D_MODEL))],
        out_specs=[row(D_MODEL), row(D_MODEL)],
        out_shape=[jax.ShapeDtypeStruct((n, D_MODEL), F32), jax.ShapeDtypeStruct((n, D_MODEL), BF16)],
        scratch_shapes=[pltpu.VMEM((TM, D_FF), BF16)],
        compiler_params=_cparams(("parallel",)),
    )(h, x, wg, wu, wd, nw_next)


N_CB = GRID_W // NA_COLS
SPAN_START = (0, 0, 16, 32)
SPAN_W = (32, 48, 48, 32)
META_SLOT = (24, 0, 0, 0)
META_PER_ROW = N_META // 2
KTILE = 16
ATTN_DEPTH = 16


def _attn_bias_tables(rpb):
    i = np.arange(NA_ROWS)
    dr = np.clip(i[None, :] - i[:, None] + NA_ROWS - 1, 0, 2 * NA_ROWS - 2)
    tables = {}
    for j in range(N_CB):
        span = SPAN_W[j]
        qc = NA_COLS * j + np.arange(NA_COLS)
        kc = SPAN_START[j] + np.arange(span)
        st = np.clip(qc - NA_COLS // 2, 0, GRID_W - NA_COLS)
        valid = (kc[None, :] >= st[:, None]) & (kc[None, :] < st[:, None] + NA_COLS)
        dc = np.clip(kc[None, :] - qc[:, None] + NA_COLS - 1, 0, 2 * NA_COLS - 2)
        slot = (np.arange(span) >= META_SLOT[j]) & (np.arange(span) < META_SLOT[j] + META_PER_ROW)
        assert not valid[:, slot].any()
        b = rpb.astype(F32)[:, dr][:, :, :, dc]
        b = jnp.where(valid[None, None, None], b, NEG_INF)
        meta_vis = np.where(i[:, None] < 2, 0.0, NEG_INF) * np.ones((1, span))
        b = jnp.where(slot[None, None, None, None, :], jnp.asarray(meta_vis, F32)[None, None, :, None, :], b)
        b = b.reshape(N_PAIRS, 2, NA_ROWS, NA_ROWS, NA_COLS, span)
        b = jnp.transpose(b, (0, 2, 1, 4, 3, 5)).reshape(N_PAIRS, NA_ROWS, 2 * NA_COLS, NA_ROWS * span)
        tables.setdefault(span, []).append(b)
    return [jnp.stack(t, axis=1) for _, t in sorted(tables.items())]


def _pair_queries(q):
    lo = lax.broadcasted_iota(jnp.int32, q.shape, 1) < HEAD_DIM
    zero = jnp.zeros_like(q)
    return jnp.concatenate([jnp.where(lo, q, zero), jnp.where(lo, zero, q)], axis=0)


def _nt_dot(a, b):
    return lax.dot_general(a, b, (((1,), (1,)), ((), ())), preferred_element_type=F32)


def _attn_kernel(prev_ref, next_ref, flag_ref, mblk_ref,
                 q_ref, kp_ref, kc_ref, kn_ref, vp_ref, vc_ref, vn_ref, km_ref, vm_ref,
                 b32_ref, b48_ref, o_ref, *spans):
    del prev_ref, next_ref, mblk_ref
    flags = flag_ref[pl.program_id(1)]
    is_first = (flags & 1) != 0
    is_last = (flags & 2) != 0
    low = lax.broadcasted_iota(jnp.int32, (KTILE, LANES), 0) < META_PER_ROW

    for refs, bufs, m_ref in (((kp_ref, kc_ref, kn_ref), spans[:N_CB], km_ref),
                              ((vp_ref, vc_ref, vn_ref), spans[N_CB:], vm_ref)):
        m = m_ref[...]
        mf = m.astype(F32)
        m_swapped = jnp.concatenate([mf[META_PER_ROW:], mf[:META_PER_ROW]], axis=0).astype(BF16)
        for rr in range(BUF_ROWS):
            if rr < HALO_ROWS:
                ref, row = refs[0], rr
            elif rr < HALO_ROWS + BAND_ROWS:
                ref, row = refs[1], rr - HALO_ROWS
            else:
                ref, row = refs[2], rr - HALO_ROWS - BAND_ROWS
            for j in range(N_CB):
                span = SPAN_W[j]
                for t in range(span // KTILE):
                    off = row * GRID_W + SPAN_START[j] + t * KTILE
                    tile = ref[off:off + KTILE, :]
                    if t == META_SLOT[j] // KTILE:
                        if META_SLOT[j] % KTILE == 0:
                            tile = jnp.where(low, m if rr % 2 == 0 else m_swapped, tile)
                        else:
                            tile = jnp.where(low, tile, m_swapped if rr % 2 == 0 else m)
                    bufs[j][rr * span + t * KTILE:rr * span + (t + 1) * KTILE, :] = tile

    lane_lo = lax.broadcasted_iota(jnp.int32, (NA_COLS, LANES), 1) < HEAD_DIM
    starts = []
    for ri in range(BAND_ROWS):
        ls = jnp.int32(ri)
        ls = jnp.where(is_first, jnp.maximum(ls, HALO_ROWS), ls)
        ls = jnp.where(is_last, jnp.minimum(ls, BAND_ROWS - HALO_ROWS), ls)
        starts.append((ls, HALO_ROWS + ri - ls))

    def scores(u):
        ri, j = divmod(u, N_CB)
        ls, delta = starts[ri]
        span = SPAN_W[j]
        kwin = spans[j][pl.ds(pl.multiple_of(ls * span, KTILE), NA_ROWS * span), :]
        q0 = ri * GRID_W + j * NA_COLS
        bias_ref = b32_ref if span == SPAN_W[0] else b48_ref
        return _nt_dot(_pair_queries(q_ref[q0:q0 + NA_COLS, :]), kwin) + bias_ref[SPAN_W[:j].count(span), delta]

    def finish(u, s):
        ri, j = divmod(u, N_CB)
        span = SPAN_W[j]
        vwin = spans[N_CB + j][pl.ds(pl.multiple_of(starts[ri][0] * span, KTILE), NA_ROWS * span), :]
        p = jnp.exp(s - jnp.max(s, axis=-1, keepdims=True))
        l = jnp.sum(p, axis=-1, keepdims=True)
        o = jnp.dot(p.astype(BF16), vwin, preferred_element_type=F32) / l
        q0 = ri * GRID_W + j * NA_COLS
        o_ref[q0:q0 + NA_COLS, :] = jnp.where(lane_lo, o[:NA_COLS], o[NA_COLS:]).astype(BF16)

    n_units = BAND_ROWS * N_CB
    pending = {}
    for t in range(n_units + ATTN_DEPTH):
        if t < n_units:
            pending[t] = scores(t)
        if t >= ATTN_DEPTH:
            finish(t - ATTN_DEPTH, pending.pop(t - ATTN_DEPTH))


def _attn_meta_kernel(n_seq, qk_ref, v_ref, mbias_ref, alias_ref, o_ref):
    del alias_ref
    s = pl.program_id(0)
    o_ref[...] = jnp.zeros_like(o_ref)

    @pl.when(s < n_seq)
    def _():
        mbias = mbias_ref[...]
        for p in range(N_PAIRS):
            c = slice(p * LANES, (p + 1) * LANES)
            q = qk_ref[META_OFF:CHUNK, c]
            k = qk_ref[:, ATT_W + p * LANES:ATT_W + (p + 1) * LANES]
            qs = _pair_queries(q)
            sc = _nt_dot(qs, k) + mbias
            m = jnp.max(sc, axis=-1, keepdims=True)
            e = jnp.exp(sc - m)
            l = jnp.sum(e, axis=-1, keepdims=True)
            o = jnp.dot(e.astype(BF16), v_ref[:, c], preferred_element_type=F32) / l
            lo = lax.broadcasted_iota(jnp.int32, (N_META, LANES), 1) < HEAD_DIM
            o_ref[META_OFF:CHUNK, c] = jnp.where(lo, o[0:N_META], o[N_META:]).astype(BF16)


def _attention(lay, qk, v, rpb):
    n = qk.shape[0]
    prev, nxt, flags, mblk = lay.attn_tables()
    n_bands = lay.n_grid // BAND
    b32, b48 = _attn_bias_tables(rpb)
    lane = np.arange(LANES)
    mbias_meta = jnp.asarray(np.where(lane >= META_OFF, 0.0, NEG_INF)[None, :], F32)
    kcol = ATT_W // LANES
    halo = HALO_ROWS * GRID_W
    bias_spec = lambda a: pl.BlockSpec((None,) + a.shape[1:], lambda p, b, *_: (p, 0, 0, 0, 0))
    grid_spec = pltpu.PrefetchScalarGridSpec(
        num_scalar_prefetch=4,
        grid=(N_PAIRS, n_bands),
        in_specs=[
            pl.BlockSpec((BAND, LANES), lambda p, b, *_: (b, p)),
            pl.BlockSpec((halo, LANES), lambda p, b, pv, nx, fl, mb: (pv[b], kcol + p)),
            pl.BlockSpec((BAND, LANES), lambda p, b, pv, nx, fl, mb: (b, kcol + p)),
            pl.BlockSpec((halo, LANES), lambda p, b, pv, nx, fl, mb: (nx[b], kcol + p)),
            pl.BlockSpec((halo, LANES), lambda p, b, pv, nx, fl, mb: (pv[b], p)),
            pl.BlockSpec((BAND, LANES), lambda p, b, pv, nx, fl, mb: (b, p)),
            pl.BlockSpec((halo, LANES), lambda p, b, pv, nx, fl, mb: (nx[b], p)),
            pl.BlockSpec((N_META, LANES), lambda p, b, pv, nx, fl, mb: (mb[b], kcol + p)),
            pl.BlockSpec((N_META, LANES), lambda p, b, pv, nx, fl, mb: (mb[b], p)),
            bias_spec(b32), bias_spec(b48),
        ],
        out_specs=pl.BlockSpec((BAND, LANES), lambda p, b, *_: (b, p)),
        scratch_shapes=[pltpu.VMEM((BUF_ROWS * w, LANES), BF16) for w in SPAN_W] * 2,
    )
    attn = pl.pallas_call(
        _attn_kernel, grid_spec=grid_spec,
        out_shape=jax.ShapeDtypeStruct((n, ATT_W), BF16),
        compiler_params=_cparams(("arbitrary", "arbitrary")),
    )(jnp.asarray(prev), jnp.asarray(nxt), jnp.asarray(flags), jnp.asarray(mblk),
      qk, qk, qk, qk, v, v, v, qk, v, b32, b48)
    mb0 = lay.n_grid // CHUNK
    n_tail = (n - lay.n_grid) // CHUNK
    return pl.pallas_call(
        functools.partial(_attn_meta_kernel, lay.n_seq),
        grid=(n_tail,),
        in_specs=[pl.BlockSpec((CHUNK, 2 * ATT_W), lambda s: (mb0 + s, 0)),
                  pl.BlockSpec((CHUNK, ATT_W), lambda s: (mb0 + s, 0)),
                  _const_spec((1, LANES)),
                  pl.BlockSpec(memory_space=pl.ANY)],
        out_specs=pl.BlockSpec((CHUNK, ATT_W), lambda s: (mb0 + s, 0)),
        out_shape=jax.ShapeDtypeStruct((n, ATT_W), BF16),
        input_output_aliases={3: 0},
        compiler_params=_cparams(("arbitrary",)),
    )(qk, v, mbias_meta, attn)


def _softplus(x):
    return jnp.maximum(x, 0.0) + jnp.log(1.0 + jnp.exp(-jnp.abs(x)))


def _split3_bf16(x):
    hi = x.astype(BF16)
    r = x - hi.astype(F32)
    mid = r.astype(BF16)
    lo = (r - mid.astype(F32)).astype(BF16)
    return hi, mid, lo


def _expand_heads(w, e):
    hi, lo = _split_bf16(w)
    return jnp.dot(hi, e, preferred_element_type=F32) + jnp.dot(lo, e, preferred_element_type=F32)


def _ssd_kernel(reverse, blk_ref, prev_ref, next_ref, flag_ref, *refs):
    if reverse:
        (u_ref, dt_ref, dtb_ref, arow_ref, e_ref, tri_ref, yf_ref, z_ref, dsk_ref, nw_ref,
         o_ref, state, ybuf) = refs
    else:
        (xbc_ref, xp_ref, xn_ref, shift_ref, cw_ref, dt_ref, dtb_ref, arow_ref, e_ref, tri_ref,
         o_ref, u_ref, xe, state) = refs
        ybuf = o_ref
    del blk_ref, prev_ref, next_ref
    flags = flag_ref[pl.program_id(0)]
    is_meta = (flags & 4) != 0

    @pl.when((flags & 8) != 0)
    def _():
        state[...] = jnp.zeros_like(state)

    if reverse:
        u = u_ref[...].astype(F32)
    else:
        zero = jnp.zeros((N_META, CONV_CH), BF16)
        xe[0:N_META, :] = jnp.where((flags & 1) != 0, xp_ref[...], zero)
        xe[N_META:N_META + CHUNK, :] = xbc_ref[...]
        xe[N_META + CHUNK:, :] = jnp.where((flags & 2) != 0, xn_ref[...], zero)

        @pl.when(is_meta)
        def _():
            xe[0:N_META + META_OFF, :] = jnp.zeros((N_META + META_OFF, CONV_CH), BF16)

        shifted = jnp.dot(shift_ref[...], xe[...], preferred_element_type=F32)
        half = CONV_K // 2
        u = cw_ref[CONV_K:CONV_K + 1, :] + cw_ref[half:half + 1, :] * xe[N_META:N_META + CHUNK, :].astype(F32)
        for n, k in enumerate(k for k in range(CONV_K) if k != half):
            u = u + cw_ref[k:k + 1, :] * shifted[n * CHUNK:(n + 1) * CHUNK, :]
        u = _silu(u)
        u_ref[...] = u.astype(BF16)
    xs = u[:, 0:D_SSM]
    x_bf = xs.astype(BF16)

    rid = lax.broadcasted_iota(jnp.int32, (CHUNK, 1), 0)
    valid = rid >= jnp.where(is_meta, META_OFF, 0)
    dt = jnp.where(valid, _softplus(dt_ref[...] + dtb_ref[...]), 0.0)
    a = dt * arow_ref[...]
    tri = tri_ref[...]
    ah, am, al = _split3_bf16(a)
    acum = (jnp.dot(tri, ah, preferred_element_type=F32) + jnp.dot(tri, am, preferred_element_type=F32)
            + jnp.dot(tri, al, preferred_element_type=F32))
    acum = acum * LOG2E
    acum_t = acum.T
    dt_t = dt.T
    edge = 0 if reverse else CHUNK - 1
    a_tot = acum[edge:edge + 1, :]
    e = e_ref[...]
    w1e = jnp.dot((dt * jnp.exp2(a_tot - acum)).astype(BF16), e, preferred_element_type=F32)
    w2e = jnp.dot(jnp.exp2(acum).astype(BF16), e, preferred_element_type=F32)
    decay_row = _expand_heads(jnp.broadcast_to(jnp.exp2(a_tot), (8, LANES)), e)[0:1]
    xw = (xs * w1e).astype(BF16)

    li = lax.broadcasted_iota(jnp.int32, (CHUNK, CHUNK), 0)
    si = lax.broadcasted_iota(jnp.int32, (CHUNK, CHUNK), 1)
    causal = (si >= li) if reverse else (li >= si)
    lane_lo = si < HEAD_DIM
    ho = N_HEADS if reverse else 0
    heads_per_group = N_HEADS // SSM_GROUPS
    for g in range(SSM_GROUPS):
        gsl = slice(g * GROUP_W, (g + 1) * GROUP_W)
        bg = u[:, D_SSM + g * SSM_STATE:D_SSM + (g + 1) * SSM_STATE]
        cg = u[:, D_SSM + (SSM_GROUPS + g) * SSM_STATE:D_SSM + (SSM_GROUPS + g + 1) * SSM_STATE].astype(BF16)
        cb = _nt_dot(cg, bg.astype(BF16))
        st_old = state[:, gsl]
        y_off = jnp.dot(cg, st_old.astype(BF16), preferred_element_type=F32) * w2e[:, gsl]
        st_new = jnp.dot(bg.T.astype(BF16), xw[:, gsl], preferred_element_type=F32)
        state[:, gsl] = st_old * decay_row[:, gsl] + st_new
        for j in range(heads_per_group // 2):
            col = g * GROUP_W + j * LANES
            xpair = x_bf[:, col:col + LANES]
            zero = jnp.zeros_like(xpair)
            mhs = []
            for hh in range(2):
                hi = ho + g * heads_per_group + 2 * j + hh
                seg = acum[:, hi:hi + 1] - acum_t[hi:hi + 1, :]
                lm = jnp.exp2(jnp.where(causal, seg, NEG_INF))
                mhs.append((cb * lm * dt_t[hi:hi + 1, :]).astype(BF16))
            xsplit = jnp.concatenate([jnp.where(lane_lo, xpair, zero), jnp.where(lane_lo, zero, xpair)], axis=0)
            ybuf[:, col:col + LANES] = y_off[:, j * LANES:(j + 1) * LANES] + jnp.dot(
                jnp.concatenate(mhs, axis=1), xsplit, preferred_element_type=F32)
    if reverse:
        for g in range(SSM_GROUPS):
            gsl = slice(g * GROUP_W, (g + 1) * GROUP_W)
            yg = ybuf[:, gsl] + yf_ref[:, gsl] + dsk_ref[:, gsl] * xs[:, gsl]
            yg = yg * _silu(z_ref[:, gsl].astype(F32))
            ms = jnp.mean(yg * yg, axis=-1, keepdims=True)
            o_ref[:, gsl] = (yg * lax.rsqrt(ms + RMS_EPS) * nw_ref[:, gsl]).astype(BF16)

    @pl.when((flags & 16) != 0)
    def _():
        o_ref[...] = jnp.zeros_like(o_ref)


def _ssd_constants(reverse):
    ho = N_HEADS if reverse else 0
    e = np.zeros((LANES, D_SSM), np.float32)
    for h in range(N_HEADS):
        e[ho + h, h * HEAD_DIM:(h + 1) * HEAD_DIM] = 1.0
    i = np.arange(CHUNK)
    tri = (i[None, :] >= i[:, None]) if reverse else (i[:, None] >= i[None, :])
    return jnp.asarray(e, BF16), jnp.asarray(tri.astype(np.float32), BF16)


def _shift_matrices():
    s = np.zeros(((CONV_K - 1) * CHUNK, CHUNK + 2 * N_META), np.float32)
    i = np.arange(CHUNK)
    for n, k in enumerate(k for k in range(CONV_K) if k != CONV_K // 2):
        s[n * CHUNK + i, N_META + i + k - CONV_K // 2] = 1.0
    return jnp.asarray(s, BF16)


def _ssd_pass(lay, reverse, dt, dtb, arow, fwd_in=(), rev_in=()):
    n = dt.shape[0]
    blk, prev, nxt, flags = lay.ssd_tables(reverse)
    e, tri = _ssd_constants(reverse)
    cur = lambda w: pl.BlockSpec((CHUNK, w), lambda i, bk, pv, nx, fl: (bk[i], 0))
    common = [cur(LANES), _const_spec((1, LANES)), _const_spec((1, LANES)),
              _const_spec((LANES, D_SSM)), _const_spec((CHUNK, CHUNK))]
    state = pltpu.VMEM((SSM_STATE, D_SSM), F32)
    if reverse:
        u, y_f, z, dsk, nw = rev_in
        args = (u, dt, dtb, arow, e, tri, y_f, z, dsk, nw)
        in_specs = [cur(CONV_CH)] + common + [cur(D_SSM), cur(D_SSM), _const_spec((1, D_SSM)),
                                              _const_spec((1, D_SSM))]
        out_specs = cur(D_SSM)
        out_shape = jax.ShapeDtypeStruct((n, D_SSM), BF16)
        scratch = [state, pltpu.VMEM((CHUNK, D_SSM), F32)]
    else:
        xbc, cw = fwd_in
        shift = _shift_matrices()
        args = (xbc, xbc, xbc, shift, cw, dt, dtb, arow, e, tri)
        in_specs = [cur(CONV_CH),
                    pl.BlockSpec((N_META, CONV_CH), lambda i, bk, pv, nx, fl: (pv[i], 0)),
                    pl.BlockSpec((N_META, CONV_CH), lambda i, bk, pv, nx, fl: (nx[i], 0)),
                    _const_spec(shift.shape), _const_spec((8, CONV_CH))] + common
        out_specs = [cur(D_SSM), cur(CONV_CH)]
        out_shape = [jax.ShapeDtypeStruct((n, D_SSM), F32), jax.ShapeDtypeStruct((n, CONV_CH), BF16)]
        scratch = [pltpu.VMEM((CHUNK + 2 * N_META, CONV_CH), BF16), state]
    grid_spec = pltpu.PrefetchScalarGridSpec(
        num_scalar_prefetch=4, grid=(len(blk),), in_specs=in_specs,
        out_specs=out_specs, scratch_shapes=scratch)
    return pl.pallas_call(
        functools.partial(_ssd_kernel, reverse), grid_spec=grid_spec, out_shape=out_shape,
        compiler_params=_cparams(("arbitrary",)),
    )(jnp.asarray(blk), jnp.asarray(prev), jnp.asarray(nxt), jnp.asarray(flags), *args)


def _moe_plan(route_t):
    n = route_t.shape[1]
    flat_e = route_t[0:2].T.astype(jnp.int32).reshape(-1)
    onehot = (flat_e[:, None] == jnp.arange(N_EXPERTS, dtype=jnp.int32)[None, :]).astype(jnp.int32)
    csum = jnp.cumsum(onehot, axis=0)
    counts = csum[-1]
    rank = jnp.sum(csum * onehot, axis=1) - 1
    padded = (counts + TME - 1) // TME * TME
    pends = jnp.cumsum(padded)
    pstarts = pends - padded
    dest = jnp.sum(onehot * pstarts[None, :], axis=1) + rank
    n_rows = _round_up(2 * n + N_EXPERTS * TME, TME)
    n_blocks = n_rows // TME
    block_e = jnp.sum((jnp.arange(n_blocks, dtype=jnp.int32)[:, None] * TME >= pends[None, :]).astype(jnp.int32),
                      axis=1)
    block_e = jnp.minimum(block_e, N_EXPERTS - 1)
    n_used = (pends[-1] // TME).astype(jnp.int32).reshape(1)
    tail = jnp.minimum(pends[-1] + jnp.arange(N_PAD_RANGES - N_EXPERTS + 1, dtype=jnp.int32) * TME, n_rows)
    pad_rows = jnp.concatenate([pstarts + counts, tail[:-1], pends, tail[1:]]).astype(jnp.int32)
    return dest.astype(jnp.int32), block_e, n_used, pad_rows, n_rows


def _row_tile(r):
    return pl.multiple_of(r * ROW_TILE, ROW_TILE)


def _dispatch_kernel(pad_ref, dest_hbm, h_ref, xs_hbm, idx, stage, ztile, sem_i, sem, sem_z):
    @pl.when(pl.program_id(0) == 0)
    def _():
        ztile[...] = jnp.zeros_like(ztile)

        def zero_copy(r):
            return pltpu.make_async_copy(ztile, xs_hbm.at[pl.ds(_row_tile(r), ROW_TILE)], sem_z)

        for e in range(N_PAD_RANGES):
            lo, hi = pad_ref[e], pad_ref[N_PAD_RANGES + e]

            def fill(r, c):
                zero_copy(r).start()
                return c

            def drain_fill(r, c):
                zero_copy(0).wait()
                return c

            lax.fori_loop(lo, hi, fill, 0)
            lax.fori_loop(lo, hi, drain_fill, 0)

    cp = pltpu.make_async_copy(dest_hbm.at[pl.program_id(0)], idx, sem_i)
    cp.start()
    hf = h_ref[...].astype(F32)
    for j in range(ROW_TILE):
        stage[pl.ds(j, TM, stride=ROW_TILE), :] = hf[:, j * LANES:(j + 1) * LANES]
    cp.wait()

    def row_copy(src_row, dst_row):
        return pltpu.make_async_copy(stage.at[pl.ds(_row_tile(src_row), ROW_TILE)],
                                     xs_hbm.at[pl.ds(_row_tile(dst_row), ROW_TILE)], sem)

    def issue(b, c):
        for u in range(DMA_UNROLL):
            r = b * DMA_UNROLL + u
            row_copy(lax.shift_right_logical(r, 1), idx[r]).start(priority=u % 2)
        return c

    lax.fori_loop(0, RB // DMA_UNROLL, issue, 0)

    def drain(r, c):
        row_copy(0, 0).wait()
        return c

    lax.fori_loop(0, RB, drain, 0, unroll=DMA_UNROLL)


def _dispatch(h, dest, pad_rows, n_rows):
    n = h.shape[0]
    steps = n // TM
    grid_spec = pltpu.PrefetchScalarGridSpec(
        num_scalar_prefetch=1, grid=(steps,),
        in_specs=[pl.BlockSpec(memory_space=pl.ANY), pl.BlockSpec((TM, D_MODEL), lambda i, pad: (i, 0))],
        out_specs=pl.BlockSpec(memory_space=pl.ANY),
        scratch_shapes=[pltpu.SMEM((RB,), jnp.int32), pltpu.VMEM((TM * ROW_TILE, LANES), F32),
                        pltpu.VMEM((ROW_TILE, LANES), F32),
                        pltpu.SemaphoreType.DMA, pltpu.SemaphoreType.DMA, pltpu.SemaphoreType.DMA])
    return pl.pallas_call(
        _dispatch_kernel, grid_spec=grid_spec,
        out_shape=jax.ShapeDtypeStruct((n_rows * ROW_TILE, LANES), F32),
        compiler_params=_cparams(("arbitrary",)),
    )(pad_rows, dest.reshape(steps, RB), h)


def _expert_kernel(be_ref, nu_ref, x_ref, wg_ref, wu_ref, wd_ref, o_ref, xbf, act):
    del be_ref
    used = pl.program_id(0) < nu_ref[0]

    @pl.when(used)
    def _():
        for j in range(ROW_TILE):
            xbf[:, j * LANES:(j + 1) * LANES] = x_ref[pl.ds(j, TME, stride=ROW_TILE), :].astype(BF16)
        x = xbf[...]
        for c in range(D_FF_EXPERT // FF_CHUNK):
            sl = slice(c * FF_CHUNK, (c + 1) * FF_CHUNK)
            g = jnp.dot(x, wg_ref[:, sl], preferred_element_type=F32)
            u = jnp.dot(x, wu_ref[:, sl], preferred_element_type=F32)
            act[:, sl] = (_silu(g) * u).astype(BF16)
        out = jnp.dot(act[...], wd_ref[...], preferred_element_type=F32)
        for j in range(ROW_TILE):
            o_ref[pl.ds(j, TME, stride=ROW_TILE), :] = out[:, j * LANES:(j + 1) * LANES]

    @pl.when(jnp.logical_not(used))
    def _():
        o_ref[...] = jnp.zeros_like(o_ref)


def _experts(xs, block_e, n_used, wg, wu, wd):
    n_blocks = xs.shape[0] // (TME * ROW_TILE)
    rows = pl.BlockSpec((TME * ROW_TILE, LANES), lambda i, be, nu: (i, 0))
    resident = lambda shape: pl.BlockSpec((None,) + shape, lambda i, be, nu: (be[i], 0, 0),
                                          pipeline_mode=pl.Buffered(1))
    grid_spec = pltpu.PrefetchScalarGridSpec(
        num_scalar_prefetch=2, grid=(n_blocks,),
        in_specs=[pl.BlockSpec((TME * ROW_TILE, LANES), lambda i, be, nu: (jnp.minimum(i, nu[0] - 1), 0)),
                  resident((D_MODEL, D_FF_EXPERT)), resident((D_MODEL, D_FF_EXPERT)),
                  resident((D_FF_EXPERT, D_MODEL))],
        out_specs=rows,
        scratch_shapes=[pltpu.VMEM((TME, D_MODEL), BF16), pltpu.VMEM((TME, D_FF_EXPERT), BF16)])
    return pl.pallas_call(
        _expert_kernel, grid_spec=grid_spec,
        out_shape=jax.ShapeDtypeStruct(xs.shape, F32),
        compiler_params=_cparams(("arbitrary",)),
    )(block_e, n_used, xs, wg, wu, wd)


def _combine_kernel(dest_hbm, eo_hbm, x_ref, route_ref, nw_ref, xo_ref, ho_ref, idx, gbuf, sem_i, sem):
    cp = pltpu.make_async_copy(dest_hbm.at[pl.program_id(0)], idx, sem_i)
    cp.start()
    cp.wait()

    def row_copy(src_row, k, t):
        return pltpu.make_async_copy(eo_hbm.at[pl.ds(_row_tile(src_row), ROW_TILE)],
                                     gbuf.at[k, pl.ds(_row_tile(t), ROW_TILE)], sem)

    def issue(b, c):
        for u in range(DMA_UNROLL):
            r = b * DMA_UNROLL + u
            row_copy(idx[r], u % 2, b * (DMA_UNROLL // 2) + u // 2).start(priority=u % 2)
        return c

    lax.fori_loop(0, 2 * TC // DMA_UNROLL, issue, 0)

    def drain(r, c):
        row_copy(0, 0, 0).wait()
        return c

    lax.fori_loop(0, 2 * TC, drain, 0, unroll=DMA_UNROLL)
    w1 = route_ref[:, 0:1]
    w2 = route_ref[:, 1:2]
    for j in range(ROW_TILE):
        sl = slice(j * LANES, (j + 1) * LANES)
        xo_ref[:, sl] = x_ref[:, sl] + (gbuf[0, pl.ds(j, TC, stride=ROW_TILE), :] * w1
                                        + gbuf[1, pl.ds(j, TC, stride=ROW_TILE), :] * w2)
    ho_ref[...] = _rms_rows(xo_ref[...], nw_ref[...]).astype(BF16)


def _combine(eo, dest, x, route, nw_next):
    n = x.shape[0]
    steps = n // TC
    row = lambda w: pl.BlockSpec((TC, w), lambda i: (i, 0))
    return pl.pallas_call(
        _combine_kernel,
        grid=(steps,),
        in_specs=[pl.BlockSpec(memory_space=pl.ANY), pl.BlockSpec(memory_space=pl.ANY),
                  row(D_MODEL), row(2), _const_spec((1, D_MODEL))],
        out_specs=[row(D_MODEL), row(D_MODEL)],
        out_shape=[jax.ShapeDtypeStruct((n, D_MODEL), F32), jax.ShapeDtypeStruct((n, D_MODEL), BF16)],
        scratch_shapes=[pltpu.SMEM((2 * TC,), jnp.int32), pltpu.VMEM((2, TC * ROW_TILE, LANES), F32),
                        pltpu.SemaphoreType.DMA, pltpu.SemaphoreType.DMA],
        compiler_params=_cparams(("arbitrary",)),
    )(dest.reshape(steps, 2 * TC), eo, x, route, nw_next)


def _trunk(lay, x, p):
    depth = p['norm1_w'].shape[0]
    row = lambda v: v.reshape(1, -1).astype(F32)
    h = _norm(x, p['norm1_w'][0])
    for l in range(depth):
        w_in = p['w_in'][l]
        wqk = w_in[:, 0:2 * ATT_W].astype(BF16)
        wvzx = w_in[:, 2 * ATT_W:3 * ATT_W + D_SSM + CONV_CH].astype(BF16)
        wdt = jnp.pad(w_in[:, 3 * ATT_W + D_SSM + CONV_CH:], ((0, 0), (0, LANES - 2 * N_HEADS))).astype(BF16)
        nw_qk = jnp.concatenate([jnp.tile(p['q_norm_w'][l].astype(F32), N_HEADS) * (HEAD_DIM ** -0.5),
                                 jnp.tile(p['k_norm_w'][l].astype(F32), N_HEADS)]).reshape(1, -1)
        qk, v, z, xbc, dt_raw = _inproj(h, wqk, wvzx, wdt, nw_qk)

        attn = _attention(lay, qk, v, p['rpb'][l])

        cw = jnp.concatenate([p['conv_w'][l].astype(F32).T, p['conv_b'][l].astype(F32)[None],
                              jnp.zeros((8 - CONV_K - 1, CONV_CH), F32)], axis=0)
        pad = jnp.zeros((LANES - 2 * N_HEADS,), F32)
        dtb = jnp.concatenate([p['dt_bias'][l].astype(F32).reshape(-1), pad]).reshape(1, -1)
        arow = jnp.concatenate([-jnp.exp(p['a_log'][l].astype(F32)).reshape(-1), pad]).reshape(1, -1)
        y_f, u = _ssd_pass(lay, False, dt_raw, dtb, arow, fwd_in=(xbc, cw))
        dsk = jnp.repeat(p['d_skip'][l].astype(F32), HEAD_DIM).reshape(1, -1)
        y = _ssd_pass(lay, True, dt_raw, dtb, arow, rev_in=(u, y_f, z, dsk, row(p['ssm_norm_w'][l])))

        w_out = p['w_out'][l].astype(BF16)
        nw_next = row(p['norm1_w'][l + 1]) if l + 1 < depth else jnp.ones((1, D_MODEL), F32)
        j = l // 2
        if l % 2 == 0:
            x, h2 = _outproj(attn, y, x, w_out, row(p['attn_out_norm_w'][l]), row(p['norm2_w'][l]))
            x, h = _ffn(h2, x, p['ffn_w_gate'][j].astype(BF16), p['ffn_w_up'][j].astype(BF16),
                        p['ffn_w_down'][j].astype(BF16), nw_next)
        else:
            wr = jnp.pad(p['moe_router'][j].astype(F32).T, ((0, ROUTER_ROWS - N_EXPERTS), (0, 0)))
            x, h2, route_t = _outproj(attn, y, x, w_out, row(p['attn_out_norm_w'][l]), row(p['norm2_w'][l]),
                                      router=jnp.concatenate(_split_bf16(wr), axis=0))
            dest, block_e, n_used, pad_rows, n_rows = _moe_plan(route_t)
            xs = _dispatch(h2, dest, pad_rows, n_rows)
            eo = _experts(xs, block_e, n_used, p['moe_w_gate'][j].astype(BF16),
                          p['moe_w_up'][j].astype(BF16), p['moe_w_down'][j].astype(BF16))
            x, h = _combine(eo, dest, x, route_t[2:4].T, nw_next)
    return x


def kernel(x_prompt, x_sample, meta_tokens, norm1_w, w_in, q_norm_w, k_norm_w, rpb, attn_out_norm_w,
           conv_w, conv_b, dt_bias, a_log, d_skip, ssm_norm_w, w_out, norm2_w,
           ffn_w_gate, ffn_w_up, ffn_w_down, moe_router, moe_w_gate, moe_w_up, moe_w_down):
    p = dict(norm1_w=norm1_w, w_in=w_in, q_norm_w=q_norm_w, k_norm_w=k_norm_w, rpb=rpb,
             attn_out_norm_w=attn_out_norm_w, conv_w=conv_w, conv_b=conv_b, dt_bias=dt_bias, a_log=a_log,
             d_skip=d_skip, ssm_norm_w=ssm_norm_w, w_out=w_out, norm2_w=norm2_w, ffn_w_gate=ffn_w_gate,
             ffn_w_up=ffn_w_up, ffn_w_down=ffn_w_down, moe_router=moe_router, moe_w_gate=moe_w_gate,
             moe_w_up=moe_w_up, moe_w_down=moe_w_down)
    groups = (x_prompt, x_sample)
    lay = _Layout([g.shape[1] for g in groups for _ in range(g.shape[0])])
    meta_block = jnp.concatenate([jnp.zeros((META_OFF, D_MODEL), F32), meta_tokens.astype(F32)], axis=0)
    tail = lay.n_tok - lay.n_grid - CHUNK * lay.n_seq
    x = jnp.concatenate([g.reshape(-1, D_MODEL).astype(F32) for g in groups]
                        + [jnp.tile(meta_block, (lay.n_seq, 1)), jnp.zeros((tail, D_MODEL), F32)], axis=0)
    x = _trunk(lay, x, p)
    outs, off = [], 0
    for g in groups:
        cnt = g.shape[0] * g.shape[1]
        outs.append(x[off:off + cnt].reshape(g.shape).astype(g.dtype))
        off += cnt
    return tuple(outs)
```

```python
import functools

import numpy as np
import jax
import jax.numpy as jnp
from jax import lax
from jax.experimental import pallas as pl
from jax.experimental.pallas import tpu as pltpu

F32 = jnp.float32
BF16 = jnp.bfloat16

D_MODEL = 1024
N_META = 16
GRID_W = 64
NA_ROWS = 8
NA_COLS = 16
N_HEADS = 16
HEAD_DIM = 64
N_PAIRS = N_HEADS // 2
ATT_W = N_HEADS * HEAD_DIM
D_SSM = 1024
SSM_GROUPS = 2
SSM_STATE = 128
GROUP_W = D_SSM // SSM_GROUPS
CONV_K = 5
CONV_CH = D_SSM + 2 * SSM_GROUPS * SSM_STATE
D_FF = 2816
N_EXPERTS = 8
D_FF_EXPERT = 3584
RMS_EPS = 1e-6
NEG_INF = -1e30
LOG2E = 1.4426950408889634

LANES = 128
ROW_TILE = D_MODEL // LANES
CHUNK = 128
META_OFF = CHUNK - N_META
TM = 512
BAND_ROWS = 16
BAND = BAND_ROWS * GRID_W
HALO_ROWS = NA_ROWS // 2
BUF_ROWS = BAND_ROWS + NA_ROWS - 1
FF_CHUNK = 256
TME = 512
ROUTER_ROWS = 16
N_PAD_RANGES = 2 * N_EXPERTS + 1
RB = 2 * TM
TC = 256
DMA_UNROLL = 8
VMEM_LIMIT = 56 * 1024 * 1024


def _cparams(sem):
    return pltpu.CompilerParams(dimension_semantics=sem, vmem_limit_bytes=VMEM_LIMIT)


def _round_up(a, b):
    return (a + b - 1) // b * b


def _const_spec(shape):
    nd = len(shape)
    return pl.BlockSpec(shape, lambda *_: (0,) * nd)


class _Layout:
    def __init__(self, seq_lens):
        self.seq_lens = tuple(seq_lens)
        self.n_seq = len(seq_lens)
        self.n_grid = sum(seq_lens)
        self.starts = np.concatenate([[0], np.cumsum(seq_lens)[:-1]]).astype(np.int64)
        self.n_tok = _round_up(self.n_grid + CHUNK * self.n_seq, TM)
        assert all(s % BAND == 0 and s // GRID_W >= NA_ROWS for s in seq_lens)

    def meta_block(self, s):
        return self.n_grid // CHUNK + s

    def attn_tables(self):
        prev, nxt, flags, mblk = [], [], [], []
        for s, (st, ln) in enumerate(zip(self.starts, self.seq_lens)):
            lo, hi = st // BAND, (st + ln) // BAND
            for b in range(lo, hi):
                per = BAND_ROWS // HALO_ROWS
                prev.append(max(b * per - 1, lo * per))
                nxt.append(min((b + 1) * per, hi * per - 1))
                flags.append((1 if b == lo else 0) | (2 if b == hi - 1 else 0))
                mblk.append((self.n_grid + CHUNK * s + META_OFF) // N_META)
        return [np.asarray(a, np.int32) for a in (prev, nxt, flags, mblk)]

    def ssd_tables(self, reverse):
        blk, prev, nxt, flags = [], [], [], []
        for s, (st, ln) in enumerate(zip(self.starts, self.seq_lens)):
            nc = ln // CHUNK
            b0 = st // CHUNK
            meta16 = (self.n_grid + CHUNK * s + META_OFF) // N_META
            steps = []
            steps.append((self.meta_block(s), 0, b0 * 8, 2 | 4))
            for c in range(nc):
                b = b0 + c
                p16 = meta16 if c == 0 else b * 8 - 1
                n16 = (b + 1) * 8 if c < nc - 1 else 0
                steps.append((b, p16, n16, 1 | (2 if c < nc - 1 else 0)))
            if reverse:
                steps = steps[::-1]
            for i, (b, p, n, f) in enumerate(steps):
                blk.append(b); prev.append(p); nxt.append(n)
                flags.append(f | (8 if i == 0 else 0))
        for b in range(self.n_grid // CHUNK + self.n_seq, self.n_tok // CHUNK):
            blk.append(b); prev.append(0); nxt.append(0)
            flags.append(8 | 16)
        return [np.asarray(a, np.int32) for a in (blk, prev, nxt, flags)]


def _rms_rows(x, w):
    ms = jnp.mean(x * x, axis=-1, keepdims=True)
    return x * lax.rsqrt(ms + RMS_EPS) * w


def _norm_kernel(x_ref, w_ref, o_ref):
    o_ref[...] = _rms_rows(x_ref[...], w_ref[...]).astype(BF16)


def _norm(x, w):
    n = x.shape[0]
    return pl.pallas_call(
        _norm_kernel,
        grid=(n // TM,),
        in_specs=[pl.BlockSpec((TM, D_MODEL), lambda i: (i, 0)), _const_spec((1, D_MODEL))],
        out_specs=pl.BlockSpec((TM, D_MODEL), lambda i: (i, 0)),
        out_shape=jax.ShapeDtypeStruct((n, D_MODEL), BF16),
        compiler_params=_cparams(("parallel",)),
    )(x, w.reshape(1, D_MODEL))


def _inproj_kernel(h_ref, wqk_ref, wvzx_ref, wdt_ref, g_ref, nw_ref,
                   qk_ref, v_ref, z_ref, xbc_ref, dt_ref):
    h = h_ref[...]
    g = g_ref[...]
    n_qk = 2 * ATT_W // FF_CHUNK
    col = lambda c: slice(c * FF_CHUNK, (c + 1) * FF_CHUNK)
    y_next = jnp.dot(h, wqk_ref[:, col(0)], preferred_element_type=F32)
    for c in range(n_qk):
        y = y_next
        if c + 1 < n_qk:
            y_next = jnp.dot(h, wqk_ref[:, col(c + 1)], preferred_element_type=F32)
        ss = jnp.dot((y * y).astype(BF16), g, preferred_element_type=F32)
        inv = lax.rsqrt(ss * (1.0 / HEAD_DIM) + RMS_EPS)
        qk_ref[:, col(c)] = (y * inv * nw_ref[:, col(c)]).astype(BF16)
    for c in range(ATT_W // FF_CHUNK):
        sl = slice(c * FF_CHUNK, (c + 1) * FF_CHUNK)
        v_ref[:, sl] = jnp.dot(h, wvzx_ref[:, sl], preferred_element_type=F32).astype(BF16)
    for c in range(D_SSM // FF_CHUNK):
        sl = slice(c * FF_CHUNK, (c + 1) * FF_CHUNK)
        src = slice(ATT_W + c * FF_CHUNK, ATT_W + (c + 1) * FF_CHUNK)
        z_ref[:, sl] = jnp.dot(h, wvzx_ref[:, src], preferred_element_type=F32).astype(BF16)
    for c in range(CONV_CH // FF_CHUNK):
        sl = slice(c * FF_CHUNK, (c + 1) * FF_CHUNK)
        src = slice(ATT_W + D_SSM + c * FF_CHUNK, ATT_W + D_SSM + (c + 1) * FF_CHUNK)
        xbc_ref[:, sl] = jnp.dot(h, wvzx_ref[:, src], preferred_element_type=F32).astype(BF16)
    dt_ref[...] = jnp.dot(h, wdt_ref[...], preferred_element_type=F32)


def _head_sum_matrix():
    i = np.arange(FF_CHUNK)
    return jnp.asarray((i[:, None] // HEAD_DIM == i[None, :] // HEAD_DIM).astype(np.float32), BF16)


def _inproj(h, wqk, wvzx, wdt, nw):
    n = h.shape[0]
    row = lambda w: pl.BlockSpec((TM, w), lambda i: (i, 0))
    return pl.pallas_call(
        _inproj_kernel,
        grid=(n // TM,),
        in_specs=[row(D_MODEL), _const_spec(wqk.shape), _const_spec(wvzx.shape), _const_spec(wdt.shape),
                  _const_spec((FF_CHUNK, FF_CHUNK)), _const_spec((1, 2 * ATT_W))],
        out_specs=[row(2 * ATT_W), row(ATT_W), row(D_SSM), row(CONV_CH), row(LANES)],
        out_shape=[jax.ShapeDtypeStruct((n, 2 * ATT_W), BF16), jax.ShapeDtypeStruct((n, ATT_W), BF16),
                   jax.ShapeDtypeStruct((n, D_SSM), BF16), jax.ShapeDtypeStruct((n, CONV_CH), BF16),
                   jax.ShapeDtypeStruct((n, LANES), F32)],
        compiler_params=_cparams(("parallel",)),
    )(h, wqk, wvzx, wdt, _head_sum_matrix(), nw)


def _split_bf16(x):
    hi = x.astype(BF16)
    lo = (x - hi.astype(F32)).astype(BF16)
    return hi, lo


def _outproj_kernel(with_router, attn_ref, y_ref, x_ref, w_ref, aw_ref, nw_ref, *rest):
    if with_router:
        wr_ref, xo_ref, h_ref, route_ref = rest
    else:
        xo_ref, h_ref = rest
    acc = x_ref[...] + jnp.dot(y_ref[...], w_ref[ATT_W:ATT_W + D_SSM, :], preferred_element_type=F32)
    a = _rms_rows(attn_ref[...].astype(F32), aw_ref[...]).astype(BF16)
    acc = acc + jnp.dot(a, w_ref[0:ATT_W, :], preferred_element_type=F32)
    xo_ref[...] = acc
    h2 = _rms_rows(acc, nw_ref[...])
    h_ref[...] = h2.astype(BF16)
    if with_router:
        hh, hl = _split_bf16(h2)
        wr = wr_ref[...]
        lg = _nt_dot(wr, hh)
        logits = lg[0:ROUTER_ROWS] + lg[ROUTER_ROWS:] + _nt_dot(wr[0:ROUTER_ROWS], hl)
        sub = lax.broadcasted_iota(jnp.int32, logits.shape, 0)
        logits = jnp.where(sub < N_EXPERTS, logits, NEG_INF)
        m1 = jnp.max(logits, axis=0, keepdims=True)
        i1 = jnp.min(jnp.where(logits == m1, sub, ROUTER_ROWS), axis=0, keepdims=True)
        rest_l = jnp.where(sub == i1, NEG_INF, logits)
        m2 = jnp.max(rest_l, axis=0, keepdims=True)
        i2 = jnp.min(jnp.where(rest_l == m2, sub, ROUTER_ROWS), axis=0, keepdims=True)
        e = jnp.exp(m2 - m1)
        w1 = 1.0 / (1.0 + e)
        route_ref[...] = jnp.concatenate(
            [i1.astype(F32), i2.astype(F32), w1, e * w1, jnp.zeros((4, logits.shape[1]), F32)], axis=0)


def _outproj(attn, y, x, w_out, aw, nw, router=None):
    n = x.shape[0]
    row = lambda w: pl.BlockSpec((TM, w), lambda i: (i, 0))
    in_specs = [row(ATT_W), row(D_SSM), row(D_MODEL), _const_spec(w_out.shape),
                _const_spec((1, ATT_W)), _const_spec((1, D_MODEL))]
    out_specs = [row(D_MODEL), row(D_MODEL)]
    out_shape = [jax.ShapeDtypeStruct((n, D_MODEL), F32), jax.ShapeDtypeStruct((n, D_MODEL), BF16)]
    args = [attn, y, x, w_out, aw, nw]
    if router is not None:
        in_specs.append(_const_spec((2 * ROUTER_ROWS, D_MODEL)))
        out_specs.append(pl.BlockSpec((8, TM), lambda i: (0, i)))
        out_shape.append(jax.ShapeDtypeStruct((8, n), F32))
        args.append(router)
    return pl.pallas_call(
        functools.partial(_outproj_kernel, router is not None),
        grid=(n // TM,), in_specs=in_specs, out_specs=out_specs, out_shape=out_shape,
        compiler_params=_cparams(("parallel",)),
    )(*args)


def _silu(x):
    return x * (1.0 / (1.0 + jnp.exp(-x)))


def _ffn_kernel(h_ref, x_ref, wg_ref, wu_ref, wd_ref, nw_ref, xo_ref, ho_ref, act_ref):
    h = h_ref[...]
    for c in range(D_FF // FF_CHUNK):
        sl = slice(c * FF_CHUNK, (c + 1) * FF_CHUNK)
        g = jnp.dot(h, wg_ref[:, sl], preferred_element_type=F32)
        u = jnp.dot(h, wu_ref[:, sl], preferred_element_type=F32)
        act_ref[:, sl] = (_silu(g) * u).astype(BF16)
    xn = x_ref[...] + jnp.dot(act_ref[...], wd_ref[...], preferred_element_type=F32)
    xo_ref[...] = xn
    ho_ref[...] = _rms_rows(xn, nw_ref[...]).astype(BF16)


def _ffn(h, x, wg, wu, wd, nw_next):
    n = x.shape[0]
    row = lambda w: pl.BlockSpec((TM, w), lambda i: (i, 0))
    return pl.pallas_call(
        _ffn_kernel,
        grid=(n // TM,),
        in_specs=[row(D_MODEL), row(D_MODEL), _const_spec(wg.shape), _const_spec(wu.shape),
                  _const_spec(wd.shape), _const_spec((1, D_MODEL))],
        out_specs=[row(D_MODEL), row(D_MODEL)],
        out_shape=[jax.ShapeDtypeStruct((n, D_MODEL), F32), jax.ShapeDtypeStruct((n, D_MODEL), BF16)],
        scratch_shapes=[pltpu.VMEM((TM, D_FF), BF16)],
        compiler_params=_cparams(("parallel",)),
    )(h, x, wg, wu, wd, nw_next)


N_CB = GRID_W // NA_COLS
SPAN_START = (0, 0, 16, 32)
SPAN_W = (32, 48, 48, 32)
META_SLOT = (24, 0, 0, 0)
META_PER_ROW = N_META // 2
KTILE = 16
ATTN_DEPTH = 16


def _attn_bias_tables(rpb):
    i = np.arange(NA_ROWS)
    dr = np.clip(i[None, :] - i[:, None] + NA_ROWS - 1, 0, 2 * NA_ROWS - 2)
    tables = {}
    for j in range(N_CB):
        span = SPAN_W[j]
        qc = NA_COLS * j + np.arange(NA_COLS)
        kc = SPAN_START[j] + np.arange(span)
        st = np.clip(qc - NA_COLS // 2, 0, GRID_W - NA_COLS)
        valid = (kc[None, :] >= st[:, None]) & (kc[None, :] < st[:, None] + NA_COLS)
        dc = np.clip(kc[None, :] - qc[:, None] + NA_COLS - 1, 0, 2 * NA_COLS - 2)
        slot = (np.arange(span) >= META_SLOT[j]) & (np.arange(span) < META_SLOT[j] + META_PER_ROW)
        assert not valid[:, slot].any()
        b = rpb.astype(F32)[:, dr][:, :, :, dc]
        b = jnp.where(valid[None, None, None], b, NEG_INF)
        meta_vis = np.where(i[:, None] < 2, 0.0, NEG_INF) * np.ones((1, span))
        b = jnp.where(slot[None, None, None, None, :], jnp.asarray(meta_vis, F32)[None, None, :, None, :], b)
        b = b.reshape(N_PAIRS, 2, NA_ROWS, NA_ROWS, NA_COLS, span)
        b = jnp.transpose(b, (0, 2, 1, 4, 3, 5)).reshape(N_PAIRS, NA_ROWS, 2 * NA_COLS, NA_ROWS * span)
        tables.setdefault(span, []).append(b)
    return [jnp.stack(t, axis=1) for _, t in sorted(tables.items())]


def _pair_queries(q):
    lo = lax.broadcasted_iota(jnp.int32, q.shape, 1) < HEAD_DIM
    zero = jnp.zeros_like(q)
    return jnp.concatenate([jnp.where(lo, q, zero), jnp.where(lo, zero, q)], axis=0)


def _nt_dot(a, b):
    return lax.dot_general(a, b, (((1,), (1,)), ((), ())), preferred_element_type=F32)


def _attn_kernel(prev_ref, next_ref, flag_ref, mblk_ref,
                 q_ref, kp_ref, kc_ref, kn_ref, vp_ref, vc_ref, vn_ref, km_ref, vm_ref,
                 b32_ref, b48_ref, o_ref, *spans):
    del prev_ref, next_ref, mblk_ref
    flags = flag_ref[pl.program_id(1)]
    is_first = (flags & 1) != 0
    is_last = (flags & 2) != 0
    low = lax.broadcasted_iota(jnp.int32, (KTILE, LANES), 0) < META_PER_ROW

    for refs, bufs, m_ref in (((kp_ref, kc_ref, kn_ref), spans[:N_CB], km_ref),
                              ((vp_ref, vc_ref, vn_ref), spans[N_CB:], vm_ref)):
        m = m_ref[...]
        mf = m.astype(F32)
        m_swapped = jnp.concatenate([mf[META_PER_ROW:], mf[:META_PER_ROW]], axis=0).astype(BF16)
        for rr in range(BUF_ROWS):
            if rr < HALO_ROWS:
                ref, row = refs[0], rr
            elif rr < HALO_ROWS + BAND_ROWS:
                ref, row = refs[1], rr - HALO_ROWS
            else:
                ref, row = refs[2], rr - HALO_ROWS - BAND_ROWS
            for j in range(N_CB):
                span = SPAN_W[j]
                for t in range(span // KTILE):
                    off = row * GRID_W + SPAN_START[j] + t * KTILE
                    tile = ref[off:off + KTILE, :]
                    if t == META_SLOT[j] // KTILE:
                        if META_SLOT[j] % KTILE == 0:
                            tile = jnp.where(low, m if rr % 2 == 0 else m_swapped, tile)
                        else:
                            tile = jnp.where(low, tile, m_swapped if rr % 2 == 0 else m)
                    bufs[j][rr * span + t * KTILE:rr * span + (t + 1) * KTILE, :] = tile

    lane_lo = lax.broadcasted_iota(jnp.int32, (NA_COLS, LANES), 1) < HEAD_DIM
    starts = []
    for ri in range(BAND_ROWS):
        ls = jnp.int32(ri)
        ls = jnp.where(is_first, jnp.maximum(ls, HALO_ROWS), ls)
        ls = jnp.where(is_last, jnp.minimum(ls, BAND_ROWS - HALO_ROWS), ls)
        starts.append((ls, HALO_ROWS + ri - ls))

    def scores(u):
        ri, j = divmod(u, N_CB)
        ls, delta = starts[ri]
        span = SPAN_W[j]
        kwin = spans[j][pl.ds(pl.multiple_of(ls * span, KTILE), NA_ROWS * span), :]
        q0 = ri * GRID_W + j * NA_COLS
        bias_ref = b32_ref if span == SPAN_W[0] else b48_ref
        return _nt_dot(_pair_queries(q_ref[q0:q0 + NA_COLS, :]), kwin) + bias_ref[SPAN_W[:j].count(span), delta]

    def finish(u, s):
        ri, j = divmod(u, N_CB)
        span = SPAN_W[j]
        vwin = spans[N_CB + j][pl.ds(pl.multiple_of(starts[ri][0] * span, KTILE), NA_ROWS * span), :]
        p = jnp.exp(s - jnp.max(s, axis=-1, keepdims=True))
        l = jnp.sum(p, axis=-1, keepdims=True)
        o = jnp.dot(p.astype(BF16), vwin, preferred_element_type=F32) / l
        q0 = ri * GRID_W + j * NA_COLS
        o_ref[q0:q0 + NA_COLS, :] = jnp.where(lane_lo, o[:NA_COLS], o[NA_COLS:]).astype(BF16)

    n_units = BAND_ROWS * N_CB
    pending = {}
    for t in range(n_units + ATTN_DEPTH):
        if t < n_units:
            pending[t] = scores(t)
        if t >= ATTN_DEPTH:
            finish(t - ATTN_DEPTH, pending.pop(t - ATTN_DEPTH))


def _attn_meta_kernel(n_seq, qk_ref, v_ref, mbias_ref, alias_ref, o_ref):
    del alias_ref
    s = pl.program_id(0)
    o_ref[...] = jnp.zeros_like(o_ref)

    @pl.when(s < n_seq)
    def _():
        mbias = mbias_ref[...]
        for p in range(N_PAIRS):
            c = slice(p * LANES, (p + 1) * LANES)
            q = qk_ref[META_OFF:CHUNK, c]
            k = qk_ref[:, ATT_W + p * LANES:ATT_W + (p + 1) * LANES]
            qs = _pair_queries(q)
            sc = _nt_dot(qs, k) + mbias
            m = jnp.max(sc, axis=-1, keepdims=True)
            e = jnp.exp(sc - m)
            l = jnp.sum(e, axis=-1, keepdims=True)
            o = jnp.dot(e.astype(BF16), v_ref[:, c], preferred_element_type=F32) / l
            lo = lax.broadcasted_iota(jnp.int32, (N_META, LANES), 1) < HEAD_DIM
            o_ref[META_OFF:CHUNK, c] = jnp.where(lo, o[0:N_META], o[N_META:]).astype(BF16)


def _attention(lay, qk, v, rpb):
    n = qk.shape[0]
    prev, nxt, flags, mblk = lay.attn_tables()
    n_bands = lay.n_grid // BAND
    b32, b48 = _attn_bias_tables(rpb)
    lane = np.arange(LANES)
    mbias_meta = jnp.asarray(np.where(lane >= META_OFF, 0.0, NEG_INF)[None, :], F32)
    kcol = ATT_W // LANES
    halo = HALO_ROWS * GRID_W
    bias_spec = lambda a: pl.BlockSpec((None,) + a.shape[1:], lambda p, b, *_: (p, 0, 0, 0, 0))
    grid_spec = pltpu.PrefetchScalarGridSpec(
        num_scalar_prefetch=4,
        grid=(N_PAIRS, n_bands),
        in_specs=[
            pl.BlockSpec((BAND, LANES), lambda p, b, *_: (b, p)),
            pl.BlockSpec((halo, LANES), lambda p, b, pv, nx, fl, mb: (pv[b], kcol + p)),
            pl.BlockSpec((BAND, LANES), lambda p, b, pv, nx, fl, mb: (b, kcol + p)),
            pl.BlockSpec((halo, LANES), lambda p, b, pv, nx, fl, mb: (nx[b], kcol + p)),
            pl.BlockSpec((halo, LANES), lambda p, b, pv, nx, fl, mb: (pv[b], p)),
            pl.BlockSpec((BAND, LANES), lambda p, b, pv, nx, fl, mb: (b, p)),
            pl.BlockSpec((halo, LANES), lambda p, b, pv, nx, fl, mb: (nx[b], p)),
            pl.BlockSpec((N_META, LANES), lambda p, b, pv, nx, fl, mb: (mb[b], kcol + p)),
            pl.BlockSpec((N_META, LANES), lambda p, b, pv, nx, fl, mb: (mb[b], p)),
            bias_spec(b32), bias_spec(b48),
        ],
        out_specs=pl.BlockSpec((BAND, LANES), lambda p, b, *_: (b, p)),
        scratch_shapes=[pltpu.VMEM((BUF_ROWS * w, LANES), BF16) for w in SPAN_W] * 2,
    )
    attn = pl.pallas_call(
        _attn_kernel, grid_spec=grid_spec,
        out_shape=jax.ShapeDtypeStruct((n, ATT_W), BF16),
        compiler_params=_cparams(("arbitrary", "arbitrary")),
    )(jnp.asarray(prev), jnp.asarray(nxt), jnp.asarray(flags), jnp.asarray(mblk),
      qk, qk, qk, qk, v, v, v, qk, v, b32, b48)
    mb0 = lay.n_grid // CHUNK
    n_tail = (n - lay.n_grid) // CHUNK
    return pl.pallas_call(
        functools.partial(_attn_meta_kernel, lay.n_seq),
        grid=(n_tail,),
        in_specs=[pl.BlockSpec((CHUNK, 2 * ATT_W), lambda s: (mb0 + s, 0)),
                  pl.BlockSpec((CHUNK, ATT_W), lambda s: (mb0 + s, 0)),
                  _const_spec((1, LANES)),
                  pl.BlockSpec(memory_space=pl.ANY)],
        out_specs=pl.BlockSpec((CHUNK, ATT_W), lambda s: (mb0 + s, 0)),
        out_shape=jax.ShapeDtypeStruct((n, ATT_W), BF16),
        input_output_aliases={3: 0},
        compiler_params=_cparams(("arbitrary",)),
    )(qk, v, mbias_meta, attn)


def _softplus(x):
    return jnp.maximum(x, 0.0) + jnp.log(1.0 + jnp.exp(-jnp.abs(x)))


def _split3_bf16(x):
    hi = x.astype(BF16)
    r = x - hi.astype(F32)
    mid = r.astype(BF16)
    lo = (r - mid.astype(F32)).astype(BF16)
    return hi, mid, lo


def _expand_heads(w, e):
    hi, lo = _split_bf16(w)
    return jnp.dot(hi, e, preferred_element_type=F32) + jnp.dot(lo, e, preferred_element_type=F32)


def _ssd_kernel(reverse, blk_ref, prev_ref, next_ref, flag_ref, *refs):
    if reverse:
        (u_ref, dt_ref, dtb_ref, arow_ref, e_ref, tri_ref, yf_ref, z_ref, dsk_ref, nw_ref,
         o_ref, state, ybuf) = refs
    else:
        (xbc_ref, xp_ref, xn_ref, shift_ref, cw_ref, dt_ref, dtb_ref, arow_ref, e_ref, tri_ref,
         o_ref, u_ref, xe, state) = refs
        ybuf = o_ref
    del blk_ref, prev_ref, next_ref
    flags = flag_ref[pl.program_id(0)]
    is_meta = (flags & 4) != 0

    @pl.when((flags & 8) != 0)
    def _():
        state[...] = jnp.zeros_like(state)

    if reverse:
        u = u_ref[...].astype(F32)
    else:
        zero = jnp.zeros((N_META, CONV_CH), BF16)
        xe[0:N_META, :] = jnp.where((flags & 1) != 0, xp_ref[...], zero)
        xe[N_META:N_META + CHUNK, :] = xbc_ref[...]
        xe[N_META + CHUNK:, :] = jnp.where((flags & 2) != 0, xn_ref[...], zero)

        @pl.when(is_meta)
        def _():
            xe[0:N_META + META_OFF, :] = jnp.zeros((N_META + META_OFF, CONV_CH), BF16)

        shifted = jnp.dot(shift_ref[...], xe[...], preferred_element_type=F32)
        half = CONV_K // 2
        u = cw_ref[CONV_K:CONV_K + 1, :] + cw_ref[half:half + 1, :] * xe[N_META:N_META + CHUNK, :].astype(F32)
        for n, k in enumerate(k for k in range(CONV_K) if k != half):
            u = u + cw_ref[k:k + 1, :] * shifted[n * CHUNK:(n + 1) * CHUNK, :]
        u = _silu(u)
        u_ref[...] = u.astype(BF16)
    xs = u[:, 0:D_SSM]
    x_bf = xs.astype(BF16)

    rid = lax.broadcasted_iota(jnp.int32, (CHUNK, 1), 0)
    valid = rid >= jnp.where(is_meta, META_OFF, 0)
    dt = jnp.where(valid, _softplus(dt_ref[...] + dtb_ref[...]), 0.0)
    a = dt * arow_ref[...]
    tri = tri_ref[...]
    ah, am, al = _split3_bf16(a)
    acum = (jnp.dot(tri, ah, preferred_element_type=F32) + jnp.dot(tri, am, preferred_element_type=F32)
            + jnp.dot(tri, al, preferred_element_type=F32))
    acum = acum * LOG2E
    acum_t = acum.T
    dt_t = dt.T
    edge = 0 if reverse else CHUNK - 1
    a_tot = acum[edge:edge + 1, :]
    e = e_ref[...]
    w1e = jnp.dot((dt * jnp.exp2(a_tot - acum)).astype(BF16), e, preferred_element_type=F32)
    w2e = jnp.dot(jnp.exp2(acum).astype(BF16), e, preferred_element_type=F32)
    decay_row = _expand_heads(jnp.broadcast_to(jnp.exp2(a_tot), (8, LANES)), e)[0:1]
    xw = (xs * w1e).astype(BF16)

    li = lax.broadcasted_iota(jnp.int32, (CHUNK, CHUNK), 0)
    si = lax.broadcasted_iota(jnp.int32, (CHUNK, CHUNK), 1)
    causal = (si >= li) if reverse else (li >= si)
    lane_lo = si < HEAD_DIM
    ho = N_HEADS if reverse else 0
    heads_per_group = N_HEADS // SSM_GROUPS
    for g in range(SSM_GROUPS):
        gsl = slice(g * GROUP_W, (g + 1) * GROUP_W)
        bg = u[:, D_SSM + g * SSM_STATE:D_SSM + (g + 1) * SSM_STATE]
        cg = u[:, D_SSM + (SSM_GROUPS + g) * SSM_STATE:D_SSM + (SSM_GROUPS + g + 1) * SSM_STATE].astype(BF16)
        cb = _nt_dot(cg, bg.astype(BF16))
        st_old = state[:, gsl]
        y_off = jnp.dot(cg, st_old.astype(BF16), preferred_element_type=F32) * w2e[:, gsl]
        st_new = jnp.dot(bg.T.astype(BF16), xw[:, gsl], preferred_element_type=F32)
        state[:, gsl] = st_old * decay_row[:, gsl] + st_new
        for j in range(heads_per_group // 2):
            col = g * GROUP_W + j * LANES
            xpair = x_bf[:, col:col + LANES]
            zero = jnp.zeros_like(xpair)
            mhs = []
            for hh in range(2):
                hi = ho + g * heads_per_group + 2 * j + hh
                seg = acum[:, hi:hi + 1] - acum_t[hi:hi + 1, :]
                lm = jnp.exp2(jnp.where(causal, seg, NEG_INF))
                mhs.append((cb * lm * dt_t[hi:hi + 1, :]).astype(BF16))
            xsplit = jnp.concatenate([jnp.where(lane_lo, xpair, zero), jnp.where(lane_lo, zero, xpair)], axis=0)
            ybuf[:, col:col + LANES] = y_off[:, j * LANES:(j + 1) * LANES] + jnp.dot(
                jnp.concatenate(mhs, axis=1), xsplit, preferred_element_type=F32)
    if reverse:
        for g in range(SSM_GROUPS):
            gsl = slice(g * GROUP_W, (g + 1) * GROUP_W)
            yg = ybuf[:, gsl] + yf_ref[:, gsl] + dsk_ref[:, gsl] * xs[:, gsl]
            yg = yg * _silu(z_ref[:, gsl].astype(F32))
            ms = jnp.mean(yg * yg, axis=-1, keepdims=True)
            o_ref[:, gsl] = (yg * lax.rsqrt(ms + RMS_EPS) * nw_ref[:, gsl]).astype(BF16)

    @pl.when((flags & 16) != 0)
    def _():
        o_ref[...] = jnp.zeros_like(o_ref)


def _ssd_constants(reverse):
    ho = N_HEADS if reverse else 0
    e = np.zeros((LANES, D_SSM), np.float32)
    for h in range(N_HEADS):
        e[ho + h, h * HEAD_DIM:(h + 1) * HEAD_DIM] = 1.0
    i = np.arange(CHUNK)
    tri = (i[None, :] >= i[:, None]) if reverse else (i[:, None] >= i[None, :])
    return jnp.asarray(e, BF16), jnp.asarray(tri.astype(np.float32), BF16)


def _shift_matrices():
    s = np.zeros(((CONV_K - 1) * CHUNK, CHUNK + 2 * N_META), np.float32)
    i = np.arange(CHUNK)
    for n, k in enumerate(k for k in range(CONV_K) if k != CONV_K // 2):
        s[n * CHUNK + i, N_META + i + k - CONV_K // 2] = 1.0
    return jnp.asarray(s, BF16)


def _ssd_pass(lay, reverse, dt, dtb, arow, fwd_in=(), rev_in=()):
    n = dt.shape[0]
    blk, prev, nxt, flags = lay.ssd_tables(reverse)
    e, tri = _ssd_constants(reverse)
    cur = lambda w: pl.BlockSpec((CHUNK, w), lambda i, bk, pv, nx, fl: (bk[i], 0))
    common = [cur(LANES), _const_spec((1, LANES)), _const_spec((1, LANES)),
              _const_spec((LANES, D_SSM)), _const_spec((CHUNK, CHUNK))]
    state = pltpu.VMEM((SSM_STATE, D_SSM), F32)
    if reverse:
        u, y_f, z, dsk, nw = rev_in
        args = (u, dt, dtb, arow, e, tri, y_f, z, dsk, nw)
        in_specs = [cur(CONV_CH)] + common + [cur(D_SSM), cur(D_SSM), _const_spec((1, D_SSM)),
                                              _const_spec((1, D_SSM))]
        out_specs = cur(D_SSM)
        out_shape = jax.ShapeDtypeStruct((n, D_SSM), BF16)
        scratch = [state, pltpu.VMEM((CHUNK, D_SSM), F32)]
    else:
        xbc, cw = fwd_in
        shift = _shift_matrices()
        args = (xbc, xbc, xbc, shift, cw, dt, dtb, arow, e, tri)
        in_specs = [cur(CONV_CH),
                    pl.BlockSpec((N_META, CONV_CH), lambda i, bk, pv, nx, fl: (pv[i], 0)),
                    pl.BlockSpec((N_META, CONV_CH), lambda i, bk, pv, nx, fl: (nx[i], 0)),
                    _const_spec(shift.shape), _const_spec((8, CONV_CH))] + common
        out_specs = [cur(D_SSM), cur(CONV_CH)]
        out_shape = [jax.ShapeDtypeStruct((n, D_SSM), F32), jax.ShapeDtypeStruct((n, CONV_CH), BF16)]
        scratch = [pltpu.VMEM((CHUNK + 2 * N_META, CONV_CH), BF16), state]
    grid_spec = pltpu.PrefetchScalarGridSpec(
        num_scalar_prefetch=4, grid=(len(blk),), in_specs=in_specs,
        out_specs=out_specs, scratch_shapes=scratch)
    return pl.pallas_call(
        functools.partial(_ssd_kernel, reverse), grid_spec=grid_spec, out_shape=out_shape,
        compiler_params=_cparams(("arbitrary",)),
    )(jnp.asarray(blk), jnp.asarray(prev), jnp.asarray(nxt), jnp.asarray(flags), *args)


def _moe_plan(route_t):
    n = route_t.shape[1]
    flat_e = route_t[0:2].T.astype(jnp.int32).reshape(-1)
    onehot = (flat_e[:, None] == jnp.arange(N_EXPERTS, dtype=jnp.int32)[None, :]).astype(jnp.int32)
    csum = jnp.cumsum(onehot, axis=0)
    counts = csum[-1]
    rank = jnp.sum(csum * onehot, axis=1) - 1
    padded = (counts + TME - 1) // TME * TME
    pends = jnp.cumsum(padded)
    pstarts = pends - padded
    dest = jnp.sum(onehot * pstarts[None, :], axis=1) + rank
    n_rows = _round_up(2 * n + N_EXPERTS * TME, TME)
    n_blocks = n_rows // TME
    block_e = jnp.sum((jnp.arange(n_blocks, dtype=jnp.int32)[:, None] * TME >= pends[None, :]).astype(jnp.int32),
                      axis=1)
    block_e = jnp.minimum(block_e, N_EXPERTS - 1)
    n_used = (pends[-1] // TME).astype(jnp.int32).reshape(1)
    tail = jnp.minimum(pends[-1] + jnp.arange(N_PAD_RANGES - N_EXPERTS + 1, dtype=jnp.int32) * TME, n_rows)
    pad_rows = jnp.concatenate([pstarts + counts, tail[:-1], pends, tail[1:]]).astype(jnp.int32)
    return dest.astype(jnp.int32), block_e, n_used, pad_rows, n_rows


def _row_tile(r):
    return pl.multiple_of(r * ROW_TILE, ROW_TILE)


def _dispatch_kernel(pad_ref, dest_hbm, h_ref, xs_hbm, idx, stage, ztile, sem_i, sem, sem_z):
    @pl.when(pl.program_id(0) == 0)
    def _():
        ztile[...] = jnp.zeros_like(ztile)

        def zero_copy(r):
            return pltpu.make_async_copy(ztile, xs_hbm.at[pl.ds(_row_tile(r), ROW_TILE)], sem_z)

        for e in range(N_PAD_RANGES):
            lo, hi = pad_ref[e], pad_ref[N_PAD_RANGES + e]

            def fill(r, c):
                zero_copy(r).start()
                return c

            def drain_fill(r, c):
                zero_copy(0).wait()
                return c

            lax.fori_loop(lo, hi, fill, 0)
            lax.fori_loop(lo, hi, drain_fill, 0)

    cp = pltpu.make_async_copy(dest_hbm.at[pl.program_id(0)], idx, sem_i)
    cp.start()
    hf = h_ref[...].astype(F32)
    for j in range(ROW_TILE):
        stage[pl.ds(j, TM, stride=ROW_TILE), :] = hf[:, j * LANES:(j + 1) * LANES]
    cp.wait()

    def row_copy(src_row, dst_row):
        return pltpu.make_async_copy(stage.at[pl.ds(_row_tile(src_row), ROW_TILE)],
                                     xs_hbm.at[pl.ds(_row_tile(dst_row), ROW_TILE)], sem)

    def issue(b, c):
        for u in range(DMA_UNROLL):
            r = b * DMA_UNROLL + u
            row_copy(lax.shift_right_logical(r, 1), idx[r]).start(priority=u % 2)
        return c

    lax.fori_loop(0, RB // DMA_UNROLL, issue, 0)

    def drain(r, c):
        row_copy(0, 0).wait()
        return c

    lax.fori_loop(0, RB, drain, 0, unroll=DMA_UNROLL)


def _dispatch(h, dest, pad_rows, n_rows):
    n = h.shape[0]
    steps = n // TM
    grid_spec = pltpu.PrefetchScalarGridSpec(
        num_scalar_prefetch=1, grid=(steps,),
        in_specs=[pl.BlockSpec(memory_space=pl.ANY), pl.BlockSpec((TM, D_MODEL), lambda i, pad: (i, 0))],
        out_specs=pl.BlockSpec(memory_space=pl.ANY),
        scratch_shapes=[pltpu.SMEM((RB,), jnp.int32), pltpu.VMEM((TM * ROW_TILE, LANES), F32),
                        pltpu.VMEM((ROW_TILE, LANES), F32),
                        pltpu.SemaphoreType.DMA, pltpu.SemaphoreType.DMA, pltpu.SemaphoreType.DMA])
    return pl.pallas_call(
        _dispatch_kernel, grid_spec=grid_spec,
        out_shape=jax.ShapeDtypeStruct((n_rows * ROW_TILE, LANES), F32),
        compiler_params=_cparams(("arbitrary",)),
    )(pad_rows, dest.reshape(steps, RB), h)


def _expert_kernel(be_ref, nu_ref, x_ref, wg_ref, wu_ref, wd_ref, o_ref, xbf, act):
    del be_ref
    used = pl.program_id(0) < nu_ref[0]

    @pl.when(used)
    def _():
        for j in range(ROW_TILE):
            xbf[:, j * LANES:(j + 1) * LANES] = x_ref[pl.ds(j, TME, stride=ROW_TILE), :].astype(BF16)
        x = xbf[...]
        for c in range(D_FF_EXPERT // FF_CHUNK):
            sl = slice(c * FF_CHUNK, (c + 1) * FF_CHUNK)
            g = jnp.dot(x, wg_ref[:, sl], preferred_element_type=F32)
            u = jnp.dot(x, wu_ref[:, sl], preferred_element_type=F32)
            act[:, sl] = (_silu(g) * u).astype(BF16)
        out = jnp.dot(act[...], wd_ref[...], preferred_element_type=F32)
        for j in range(ROW_TILE):
            o_ref[pl.ds(j, TME, stride=ROW_TILE), :] = out[:, j * LANES:(j + 1) * LANES]

    @pl.when(jnp.logical_not(used))
    def _():
        o_ref[...] = jnp.zeros_like(o_ref)


def _experts(xs, block_e, n_used, wg, wu, wd):
    n_blocks = xs.shape[0] // (TME * ROW_TILE)
    rows = pl.BlockSpec((TME * ROW_TILE, LANES), lambda i, be, nu: (i, 0))
    resident = lambda shape: pl.BlockSpec((None,) + shape, lambda i, be, nu: (be[i], 0, 0),
                                          pipeline_mode=pl.Buffered(1))
    grid_spec = pltpu.PrefetchScalarGridSpec(
        num_scalar_prefetch=2, grid=(n_blocks,),
        in_specs=[pl.BlockSpec((TME * ROW_TILE, LANES), lambda i, be, nu: (jnp.minimum(i, nu[0] - 1), 0)),
                  resident((D_MODEL, D_FF_EXPERT)), resident((D_MODEL, D_FF_EXPERT)),
                  resident((D_FF_EXPERT, D_MODEL))],
        out_specs=rows,
        scratch_shapes=[pltpu.VMEM((TME, D_MODEL), BF16), pltpu.VMEM((TME, D_FF_EXPERT), BF16)])
    return pl.pallas_call(
        _expert_kernel, grid_spec=grid_spec,
        out_shape=jax.ShapeDtypeStruct(xs.shape, F32),
        compiler_params=_cparams(("arbitrary",)),
    )(block_e, n_used, xs, wg, wu, wd)


def _combine_kernel(split, dest_hbm, eo_hbm, x_ref, route_ref, nw_ref, out_a, out_b, idx, gbuf, xn, sem_i, sem_g):
    i = pl.program_id(0)
    n = pl.num_programs(0)

    def idx_copy(step, slot):
        return pltpu.make_async_copy(dest_hbm.at[step], idx.at[slot], sem_i.at[slot])

    def row_copy(slot, src_row, k, t):
        return pltpu.make_async_copy(eo_hbm.at[pl.ds(_row_tile(src_row), ROW_TILE)],
                                     gbuf.at[slot, k, pl.ds(_row_tile(t), ROW_TILE)], sem_g.at[slot])

    def issue_rows(slot):
        def issue(b, c):
            for u in range(DMA_UNROLL):
                r = b * DMA_UNROLL + u
                row_copy(slot, idx[slot, r], u % 2, b * (DMA_UNROLL // 2) + u // 2).start(priority=u % 2)
            return c

        lax.fori_loop(0, 2 * TC // DMA_UNROLL, issue, 0)

    @pl.when(i == 0)
    def _():
        idx_copy(0, 0).start()
        idx_copy(0, 0).wait()
        issue_rows(0)

        @pl.when(n > 1)
        def _():
            idx_copy(1, 1).start()

    for slot in range(2):
        @pl.when(jnp.logical_and(i % 2 == slot, i + 1 < n))
        def _(slot=slot):
            idx_copy(0, 1 - slot).wait()
            issue_rows(1 - slot)

            @pl.when(i + 2 < n)
            def _():
                idx_copy(i + 2, slot).start()

    w1 = route_ref[:, 0:1]
    w2 = route_ref[:, 1:2]
    for slot in range(2):
        @pl.when(i % 2 == slot)
        def _(slot=slot):
            def drain(r, c):
                row_copy(slot, 0, 0, 0).wait()
                return c

            lax.fori_loop(0, 2 * TC, drain, 0, unroll=DMA_UNROLL)
            for j in range(ROW_TILE):
                sl = slice(j * LANES, (j + 1) * LANES)
                xn[:, sl] = x_ref[:, sl] + (gbuf[slot, 0, pl.ds(j, TC, stride=ROW_TILE), :] * w1
                                            + gbuf[slot, 1, pl.ds(j, TC, stride=ROW_TILE), :] * w2)

    if split is None:
        out_a[...] = xn[...]
        out_b[...] = _rms_rows(xn[...], nw_ref[...]).astype(BF16)
    else:
        tiles_a, tiles_b = split

        @pl.when(i < tiles_a)
        def _():
            out_a[...] = xn[...]

        @pl.when(jnp.logical_and(i >= tiles_a, i < tiles_a + tiles_b))
        def _():
            out_b[...] = xn[...]


def _combine(eo, dest, x, route, nw_next, split_rows=None):
    n = x.shape[0]
    steps = n // TC
    row = lambda w: pl.BlockSpec((TC, w), lambda i: (i, 0))
    if split_rows is None:
        split = None
        out_specs = [row(D_MODEL), row(D_MODEL)]
        out_shape = [jax.ShapeDtypeStruct((n, D_MODEL), F32), jax.ShapeDtypeStruct((n, D_MODEL), BF16)]
    else:
        ta, tb = split = tuple(r // TC for r in split_rows)
        out_specs = [pl.BlockSpec((TC, D_MODEL), lambda i: (jnp.minimum(i, ta - 1), 0)),
                     pl.BlockSpec((TC, D_MODEL), lambda i: (jnp.clip(i - ta, 0, tb - 1), 0))]
        out_shape = [jax.ShapeDtypeStruct((r, D_MODEL), F32) for r in split_rows]
    return pl.pallas_call(
        functools.partial(_combine_kernel, split),
        grid=(steps,),
        in_specs=[pl.BlockSpec(memory_space=pl.ANY), pl.BlockSpec(memory_space=pl.ANY),
                  row(D_MODEL), row(2), _const_spec((1, D_MODEL))],
        out_specs=out_specs, out_shape=out_shape,
        scratch_shapes=[pltpu.SMEM((2, 2 * TC), jnp.int32), pltpu.VMEM((2, 2, TC * ROW_TILE, LANES), F32),
                        pltpu.VMEM((TC, D_MODEL), F32),
                        pltpu.SemaphoreType.DMA((2,)), pltpu.SemaphoreType.DMA((2,))],
        compiler_params=_cparams(("arbitrary",)),
    )(dest.reshape(steps, 2 * TC), eo, x, route, nw_next)


def _trunk(lay, x, p, group_rows):
    depth = p['norm1_w'].shape[0]
    row = lambda v: v.reshape(1, -1).astype(F32)
    h = _norm(x, p['norm1_w'][0])
    for l in range(depth):
        w_in = p['w_in'][l]
        wqk = w_in[:, 0:2 * ATT_W].astype(BF16)
        wvzx = w_in[:, 2 * ATT_W:3 * ATT_W + D_SSM + CONV_CH].astype(BF16)
        wdt = jnp.pad(w_in[:, 3 * ATT_W + D_SSM + CONV_CH:], ((0, 0), (0, LANES - 2 * N_HEADS))).astype(BF16)
        nw_qk = jnp.concatenate([jnp.tile(p['q_norm_w'][l].astype(F32), N_HEADS) * (HEAD_DIM ** -0.5),
                                 jnp.tile(p['k_norm_w'][l].astype(F32), N_HEADS)]).reshape(1, -1)
        qk, v, z, xbc, dt_raw = _inproj(h, wqk, wvzx, wdt, nw_qk)

        attn = _attention(lay, qk, v, p['rpb'][l])

        cw = jnp.concatenate([p['conv_w'][l].astype(F32).T, p['conv_b'][l].astype(F32)[None],
                              jnp.zeros((8 - CONV_K - 1, CONV_CH), F32)], axis=0)
        pad = jnp.zeros((LANES - 2 * N_HEADS,), F32)
        dtb = jnp.concatenate([p['dt_bias'][l].astype(F32).reshape(-1), pad]).reshape(1, -1)
        arow = jnp.concatenate([-jnp.exp(p['a_log'][l].astype(F32)).reshape(-1), pad]).reshape(1, -1)
        y_f, u = _ssd_pass(lay, False, dt_raw, dtb, arow, fwd_in=(xbc, cw))
        dsk = jnp.repeat(p['d_skip'][l].astype(F32), HEAD_DIM).reshape(1, -1)
        y = _ssd_pass(lay, True, dt_raw, dtb, arow, rev_in=(u, y_f, z, dsk, row(p['ssm_norm_w'][l])))

        w_out = p['w_out'][l].astype(BF16)
        nw_next = row(p['norm1_w'][l + 1]) if l + 1 < depth else jnp.ones((1, D_MODEL), F32)
        j = l // 2
        if l % 2 == 0:
            x, h2 = _outproj(attn, y, x, w_out, row(p['attn_out_norm_w'][l]), row(p['norm2_w'][l]))
            x, h = _ffn(h2, x, p['ffn_w_gate'][j].astype(BF16), p['ffn_w_up'][j].astype(BF16),
                        p['ffn_w_down'][j].astype(BF16), nw_next)
        else:
            wr = jnp.pad(p['moe_router'][j].astype(F32).T, ((0, ROUTER_ROWS - N_EXPERTS), (0, 0)))
            x, h2, route_t = _outproj(attn, y, x, w_out, row(p['attn_out_norm_w'][l]), row(p['norm2_w'][l]),
                                      router=jnp.concatenate(_split_bf16(wr), axis=0))
            dest, block_e, n_used, pad_rows, n_rows = _moe_plan(route_t)
            xs = _dispatch(h2, dest, pad_rows, n_rows)
            eo = _experts(xs, block_e, n_used, p['moe_w_gate'][j].astype(BF16),
                          p['moe_w_up'][j].astype(BF16), p['moe_w_down'][j].astype(BF16))
            if l + 1 == depth and len(group_rows) == 2:
                return _combine(eo, dest, x, route_t[2:4].T, nw_next, split_rows=group_rows)
            x, h = _combine(eo, dest, x, route_t[2:4].T, nw_next)
    offs = np.concatenate([[0], np.cumsum(group_rows)])
    return [x[offs[g]:offs[g + 1]] for g in range(len(group_rows))]


def kernel(x_prompt, x_sample, meta_tokens, norm1_w, w_in, q_norm_w, k_norm_w, rpb, attn_out_norm_w,
           conv_w, conv_b, dt_bias, a_log, d_skip, ssm_norm_w, w_out, norm2_w,
           ffn_w_gate, ffn_w_up, ffn_w_down, moe_router, moe_w_gate, moe_w_up, moe_w_down):
    p = dict(norm1_w=norm1_w, w_in=w_in, q_norm_w=q_norm_w, k_norm_w=k_norm_w, rpb=rpb,
             attn_out_norm_w=attn_out_norm_w, conv_w=conv_w, conv_b=conv_b, dt_bias=dt_bias, a_log=a_log,
             d_skip=d_skip, ssm_norm_w=ssm_norm_w, w_out=w_out, norm2_w=norm2_w, ffn_w_gate=ffn_w_gate,
             ffn_w_up=ffn_w_up, ffn_w_down=ffn_w_down, moe_router=moe_router, moe_w_gate=moe_w_gate,
             moe_w_up=moe_w_up, moe_w_down=moe_w_down)
    groups = (x_prompt, x_sample)
    lay = _Layout([g.shape[1] for g in groups for _ in range(g.shape[0])])
    meta_block = jnp.concatenate([jnp.zeros((META_OFF, D_MODEL), F32), meta_tokens.astype(F32)], axis=0)
    tail = lay.n_tok - lay.n_grid - CHUNK * lay.n_seq
    x = jnp.concatenate([g.reshape(-1, D_MODEL).astype(F32) for g in groups]
                        + [jnp.tile(meta_block, (lay.n_seq, 1)), jnp.zeros((tail, D_MODEL), F32)], axis=0)
    flat = _trunk(lay, x, p, [g.shape[0] * g.shape[1] for g in groups])
    return tuple(f.reshape(g.shape).astype(g.dtype) for f, g in zip(flat, groups))
```

```python
import functools

import numpy as np
import jax
import jax.numpy as jnp
from jax import lax
from jax.experimental import pallas as pl
from jax.experimental.pallas import tpu as pltpu

F32 = jnp.float32
BF16 = jnp.bfloat16

D_MODEL = 1024
N_META = 16
GRID_W = 64
NA_ROWS = 8
NA_COLS = 16
N_HEADS = 16
HEAD_DIM = 64
N_PAIRS = N_HEADS // 2
UNIT_HEADS = 4
UNIT_W = UNIT_HEADS * HEAD_DIM
N_UNITS = N_HEADS // UNIT_HEADS
ATT_W = N_HEADS * HEAD_DIM
D_SSM = 1024
SSM_GROUPS = 2
SSM_STATE = 128
GROUP_W = D_SSM // SSM_GROUPS
CONV_K = 5
CONV_CH = D_SSM + 2 * SSM_GROUPS * SSM_STATE
D_FF = 2816
N_EXPERTS = 8
D_FF_EXPERT = 3584
RMS_EPS = 1e-6
NEG_INF = -1e30
LOG2E = 1.4426950408889634

LANES = 128
ROW_TILE = D_MODEL // LANES
CHUNK = 128
META_OFF = CHUNK - N_META
TM = 512
BAND_ROWS = 16
BAND = BAND_ROWS * GRID_W
HALO_ROWS = NA_ROWS // 2
BUF_ROWS = BAND_ROWS + NA_ROWS - 1
FF_CHUNK = 256
TME = 512
ROUTER_ROWS = 16
N_PAD_RANGES = 2 * N_EXPERTS + 1
RB = 2 * TM
TC = 256
DMA_UNROLL = 8
VMEM_LIMIT = 56 * 1024 * 1024


def _cparams(sem):
    return pltpu.CompilerParams(dimension_semantics=sem, vmem_limit_bytes=VMEM_LIMIT)


def _round_up(a, b):
    return (a + b - 1) // b * b


def _const_spec(shape):
    nd = len(shape)
    return pl.BlockSpec(shape, lambda *_: (0,) * nd)


class _Layout:
    def __init__(self, seq_lens):
        self.seq_lens = tuple(seq_lens)
        self.n_seq = len(seq_lens)
        self.n_grid = sum(seq_lens)
        self.starts = np.concatenate([[0], np.cumsum(seq_lens)[:-1]]).astype(np.int64)
        self.n_tok = _round_up(self.n_grid + CHUNK * self.n_seq, TM)
        assert all(s % BAND == 0 and s // GRID_W >= NA_ROWS for s in seq_lens)

    def meta_block(self, s):
        return self.n_grid // CHUNK + s

    def attn_tables(self):
        prev, nxt, flags, mblk = [], [], [], []
        for s, (st, ln) in enumerate(zip(self.starts, self.seq_lens)):
            lo, hi = st // BAND, (st + ln) // BAND
            for b in range(lo, hi):
                per = BAND_ROWS // HALO_ROWS
                prev.append(max(b * per - 1, lo * per))
                nxt.append(min((b + 1) * per, hi * per - 1))
                flags.append((1 if b == lo else 0) | (2 if b == hi - 1 else 0))
                mblk.append((self.n_grid + CHUNK * s + META_OFF) // N_META)
        return [np.asarray(a, np.int32) for a in (prev, nxt, flags, mblk)]

    def ssd_tables(self, reverse):
        blk, prev, nxt, flags = [], [], [], []
        for s, (st, ln) in enumerate(zip(self.starts, self.seq_lens)):
            nc = ln // CHUNK
            b0 = st // CHUNK
            meta16 = (self.n_grid + CHUNK * s + META_OFF) // N_META
            steps = []
            steps.append((self.meta_block(s), 0, b0 * 8, 2 | 4))
            for c in range(nc):
                b = b0 + c
                p16 = meta16 if c == 0 else b * 8 - 1
                n16 = (b + 1) * 8 if c < nc - 1 else 0
                steps.append((b, p16, n16, 1 | (2 if c < nc - 1 else 0)))
            if reverse:
                steps = steps[::-1]
            for i, (b, p, n, f) in enumerate(steps):
                blk.append(b); prev.append(p); nxt.append(n)
                flags.append(f | (8 if i == 0 else 0))
        for b in range(self.n_grid // CHUNK + self.n_seq, self.n_tok // CHUNK):
            blk.append(b); prev.append(0); nxt.append(0)
            flags.append(8 | 16)
        return [np.asarray(a, np.int32) for a in (blk, prev, nxt, flags)]


def _rms_rows(x, w):
    ms = jnp.mean(x * x, axis=-1, keepdims=True)
    return x * lax.rsqrt(ms + RMS_EPS) * w


def _norm_kernel(x_ref, w_ref, o_ref):
    o_ref[...] = _rms_rows(x_ref[...], w_ref[...]).astype(BF16)


def _norm(x, w):
    n = x.shape[0]
    return pl.pallas_call(
        _norm_kernel,
        grid=(n // TM,),
        in_specs=[pl.BlockSpec((TM, D_MODEL), lambda i: (i, 0)), _const_spec((1, D_MODEL))],
        out_specs=pl.BlockSpec((TM, D_MODEL), lambda i: (i, 0)),
        out_shape=jax.ShapeDtypeStruct((n, D_MODEL), BF16),
        compiler_params=_cparams(("parallel",)),
    )(x, w.reshape(1, D_MODEL))


def _inproj_kernel(h_ref, wqk_ref, wvzx_ref, wdt_ref, g_ref, nw_ref,
                   qk_ref, v_ref, z_ref, xbc_ref, dt_ref):
    h = h_ref[...]
    g = g_ref[...]
    n_qk = 2 * ATT_W // FF_CHUNK
    col = lambda c: slice(c * FF_CHUNK, (c + 1) * FF_CHUNK)
    y_next = jnp.dot(h, wqk_ref[:, col(0)], preferred_element_type=F32)
    for c in range(n_qk):
        y = y_next
        if c + 1 < n_qk:
            y_next = jnp.dot(h, wqk_ref[:, col(c + 1)], preferred_element_type=F32)
        ss = jnp.dot((y * y).astype(BF16), g, preferred_element_type=F32)
        inv = lax.rsqrt(ss * (1.0 / HEAD_DIM) + RMS_EPS)
        qk_ref[:, col(c)] = (y * inv * nw_ref[:, col(c)]).astype(BF16)
    for c in range(ATT_W // FF_CHUNK):
        sl = slice(c * FF_CHUNK, (c + 1) * FF_CHUNK)
        v_ref[:, sl] = jnp.dot(h, wvzx_ref[:, sl], preferred_element_type=F32).astype(BF16)
    for c in range(D_SSM // FF_CHUNK):
        sl = slice(c * FF_CHUNK, (c + 1) * FF_CHUNK)
        src = slice(ATT_W + c * FF_CHUNK, ATT_W + (c + 1) * FF_CHUNK)
        z_ref[:, sl] = jnp.dot(h, wvzx_ref[:, src], preferred_element_type=F32).astype(BF16)
    for c in range(CONV_CH // FF_CHUNK):
        sl = slice(c * FF_CHUNK, (c + 1) * FF_CHUNK)
        src = slice(ATT_W + D_SSM + c * FF_CHUNK, ATT_W + D_SSM + (c + 1) * FF_CHUNK)
        xbc_ref[:, sl] = jnp.dot(h, wvzx_ref[:, src], preferred_element_type=F32).astype(BF16)
    dt_ref[...] = jnp.dot(h, wdt_ref[...], preferred_element_type=F32)


def _head_sum_matrix():
    i = np.arange(FF_CHUNK)
    return jnp.asarray((i[:, None] // HEAD_DIM == i[None, :] // HEAD_DIM).astype(np.float32), BF16)


def _inproj(h, wqk, wvzx, wdt, nw):
    n = h.shape[0]
    row = lambda w: pl.BlockSpec((TM, w), lambda i: (i, 0))
    return pl.pallas_call(
        _inproj_kernel,
        grid=(n // TM,),
        in_specs=[row(D_MODEL), _const_spec(wqk.shape), _const_spec(wvzx.shape), _const_spec(wdt.shape),
                  _const_spec((FF_CHUNK, FF_CHUNK)), _const_spec((1, 2 * ATT_W))],
        out_specs=[row(2 * ATT_W), row(ATT_W), row(D_SSM), row(CONV_CH), row(LANES)],
        out_shape=[jax.ShapeDtypeStruct((n, 2 * ATT_W), BF16), jax.ShapeDtypeStruct((n, ATT_W), BF16),
                   jax.ShapeDtypeStruct((n, D_SSM), BF16), jax.ShapeDtypeStruct((n, CONV_CH), BF16),
                   jax.ShapeDtypeStruct((n, LANES), F32)],
        compiler_params=_cparams(("parallel",)),
    )(h, wqk, wvzx, wdt, _head_sum_matrix(), nw)


def _split_bf16(x):
    hi = x.astype(BF16)
    lo = (x - hi.astype(F32)).astype(BF16)
    return hi, lo


def _outproj_kernel(with_router, attn_ref, y_ref, x_ref, w_ref, aw_ref, nw_ref, *rest):
    if with_router:
        wr_ref, xo_ref, h_ref, route_ref = rest
    else:
        xo_ref, h_ref = rest
    acc = x_ref[...] + jnp.dot(y_ref[...], w_ref[ATT_W:ATT_W + D_SSM, :], preferred_element_type=F32)
    a = _rms_rows(attn_ref[...].astype(F32), aw_ref[...]).astype(BF16)
    acc = acc + jnp.dot(a, w_ref[0:ATT_W, :], preferred_element_type=F32)
    xo_ref[...] = acc
    h2 = _rms_rows(acc, nw_ref[...])
    h_ref[...] = h2.astype(BF16)
    if with_router:
        hh, hl = _split_bf16(h2)
        wr = wr_ref[...]
        lg = _nt_dot(wr, hh)
        logits = lg[0:ROUTER_ROWS] + lg[ROUTER_ROWS:] + _nt_dot(wr[0:ROUTER_ROWS], hl)
        sub = lax.broadcasted_iota(jnp.int32, logits.shape, 0)
        logits = jnp.where(sub < N_EXPERTS, logits, NEG_INF)
        m1 = jnp.max(logits, axis=0, keepdims=True)
        i1 = jnp.min(jnp.where(logits == m1, sub, ROUTER_ROWS), axis=0, keepdims=True)
        rest_l = jnp.where(sub == i1, NEG_INF, logits)
        m2 = jnp.max(rest_l, axis=0, keepdims=True)
        i2 = jnp.min(jnp.where(rest_l == m2, sub, ROUTER_ROWS), axis=0, keepdims=True)
        e = jnp.exp(m2 - m1)
        w1 = 1.0 / (1.0 + e)
        route_ref[...] = jnp.concatenate(
            [i1.astype(F32), i2.astype(F32), w1, e * w1, jnp.zeros((4, logits.shape[1]), F32)], axis=0)


def _outproj(attn, y, x, w_out, aw, nw, router=None):
    n = x.shape[0]
    row = lambda w: pl.BlockSpec((TM, w), lambda i: (i, 0))
    in_specs = [row(ATT_W), row(D_SSM), row(D_MODEL), _const_spec(w_out.shape),
                _const_spec((1, ATT_W)), _const_spec((1, D_MODEL))]
    out_specs = [row(D_MODEL), row(D_MODEL)]
    out_shape = [jax.ShapeDtypeStruct((n, D_MODEL), F32), jax.ShapeDtypeStruct((n, D_MODEL), BF16)]
    args = [attn, y, x, w_out, aw, nw]
    if router is not None:
        in_specs.append(_const_spec((2 * ROUTER_ROWS, D_MODEL)))
        out_specs.append(pl.BlockSpec((8, TM), lambda i: (0, i)))
        out_shape.append(jax.ShapeDtypeStruct((8, n), F32))
        args.append(router)
    return pl.pallas_call(
        functools.partial(_outproj_kernel, router is not None),
        grid=(n // TM,), in_specs=in_specs, out_specs=out_specs, out_shape=out_shape,
        compiler_params=_cparams(("parallel",)),
    )(*args)


def _silu(x):
    return x * (1.0 / (1.0 + jnp.exp(-x)))


def _ffn_kernel(h_ref, x_ref, wg_ref, wu_ref, wd_ref, nw_ref, xo_ref, ho_ref, act_ref):
    h = h_ref[...]
    for c in range(D_FF // FF_CHUNK):
        sl = slice(c * FF_CHUNK, (c + 1) * FF_CHUNK)
        g = jnp.dot(h, wg_ref[:, sl], preferred_element_type=F32)
        u = jnp.dot(h, wu_ref[:, sl], preferred_element_type=F32)
        act_ref[:, sl] = (_silu(g) * u).astype(BF16)
    xn = x_ref[...] + jnp.dot(act_ref[...], wd_ref[...], preferred_element_type=F32)
    xo_ref[...] = xn
    ho_ref[...] = _rms_rows(xn, nw_ref[...]).astype(BF16)


def _ffn(h, x, wg, wu, wd, nw_next):
    n = x.shape[0]
    row = lambda w: pl.BlockSpec((TM, w), lambda i: (i, 0))
    return pl.pallas_call(
        _ffn_kernel,
        grid=(n // TM,),
        in_specs=[row(D_MODEL), row(D_MODEL), _const_spec(wg.shape), _const_spec(wu.shape),
                  _const_spec(wd.shape), _const_spec((1, D_MODEL))],
        out_specs=[row(D_MODEL), row(D_MODEL)],
        out_shape=[jax.ShapeDtypeStruct((n, D_MODEL), F32), jax.ShapeDtypeStruct((n, D_MODEL), BF16)],
        scratch_shapes=[pltpu.VMEM((TM, D_FF), BF16)],
        compiler_params=_cparams(("parallel",)),
    )(h, x, wg, wu, wd, nw_next)


N_CB = GRID_W // NA_COLS
SPAN_START = (0, 0, 16, 32)
SPAN_W = (32, 48, 48, 32)
META_SLOT = (24, 0, 0, 0)
META_PER_ROW = N_META // 2
KTILE = 16
ATTN_DEPTH = 16


def _attn_bias_tables(rpb):
    i = np.arange(NA_ROWS)
    dr = np.clip(i[None, :] - i[:, None] + NA_ROWS - 1, 0, 2 * NA_ROWS - 2)
    tables = {}
    for j in range(N_CB):
        span = SPAN_W[j]
        qc = NA_COLS * j + np.arange(NA_COLS)
        kc = SPAN_START[j] + np.arange(span)
        st = np.clip(qc - NA_COLS // 2, 0, GRID_W - NA_COLS)
        valid = (kc[None, :] >= st[:, None]) & (kc[None, :] < st[:, None] + NA_COLS)
        dc = np.clip(kc[None, :] - qc[:, None] + NA_COLS - 1, 0, 2 * NA_COLS - 2)
        slot = (np.arange(span) >= META_SLOT[j]) & (np.arange(span) < META_SLOT[j] + META_PER_ROW)
        assert not valid[:, slot].any()
        b = rpb.astype(F32)[:, dr][:, :, :, dc]
        b = jnp.where(valid[None, None, None], b, NEG_INF)
        meta_vis = np.where(i[:, None] < 2, 0.0, NEG_INF) * np.ones((1, span))
        b = jnp.where(slot[None, None, None, None, :], jnp.asarray(meta_vis, F32)[None, None, :, None, :], b)
        b = b.reshape(N_UNITS, UNIT_HEADS, NA_ROWS, NA_ROWS, NA_COLS, span)
        b = jnp.transpose(b, (0, 2, 1, 4, 3, 5)).reshape(N_UNITS, NA_ROWS, UNIT_HEADS * NA_COLS, NA_ROWS * span)
        tables.setdefault(span, []).append(b)
    return [jnp.stack(t, axis=1) for _, t in sorted(tables.items())]


def _pair_queries(q):
    lo = lax.broadcasted_iota(jnp.int32, q.shape, 1) < HEAD_DIM
    zero = jnp.zeros_like(q)
    return jnp.concatenate([jnp.where(lo, q, zero), jnp.where(lo, zero, q)], axis=0)


def _head_queries(q):
    head = lax.broadcasted_iota(jnp.int32, q.shape, 1) // HEAD_DIM
    zero = jnp.zeros_like(q)
    return jnp.concatenate([jnp.where(head == h, q, zero) for h in range(UNIT_HEADS)], axis=0)


def _nt_dot(a, b):
    return lax.dot_general(a, b, (((1,), (1,)), ((), ())), preferred_element_type=F32)


def _attn_kernel(prev_ref, next_ref, flag_ref, mblk_ref,
                 q_ref, kp_ref, kc_ref, kn_ref, vp_ref, vc_ref, vn_ref, km_ref, vm_ref,
                 b32_ref, b48_ref, o_ref, *spans):
    del prev_ref, next_ref, mblk_ref
    flags = flag_ref[pl.program_id(1)]
    is_first = (flags & 1) != 0
    is_last = (flags & 2) != 0
    low = lax.broadcasted_iota(jnp.int32, (KTILE, UNIT_W), 0) < META_PER_ROW

    for refs, bufs, m_ref in (((kp_ref, kc_ref, kn_ref), spans[:N_CB], km_ref),
                              ((vp_ref, vc_ref, vn_ref), spans[N_CB:], vm_ref)):
        m = m_ref[...]
        mf = m.astype(F32)
        m_swapped = jnp.concatenate([mf[META_PER_ROW:], mf[:META_PER_ROW]], axis=0).astype(BF16)
        for rr in range(BUF_ROWS):
            if rr < HALO_ROWS:
                ref, row = refs[0], rr
            elif rr < HALO_ROWS + BAND_ROWS:
                ref, row = refs[1], rr - HALO_ROWS
            else:
                ref, row = refs[2], rr - HALO_ROWS - BAND_ROWS
            for j in range(N_CB):
                span = SPAN_W[j]
                for t in range(span // KTILE):
                    off = row * GRID_W + SPAN_START[j] + t * KTILE
                    tile = ref[off:off + KTILE, :]
                    if t == META_SLOT[j] // KTILE:
                        if META_SLOT[j] % KTILE == 0:
                            tile = jnp.where(low, m if rr % 2 == 0 else m_swapped, tile)
                        else:
                            tile = jnp.where(low, tile, m_swapped if rr % 2 == 0 else m)
                    bufs[j][rr * span + t * KTILE:rr * span + (t + 1) * KTILE, :] = tile

    lane_head = lax.broadcasted_iota(jnp.int32, (NA_COLS, UNIT_W), 1) // HEAD_DIM
    starts = []
    for ri in range(BAND_ROWS):
        ls = jnp.int32(ri)
        ls = jnp.where(is_first, jnp.maximum(ls, HALO_ROWS), ls)
        ls = jnp.where(is_last, jnp.minimum(ls, BAND_ROWS - HALO_ROWS), ls)
        starts.append((ls, HALO_ROWS + ri - ls))

    def scores(u):
        ri, j = divmod(u, N_CB)
        ls, delta = starts[ri]
        span = SPAN_W[j]
        kwin = spans[j][pl.ds(pl.multiple_of(ls * span, KTILE), NA_ROWS * span), :]
        q0 = ri * GRID_W + j * NA_COLS
        bias_ref = b32_ref if span == SPAN_W[0] else b48_ref
        return _nt_dot(_head_queries(q_ref[q0:q0 + NA_COLS, :]), kwin) + bias_ref[SPAN_W[:j].count(span), delta]

    def finish(u, s):
        ri, j = divmod(u, N_CB)
        span = SPAN_W[j]
        vwin = spans[N_CB + j][pl.ds(pl.multiple_of(starts[ri][0] * span, KTILE), NA_ROWS * span), :]
        p = jnp.exp(s - jnp.max(s, axis=-1, keepdims=True))
        l = jnp.sum(p, axis=-1, keepdims=True)
        o = jnp.dot(p.astype(BF16), vwin, preferred_element_type=F32) / l
        q0 = ri * GRID_W + j * NA_COLS
        out = o[:NA_COLS]
        for h in range(1, UNIT_HEADS):
            out = jnp.where(lane_head == h, o[h * NA_COLS:(h + 1) * NA_COLS], out)
        o_ref[q0:q0 + NA_COLS, :] = out.astype(BF16)

    n_units = BAND_ROWS * N_CB
    pending = {}
    for t in range(n_units + ATTN_DEPTH):
        if t < n_units:
            pending[t] = scores(t)
        if t >= ATTN_DEPTH:
            finish(t - ATTN_DEPTH, pending.pop(t - ATTN_DEPTH))


def _attn_meta_kernel(n_seq, qk_ref, v_ref, mbias_ref, alias_ref, o_ref):
    del alias_ref
    s = pl.program_id(0)
    o_ref[...] = jnp.zeros_like(o_ref)

    @pl.when(s < n_seq)
    def _():
        mbias = mbias_ref[...]
        for p in range(N_PAIRS):
            c = slice(p * LANES, (p + 1) * LANES)
            q = qk_ref[META_OFF:CHUNK, c]
            k = qk_ref[:, ATT_W + p * LANES:ATT_W + (p + 1) * LANES]
            qs = _pair_queries(q)
            sc = _nt_dot(qs, k) + mbias
            m = jnp.max(sc, axis=-1, keepdims=True)
            e = jnp.exp(sc - m)
            l = jnp.sum(e, axis=-1, keepdims=True)
            o = jnp.dot(e.astype(BF16), v_ref[:, c], preferred_element_type=F32) / l
            lo = lax.broadcasted_iota(jnp.int32, (N_META, LANES), 1) < HEAD_DIM
            o_ref[META_OFF:CHUNK, c] = jnp.where(lo, o[0:N_META], o[N_META:]).astype(BF16)


def _attention(lay, qk, v, rpb):
    n = qk.shape[0]
    prev, nxt, flags, mblk = lay.attn_tables()
    n_bands = lay.n_grid // BAND
    b32, b48 = _attn_bias_tables(rpb)
    lane = np.arange(LANES)
    mbias_meta = jnp.asarray(np.where(lane >= META_OFF, 0.0, NEG_INF)[None, :], F32)
    kcol = ATT_W // UNIT_W
    halo = HALO_ROWS * GRID_W
    bias_spec = lambda a: pl.BlockSpec((None,) + a.shape[1:], lambda p, b, *_: (p, 0, 0, 0, 0))
    grid_spec = pltpu.PrefetchScalarGridSpec(
        num_scalar_prefetch=4,
        grid=(N_UNITS, n_bands),
        in_specs=[
            pl.BlockSpec((BAND, UNIT_W), lambda p, b, *_: (b, p)),
            pl.BlockSpec((halo, UNIT_W), lambda p, b, pv, nx, fl, mb: (pv[b], kcol + p)),
            pl.BlockSpec((BAND, UNIT_W), lambda p, b, pv, nx, fl, mb: (b, kcol + p)),
            pl.BlockSpec((halo, UNIT_W), lambda p, b, pv, nx, fl, mb: (nx[b], kcol + p)),
            pl.BlockSpec((halo, UNIT_W), lambda p, b, pv, nx, fl, mb: (pv[b], p)),
            pl.BlockSpec((BAND, UNIT_W), lambda p, b, pv, nx, fl, mb: (b, p)),
            pl.BlockSpec((halo, UNIT_W), lambda p, b, pv, nx, fl, mb: (nx[b], p)),
            pl.BlockSpec((N_META, UNIT_W), lambda p, b, pv, nx, fl, mb: (mb[b], kcol + p)),
            pl.BlockSpec((N_META, UNIT_W), lambda p, b, pv, nx, fl, mb: (mb[b], p)),
            bias_spec(b32), bias_spec(b48),
        ],
        out_specs=pl.BlockSpec((BAND, UNIT_W), lambda p, b, *_: (b, p)),
        scratch_shapes=[pltpu.VMEM((BUF_ROWS * w, UNIT_W), BF16) for w in SPAN_W] * 2,
    )
    attn = pl.pallas_call(
        _attn_kernel, grid_spec=grid_spec,
        out_shape=jax.ShapeDtypeStruct((n, ATT_W), BF16),
        compiler_params=_cparams(("arbitrary", "arbitrary")),
    )(jnp.asarray(prev), jnp.asarray(nxt), jnp.asarray(flags), jnp.asarray(mblk),
      qk, qk, qk, qk, v, v, v, qk, v, b32, b48)
    mb0 = lay.n_grid // CHUNK
    n_tail = (n - lay.n_grid) // CHUNK
    return pl.pallas_call(
        functools.partial(_attn_meta_kernel, lay.n_seq),
        grid=(n_tail,),
        in_specs=[pl.BlockSpec((CHUNK, 2 * ATT_W), lambda s: (mb0 + s, 0)),
                  pl.BlockSpec((CHUNK, ATT_W), lambda s: (mb0 + s, 0)),
                  _const_spec((1, LANES)),
                  pl.BlockSpec(memory_space=pl.ANY)],
        out_specs=pl.BlockSpec((CHUNK, ATT_W), lambda s: (mb0 + s, 0)),
        out_shape=jax.ShapeDtypeStruct((n, ATT_W), BF16),
        input_output_aliases={3: 0},
        compiler_params=_cparams(("arbitrary",)),
    )(qk, v, mbias_meta, attn)


def _softplus(x):
    return jnp.maximum(x, 0.0) + jnp.log(1.0 + jnp.exp(-jnp.abs(x)))


def _split3_bf16(x):
    hi = x.astype(BF16)
    r = x - hi.astype(F32)
    mid = r.astype(BF16)
    lo = (r - mid.astype(F32)).astype(BF16)
    return hi, mid, lo


def _expand_heads(w, e):
    hi, lo = _split_bf16(w)
    return jnp.dot(hi, e, preferred_element_type=F32) + jnp.dot(lo, e, preferred_element_type=F32)


def _ssd_kernel(reverse, blk_ref, prev_ref, next_ref, flag_ref, *refs):
    if reverse:
        (u_ref, dt_ref, dtb_ref, arow_ref, e_ref, tri_ref, yf_ref, z_ref, dsk_ref, nw_ref,
         o_ref, state, ybuf) = refs
    else:
        (xbc_ref, xp_ref, xn_ref, shift_ref, cw_ref, dt_ref, dtb_ref, arow_ref, e_ref, tri_ref,
         o_ref, u_ref, xe, state) = refs
        ybuf = o_ref
    del blk_ref, prev_ref, next_ref
    flags = flag_ref[pl.program_id(0)]
    is_meta = (flags & 4) != 0

    @pl.when((flags & 8) != 0)
    def _():
        state[...] = jnp.zeros_like(state)

    if reverse:
        u = u_ref[...].astype(F32)
    else:
        zero = jnp.zeros((N_META, CONV_CH), BF16)
        xe[0:N_META, :] = jnp.where((flags & 1) != 0, xp_ref[...], zero)
        xe[N_META:N_META + CHUNK, :] = xbc_ref[...]
        xe[N_META + CHUNK:, :] = jnp.where((flags & 2) != 0, xn_ref[...], zero)

        @pl.when(is_meta)
        def _():
            xe[0:N_META + META_OFF, :] = jnp.zeros((N_META + META_OFF, CONV_CH), BF16)

        shifted = jnp.dot(shift_ref[...], xe[...], preferred_element_type=F32)
        half = CONV_K // 2
        u = cw_ref[CONV_K:CONV_K + 1, :] + cw_ref[half:half + 1, :] * xe[N_META:N_META + CHUNK, :].astype(F32)
        for n, k in enumerate(k for k in range(CONV_K) if k != half):
            u = u + cw_ref[k:k + 1, :] * shifted[n * CHUNK:(n + 1) * CHUNK, :]
        u = _silu(u)
        u_ref[...] = u.astype(BF16)
    xs = u[:, 0:D_SSM]
    x_bf = xs.astype(BF16)

    rid = lax.broadcasted_iota(jnp.int32, (CHUNK, 1), 0)
    valid = rid >= jnp.where(is_meta, META_OFF, 0)
    dt = jnp.where(valid, _softplus(dt_ref[...] + dtb_ref[...]), 0.0)
    a = dt * arow_ref[...]
    tri = tri_ref[...]
    ah, am, al = _split3_bf16(a)
    acum = (jnp.dot(tri, ah, preferred_element_type=F32) + jnp.dot(tri, am, preferred_element_type=F32)
            + jnp.dot(tri, al, preferred_element_type=F32))
    acum = acum * LOG2E
    acum_t = acum.T
    dt_t = dt.T
    edge = 0 if reverse else CHUNK - 1
    a_tot = acum[edge:edge + 1, :]
    e = e_ref[...]
    w1e = jnp.dot((dt * jnp.exp2(a_tot - acum)).astype(BF16), e, preferred_element_type=F32)
    w2e = jnp.dot(jnp.exp2(acum).astype(BF16), e, preferred_element_type=F32)
    decay_row = _expand_heads(jnp.broadcast_to(jnp.exp2(a_tot), (8, LANES)), e)[0:1]
    xw = (xs * w1e).astype(BF16)

    li = lax.broadcasted_iota(jnp.int32, (CHUNK, CHUNK), 0)
    si = lax.broadcasted_iota(jnp.int32, (CHUNK, CHUNK), 1)
    causal = (si >= li) if reverse else (li >= si)
    lane_lo = si < HEAD_DIM
    ho = N_HEADS if reverse else 0
    heads_per_group = N_HEADS // SSM_GROUPS
    for g in range(SSM_GROUPS):
        gsl = slice(g * GROUP_W, (g + 1) * GROUP_W)
        bg = u[:, D_SSM + g * SSM_STATE:D_SSM + (g + 1) * SSM_STATE]
        cg = u[:, D_SSM + (SSM_GROUPS + g) * SSM_STATE:D_SSM + (SSM_GROUPS + g + 1) * SSM_STATE].astype(BF16)
        cb = _nt_dot(cg, bg.astype(BF16))
        st_old = state[:, gsl]
        y_off = jnp.dot(cg, st_old.astype(BF16), preferred_element_type=F32) * w2e[:, gsl]
        st_new = jnp.dot(bg.T.astype(BF16), xw[:, gsl], preferred_element_type=F32)
        state[:, gsl] = st_old * decay_row[:, gsl] + st_new
        for j in range(heads_per_group // 2):
            col = g * GROUP_W + j * LANES
            xpair = x_bf[:, col:col + LANES]
            zero = jnp.zeros_like(xpair)
            mhs = []
            for hh in range(2):
                hi = ho + g * heads_per_group + 2 * j + hh
                seg = acum[:, hi:hi + 1] - acum_t[hi:hi + 1, :]
                lm = jnp.exp2(jnp.where(causal, seg, NEG_INF))
                mhs.append((cb * lm * dt_t[hi:hi + 1, :]).astype(BF16))
            xsplit = jnp.concatenate([jnp.where(lane_lo, xpair, zero), jnp.where(lane_lo, zero, xpair)], axis=0)
            ybuf[:, col:col + LANES] = y_off[:, j * LANES:(j + 1) * LANES] + jnp.dot(
                jnp.concatenate(mhs, axis=1), xsplit, preferred_element_type=F32)
    if reverse:
        for g in range(SSM_GROUPS):
            gsl = slice(g * GROUP_W, (g + 1) * GROUP_W)
            yg = ybuf[:, gsl] + yf_ref[:, gsl] + dsk_ref[:, gsl] * xs[:, gsl]
            yg = yg * _silu(z_ref[:, gsl].astype(F32))
            ms = jnp.mean(yg * yg, axis=-1, keepdims=True)
            o_ref[:, gsl] = (yg * lax.rsqrt(ms + RMS_EPS) * nw_ref[:, gsl]).astype(BF16)

    @pl.when((flags & 16) != 0)
    def _():
        o_ref[...] = jnp.zeros_like(o_ref)


def _ssd_constants(reverse):
    ho = N_HEADS if reverse else 0
    e = np.zeros((LANES, D_SSM), np.float32)
    for h in range(N_HEADS):
        e[ho + h, h * HEAD_DIM:(h + 1) * HEAD_DIM] = 1.0
    i = np.arange(CHUNK)
    tri = (i[None, :] >= i[:, None]) if reverse else (i[:, None] >= i[None, :])
    return jnp.asarray(e, BF16), jnp.asarray(tri.astype(np.float32), BF16)


def _shift_matrices():
    s = np.zeros(((CONV_K - 1) * CHUNK, CHUNK + 2 * N_META), np.float32)
    i = np.arange(CHUNK)
    for n, k in enumerate(k for k in range(CONV_K) if k != CONV_K // 2):
        s[n * CHUNK + i, N_META + i + k - CONV_K // 2] = 1.0
    return jnp.asarray(s, BF16)


def _ssd_pass(lay, reverse, dt, dtb, arow, fwd_in=(), rev_in=()):
    n = dt.shape[0]
    blk, prev, nxt, flags = lay.ssd_tables(reverse)
    e, tri = _ssd_constants(reverse)
    cur = lambda w: pl.BlockSpec((CHUNK, w), lambda i, bk, pv, nx, fl: (bk[i], 0))
    common = [cur(LANES), _const_spec((1, LANES)), _const_spec((1, LANES)),
              _const_spec((LANES, D_SSM)), _const_spec((CHUNK, CHUNK))]
    state = pltpu.VMEM((SSM_STATE, D_SSM), F32)
    if reverse:
        u, y_f, z, dsk, nw = rev_in
        args = (u, dt, dtb, arow, e, tri, y_f, z, dsk, nw)
        in_specs = [cur(CONV_CH)] + common + [cur(D_SSM), cur(D_SSM), _const_spec((1, D_SSM)),
                                              _const_spec((1, D_SSM))]
        out_specs = cur(D_SSM)
        out_shape = jax.ShapeDtypeStruct((n, D_SSM), BF16)
        scratch = [state, pltpu.VMEM((CHUNK, D_SSM), F32)]
    else:
        xbc, cw = fwd_in
        shift = _shift_matrices()
        args = (xbc, xbc, xbc, shift, cw, dt, dtb, arow, e, tri)
        in_specs = [cur(CONV_CH),
                    pl.BlockSpec((N_META, CONV_CH), lambda i, bk, pv, nx, fl: (pv[i], 0)),
                    pl.BlockSpec((N_META, CONV_CH), lambda i, bk, pv, nx, fl: (nx[i], 0)),
                    _const_spec(shift.shape), _const_spec((8, CONV_CH))] + common
        out_specs = [cur(D_SSM), cur(CONV_CH)]
        out_shape = [jax.ShapeDtypeStruct((n, D_SSM), F32), jax.ShapeDtypeStruct((n, CONV_CH), BF16)]
        scratch = [pltpu.VMEM((CHUNK + 2 * N_META, CONV_CH), BF16), state]
    grid_spec = pltpu.PrefetchScalarGridSpec(
        num_scalar_prefetch=4, grid=(len(blk),), in_specs=in_specs,
        out_specs=out_specs, scratch_shapes=scratch)
    return pl.pallas_call(
        functools.partial(_ssd_kernel, reverse), grid_spec=grid_spec, out_shape=out_shape,
        compiler_params=_cparams(("arbitrary",)),
    )(jnp.asarray(blk), jnp.asarray(prev), jnp.asarray(nxt), jnp.asarray(flags), *args)


def _moe_plan(route_t):
    n = route_t.shape[1]
    flat_e = route_t[0:2].T.astype(jnp.int32).reshape(-1)
    onehot = (flat_e[:, None] == jnp.arange(N_EXPERTS, dtype=jnp.int32)[None, :]).astype(jnp.int32)
    csum = jnp.cumsum(onehot, axis=0)
    counts = csum[-1]
    rank = jnp.sum(csum * onehot, axis=1) - 1
    padded = (counts + TME - 1) // TME * TME
    pends = jnp.cumsum(padded)
    pstarts = pends - padded
    dest = jnp.sum(onehot * pstarts[None, :], axis=1) + rank
    n_rows = _round_up(2 * n + N_EXPERTS * TME, TME)
    n_blocks = n_rows // TME
    block_e = jnp.sum((jnp.arange(n_blocks, dtype=jnp.int32)[:, None] * TME >= pends[None, :]).astype(jnp.int32),
                      axis=1)
    block_e = jnp.minimum(block_e, N_EXPERTS - 1)
    n_used = (pends[-1] // TME).astype(jnp.int32).reshape(1)
    tail = jnp.minimum(pends[-1] + jnp.arange(N_PAD_RANGES - N_EXPERTS + 1, dtype=jnp.int32) * TME, n_rows)
    pad_rows = jnp.concatenate([pstarts + counts, tail[:-1], pends, tail[1:]]).astype(jnp.int32)
    return dest.astype(jnp.int32), block_e, n_used, pad_rows, n_rows


def _row_tile(r):
    return pl.multiple_of(r * ROW_TILE, ROW_TILE)


def _dispatch_kernel(pad_ref, dest_hbm, h_ref, xs_hbm, idx, stage, ztile, sem_i, sem, sem_z):
    @pl.when(pl.program_id(0) == 0)
    def _():
        ztile[...] = jnp.zeros_like(ztile)

        def zero_copy(r):
            return pltpu.make_async_copy(ztile, xs_hbm.at[pl.ds(_row_tile(r), ROW_TILE)], sem_z)

        for e in range(N_PAD_RANGES):
            lo, hi = pad_ref[e], pad_ref[N_PAD_RANGES + e]

            def fill(r, c):
                zero_copy(r).start()
                return c

            def drain_fill(r, c):
                zero_copy(0).wait()
                return c

            lax.fori_loop(lo, hi, fill, 0)
            lax.fori_loop(lo, hi, drain_fill, 0)

    cp = pltpu.make_async_copy(dest_hbm.at[pl.program_id(0)], idx, sem_i)
    cp.start()
    hf = h_ref[...].astype(F32)
    for j in range(ROW_TILE):
        stage[pl.ds(j, TM, stride=ROW_TILE), :] = hf[:, j * LANES:(j + 1) * LANES]
    cp.wait()

    def row_copy(src_row, dst_row):
        return pltpu.make_async_copy(stage.at[pl.ds(_row_tile(src_row), ROW_TILE)],
                                     xs_hbm.at[pl.ds(_row_tile(dst_row), ROW_TILE)], sem)

    def issue(b, c):
        for u in range(DMA_UNROLL):
            r = b * DMA_UNROLL + u
            row_copy(lax.shift_right_logical(r, 1), idx[r]).start(priority=u % 2)
        return c

    lax.fori_loop(0, RB // DMA_UNROLL, issue, 0)

    def drain(r, c):
        row_copy(0, 0).wait()
        return c

    lax.fori_loop(0, RB, drain, 0, unroll=DMA_UNROLL)


def _dispatch(h, dest, pad_rows, n_rows):
    n = h.shape[0]
    steps = n // TM
    grid_spec = pltpu.PrefetchScalarGridSpec(
        num_scalar_prefetch=1, grid=(steps,),
        in_specs=[pl.BlockSpec(memory_space=pl.ANY), pl.BlockSpec((TM, D_MODEL), lambda i, pad: (i, 0))],
        out_specs=pl.BlockSpec(memory_space=pl.ANY),
        scratch_shapes=[pltpu.SMEM((RB,), jnp.int32), pltpu.VMEM((TM * ROW_TILE, LANES), F32),
                        pltpu.VMEM((ROW_TILE, LANES), F32),
                        pltpu.SemaphoreType.DMA, pltpu.SemaphoreType.DMA, pltpu.SemaphoreType.DMA])
    return pl.pallas_call(
        _dispatch_kernel, grid_spec=grid_spec,
        out_shape=jax.ShapeDtypeStruct((n_rows * ROW_TILE, LANES), F32),
        compiler_params=_cparams(("arbitrary",)),
    )(pad_rows, dest.reshape(steps, RB), h)


def _expert_kernel(be_ref, nu_ref, x_ref, wg_ref, wu_ref, wd_ref, o_ref, xbf, act):
    del be_ref
    used = pl.program_id(0) < nu_ref[0]

    @pl.when(used)
    def _():
        for j in range(ROW_TILE):
            xbf[:, j * LANES:(j + 1) * LANES] = x_ref[pl.ds(j, TME, stride=ROW_TILE), :].astype(BF16)
        x = xbf[...]
        for c in range(D_FF_EXPERT // FF_CHUNK):
            sl = slice(c * FF_CHUNK, (c + 1) * FF_CHUNK)
            g = jnp.dot(x, wg_ref[:, sl], preferred_element_type=F32)
            u = jnp.dot(x, wu_ref[:, sl], preferred_element_type=F32)
            act[:, sl] = (_silu(g) * u).astype(BF16)
        out = jnp.dot(act[...], wd_ref[...], preferred_element_type=F32)
        for j in range(ROW_TILE):
            o_ref[pl.ds(j, TME, stride=ROW_TILE), :] = out[:, j * LANES:(j + 1) * LANES]

    @pl.when(jnp.logical_not(used))
    def _():
        o_ref[...] = jnp.zeros_like(o_ref)


def _experts(xs, block_e, n_used, wg, wu, wd):
    n_blocks = xs.shape[0] // (TME * ROW_TILE)
    rows = pl.BlockSpec((TME * ROW_TILE, LANES), lambda i, be, nu: (i, 0))
    resident = lambda shape: pl.BlockSpec((None,) + shape, lambda i, be, nu: (be[i], 0, 0),
                                          pipeline_mode=pl.Buffered(1))
    grid_spec = pltpu.PrefetchScalarGridSpec(
        num_scalar_prefetch=2, grid=(n_blocks,),
        in_specs=[pl.BlockSpec((TME * ROW_TILE, LANES), lambda i, be, nu: (jnp.minimum(i, nu[0] - 1), 0)),
                  resident((D_MODEL, D_FF_EXPERT)), resident((D_MODEL, D_FF_EXPERT)),
                  resident((D_FF_EXPERT, D_MODEL))],
        out_specs=rows,
        scratch_shapes=[pltpu.VMEM((TME, D_MODEL), BF16), pltpu.VMEM((TME, D_FF_EXPERT), BF16)])
    return pl.pallas_call(
        _expert_kernel, grid_spec=grid_spec,
        out_shape=jax.ShapeDtypeStruct(xs.shape, F32),
        compiler_params=_cparams(("arbitrary",)),
    )(block_e, n_used, xs, wg, wu, wd)


def _combine_kernel(split, dest_hbm, eo_hbm, x_ref, route_ref, nw_ref, out_a, out_b, idx, gbuf, xn, sem_i, sem_g):
    i = pl.program_id(0)
    n = pl.num_programs(0)

    def idx_copy(step, slot):
        return pltpu.make_async_copy(dest_hbm.at[step], idx.at[slot], sem_i.at[slot])

    def row_copy(slot, src_row, k, t):
        return pltpu.make_async_copy(eo_hbm.at[pl.ds(_row_tile(src_row), ROW_TILE)],
                                     gbuf.at[slot, k, pl.ds(_row_tile(t), ROW_TILE)], sem_g.at[slot])

    def issue_rows(slot):
        def issue(b, c):
            for u in range(DMA_UNROLL):
                r = b * DMA_UNROLL + u
                row_copy(slot, idx[slot, r], u % 2, b * (DMA_UNROLL // 2) + u // 2).start(priority=u % 2)
            return c

        lax.fori_loop(0, 2 * TC // DMA_UNROLL, issue, 0)

    @pl.when(i == 0)
    def _():
        idx_copy(0, 0).start()
        idx_copy(0, 0).wait()
        issue_rows(0)

        @pl.when(n > 1)
        def _():
            idx_copy(1, 1).start()

    for slot in range(2):
        @pl.when(jnp.logical_and(i % 2 == slot, i + 1 < n))
        def _(slot=slot):
            idx_copy(0, 1 - slot).wait()
            issue_rows(1 - slot)

            @pl.when(i + 2 < n)
            def _():
                idx_copy(i + 2, slot).start()

    w1 = route_ref[:, 0:1]
    w2 = route_ref[:, 1:2]
    for slot in range(2):
        @pl.when(i % 2 == slot)
        def _(slot=slot):
            def drain(r, c):
                row_copy(slot, 0, 0, 0).wait()
                return c

            lax.fori_loop(0, 2 * TC, drain, 0, unroll=DMA_UNROLL)
            for j in range(ROW_TILE):
                sl = slice(j * LANES, (j + 1) * LANES)
                xn[:, sl] = x_ref[:, sl] + (gbuf[slot, 0, pl.ds(j, TC, stride=ROW_TILE), :] * w1
                                            + gbuf[slot, 1, pl.ds(j, TC, stride=ROW_TILE), :] * w2)

    if split is None:
        out_a[...] = xn[...]
        out_b[...] = _rms_rows(xn[...], nw_ref[...]).astype(BF16)
    else:
        tiles_a, tiles_b = split

        @pl.when(i < tiles_a)
        def _():
            out_a[...] = xn[...]

        @pl.when(jnp.logical_and(i >= tiles_a, i < tiles_a + tiles_b))
        def _():
            out_b[...] = xn[...]


def _combine(eo, dest, x, route, nw_next, split_rows=None):
    n = x.shape[0]
    steps = n // TC
    row = lambda w: pl.BlockSpec((TC, w), lambda i: (i, 0))
    if split_rows is None:
        split = None
        out_specs = [row(D_MODEL), row(D_MODEL)]
        out_shape = [jax.ShapeDtypeStruct((n, D_MODEL), F32), jax.ShapeDtypeStruct((n, D_MODEL), BF16)]
    else:
        ta, tb = split = tuple(r // TC for r in split_rows)
        out_specs = [pl.BlockSpec((TC, D_MODEL), lambda i: (jnp.minimum(i, ta - 1), 0)),
                     pl.BlockSpec((TC, D_MODEL), lambda i: (jnp.clip(i - ta, 0, tb - 1), 0))]
        out_shape = [jax.ShapeDtypeStruct((r, D_MODEL), F32) for r in split_rows]
    return pl.pallas_call(
        functools.partial(_combine_kernel, split),
        grid=(steps,),
        in_specs=[pl.BlockSpec(memory_space=pl.ANY), pl.BlockSpec(memory_space=pl.ANY),
                  row(D_MODEL), row(2), _const_spec((1, D_MODEL))],
        out_specs=out_specs, out_shape=out_shape,
        scratch_shapes=[pltpu.SMEM((2, 2 * TC), jnp.int32), pltpu.VMEM((2, 2, TC * ROW_TILE, LANES), F32),
                        pltpu.VMEM((TC, D_MODEL), F32),
                        pltpu.SemaphoreType.DMA((2,)), pltpu.SemaphoreType.DMA((2,))],
        compiler_params=_cparams(("arbitrary",)),
    )(dest.reshape(steps, 2 * TC), eo, x, route, nw_next)


def _trunk(lay, x, p, group_rows):
    depth = p['norm1_w'].shape[0]
    row = lambda v: v.reshape(1, -1).astype(F32)
    h = _norm(x, p['norm1_w'][0])
    for l in range(depth):
        w_in = p['w_in'][l]
        wqk = w_in[:, 0:2 * ATT_W].astype(BF16)
        wvzx = w_in[:, 2 * ATT_W:3 * ATT_W + D_SSM + CONV_CH].astype(BF16)
        wdt = jnp.pad(w_in[:, 3 * ATT_W + D_SSM + CONV_CH:], ((0, 0), (0, LANES - 2 * N_HEADS))).astype(BF16)
        nw_qk = jnp.concatenate([jnp.tile(p['q_norm_w'][l].astype(F32), N_HEADS) * (HEAD_DIM ** -0.5),
                                 jnp.tile(p['k_norm_w'][l].astype(F32), N_HEADS)]).reshape(1, -1)
        qk, v, z, xbc, dt_raw = _inproj(h, wqk, wvzx, wdt, nw_qk)

        attn = _attention(lay, qk, v, p['rpb'][l])

        cw = jnp.concatenate([p['conv_w'][l].astype(F32).T, p['conv_b'][l].astype(F32)[None],
                              jnp.zeros((8 - CONV_K - 1, CONV_CH), F32)], axis=0)
        pad = jnp.zeros((LANES - 2 * N_HEADS,), F32)
        dtb = jnp.concatenate([p['dt_bias'][l].astype(F32).reshape(-1), pad]).reshape(1, -1)
        arow = jnp.concatenate([-jnp.exp(p['a_log'][l].astype(F32)).reshape(-1), pad]).reshape(1, -1)
        y_f, u = _ssd_pass(lay, False, dt_raw, dtb, arow, fwd_in=(xbc, cw))
        dsk = jnp.repeat(p['d_skip'][l].astype(F32), HEAD_DIM).reshape(1, -1)
        y = _ssd_pass(lay, True, dt_raw, dtb, arow, rev_in=(u, y_f, z, dsk, row(p['ssm_norm_w'][l])))

        w_out = p['w_out'][l].astype(BF16)
        nw_next = row(p['norm1_w'][l + 1]) if l + 1 < depth else jnp.ones((1, D_MODEL), F32)
        j = l // 2
        if l % 2 == 0:
            x, h2 = _outproj(attn, y, x, w_out, row(p['attn_out_norm_w'][l]), row(p['norm2_w'][l]))
            x, h = _ffn(h2, x, p['ffn_w_gate'][j].astype(BF16), p['ffn_w_up'][j].astype(BF16),
                        p['ffn_w_down'][j].astype(BF16), nw_next)
        else:
            wr = jnp.pad(p['moe_router'][j].astype(F32).T, ((0, ROUTER_ROWS - N_EXPERTS), (0, 0)))
            x, h2, route_t = _outproj(attn, y, x, w_out, row(p['attn_out_norm_w'][l]), row(p['norm2_w'][l]),
                                      router=jnp.concatenate(_split_bf16(wr), axis=0))
            dest, block_e, n_used, pad_rows, n_rows = _moe_plan(route_t)
            xs = _dispatch(h2, dest, pad_rows, n_rows)
            eo = _experts(xs, block_e, n_used, p['moe_w_gate'][j].astype(BF16),
                          p['moe_w_up'][j].astype(BF16), p['moe_w_down'][j].astype(BF16))
            if l + 1 == depth and len(group_rows) == 2:
                return _combine(eo, dest, x, route_t[2:4].T, nw_next, split_rows=group_rows)
            x, h = _combine(eo, dest, x, route_t[2:4].T, nw_next)
    offs = np.concatenate([[0], np.cumsum(group_rows)])
    return [x[offs[g]:offs[g + 1]] for g in range(len(group_rows))]


def kernel(x_prompt, x_sample, meta_tokens, norm1_w, w_in, q_norm_w, k_norm_w, rpb, attn_out_norm_w,
           conv_w, conv_b, dt_bias, a_log, d_skip, ssm_norm_w, w_out, norm2_w,
           ffn_w_gate, ffn_w_up, ffn_w_down, moe_router, moe_w_gate, moe_w_up, moe_w_down):
    p = dict(norm1_w=norm1_w, w_in=w_in, q_norm_w=q_norm_w, k_norm_w=k_norm_w, rpb=rpb,
             attn_out_norm_w=attn_out_norm_w, conv_w=conv_w, conv_b=conv_b, dt_bias=dt_bias, a_log=a_log,
             d_skip=d_skip, ssm_norm_w=ssm_norm_w, w_out=w_out, norm2_w=norm2_w, ffn_w_gate=ffn_w_gate,
             ffn_w_up=ffn_w_up, ffn_w_down=ffn_w_down, moe_router=moe_router, moe_w_gate=moe_w_gate,
             moe_w_up=moe_w_up, moe_w_down=moe_w_down)
    groups = (x_prompt, x_sample)
    lay = _Layout([g.shape[1] for g in groups for _ in range(g.shape[0])])
    meta_block = jnp.concatenate([jnp.zeros((META_OFF, D_MODEL), F32), meta_tokens.astype(F32)], axis=0)
    tail = lay.n_tok - lay.n_grid - CHUNK * lay.n_seq
    x = jnp.concatenate([g.reshape(-1, D_MODEL).astype(F32) for g in groups]
                        + [jnp.tile(meta_block, (lay.n_seq, 1)), jnp.zeros((tail, D_MODEL), F32)], axis=0)
    flat = _trunk(lay, x, p, [g.shape[0] * g.shape[1] for g in groups])
    return tuple(f.reshape(g.shape).astype(g.dtype) for f, g in zip(flat, groups))
```

```python
import functools

import numpy as np
import jax
import jax.numpy as jnp
from jax import lax
from jax.experimental import pallas as pl
from jax.experimental.pallas import tpu as pltpu

F32 = jnp.float32
BF16 = jnp.bfloat16

D_MODEL = 1024
N_META = 16
GRID_W = 64
NA_ROWS = 8
NA_COLS = 16
N_HEADS = 16
HEAD_DIM = 64
N_PAIRS = N_HEADS // 2
UNIT_HEADS = 4
UNIT_W = UNIT_HEADS * HEAD_DIM
N_UNITS = N_HEADS // UNIT_HEADS
ATT_W = N_HEADS * HEAD_DIM
D_SSM = 1024
SSM_GROUPS = 2
SSM_STATE = 128
GROUP_W = D_SSM // SSM_GROUPS
CONV_K = 5
CONV_CH = D_SSM + 2 * SSM_GROUPS * SSM_STATE
D_FF = 2816
N_EXPERTS = 8
D_FF_EXPERT = 3584
RMS_EPS = 1e-6
NEG_INF = -1e30
LOG2E = 1.4426950408889634

LANES = 128
ROW_TILE = D_MODEL // LANES
CHUNK = 128
META_OFF = CHUNK - N_META
TM = 512
BAND_ROWS = 16
BAND = BAND_ROWS * GRID_W
HALO_ROWS = NA_ROWS // 2
BUF_ROWS = BAND_ROWS + NA_ROWS - 1
FF_CHUNK = 256
TME = 512
ROUTER_ROWS = 16
N_PAD_RANGES = 2 * N_EXPERTS + 1
RB = 2 * TM
TC = 512
DMA_UNROLL = 8
VMEM_LIMIT = 56 * 1024 * 1024


def _cparams(sem):
    return pltpu.CompilerParams(dimension_semantics=sem, vmem_limit_bytes=VMEM_LIMIT)


def _round_up(a, b):
    return (a + b - 1) // b * b


def _const_spec(shape):
    nd = len(shape)
    return pl.BlockSpec(shape, lambda *_: (0,) * nd)


class _Layout:
    def __init__(self, seq_lens):
        self.seq_lens = tuple(seq_lens)
        self.n_seq = len(seq_lens)
        self.n_grid = sum(seq_lens)
        self.starts = np.concatenate([[0], np.cumsum(seq_lens)[:-1]]).astype(np.int64)
        self.n_tok = _round_up(self.n_grid + CHUNK * self.n_seq, TM)
        assert all(s % BAND == 0 and s // GRID_W >= NA_ROWS for s in seq_lens)

    def meta_block(self, s):
        return self.n_grid // CHUNK + s

    def attn_tables(self):
        prev, nxt, flags, mblk = [], [], [], []
        for s, (st, ln) in enumerate(zip(self.starts, self.seq_lens)):
            lo, hi = st // BAND, (st + ln) // BAND
            for b in range(lo, hi):
                per = BAND_ROWS // HALO_ROWS
                prev.append(max(b * per - 1, lo * per))
                nxt.append(min((b + 1) * per, hi * per - 1))
                flags.append((1 if b == lo else 0) | (2 if b == hi - 1 else 0))
                mblk.append((self.n_grid + CHUNK * s + META_OFF) // N_META)
        return [np.asarray(a, np.int32) for a in (prev, nxt, flags, mblk)]

    def ssd_tables(self, reverse):
        blk, prev, nxt, flags = [], [], [], []
        for s, (st, ln) in enumerate(zip(self.starts, self.seq_lens)):
            nc = ln // CHUNK
            b0 = st // CHUNK
            meta16 = (self.n_grid + CHUNK * s + META_OFF) // N_META
            steps = []
            steps.append((self.meta_block(s), 0, b0 * 8, 2 | 4))
            for c in range(nc):
                b = b0 + c
                p16 = meta16 if c == 0 else b * 8 - 1
                n16 = (b + 1) * 8 if c < nc - 1 else 0
                steps.append((b, p16, n16, 1 | (2 if c < nc - 1 else 0)))
            if reverse:
                steps = steps[::-1]
            for i, (b, p, n, f) in enumerate(steps):
                blk.append(b); prev.append(p); nxt.append(n)
                flags.append(f | (8 if i == 0 else 0))
        for b in range(self.n_grid // CHUNK + self.n_seq, self.n_tok // CHUNK):
            blk.append(b); prev.append(0); nxt.append(0)
            flags.append(8 | 16)
        return [np.asarray(a, np.int32) for a in (blk, prev, nxt, flags)]


def _rms_rows(x, w):
    ms = jnp.mean(x * x, axis=-1, keepdims=True)
    return x * lax.rsqrt(ms + RMS_EPS) * w


def _embed_kernel(bounds, *refs):
    srcs, (w_ref, x_ref, h_ref) = refs[:len(bounds)], refs[len(bounds):]
    i = pl.program_id(0)
    lo = 0
    for src, hi in zip(srcs, bounds):
        @pl.when(jnp.logical_and(i >= lo, i < hi))
        def _(src=src):
            x = src[...].astype(F32)
            x_ref[...] = x
            h_ref[...] = _rms_rows(x, w_ref[...]).astype(BF16)
        lo = hi


def _embed(parts, w):
    tiles = [a.shape[0] // TM for a in parts]
    assert all(a.shape[0] % TM == 0 for a in parts)
    bounds = tuple(int(b) for b in np.cumsum(tiles))
    starts = (0,) + bounds[:-1]
    n = bounds[-1] * TM
    src_spec = lambda s, t: pl.BlockSpec((TM, D_MODEL), lambda i: (jnp.clip(i - s, 0, t - 1), 0))
    row = pl.BlockSpec((TM, D_MODEL), lambda i: (i, 0))
    return pl.pallas_call(
        functools.partial(_embed_kernel, bounds),
        grid=(bounds[-1],),
        in_specs=[src_spec(s, t) for s, t in zip(starts, tiles)] + [_const_spec((1, D_MODEL))],
        out_specs=[row, row],
        out_shape=[jax.ShapeDtypeStruct((n, D_MODEL), F32), jax.ShapeDtypeStruct((n, D_MODEL), BF16)],
        compiler_params=_cparams(("arbitrary",)),
    )(*parts, w.reshape(1, D_MODEL))


def _inproj_kernel(h_ref, wqk_ref, wvzx_ref, wdt_ref, g_ref, nw_ref,
                   qk_ref, v_ref, z_ref, xbc_ref, dt_ref):
    h = h_ref[...]
    g = g_ref[...]
    n_qk = 2 * ATT_W // FF_CHUNK
    col = lambda c: slice(c * FF_CHUNK, (c + 1) * FF_CHUNK)
    y_next = jnp.dot(h, wqk_ref[:, col(0)], preferred_element_type=F32)
    for c in range(n_qk):
        y = y_next
        if c + 1 < n_qk:
            y_next = jnp.dot(h, wqk_ref[:, col(c + 1)], preferred_element_type=F32)
        ss = jnp.dot((y * y).astype(BF16), g, preferred_element_type=F32)
        inv = lax.rsqrt(ss * (1.0 / HEAD_DIM) + RMS_EPS)
        qk_ref[:, col(c)] = (y * inv * nw_ref[:, col(c)]).astype(BF16)
    for c in range(ATT_W // FF_CHUNK):
        sl = slice(c * FF_CHUNK, (c + 1) * FF_CHUNK)
        v_ref[:, sl] = jnp.dot(h, wvzx_ref[:, sl], preferred_element_type=F32).astype(BF16)
    for c in range(D_SSM // FF_CHUNK):
        sl = slice(c * FF_CHUNK, (c + 1) * FF_CHUNK)
        src = slice(ATT_W + c * FF_CHUNK, ATT_W + (c + 1) * FF_CHUNK)
        z_ref[:, sl] = jnp.dot(h, wvzx_ref[:, src], preferred_element_type=F32).astype(BF16)
    for c in range(CONV_CH // FF_CHUNK):
        sl = slice(c * FF_CHUNK, (c + 1) * FF_CHUNK)
        src = slice(ATT_W + D_SSM + c * FF_CHUNK, ATT_W + D_SSM + (c + 1) * FF_CHUNK)
        xbc_ref[:, sl] = jnp.dot(h, wvzx_ref[:, src], preferred_element_type=F32).astype(BF16)
    dt_ref[...] = jnp.dot(h, wdt_ref[...], preferred_element_type=F32)


def _head_sum_matrix():
    i = np.arange(FF_CHUNK)
    return jnp.asarray((i[:, None] // HEAD_DIM == i[None, :] // HEAD_DIM).astype(np.float32), BF16)


def _inproj(h, wqk, wvzx, wdt, nw):
    n = h.shape[0]
    row = lambda w: pl.BlockSpec((TM, w), lambda i: (i, 0))
    return pl.pallas_call(
        _inproj_kernel,
        grid=(n // TM,),
        in_specs=[row(D_MODEL), _const_spec(wqk.shape), _const_spec(wvzx.shape), _const_spec(wdt.shape),
                  _const_spec((FF_CHUNK, FF_CHUNK)), _const_spec((1, 2 * ATT_W))],
        out_specs=[row(2 * ATT_W), row(ATT_W), row(D_SSM), row(CONV_CH), row(LANES)],
        out_shape=[jax.ShapeDtypeStruct((n, 2 * ATT_W), BF16), jax.ShapeDtypeStruct((n, ATT_W), BF16),
                   jax.ShapeDtypeStruct((n, D_SSM), BF16), jax.ShapeDtypeStruct((n, CONV_CH), BF16),
                   jax.ShapeDtypeStruct((n, LANES), F32)],
        compiler_params=_cparams(("parallel",)),
    )(h, wqk, wvzx, wdt, _head_sum_matrix(), nw)


def _split_bf16(x):
    hi = x.astype(BF16)
    lo = (x - hi.astype(F32)).astype(BF16)
    return hi, lo


def _outproj_kernel(with_router, attn_ref, y_ref, x_ref, w_ref, aw_ref, nw_ref, *rest):
    if with_router:
        wr_ref, xo_ref, h_ref, route_ref = rest
    else:
        xo_ref, h_ref = rest
    acc = x_ref[...] + jnp.dot(y_ref[...], w_ref[ATT_W:ATT_W + D_SSM, :], preferred_element_type=F32)
    a = _rms_rows(attn_ref[...].astype(F32), aw_ref[...]).astype(BF16)
    acc = acc + jnp.dot(a, w_ref[0:ATT_W, :], preferred_element_type=F32)
    xo_ref[...] = acc
    h2 = _rms_rows(acc, nw_ref[...])
    h_ref[...] = h2.astype(BF16)
    if with_router:
        hh, hl = _split_bf16(h2)
        wr = wr_ref[...]
        lg = _nt_dot(wr, hh)
        logits = lg[0:ROUTER_ROWS] + lg[ROUTER_ROWS:] + _nt_dot(wr[0:ROUTER_ROWS], hl)
        sub = lax.broadcasted_iota(jnp.int32, logits.shape, 0)
        logits = jnp.where(sub < N_EXPERTS, logits, NEG_INF)
        m1 = jnp.max(logits, axis=0, keepdims=True)
        i1 = jnp.min(jnp.where(logits == m1, sub, ROUTER_ROWS), axis=0, keepdims=True)
        rest_l = jnp.where(sub == i1, NEG_INF, logits)
        m2 = jnp.max(rest_l, axis=0, keepdims=True)
        i2 = jnp.min(jnp.where(rest_l == m2, sub, ROUTER_ROWS), axis=0, keepdims=True)
        e = jnp.exp(m2 - m1)
        w1 = 1.0 / (1.0 + e)
        route_ref[...] = jnp.concatenate(
            [i1.astype(F32), i2.astype(F32), w1, e * w1, jnp.zeros((4, logits.shape[1]), F32)], axis=0)


def _outproj(attn, y, x, w_out, aw, nw, router=None):
    n = x.shape[0]
    row = lambda w: pl.BlockSpec((TM, w), lambda i: (i, 0))
    in_specs = [row(ATT_W), row(D_SSM), row(D_MODEL), _const_spec(w_out.shape),
                _const_spec((1, ATT_W)), _const_spec((1, D_MODEL))]
    out_specs = [row(D_MODEL), row(D_MODEL)]
    out_shape = [jax.ShapeDtypeStruct((n, D_MODEL), F32), jax.ShapeDtypeStruct((n, D_MODEL), BF16)]
    args = [attn, y, x, w_out, aw, nw]
    if router is not None:
        in_specs.append(_const_spec((2 * ROUTER_ROWS, D_MODEL)))
        out_specs.append(pl.BlockSpec((8, TM), lambda i: (0, i)))
        out_shape.append(jax.ShapeDtypeStruct((8, n), F32))
        args.append(router)
    return pl.pallas_call(
        functools.partial(_outproj_kernel, router is not None),
        grid=(n // TM,), in_specs=in_specs, out_specs=out_specs, out_shape=out_shape,
        compiler_params=_cparams(("parallel",)),
    )(*args)


def _silu(x):
    return x * (1.0 / (1.0 + jnp.exp(-x)))


def _ffn_kernel(h_ref, x_ref, wg_ref, wu_ref, wd_ref, nw_ref, xo_ref, ho_ref, act_ref):
    h = h_ref[...]
    for c in range(D_FF // FF_CHUNK):
        sl = slice(c * FF_CHUNK, (c + 1) * FF_CHUNK)
        g = jnp.dot(h, wg_ref[:, sl], preferred_element_type=F32)
        u = jnp.dot(h, wu_ref[:, sl], preferred_element_type=F32)
        act_ref[:, sl] = (_silu(g) * u).astype(BF16)
    xn = x_ref[...] + jnp.dot(act_ref[...], wd_ref[...], preferred_element_type=F32)
    xo_ref[...] = xn
    ho_ref[...] = _rms_rows(xn, nw_ref[...]).astype(BF16)


def _ffn(h, x, wg, wu, wd, nw_next):
    n = x.shape[0]
    row = lambda w: pl.BlockSpec((TM, w), lambda i: (i, 0))
    return pl.pallas_call(
        _ffn_kernel,
        grid=(n // TM,),
        in_specs=[row(D_MODEL), row(D_MODEL), _const_spec(wg.shape), _const_spec(wu.shape),
                  _const_spec(wd.shape), _const_spec((1, D_MODEL))],
        out_specs=[row(D_MODEL), row(D_MODEL)],
        out_shape=[jax.ShapeDtypeStruct((n, D_MODEL), F32), jax.ShapeDtypeStruct((n, D_MODEL), BF16)],
        scratch_shapes=[pltpu.VMEM((TM, D_FF), BF16)],
        compiler_params=_cparams(("parallel",)),
    )(h, x, wg, wu, wd, nw_next)


N_CB = GRID_W // NA_COLS
SPAN_START = (0, 0, 16, 32)
SPAN_W = (32, 48, 48, 32)
META_SLOT = (24, 0, 0, 0)
META_PER_ROW = N_META // 2
KTILE = 16
ATTN_DEPTH = 16


def _attn_bias_tables(rpb):
    i = np.arange(NA_ROWS)
    dr = np.clip(i[None, :] - i[:, None] + NA_ROWS - 1, 0, 2 * NA_ROWS - 2)
    tables = {}
    for j in range(N_CB):
        span = SPAN_W[j]
        qc = NA_COLS * j + np.arange(NA_COLS)
        kc = SPAN_START[j] + np.arange(span)
        st = np.clip(qc - NA_COLS // 2, 0, GRID_W - NA_COLS)
        valid = (kc[None, :] >= st[:, None]) & (kc[None, :] < st[:, None] + NA_COLS)
        dc = np.clip(kc[None, :] - qc[:, None] + NA_COLS - 1, 0, 2 * NA_COLS - 2)
        slot = (np.arange(span) >= META_SLOT[j]) & (np.arange(span) < META_SLOT[j] + META_PER_ROW)
        assert not valid[:, slot].any()
        b = rpb.astype(F32)[:, dr][:, :, :, dc]
        b = jnp.where(valid[None, None, None], b, NEG_INF)
        meta_vis = np.where(i[:, None] < 2, 0.0, NEG_INF) * np.ones((1, span))
        b = jnp.where(slot[None, None, None, None, :], jnp.asarray(meta_vis, F32)[None, None, :, None, :], b)
        b = b.reshape(N_UNITS, UNIT_HEADS, NA_ROWS, NA_ROWS, NA_COLS, span)
        b = jnp.transpose(b, (0, 2, 1, 4, 3, 5)).reshape(N_UNITS, NA_ROWS, UNIT_HEADS * NA_COLS, NA_ROWS * span)
        tables.setdefault(span, []).append(b)
    return [jnp.stack(t, axis=1) for _, t in sorted(tables.items())]


def _pair_queries(q):
    lo = lax.broadcasted_iota(jnp.int32, q.shape, 1) < HEAD_DIM
    zero = jnp.zeros_like(q)
    return jnp.concatenate([jnp.where(lo, q, zero), jnp.where(lo, zero, q)], axis=0)


def _head_queries(q):
    head = lax.broadcasted_iota(jnp.int32, q.shape, 1) // HEAD_DIM
    zero = jnp.zeros_like(q)
    return jnp.concatenate([jnp.where(head == h, q, zero) for h in range(UNIT_HEADS)], axis=0)


def _nt_dot(a, b):
    return lax.dot_general(a, b, (((1,), (1,)), ((), ())), preferred_element_type=F32)


def _attn_kernel(prev_ref, next_ref, flag_ref, mblk_ref,
                 q_ref, kp_ref, kc_ref, kn_ref, vp_ref, vc_ref, vn_ref, km_ref, vm_ref,
                 b32_ref, b48_ref, o_ref, *spans):
    del prev_ref, next_ref, mblk_ref
    flags = flag_ref[pl.program_id(1)]
    is_first = (flags & 1) != 0
    is_last = (flags & 2) != 0
    low = lax.broadcasted_iota(jnp.int32, (KTILE, UNIT_W), 0) < META_PER_ROW

    for refs, bufs, m_ref in (((kp_ref, kc_ref, kn_ref), spans[:N_CB], km_ref),
                              ((vp_ref, vc_ref, vn_ref), spans[N_CB:], vm_ref)):
        m = m_ref[...]
        mf = m.astype(F32)
        m_swapped = jnp.concatenate([mf[META_PER_ROW:], mf[:META_PER_ROW]], axis=0).astype(BF16)
        for rr in range(BUF_ROWS):
            if rr < HALO_ROWS:
                ref, row = refs[0], rr
            elif rr < HALO_ROWS + BAND_ROWS:
                ref, row = refs[1], rr - HALO_ROWS
            else:
                ref, row = refs[2], rr - HALO_ROWS - BAND_ROWS
            for j in range(N_CB):
                span = SPAN_W[j]
                for t in range(span // KTILE):
                    off = row * GRID_W + SPAN_START[j] + t * KTILE
                    tile = ref[off:off + KTILE, :]
                    if t == META_SLOT[j] // KTILE:
                        if META_SLOT[j] % KTILE == 0:
                            tile = jnp.where(low, m if rr % 2 == 0 else m_swapped, tile)
                        else:
                            tile = jnp.where(low, tile, m_swapped if rr % 2 == 0 else m)
                    bufs[j][rr * span + t * KTILE:rr * span + (t + 1) * KTILE, :] = tile

    lane_head = lax.broadcasted_iota(jnp.int32, (NA_COLS, UNIT_W), 1) // HEAD_DIM
    starts = []
    for ri in range(BAND_ROWS):
        ls = jnp.int32(ri)
        ls = jnp.where(is_first, jnp.maximum(ls, HALO_ROWS), ls)
        ls = jnp.where(is_last, jnp.minimum(ls, BAND_ROWS - HALO_ROWS), ls)
        starts.append((ls, HALO_ROWS + ri - ls))

    def scores(u):
        ri, j = divmod(u, N_CB)
        ls, delta = starts[ri]
        span = SPAN_W[j]
        kwin = spans[j][pl.ds(pl.multiple_of(ls * span, KTILE), NA_ROWS * span), :]
        q0 = ri * GRID_W + j * NA_COLS
        bias_ref = b32_ref if span == SPAN_W[0] else b48_ref
        return _nt_dot(_head_queries(q_ref[q0:q0 + NA_COLS, :]), kwin) + bias_ref[SPAN_W[:j].count(span), delta]

    def finish(u, s):
        ri, j = divmod(u, N_CB)
        span = SPAN_W[j]
        vwin = spans[N_CB + j][pl.ds(pl.multiple_of(starts[ri][0] * span, KTILE), NA_ROWS * span), :]
        p = jnp.exp(s - jnp.max(s, axis=-1, keepdims=True))
        l = jnp.sum(p, axis=-1, keepdims=True)
        o = jnp.dot(p.astype(BF16), vwin, preferred_element_type=F32) / l
        q0 = ri * GRID_W + j * NA_COLS
        out = o[:NA_COLS]
        for h in range(1, UNIT_HEADS):
            out = jnp.where(lane_head == h, o[h * NA_COLS:(h + 1) * NA_COLS], out)
        o_ref[q0:q0 + NA_COLS, :] = out.astype(BF16)

    n_units = BAND_ROWS * N_CB
    pending = {}
    for t in range(n_units + ATTN_DEPTH):
        if t < n_units:
            pending[t] = scores(t)
        if t >= ATTN_DEPTH:
            finish(t - ATTN_DEPTH, pending.pop(t - ATTN_DEPTH))


def _attn_meta_kernel(n_seq, qk_ref, v_ref, mbias_ref, alias_ref, o_ref):
    del alias_ref
    s = pl.program_id(0)
    o_ref[...] = jnp.zeros_like(o_ref)

    @pl.when(s < n_seq)
    def _():
        mbias = mbias_ref[...]
        for p in range(N_PAIRS):
            c = slice(p * LANES, (p + 1) * LANES)
            q = qk_ref[META_OFF:CHUNK, c]
            k = qk_ref[:, ATT_W + p * LANES:ATT_W + (p + 1) * LANES]
            qs = _pair_queries(q)
            sc = _nt_dot(qs, k) + mbias
            m = jnp.max(sc, axis=-1, keepdims=True)
            e = jnp.exp(sc - m)
            l = jnp.sum(e, axis=-1, keepdims=True)
            o = jnp.dot(e.astype(BF16), v_ref[:, c], preferred_element_type=F32) / l
            lo = lax.broadcasted_iota(jnp.int32, (N_META, LANES), 1) < HEAD_DIM
            o_ref[META_OFF:CHUNK, c] = jnp.where(lo, o[0:N_META], o[N_META:]).astype(BF16)


def _attention(lay, qk, v, rpb):
    n = qk.shape[0]
    prev, nxt, flags, mblk = lay.attn_tables()
    n_bands = lay.n_grid // BAND
    b32, b48 = _attn_bias_tables(rpb)
    lane = np.arange(LANES)
    mbias_meta = jnp.asarray(np.where(lane >= META_OFF, 0.0, NEG_INF)[None, :], F32)
    kcol = ATT_W // UNIT_W
    halo = HALO_ROWS * GRID_W
    bias_spec = lambda a: pl.BlockSpec((None,) + a.shape[1:], lambda p, b, *_: (p, 0, 0, 0, 0))
    grid_spec = pltpu.PrefetchScalarGridSpec(
        num_scalar_prefetch=4,
        grid=(N_UNITS, n_bands),
        in_specs=[
            pl.BlockSpec((BAND, UNIT_W), lambda p, b, *_: (b, p)),
            pl.BlockSpec((halo, UNIT_W), lambda p, b, pv, nx, fl, mb: (pv[b], kcol + p)),
            pl.BlockSpec((BAND, UNIT_W), lambda p, b, pv, nx, fl, mb: (b, kcol + p)),
            pl.BlockSpec((halo, UNIT_W), lambda p, b, pv, nx, fl, mb: (nx[b], kcol + p)),
            pl.BlockSpec((halo, UNIT_W), lambda p, b, pv, nx, fl, mb: (pv[b], p)),
            pl.BlockSpec((BAND, UNIT_W), lambda p, b, pv, nx, fl, mb: (b, p)),
            pl.BlockSpec((halo, UNIT_W), lambda p, b, pv, nx, fl, mb: (nx[b], p)),
            pl.BlockSpec((N_META, UNIT_W), lambda p, b, pv, nx, fl, mb: (mb[b], kcol + p)),
            pl.BlockSpec((N_META, UNIT_W), lambda p, b, pv, nx, fl, mb: (mb[b], p)),
            bias_spec(b32), bias_spec(b48),
        ],
        out_specs=pl.BlockSpec((BAND, UNIT_W), lambda p, b, *_: (b, p)),
        scratch_shapes=[pltpu.VMEM((BUF_ROWS * w, UNIT_W), BF16) for w in SPAN_W] * 2,
    )
    attn = pl.pallas_call(
        _attn_kernel, grid_spec=grid_spec,
        out_shape=jax.ShapeDtypeStruct((n, ATT_W), BF16),
        compiler_params=_cparams(("arbitrary", "arbitrary")),
    )(jnp.asarray(prev), jnp.asarray(nxt), jnp.asarray(flags), jnp.asarray(mblk),
      qk, qk, qk, qk, v, v, v, qk, v, b32, b48)
    mb0 = lay.n_grid // CHUNK
    n_tail = (n - lay.n_grid) // CHUNK
    return pl.pallas_call(
        functools.partial(_attn_meta_kernel, lay.n_seq),
        grid=(n_tail,),
        in_specs=[pl.BlockSpec((CHUNK, 2 * ATT_W), lambda s: (mb0 + s, 0)),
                  pl.BlockSpec((CHUNK, ATT_W), lambda s: (mb0 + s, 0)),
                  _const_spec((1, LANES)),
                  pl.BlockSpec(memory_space=pl.ANY)],
        out_specs=pl.BlockSpec((CHUNK, ATT_W), lambda s: (mb0 + s, 0)),
        out_shape=jax.ShapeDtypeStruct((n, ATT_W), BF16),
        input_output_aliases={3: 0},
        compiler_params=_cparams(("arbitrary",)),
    )(qk, v, mbias_meta, attn)


def _softplus(x):
    return jnp.maximum(x, 0.0) + jnp.log(1.0 + jnp.exp(-jnp.abs(x)))


def _split3_bf16(x):
    hi = x.astype(BF16)
    r = x - hi.astype(F32)
    mid = r.astype(BF16)
    lo = (r - mid.astype(F32)).astype(BF16)
    return hi, mid, lo


def _expand_heads(w, e):
    hi, lo = _split_bf16(w)
    return jnp.dot(hi, e, preferred_element_type=F32) + jnp.dot(lo, e, preferred_element_type=F32)


def _ssd_kernel(reverse, blk_ref, prev_ref, next_ref, flag_ref, *refs):
    if reverse:
        (u_ref, dt_ref, dtb_ref, arow_ref, e_ref, tri_ref, yf_ref, z_ref, dsk_ref, nw_ref,
         o_ref, state, ybuf) = refs
    else:
        (xbc_ref, xp_ref, xn_ref, shift_ref, cw_ref, dt_ref, dtb_ref, arow_ref, e_ref, tri_ref,
         o_ref, u_ref, xe, state) = refs
        ybuf = o_ref
    del blk_ref, prev_ref, next_ref
    flags = flag_ref[pl.program_id(0)]
    is_meta = (flags & 4) != 0

    @pl.when((flags & 8) != 0)
    def _():
        state[...] = jnp.zeros_like(state)

    if reverse:
        u = u_ref[...].astype(F32)
    else:
        zero = jnp.zeros((N_META, CONV_CH), BF16)
        xe[0:N_META, :] = jnp.where((flags & 1) != 0, xp_ref[...], zero)
        xe[N_META:N_META + CHUNK, :] = xbc_ref[...]
        xe[N_META + CHUNK:, :] = jnp.where((flags & 2) != 0, xn_ref[...], zero)

        @pl.when(is_meta)
        def _():
            xe[0:N_META + META_OFF, :] = jnp.zeros((N_META + META_OFF, CONV_CH), BF16)

        shifted = jnp.dot(shift_ref[...], xe[...], preferred_element_type=F32)
        half = CONV_K // 2
        u = cw_ref[CONV_K:CONV_K + 1, :] + cw_ref[half:half + 1, :] * xe[N_META:N_META + CHUNK, :].astype(F32)
        for n, k in enumerate(k for k in range(CONV_K) if k != half):
            u = u + cw_ref[k:k + 1, :] * shifted[n * CHUNK:(n + 1) * CHUNK, :]
        u = _silu(u)
        u_ref[...] = u.astype(BF16)
    xs = u[:, 0:D_SSM]
    x_bf = xs.astype(BF16)

    rid = lax.broadcasted_iota(jnp.int32, (CHUNK, 1), 0)
    valid = rid >= jnp.where(is_meta, META_OFF, 0)
    dt = jnp.where(valid, _softplus(dt_ref[...] + dtb_ref[...]), 0.0)
    a = dt * arow_ref[...]
    tri = tri_ref[...]
    ah, am, al = _split3_bf16(a)
    acum = (jnp.dot(tri, ah, preferred_element_type=F32) + jnp.dot(tri, am, preferred_element_type=F32)
            + jnp.dot(tri, al, preferred_element_type=F32))
    acum = acum * LOG2E
    acum_t = acum.T
    dt_t = dt.T
    edge = 0 if reverse else CHUNK - 1
    a_tot = acum[edge:edge + 1, :]
    e = e_ref[...]
    w1e = jnp.dot((dt * jnp.exp2(a_tot - acum)).astype(BF16), e, preferred_element_type=F32)
    w2e = jnp.dot(jnp.exp2(acum).astype(BF16), e, preferred_element_type=F32)
    decay_row = _expand_heads(jnp.broadcast_to(jnp.exp2(a_tot), (8, LANES)), e)[0:1]
    xw = (xs * w1e).astype(BF16)

    li = lax.broadcasted_iota(jnp.int32, (CHUNK, CHUNK), 0)
    si = lax.broadcasted_iota(jnp.int32, (CHUNK, CHUNK), 1)
    causal = (si >= li) if reverse else (li >= si)
    lane_lo = si < HEAD_DIM
    ho = N_HEADS if reverse else 0
    heads_per_group = N_HEADS // SSM_GROUPS
    for g in range(SSM_GROUPS):
        gsl = slice(g * GROUP_W, (g + 1) * GROUP_W)
        bg = u[:, D_SSM + g * SSM_STATE:D_SSM + (g + 1) * SSM_STATE]
        cg = u[:, D_SSM + (SSM_GROUPS + g) * SSM_STATE:D_SSM + (SSM_GROUPS + g + 1) * SSM_STATE].astype(BF16)
        cb = _nt_dot(cg, bg.astype(BF16))
        st_old = state[:, gsl]
        y_off = jnp.dot(cg, st_old.astype(BF16), preferred_element_type=F32) * w2e[:, gsl]
        st_new = jnp.dot(bg.T.astype(BF16), xw[:, gsl], preferred_element_type=F32)
        state[:, gsl] = st_old * decay_row[:, gsl] + st_new
        for j in range(heads_per_group // 2):
            col = g * GROUP_W + j * LANES
            xpair = x_bf[:, col:col + LANES]
            zero = jnp.zeros_like(xpair)
            mhs = []
            for hh in range(2):
                hi = ho + g * heads_per_group + 2 * j + hh
                seg = acum[:, hi:hi + 1] - acum_t[hi:hi + 1, :]
                lm = jnp.exp2(jnp.where(causal, seg, NEG_INF))
                mhs.append((cb * lm * dt_t[hi:hi + 1, :]).astype(BF16))
            xsplit = jnp.concatenate([jnp.where(lane_lo, xpair, zero), jnp.where(lane_lo, zero, xpair)], axis=0)
            ybuf[:, col:col + LANES] = y_off[:, j * LANES:(j + 1) * LANES] + jnp.dot(
                jnp.concatenate(mhs, axis=1), xsplit, preferred_element_type=F32)
    if reverse:
        for g in range(SSM_GROUPS):
            gsl = slice(g * GROUP_W, (g + 1) * GROUP_W)
            yg = ybuf[:, gsl] + yf_ref[:, gsl] + dsk_ref[:, gsl] * xs[:, gsl]
            yg = yg * _silu(z_ref[:, gsl].astype(F32))
            ms = jnp.mean(yg * yg, axis=-1, keepdims=True)
            o_ref[:, gsl] = (yg * lax.rsqrt(ms + RMS_EPS) * nw_ref[:, gsl]).astype(BF16)

    @pl.when((flags & 16) != 0)
    def _():
        o_ref[...] = jnp.zeros_like(o_ref)


def _ssd_constants(reverse):
    ho = N_HEADS if reverse else 0
    e = np.zeros((LANES, D_SSM), np.float32)
    for h in range(N_HEADS):
        e[ho + h, h * HEAD_DIM:(h + 1) * HEAD_DIM] = 1.0
    i = np.arange(CHUNK)
    tri = (i[None, :] >= i[:, None]) if reverse else (i[:, None] >= i[None, :])
    return jnp.asarray(e, BF16), jnp.asarray(tri.astype(np.float32), BF16)


def _shift_matrices():
    s = np.zeros(((CONV_K - 1) * CHUNK, CHUNK + 2 * N_META), np.float32)
    i = np.arange(CHUNK)
    for n, k in enumerate(k for k in range(CONV_K) if k != CONV_K // 2):
        s[n * CHUNK + i, N_META + i + k - CONV_K // 2] = 1.0
    return jnp.asarray(s, BF16)


def _ssd_pass(lay, reverse, dt, dtb, arow, fwd_in=(), rev_in=()):
    n = dt.shape[0]
    blk, prev, nxt, flags = lay.ssd_tables(reverse)
    e, tri = _ssd_constants(reverse)
    cur = lambda w: pl.BlockSpec((CHUNK, w), lambda i, bk, pv, nx, fl: (bk[i], 0))
    common = [cur(LANES), _const_spec((1, LANES)), _const_spec((1, LANES)),
              _const_spec((LANES, D_SSM)), _const_spec((CHUNK, CHUNK))]
    state = pltpu.VMEM((SSM_STATE, D_SSM), F32)
    if reverse:
        u, y_f, z, dsk, nw = rev_in
        args = (u, dt, dtb, arow, e, tri, y_f, z, dsk, nw)
        in_specs = [cur(CONV_CH)] + common + [cur(D_SSM), cur(D_SSM), _const_spec((1, D_SSM)),
                                              _const_spec((1, D_SSM))]
        out_specs = cur(D_SSM)
        out_shape = jax.ShapeDtypeStruct((n, D_SSM), BF16)
        scratch = [state, pltpu.VMEM((CHUNK, D_SSM), F32)]
    else:
        xbc, cw = fwd_in
        shift = _shift_matrices()
        args = (xbc, xbc, xbc, shift, cw, dt, dtb, arow, e, tri)
        in_specs = [cur(CONV_CH),
                    pl.BlockSpec((N_META, CONV_CH), lambda i, bk, pv, nx, fl: (pv[i], 0)),
                    pl.BlockSpec((N_META, CONV_CH), lambda i, bk, pv, nx, fl: (nx[i], 0)),
                    _const_spec(shift.shape), _const_spec((8, CONV_CH))] + common
        out_specs = [cur(D_SSM), cur(CONV_CH)]
        out_shape = [jax.ShapeDtypeStruct((n, D_SSM), F32), jax.ShapeDtypeStruct((n, CONV_CH), BF16)]
        scratch = [pltpu.VMEM((CHUNK + 2 * N_META, CONV_CH), BF16), state]
    grid_spec = pltpu.PrefetchScalarGridSpec(
        num_scalar_prefetch=4, grid=(len(blk),), in_specs=in_specs,
        out_specs=out_specs, scratch_shapes=scratch)
    return pl.pallas_call(
        functools.partial(_ssd_kernel, reverse), grid_spec=grid_spec, out_shape=out_shape,
        compiler_params=_cparams(("arbitrary",)),
    )(jnp.asarray(blk), jnp.asarray(prev), jnp.asarray(nxt), jnp.asarray(flags), *args)


def _moe_plan(route_t):
    n = route_t.shape[1]
    flat_e = route_t[0:2].T.astype(jnp.int32).reshape(-1)
    onehot = (flat_e[:, None] == jnp.arange(N_EXPERTS, dtype=jnp.int32)[None, :]).astype(jnp.int32)
    csum = jnp.cumsum(onehot, axis=0)
    counts = csum[-1]
    rank = jnp.sum(csum * onehot, axis=1) - 1
    padded = (counts + TME - 1) // TME * TME
    pends = jnp.cumsum(padded)
    pstarts = pends - padded
    dest = jnp.sum(onehot * pstarts[None, :], axis=1) + rank
    n_rows = _round_up(2 * n + N_EXPERTS * TME, TME)
    n_blocks = n_rows // TME
    block_e = jnp.sum((jnp.arange(n_blocks, dtype=jnp.int32)[:, None] * TME >= pends[None, :]).astype(jnp.int32),
                      axis=1)
    block_e = jnp.minimum(block_e, N_EXPERTS - 1)
    n_used = (pends[-1] // TME).astype(jnp.int32).reshape(1)
    tail = jnp.minimum(pends[-1] + jnp.arange(N_PAD_RANGES - N_EXPERTS + 1, dtype=jnp.int32) * TME, n_rows)
    pad_rows = jnp.concatenate([pstarts + counts, tail[:-1], pends, tail[1:]]).astype(jnp.int32)
    return dest.astype(jnp.int32), block_e, n_used, pad_rows, n_rows


def _row_tile(r):
    return pl.multiple_of(r * ROW_TILE, ROW_TILE)


def _dispatch_kernel(pad_ref, dest_hbm, h_ref, xs_hbm, idx, stage, ztile, sem_i, sem, sem_z):
    @pl.when(pl.program_id(0) == 0)
    def _():
        ztile[...] = jnp.zeros_like(ztile)

        def zero_copy(r):
            return pltpu.make_async_copy(ztile, xs_hbm.at[pl.ds(_row_tile(r), ROW_TILE)], sem_z)

        for e in range(N_PAD_RANGES):
            lo, hi = pad_ref[e], pad_ref[N_PAD_RANGES + e]

            def fill(r, c):
                zero_copy(r).start()
                return c

            def drain_fill(r, c):
                zero_copy(0).wait()
                return c

            lax.fori_loop(lo, hi, fill, 0)
            lax.fori_loop(lo, hi, drain_fill, 0)

    cp = pltpu.make_async_copy(dest_hbm.at[pl.program_id(0)], idx, sem_i)
    cp.start()
    hf = h_ref[...].astype(F32)
    for j in range(ROW_TILE):
        stage[pl.ds(j, TM, stride=ROW_TILE), :] = hf[:, j * LANES:(j + 1) * LANES]
    cp.wait()

    def row_copy(src_row, dst_row):
        return pltpu.make_async_copy(stage.at[pl.ds(_row_tile(src_row), ROW_TILE)],
                                     xs_hbm.at[pl.ds(_row_tile(dst_row), ROW_TILE)], sem)

    def issue(b, c):
        for u in range(DMA_UNROLL):
            r = b * DMA_UNROLL + u
            row_copy(lax.shift_right_logical(r, 1), idx[r]).start(priority=u % 2)
        return c

    lax.fori_loop(0, RB // DMA_UNROLL, issue, 0)

    def drain(r, c):
        row_copy(0, 0).wait()
        return c

    lax.fori_loop(0, RB, drain, 0, unroll=DMA_UNROLL)


def _dispatch(h, dest, pad_rows, n_rows):
    n = h.shape[0]
    steps = n // TM
    grid_spec = pltpu.PrefetchScalarGridSpec(
        num_scalar_prefetch=1, grid=(steps,),
        in_specs=[pl.BlockSpec(memory_space=pl.ANY), pl.BlockSpec((TM, D_MODEL), lambda i, pad: (i, 0))],
        out_specs=pl.BlockSpec(memory_space=pl.ANY),
        scratch_shapes=[pltpu.SMEM((RB,), jnp.int32), pltpu.VMEM((TM * ROW_TILE, LANES), F32),
                        pltpu.VMEM((ROW_TILE, LANES), F32),
                        pltpu.SemaphoreType.DMA, pltpu.SemaphoreType.DMA, pltpu.SemaphoreType.DMA])
    return pl.pallas_call(
        _dispatch_kernel, grid_spec=grid_spec,
        out_shape=jax.ShapeDtypeStruct((n_rows * ROW_TILE, LANES), F32),
        compiler_params=_cparams(("arbitrary",)),
    )(pad_rows, dest.reshape(steps, RB), h)


def _expert_kernel(be_ref, nu_ref, x_ref, wg_ref, wu_ref, wd_ref, o_ref, xbf, act):
    del be_ref
    used = pl.program_id(0) < nu_ref[0]

    @pl.when(used)
    def _():
        for j in range(ROW_TILE):
            xbf[:, j * LANES:(j + 1) * LANES] = x_ref[pl.ds(j, TME, stride=ROW_TILE), :].astype(BF16)
        x = xbf[...]
        for c in range(D_FF_EXPERT // FF_CHUNK):
            sl = slice(c * FF_CHUNK, (c + 1) * FF_CHUNK)
            g = jnp.dot(x, wg_ref[:, sl], preferred_element_type=F32)
            u = jnp.dot(x, wu_ref[:, sl], preferred_element_type=F32)
            act[:, sl] = (_silu(g) * u).astype(BF16)
        out = jnp.dot(act[...], wd_ref[...], preferred_element_type=F32)
        for j in range(ROW_TILE):
            o_ref[pl.ds(j, TME, stride=ROW_TILE), :] = out[:, j * LANES:(j + 1) * LANES]

    @pl.when(jnp.logical_not(used))
    def _():
        o_ref[...] = jnp.zeros_like(o_ref)


def _experts(xs, block_e, n_used, wg, wu, wd):
    n_blocks = xs.shape[0] // (TME * ROW_TILE)
    rows = pl.BlockSpec((TME * ROW_TILE, LANES), lambda i, be, nu: (i, 0))
    resident = lambda shape: pl.BlockSpec((None,) + shape, lambda i, be, nu: (be[i], 0, 0),
                                          pipeline_mode=pl.Buffered(1))
    grid_spec = pltpu.PrefetchScalarGridSpec(
        num_scalar_prefetch=2, grid=(n_blocks,),
        in_specs=[pl.BlockSpec((TME * ROW_TILE, LANES), lambda i, be, nu: (jnp.minimum(i, nu[0] - 1), 0)),
                  resident((D_MODEL, D_FF_EXPERT)), resident((D_MODEL, D_FF_EXPERT)),
                  resident((D_FF_EXPERT, D_MODEL))],
        out_specs=rows,
        scratch_shapes=[pltpu.VMEM((TME, D_MODEL), BF16), pltpu.VMEM((TME, D_FF_EXPERT), BF16)])
    return pl.pallas_call(
        _expert_kernel, grid_spec=grid_spec,
        out_shape=jax.ShapeDtypeStruct(xs.shape, F32),
        compiler_params=_cparams(("arbitrary",)),
    )(block_e, n_used, xs, wg, wu, wd)


def _combine_kernel(split, dest_hbm, eo_hbm, x_ref, route_ref, nw_ref, out_a, out_b, idx, gbuf, xn, sem_i, sem_g):
    i = pl.program_id(0)
    n = pl.num_programs(0)

    def idx_copy(step, slot):
        return pltpu.make_async_copy(dest_hbm.at[step], idx.at[slot], sem_i.at[slot])

    def row_copy(slot, src_row, k, t):
        return pltpu.make_async_copy(eo_hbm.at[pl.ds(_row_tile(src_row), ROW_TILE)],
                                     gbuf.at[slot, k, pl.ds(_row_tile(t), ROW_TILE)], sem_g.at[slot])

    def issue_rows(slot):
        def issue(b, c):
            for u in range(DMA_UNROLL):
                r = b * DMA_UNROLL + u
                row_copy(slot, idx[slot, r], u % 2, b * (DMA_UNROLL // 2) + u // 2).start(priority=u % 2)
            return c

        lax.fori_loop(0, 2 * TC // DMA_UNROLL, issue, 0)

    @pl.when(i == 0)
    def _():
        idx_copy(0, 0).start()
        idx_copy(0, 0).wait()
        issue_rows(0)

        @pl.when(n > 1)
        def _():
            idx_copy(1, 1).start()

    for slot in range(2):
        @pl.when(jnp.logical_and(i % 2 == slot, i + 1 < n))
        def _(slot=slot):
            idx_copy(0, 1 - slot).wait()
            issue_rows(1 - slot)

            @pl.when(i + 2 < n)
            def _():
                idx_copy(i + 2, slot).start()

    w1 = route_ref[:, 0:1]
    w2 = route_ref[:, 1:2]
    for slot in range(2):
        @pl.when(i % 2 == slot)
        def _(slot=slot):
            def drain(r, c):
                row_copy(slot, 0, 0, 0).wait()
                return c

            lax.fori_loop(0, 2 * TC, drain, 0, unroll=DMA_UNROLL)
            for j in range(ROW_TILE):
                sl = slice(j * LANES, (j + 1) * LANES)
                xn[:, sl] = x_ref[:, sl] + (gbuf[slot, 0, pl.ds(j, TC, stride=ROW_TILE), :] * w1
                                            + gbuf[slot, 1, pl.ds(j, TC, stride=ROW_TILE), :] * w2)

    if split is None:
        out_a[...] = xn[...]
        out_b[...] = _rms_rows(xn[...], nw_ref[...]).astype(BF16)
    else:
        tiles_a, tiles_b = split

        @pl.when(i < tiles_a)
        def _():
            out_a[...] = xn[...]

        @pl.when(jnp.logical_and(i >= tiles_a, i < tiles_a + tiles_b))
        def _():
            out_b[...] = xn[...]


def _combine(eo, dest, x, route, nw_next, split_rows=None):
    n = x.shape[0]
    steps = n // TC
    row = lambda w: pl.BlockSpec((TC, w), lambda i: (i, 0))
    if split_rows is None:
        split = None
        out_specs = [row(D_MODEL), row(D_MODEL)]
        out_shape = [jax.ShapeDtypeStruct((n, D_MODEL), F32), jax.ShapeDtypeStruct((n, D_MODEL), BF16)]
    else:
        ta, tb = split = tuple(r // TC for r in split_rows)
        out_specs = [pl.BlockSpec((TC, D_MODEL), lambda i: (jnp.minimum(i, ta - 1), 0)),
                     pl.BlockSpec((TC, D_MODEL), lambda i: (jnp.clip(i - ta, 0, tb - 1), 0))]
        out_shape = [jax.ShapeDtypeStruct((r, D_MODEL), F32) for r in split_rows]
    return pl.pallas_call(
        functools.partial(_combine_kernel, split),
        grid=(steps,),
        in_specs=[pl.BlockSpec(memory_space=pl.ANY), pl.BlockSpec(memory_space=pl.ANY),
                  row(D_MODEL), row(2), _const_spec((1, D_MODEL))],
        out_specs=out_specs, out_shape=out_shape,
        scratch_shapes=[pltpu.SMEM((2, 2 * TC), jnp.int32), pltpu.VMEM((2, 2, TC * ROW_TILE, LANES), F32),
                        pltpu.VMEM((TC, D_MODEL), F32),
                        pltpu.SemaphoreType.DMA((2,)), pltpu.SemaphoreType.DMA((2,))],
        compiler_params=_cparams(("arbitrary",)),
    )(dest.reshape(steps, 2 * TC), eo, x, route, nw_next)


def _trunk(lay, x, h, p, group_rows):
    depth = p['norm1_w'].shape[0]
    row = lambda v: v.reshape(1, -1).astype(F32)
    for l in range(depth):
        w_in = p['w_in'][l]
        wqk = w_in[:, 0:2 * ATT_W].astype(BF16)
        wvzx = w_in[:, 2 * ATT_W:3 * ATT_W + D_SSM + CONV_CH].astype(BF16)
        wdt = jnp.pad(w_in[:, 3 * ATT_W + D_SSM + CONV_CH:], ((0, 0), (0, LANES - 2 * N_HEADS))).astype(BF16)
        nw_qk = jnp.concatenate([jnp.tile(p['q_norm_w'][l].astype(F32), N_HEADS) * (HEAD_DIM ** -0.5),
                                 jnp.tile(p['k_norm_w'][l].astype(F32), N_HEADS)]).reshape(1, -1)
        qk, v, z, xbc, dt_raw = _inproj(h, wqk, wvzx, wdt, nw_qk)

        attn = _attention(lay, qk, v, p['rpb'][l])

        cw = jnp.concatenate([p['conv_w'][l].astype(F32).T, p['conv_b'][l].astype(F32)[None],
                              jnp.zeros((8 - CONV_K - 1, CONV_CH), F32)], axis=0)
        pad = jnp.zeros((LANES - 2 * N_HEADS,), F32)
        dtb = jnp.concatenate([p['dt_bias'][l].astype(F32).reshape(-1), pad]).reshape(1, -1)
        arow = jnp.concatenate([-jnp.exp(p['a_log'][l].astype(F32)).reshape(-1), pad]).reshape(1, -1)
        y_f, u = _ssd_pass(lay, False, dt_raw, dtb, arow, fwd_in=(xbc, cw))
        dsk = jnp.repeat(p['d_skip'][l].astype(F32), HEAD_DIM).reshape(1, -1)
        y = _ssd_pass(lay, True, dt_raw, dtb, arow, rev_in=(u, y_f, z, dsk, row(p['ssm_norm_w'][l])))

        w_out = p['w_out'][l].astype(BF16)
        nw_next = row(p['norm1_w'][l + 1]) if l + 1 < depth else jnp.ones((1, D_MODEL), F32)
        j = l // 2
        if l % 2 == 0:
            x, h2 = _outproj(attn, y, x, w_out, row(p['attn_out_norm_w'][l]), row(p['norm2_w'][l]))
            x, h = _ffn(h2, x, p['ffn_w_gate'][j].astype(BF16), p['ffn_w_up'][j].astype(BF16),
                        p['ffn_w_down'][j].astype(BF16), nw_next)
        else:
            wr = jnp.pad(p['moe_router'][j].astype(F32).T, ((0, ROUTER_ROWS - N_EXPERTS), (0, 0)))
            x, h2, route_t = _outproj(attn, y, x, w_out, row(p['attn_out_norm_w'][l]), row(p['norm2_w'][l]),
                                      router=jnp.concatenate(_split_bf16(wr), axis=0))
            dest, block_e, n_used, pad_rows, n_rows = _moe_plan(route_t)
            xs = _dispatch(h2, dest, pad_rows, n_rows)
            eo = _experts(xs, block_e, n_used, p['moe_w_gate'][j].astype(BF16),
                          p['moe_w_up'][j].astype(BF16), p['moe_w_down'][j].astype(BF16))
            if l + 1 == depth and len(group_rows) == 2:
                return _combine(eo, dest, x, route_t[2:4].T, nw_next, split_rows=group_rows)
            x, h = _combine(eo, dest, x, route_t[2:4].T, nw_next)
    offs = np.concatenate([[0], np.cumsum(group_rows)])
    return [x[offs[g]:offs[g + 1]] for g in range(len(group_rows))]


def kernel(x_prompt, x_sample, meta_tokens, norm1_w, w_in, q_norm_w, k_norm_w, rpb, attn_out_norm_w,
           conv_w, conv_b, dt_bias, a_log, d_skip, ssm_norm_w, w_out, norm2_w,
           ffn_w_gate, ffn_w_up, ffn_w_down, moe_router, moe_w_gate, moe_w_up, moe_w_down):
    p = dict(norm1_w=norm1_w, w_in=w_in, q_norm_w=q_norm_w, k_norm_w=k_norm_w, rpb=rpb,
             attn_out_norm_w=attn_out_norm_w, conv_w=conv_w, conv_b=conv_b, dt_bias=dt_bias, a_log=a_log,
             d_skip=d_skip, ssm_norm_w=ssm_norm_w, w_out=w_out, norm2_w=norm2_w, ffn_w_gate=ffn_w_gate,
             ffn_w_up=ffn_w_up, ffn_w_down=ffn_w_down, moe_router=moe_router, moe_w_gate=moe_w_gate,
             moe_w_up=moe_w_up, moe_w_down=moe_w_down)
    groups = (x_prompt, x_sample)
    lay = _Layout([g.shape[1] for g in groups for _ in range(g.shape[0])])
    meta_block = jnp.concatenate([jnp.zeros((META_OFF, D_MODEL), F32), meta_tokens.astype(F32)], axis=0)
    tail = lay.n_tok - lay.n_grid - CHUNK * lay.n_seq
    tail_rows = jnp.concatenate([jnp.tile(meta_block, (lay.n_seq, 1)), jnp.zeros((tail, D_MODEL), F32)], axis=0)
    x, h = _embed([g.reshape(-1, D_MODEL) for g in groups] + [tail_rows], norm1_w[0])
    flat = _trunk(lay, x, h, p, [g.shape[0] * g.shape[1] for g in groups])
    return tuple(f.reshape(g.shape).astype(g.dtype) for f, g in zip(flat, groups))
```

```python
import functools

import numpy as np
import jax
import jax.numpy as jnp
from jax import lax
from jax.experimental import pallas as pl
from jax.experimental.pallas import tpu as pltpu

F32 = jnp.float32
BF16 = jnp.bfloat16

D_MODEL = 1024
N_META = 16
GRID_W = 64
NA_ROWS = 8
NA_COLS = 16
N_HEADS = 16
HEAD_DIM = 64
N_PAIRS = N_HEADS // 2
UNIT_HEADS = 4
UNIT_W = UNIT_HEADS * HEAD_DIM
N_UNITS = N_HEADS // UNIT_HEADS
ATT_W = N_HEADS * HEAD_DIM
D_SSM = 1024
SSM_GROUPS = 2
SSM_STATE = 128
GROUP_W = D_SSM // SSM_GROUPS
CONV_K = 5
CONV_CH = D_SSM + 2 * SSM_GROUPS * SSM_STATE
D_FF = 2816
N_EXPERTS = 8
D_FF_EXPERT = 3584
RMS_EPS = 1e-6
NEG_INF = -1e30
LOG2E = 1.4426950408889634

LANES = 128
ROW_TILE = D_MODEL // LANES
CHUNK = 128
META_OFF = CHUNK - N_META
TM = 512
BAND_ROWS = 16
BAND = BAND_ROWS * GRID_W
HALO_ROWS = NA_ROWS // 2
BUF_ROWS = BAND_ROWS + NA_ROWS - 1
FF_CHUNK = 256
TME = 512
ROUTER_ROWS = 16
N_PAD_RANGES = 2 * N_EXPERTS + 1
RB = 2 * TM
TC = 512
DMA_UNROLL = 8
VMEM_LIMIT = 56 * 1024 * 1024


def _cparams(sem):
    return pltpu.CompilerParams(dimension_semantics=sem, vmem_limit_bytes=VMEM_LIMIT)


def _round_up(a, b):
    return (a + b - 1) // b * b


def _const_spec(shape):
    nd = len(shape)
    return pl.BlockSpec(shape, lambda *_: (0,) * nd)


class _Layout:
    def __init__(self, seq_lens):
        self.seq_lens = tuple(seq_lens)
        self.n_seq = len(seq_lens)
        self.n_grid = sum(seq_lens)
        self.starts = np.concatenate([[0], np.cumsum(seq_lens)[:-1]]).astype(np.int64)
        self.n_tok = _round_up(self.n_grid + CHUNK * self.n_seq, TM)
        assert all(s % BAND == 0 and s // GRID_W >= NA_ROWS for s in seq_lens)

    def meta_block(self, s):
        return self.n_grid // CHUNK + s

    def attn_tables(self):
        prev, nxt, flags, mblk = [], [], [], []
        for s, (st, ln) in enumerate(zip(self.starts, self.seq_lens)):
            lo, hi = st // BAND, (st + ln) // BAND
            for b in range(lo, hi):
                per = BAND_ROWS // HALO_ROWS
                prev.append(max(b * per - 1, lo * per))
                nxt.append(min((b + 1) * per, hi * per - 1))
                flags.append((1 if b == lo else 0) | (2 if b == hi - 1 else 0))
                mblk.append((self.n_grid + CHUNK * s + META_OFF) // N_META)
        return [np.asarray(a, np.int32) for a in (prev, nxt, flags, mblk)]

    def ssd_tables(self, reverse):
        blk, prev, nxt, flags = [], [], [], []
        for s, (st, ln) in enumerate(zip(self.starts, self.seq_lens)):
            nc = ln // CHUNK
            b0 = st // CHUNK
            meta16 = (self.n_grid + CHUNK * s + META_OFF) // N_META
            steps = []
            steps.append((self.meta_block(s), 0, b0 * 8, 2 | 4))
            for c in range(nc):
                b = b0 + c
                p16 = meta16 if c == 0 else b * 8 - 1
                n16 = (b + 1) * 8 if c < nc - 1 else 0
                steps.append((b, p16, n16, 1 | (2 if c < nc - 1 else 0)))
            if reverse:
                steps = steps[::-1]
            for i, (b, p, n, f) in enumerate(steps):
                blk.append(b); prev.append(p); nxt.append(n)
                flags.append(f | (8 if i == 0 else 0))
        for b in range(self.n_grid // CHUNK + self.n_seq, self.n_tok // CHUNK):
            blk.append(b); prev.append(0); nxt.append(0)
            flags.append(8 | 16)
        return [np.asarray(a, np.int32) for a in (blk, prev, nxt, flags)]


def _rms_rows(x, w):
    ms = jnp.mean(x * x, axis=-1, keepdims=True)
    return x * lax.rsqrt(ms + RMS_EPS) * w


def _embed_kernel(bounds, *refs):
    srcs, (w_ref, x_ref, h_ref) = refs[:len(bounds)], refs[len(bounds):]
    i = pl.program_id(0)
    lo = 0
    for src, hi in zip(srcs, bounds):
        @pl.when(jnp.logical_and(i >= lo, i < hi))
        def _(src=src):
            x = src[...].astype(F32)
            x_ref[...] = x
            h_ref[...] = _rms_rows(x, w_ref[...]).astype(BF16)
        lo = hi


def _embed(parts, w):
    tiles = [a.shape[0] // TM for a in parts]
    assert all(a.shape[0] % TM == 0 for a in parts)
    bounds = tuple(int(b) for b in np.cumsum(tiles))
    starts = (0,) + bounds[:-1]
    n = bounds[-1] * TM
    src_spec = lambda s, t: pl.BlockSpec((TM, D_MODEL), lambda i: (jnp.clip(i - s, 0, t - 1), 0))
    row = pl.BlockSpec((TM, D_MODEL), lambda i: (i, 0))
    return pl.pallas_call(
        functools.partial(_embed_kernel, bounds),
        grid=(bounds[-1],),
        in_specs=[src_spec(s, t) for s, t in zip(starts, tiles)] + [_const_spec((1, D_MODEL))],
        out_specs=[row, row],
        out_shape=[jax.ShapeDtypeStruct((n, D_MODEL), F32), jax.ShapeDtypeStruct((n, D_MODEL), BF16)],
        compiler_params=_cparams(("arbitrary",)),
    )(*parts, w.reshape(1, D_MODEL))


def _inproj_kernel(h_ref, wqk_ref, wvzx_ref, wdt_ref, g_ref, nw_ref,
                   qk_ref, v_ref, z_ref, xbc_ref, dt_ref):
    h = h_ref[...]
    g = g_ref[...]
    n_qk = 2 * ATT_W // FF_CHUNK
    col = lambda c: slice(c * FF_CHUNK, (c + 1) * FF_CHUNK)
    y_next = jnp.dot(h, wqk_ref[:, col(0)], preferred_element_type=F32)
    for c in range(n_qk):
        y = y_next
        if c + 1 < n_qk:
            y_next = jnp.dot(h, wqk_ref[:, col(c + 1)], preferred_element_type=F32)
        ss = jnp.dot((y * y).astype(BF16), g, preferred_element_type=F32)
        inv = lax.rsqrt(ss * (1.0 / HEAD_DIM) + RMS_EPS)
        qk_ref[:, col(c)] = (y * inv * nw_ref[:, col(c)]).astype(BF16)
    for c in range(ATT_W // FF_CHUNK):
        sl = slice(c * FF_CHUNK, (c + 1) * FF_CHUNK)
        v_ref[:, sl] = jnp.dot(h, wvzx_ref[:, sl], preferred_element_type=F32).astype(BF16)
    for c in range(D_SSM // FF_CHUNK):
        sl = slice(c * FF_CHUNK, (c + 1) * FF_CHUNK)
        src = slice(ATT_W + c * FF_CHUNK, ATT_W + (c + 1) * FF_CHUNK)
        z_ref[:, sl] = jnp.dot(h, wvzx_ref[:, src], preferred_element_type=F32).astype(BF16)
    for c in range(CONV_CH // FF_CHUNK):
        sl = slice(c * FF_CHUNK, (c + 1) * FF_CHUNK)
        src = slice(ATT_W + D_SSM + c * FF_CHUNK, ATT_W + D_SSM + (c + 1) * FF_CHUNK)
        xbc_ref[:, sl] = jnp.dot(h, wvzx_ref[:, src], preferred_element_type=F32).astype(BF16)
    dt_ref[...] = jnp.dot(h, wdt_ref[...], preferred_element_type=F32)


def _head_sum_matrix():
    i = np.arange(FF_CHUNK)
    return jnp.asarray((i[:, None] // HEAD_DIM == i[None, :] // HEAD_DIM).astype(np.float32), BF16)


def _inproj(h, wqk, wvzx, wdt, nw):
    n = h.shape[0]
    row = lambda w: pl.BlockSpec((TM, w), lambda i: (i, 0))
    return pl.pallas_call(
        _inproj_kernel,
        grid=(n // TM,),
        in_specs=[row(D_MODEL), _const_spec(wqk.shape), _const_spec(wvzx.shape), _const_spec(wdt.shape),
                  _const_spec((FF_CHUNK, FF_CHUNK)), _const_spec((1, 2 * ATT_W))],
        out_specs=[row(2 * ATT_W), row(ATT_W), row(D_SSM), row(CONV_CH), row(LANES)],
        out_shape=[jax.ShapeDtypeStruct((n, 2 * ATT_W), BF16), jax.ShapeDtypeStruct((n, ATT_W), BF16),
                   jax.ShapeDtypeStruct((n, D_SSM), BF16), jax.ShapeDtypeStruct((n, CONV_CH), BF16),
                   jax.ShapeDtypeStruct((n, LANES), F32)],
        compiler_params=_cparams(("parallel",)),
    )(h, wqk, wvzx, wdt, _head_sum_matrix(), nw)


def _split_bf16(x):
    hi = x.astype(BF16)
    lo = (x - hi.astype(F32)).astype(BF16)
    return hi, lo


def _outproj_kernel(with_router, attn_ref, y_ref, x_ref, w_ref, aw_ref, nw_ref, *rest):
    if with_router:
        wr_ref, xo_ref, h_ref, route_ref = rest
    else:
        xo_ref, h_ref = rest
    acc = x_ref[...] + jnp.dot(y_ref[...], w_ref[ATT_W:ATT_W + D_SSM, :], preferred_element_type=F32)
    a = _rms_rows(attn_ref[...].astype(F32), aw_ref[...]).astype(BF16)
    acc = acc + jnp.dot(a, w_ref[0:ATT_W, :], preferred_element_type=F32)
    xo_ref[...] = acc
    h2 = _rms_rows(acc, nw_ref[...])
    h_ref[...] = h2.astype(BF16)
    if with_router:
        hh, hl = _split_bf16(h2)
        wr = wr_ref[...]
        lg = _nt_dot(wr, hh)
        logits = lg[0:ROUTER_ROWS] + lg[ROUTER_ROWS:] + _nt_dot(wr[0:ROUTER_ROWS], hl)
        sub = lax.broadcasted_iota(jnp.int32, logits.shape, 0)
        logits = jnp.where(sub < N_EXPERTS, logits, NEG_INF)
        m1 = jnp.max(logits, axis=0, keepdims=True)
        i1 = jnp.min(jnp.where(logits == m1, sub, ROUTER_ROWS), axis=0, keepdims=True)
        rest_l = jnp.where(sub == i1, NEG_INF, logits)
        m2 = jnp.max(rest_l, axis=0, keepdims=True)
        i2 = jnp.min(jnp.where(rest_l == m2, sub, ROUTER_ROWS), axis=0, keepdims=True)
        e = jnp.exp(m2 - m1)
        w1 = 1.0 / (1.0 + e)
        route_ref[...] = jnp.concatenate(
            [i1.astype(F32), i2.astype(F32), w1, e * w1, jnp.zeros((4, logits.shape[1]), F32)], axis=0)


def _outproj(attn, y, x, w_out, aw, nw, router=None):
    n = x.shape[0]
    row = lambda w: pl.BlockSpec((TM, w), lambda i: (i, 0))
    in_specs = [row(ATT_W), row(D_SSM), row(D_MODEL), _const_spec(w_out.shape),
                _const_spec((1, ATT_W)), _const_spec((1, D_MODEL))]
    out_specs = [row(D_MODEL), row(D_MODEL)]
    out_shape = [jax.ShapeDtypeStruct((n, D_MODEL), F32), jax.ShapeDtypeStruct((n, D_MODEL), BF16)]
    args = [attn, y, x, w_out, aw, nw]
    if router is not None:
        in_specs.append(_const_spec((2 * ROUTER_ROWS, D_MODEL)))
        out_specs.append(pl.BlockSpec((8, TM), lambda i: (0, i)))
        out_shape.append(jax.ShapeDtypeStruct((8, n), F32))
        args.append(router)
    return pl.pallas_call(
        functools.partial(_outproj_kernel, router is not None),
        grid=(n // TM,), in_specs=in_specs, out_specs=out_specs, out_shape=out_shape,
        compiler_params=_cparams(("parallel",)),
    )(*args)


def _silu(x):
    return x * (1.0 / (1.0 + jnp.exp(-x)))


def _ffn_kernel(h_ref, x_ref, wg_ref, wu_ref, wd_ref, nw_ref, xo_ref, ho_ref, act_ref):
    h = h_ref[...]
    for c in range(D_FF // FF_CHUNK):
        sl = slice(c * FF_CHUNK, (c + 1) * FF_CHUNK)
        g = jnp.dot(h, wg_ref[:, sl], preferred_element_type=F32)
        u = jnp.dot(h, wu_ref[:, sl], preferred_element_type=F32)
        act_ref[:, sl] = (_silu(g) * u).astype(BF16)
    xn = x_ref[...] + jnp.dot(act_ref[...], wd_ref[...], preferred_element_type=F32)
    xo_ref[...] = xn
    ho_ref[...] = _rms_rows(xn, nw_ref[...]).astype(BF16)


def _ffn(h, x, wg, wu, wd, nw_next):
    n = x.shape[0]
    row = lambda w: pl.BlockSpec((TM, w), lambda i: (i, 0))
    return pl.pallas_call(
        _ffn_kernel,
        grid=(n // TM,),
        in_specs=[row(D_MODEL), row(D_MODEL), _const_spec(wg.shape), _const_spec(wu.shape),
                  _const_spec(wd.shape), _const_spec((1, D_MODEL))],
        out_specs=[row(D_MODEL), row(D_MODEL)],
        out_shape=[jax.ShapeDtypeStruct((n, D_MODEL), F32), jax.ShapeDtypeStruct((n, D_MODEL), BF16)],
        scratch_shapes=[pltpu.VMEM((TM, D_FF), BF16)],
        compiler_params=_cparams(("parallel",)),
    )(h, x, wg, wu, wd, nw_next)


N_CB = GRID_W // NA_COLS
SPAN_START = (0, 0, 16, 32)
SPAN_W = (32, 48, 48, 32)
META_SLOT = (24, 0, 0, 0)
META_PER_ROW = N_META // 2
KTILE = 16
ATTN_DEPTH = 16


def _attn_bias_tables(rpb):
    i = np.arange(NA_ROWS)
    dr = np.clip(i[None, :] - i[:, None] + NA_ROWS - 1, 0, 2 * NA_ROWS - 2)
    tables = {}
    for j in range(N_CB):
        span = SPAN_W[j]
        qc = NA_COLS * j + np.arange(NA_COLS)
        kc = SPAN_START[j] + np.arange(span)
        st = np.clip(qc - NA_COLS // 2, 0, GRID_W - NA_COLS)
        valid = (kc[None, :] >= st[:, None]) & (kc[None, :] < st[:, None] + NA_COLS)
        dc = np.clip(kc[None, :] - qc[:, None] + NA_COLS - 1, 0, 2 * NA_COLS - 2)
        slot = (np.arange(span) >= META_SLOT[j]) & (np.arange(span) < META_SLOT[j] + META_PER_ROW)
        assert not valid[:, slot].any()
        b = rpb.astype(F32)[:, dr][:, :, :, dc]
        b = jnp.where(valid[None, None, None], b, NEG_INF)
        meta_vis = np.where(i[:, None] < 2, 0.0, NEG_INF) * np.ones((1, span))
        b = jnp.where(slot[None, None, None, None, :], jnp.asarray(meta_vis, F32)[None, None, :, None, :], b)
        b = b.reshape(N_UNITS, UNIT_HEADS, NA_ROWS, NA_ROWS, NA_COLS, span)
        b = jnp.transpose(b, (0, 2, 1, 4, 3, 5)).reshape(N_UNITS, NA_ROWS, UNIT_HEADS * NA_COLS, NA_ROWS * span)
        tables.setdefault(span, []).append(b)
    return [jnp.stack(t, axis=1) for _, t in sorted(tables.items())]


def _pair_queries(q):
    lo = lax.broadcasted_iota(jnp.int32, q.shape, 1) < HEAD_DIM
    zero = jnp.zeros_like(q)
    return jnp.concatenate([jnp.where(lo, q, zero), jnp.where(lo, zero, q)], axis=0)


def _head_queries(q):
    head = lax.broadcasted_iota(jnp.int32, q.shape, 1) // HEAD_DIM
    zero = jnp.zeros_like(q)
    return jnp.concatenate([jnp.where(head == h, q, zero) for h in range(UNIT_HEADS)], axis=0)


def _nt_dot(a, b):
    return lax.dot_general(a, b, (((1,), (1,)), ((), ())), preferred_element_type=F32)


def _attn_kernel(prev_ref, next_ref, flag_ref, mblk_ref,
                 q_ref, kp_ref, kc_ref, kn_ref, vp_ref, vc_ref, vn_ref, km_ref, vm_ref,
                 b32_ref, b48_ref, o_ref, *spans):
    del prev_ref, next_ref, mblk_ref
    flags = flag_ref[pl.program_id(1)]
    is_first = (flags & 1) != 0
    is_last = (flags & 2) != 0
    low = lax.broadcasted_iota(jnp.int32, (KTILE, UNIT_W), 0) < META_PER_ROW

    for refs, bufs, m_ref in (((kp_ref, kc_ref, kn_ref), spans[:N_CB], km_ref),
                              ((vp_ref, vc_ref, vn_ref), spans[N_CB:], vm_ref)):
        m = m_ref[...]
        mf = m.astype(F32)
        m_swapped = jnp.concatenate([mf[META_PER_ROW:], mf[:META_PER_ROW]], axis=0).astype(BF16)
        for rr in range(BUF_ROWS):
            if rr < HALO_ROWS:
                ref, row = refs[0], rr
            elif rr < HALO_ROWS + BAND_ROWS:
                ref, row = refs[1], rr - HALO_ROWS
            else:
                ref, row = refs[2], rr - HALO_ROWS - BAND_ROWS
            for j in range(N_CB):
                span = SPAN_W[j]
                for t in range(span // KTILE):
                    off = row * GRID_W + SPAN_START[j] + t * KTILE
                    tile = ref[off:off + KTILE, :]
                    if t == META_SLOT[j] // KTILE:
                        if META_SLOT[j] % KTILE == 0:
                            tile = jnp.where(low, m if rr % 2 == 0 else m_swapped, tile)
                        else:
                            tile = jnp.where(low, tile, m_swapped if rr % 2 == 0 else m)
                    bufs[j][rr * span + t * KTILE:rr * span + (t + 1) * KTILE, :] = tile

    lane_head = lax.broadcasted_iota(jnp.int32, (NA_COLS, UNIT_W), 1) // HEAD_DIM
    starts = []
    for ri in range(BAND_ROWS):
        ls = jnp.int32(ri)
        ls = jnp.where(is_first, jnp.maximum(ls, HALO_ROWS), ls)
        ls = jnp.where(is_last, jnp.minimum(ls, BAND_ROWS - HALO_ROWS), ls)
        starts.append((ls, HALO_ROWS + ri - ls))

    def scores(u):
        ri, j = divmod(u, N_CB)
        ls, delta = starts[ri]
        span = SPAN_W[j]
        kwin = spans[j][pl.ds(pl.multiple_of(ls * span, KTILE), NA_ROWS * span), :]
        q0 = ri * GRID_W + j * NA_COLS
        bias_ref = b32_ref if span == SPAN_W[0] else b48_ref
        return _nt_dot(_head_queries(q_ref[q0:q0 + NA_COLS, :]), kwin) + bias_ref[SPAN_W[:j].count(span), delta]

    def finish(u, s):
        ri, j = divmod(u, N_CB)
        span = SPAN_W[j]
        vwin = spans[N_CB + j][pl.ds(pl.multiple_of(starts[ri][0] * span, KTILE), NA_ROWS * span), :]
        p = jnp.exp(s - jnp.max(s, axis=-1, keepdims=True))
        l = jnp.sum(p, axis=-1, keepdims=True)
        o = jnp.dot(p.astype(BF16), vwin, preferred_element_type=F32) / l
        q0 = ri * GRID_W + j * NA_COLS
        out = o[:NA_COLS]
        for h in range(1, UNIT_HEADS):
            out = jnp.where(lane_head == h, o[h * NA_COLS:(h + 1) * NA_COLS], out)
        o_ref[q0:q0 + NA_COLS, :] = out.astype(BF16)

    n_units = BAND_ROWS * N_CB
    pending = {}
    for t in range(n_units + ATTN_DEPTH):
        if t < n_units:
            pending[t] = scores(t)
        if t >= ATTN_DEPTH:
            finish(t - ATTN_DEPTH, pending.pop(t - ATTN_DEPTH))


def _attn_meta_kernel(n_seq, qk_ref, v_ref, mbias_ref, alias_ref, o_ref):
    del alias_ref
    s = pl.program_id(0)
    o_ref[...] = jnp.zeros_like(o_ref)

    @pl.when(s < n_seq)
    def _():
        mbias = mbias_ref[...]
        for p in range(N_PAIRS):
            c = slice(p * LANES, (p + 1) * LANES)
            q = qk_ref[META_OFF:CHUNK, c]
            k = qk_ref[:, ATT_W + p * LANES:ATT_W + (p + 1) * LANES]
            qs = _pair_queries(q)
            sc = _nt_dot(qs, k) + mbias
            m = jnp.max(sc, axis=-1, keepdims=True)
            e = jnp.exp(sc - m)
            l = jnp.sum(e, axis=-1, keepdims=True)
            o = jnp.dot(e.astype(BF16), v_ref[:, c], preferred_element_type=F32) / l
            lo = lax.broadcasted_iota(jnp.int32, (N_META, LANES), 1) < HEAD_DIM
            o_ref[META_OFF:CHUNK, c] = jnp.where(lo, o[0:N_META], o[N_META:]).astype(BF16)


def _attention(lay, qk, v, rpb):
    n = qk.shape[0]
    prev, nxt, flags, mblk = lay.attn_tables()
    n_bands = lay.n_grid // BAND
    b32, b48 = _attn_bias_tables(rpb)
    lane = np.arange(LANES)
    mbias_meta = jnp.asarray(np.where(lane >= META_OFF, 0.0, NEG_INF)[None, :], F32)
    kcol = ATT_W // UNIT_W
    halo = HALO_ROWS * GRID_W
    bias_spec = lambda a: pl.BlockSpec((None,) + a.shape[1:], lambda p, b, *_: (p, 0, 0, 0, 0))
    grid_spec = pltpu.PrefetchScalarGridSpec(
        num_scalar_prefetch=4,
        grid=(N_UNITS, n_bands),
        in_specs=[
            pl.BlockSpec((BAND, UNIT_W), lambda p, b, *_: (b, p)),
            pl.BlockSpec((halo, UNIT_W), lambda p, b, pv, nx, fl, mb: (pv[b], kcol + p)),
            pl.BlockSpec((BAND, UNIT_W), lambda p, b, pv, nx, fl, mb: (b, kcol + p)),
            pl.BlockSpec((halo, UNIT_W), lambda p, b, pv, nx, fl, mb: (nx[b], kcol + p)),
            pl.BlockSpec((halo, UNIT_W), lambda p, b, pv, nx, fl, mb: (pv[b], p)),
            pl.BlockSpec((BAND, UNIT_W), lambda p, b, pv, nx, fl, mb: (b, p)),
            pl.BlockSpec((halo, UNIT_W), lambda p, b, pv, nx, fl, mb: (nx[b], p)),
            pl.BlockSpec((N_META, UNIT_W), lambda p, b, pv, nx, fl, mb: (mb[b], kcol + p)),
            pl.BlockSpec((N_META, UNIT_W), lambda p, b, pv, nx, fl, mb: (mb[b], p)),
            bias_spec(b32), bias_spec(b48),
        ],
        out_specs=pl.BlockSpec((BAND, UNIT_W), lambda p, b, *_: (b, p)),
        scratch_shapes=[pltpu.VMEM((BUF_ROWS * w, UNIT_W), BF16) for w in SPAN_W] * 2,
    )
    attn = pl.pallas_call(
        _attn_kernel, grid_spec=grid_spec,
        out_shape=jax.ShapeDtypeStruct((n, ATT_W), BF16),
        compiler_params=_cparams(("arbitrary", "arbitrary")),
    )(jnp.asarray(prev), jnp.asarray(nxt), jnp.asarray(flags), jnp.asarray(mblk),
      qk, qk, qk, qk, v, v, v, qk, v, b32, b48)
    mb0 = lay.n_grid // CHUNK
    n_tail = (n - lay.n_grid) // CHUNK
    return pl.pallas_call(
        functools.partial(_attn_meta_kernel, lay.n_seq),
        grid=(n_tail,),
        in_specs=[pl.BlockSpec((CHUNK, 2 * ATT_W), lambda s: (mb0 + s, 0)),
                  pl.BlockSpec((CHUNK, ATT_W), lambda s: (mb0 + s, 0)),
                  _const_spec((1, LANES)),
                  pl.BlockSpec(memory_space=pl.ANY)],
        out_specs=pl.BlockSpec((CHUNK, ATT_W), lambda s: (mb0 + s, 0)),
        out_shape=jax.ShapeDtypeStruct((n, ATT_W), BF16),
        input_output_aliases={3: 0},
        compiler_params=_cparams(("arbitrary",)),
    )(qk, v, mbias_meta, attn)


def _softplus(x):
    return jnp.maximum(x, 0.0) + jnp.log(1.0 + jnp.exp(-jnp.abs(x)))


def _split3_bf16(x):
    hi = x.astype(BF16)
    r = x - hi.astype(F32)
    mid = r.astype(BF16)
    lo = (r - mid.astype(F32)).astype(BF16)
    return hi, mid, lo


def _expand_heads(w, e):
    hi, lo = _split_bf16(w)
    return jnp.dot(hi, e, preferred_element_type=F32) + jnp.dot(lo, e, preferred_element_type=F32)


def _ssd_kernel(reverse, blk_ref, prev_ref, next_ref, flag_ref, *refs):
    if reverse:
        (u_ref, dt_ref, dtb_ref, arow_ref, e_ref, tri_ref, yf_ref, z_ref, dsk_ref, nw_ref,
         o_ref, state, ybuf) = refs
    else:
        (xbc_ref, xp_ref, xn_ref, shift_ref, cw_ref, dt_ref, dtb_ref, arow_ref, e_ref, tri_ref,
         o_ref, u_ref, xe, state) = refs
        ybuf = o_ref
    del blk_ref, prev_ref, next_ref
    flags = flag_ref[pl.program_id(0)]
    is_meta = (flags & 4) != 0

    @pl.when((flags & 8) != 0)
    def _():
        state[...] = jnp.zeros_like(state)

    if reverse:
        u = u_ref[...].astype(F32)
    else:
        zero = jnp.zeros((N_META, CONV_CH), BF16)
        xe[0:N_META, :] = jnp.where((flags & 1) != 0, xp_ref[...], zero)
        xe[N_META:N_META + CHUNK, :] = xbc_ref[...]
        xe[N_META + CHUNK:, :] = jnp.where((flags & 2) != 0, xn_ref[...], zero)

        @pl.when(is_meta)
        def _():
            xe[0:N_META + META_OFF, :] = jnp.zeros((N_META + META_OFF, CONV_CH), BF16)

        shifted = jnp.dot(shift_ref[...], xe[...], preferred_element_type=F32)
        half = CONV_K // 2
        u = cw_ref[CONV_K:CONV_K + 1, :] + cw_ref[half:half + 1, :] * xe[N_META:N_META + CHUNK, :].astype(F32)
        for n, k in enumerate(k for k in range(CONV_K) if k != half):
            u = u + cw_ref[k:k + 1, :] * shifted[n * CHUNK:(n + 1) * CHUNK, :]
        u = _silu(u)
        u_ref[...] = u.astype(BF16)
    xs = u[:, 0:D_SSM]
    x_bf = xs.astype(BF16)

    rid = lax.broadcasted_iota(jnp.int32, (CHUNK, 1), 0)
    valid = rid >= jnp.where(is_meta, META_OFF, 0)
    dt = jnp.where(valid, _softplus(dt_ref[...] + dtb_ref[...]), 0.0)
    a = dt * arow_ref[...]
    tri = tri_ref[...]
    ah, am, al = _split3_bf16(a)
    acum = (jnp.dot(tri, ah, preferred_element_type=F32) + jnp.dot(tri, am, preferred_element_type=F32)
            + jnp.dot(tri, al, preferred_element_type=F32))
    acum = acum * LOG2E
    acum_t = acum.T
    dt_t = dt.T
    edge = 0 if reverse else CHUNK - 1
    a_tot = acum[edge:edge + 1, :]
    e = e_ref[...]
    w1e = jnp.dot((dt * jnp.exp2(a_tot - acum)).astype(BF16), e, preferred_element_type=F32)
    w2e = jnp.dot(jnp.exp2(acum).astype(BF16), e, preferred_element_type=F32)
    decay_row = _expand_heads(jnp.broadcast_to(jnp.exp2(a_tot), (8, LANES)), e)[0:1]
    xw = (xs * w1e).astype(BF16)

    li = lax.broadcasted_iota(jnp.int32, (CHUNK, CHUNK), 0)
    si = lax.broadcasted_iota(jnp.int32, (CHUNK, CHUNK), 1)
    causal = (si >= li) if reverse else (li >= si)
    lane_lo = si < HEAD_DIM
    ho = N_HEADS if reverse else 0
    heads_per_group = N_HEADS // SSM_GROUPS
    for g in range(SSM_GROUPS):
        gsl = slice(g * GROUP_W, (g + 1) * GROUP_W)
        bg = u[:, D_SSM + g * SSM_STATE:D_SSM + (g + 1) * SSM_STATE]
        cg = u[:, D_SSM + (SSM_GROUPS + g) * SSM_STATE:D_SSM + (SSM_GROUPS + g + 1) * SSM_STATE].astype(BF16)
        cb = _nt_dot(cg, bg.astype(BF16))
        st_old = state[:, gsl]
        y_off = jnp.dot(cg, st_old.astype(BF16), preferred_element_type=F32) * w2e[:, gsl]
        st_new = jnp.dot(bg.T.astype(BF16), xw[:, gsl], preferred_element_type=F32)
        state[:, gsl] = st_old * decay_row[:, gsl] + st_new
        for j in range(heads_per_group // 2):
            col = g * GROUP_W + j * LANES
            xpair = x_bf[:, col:col + LANES]
            zero = jnp.zeros_like(xpair)
            mhs = []
            for hh in range(2):
                hi = ho + g * heads_per_group + 2 * j + hh
                seg = acum[:, hi:hi + 1] - acum_t[hi:hi + 1, :]
                lm = jnp.exp2(jnp.where(causal, seg, NEG_INF))
                mhs.append((cb * lm * dt_t[hi:hi + 1, :]).astype(BF16))
            xsplit = jnp.concatenate([jnp.where(lane_lo, xpair, zero), jnp.where(lane_lo, zero, xpair)], axis=0)
            ybuf[:, col:col + LANES] = y_off[:, j * LANES:(j + 1) * LANES] + jnp.dot(
                jnp.concatenate(mhs, axis=1), xsplit, preferred_element_type=F32)
    if reverse:
        for g in range(SSM_GROUPS):
            gsl = slice(g * GROUP_W, (g + 1) * GROUP_W)
            yg = ybuf[:, gsl] + yf_ref[:, gsl] + dsk_ref[:, gsl] * xs[:, gsl]
            yg = yg * _silu(z_ref[:, gsl].astype(F32))
            ms = jnp.mean(yg * yg, axis=-1, keepdims=True)
            o_ref[:, gsl] = (yg * lax.rsqrt(ms + RMS_EPS) * nw_ref[:, gsl]).astype(BF16)

    @pl.when((flags & 16) != 0)
    def _():
        o_ref[...] = jnp.zeros_like(o_ref)


def _ssd_constants(reverse):
    ho = N_HEADS if reverse else 0
    e = np.zeros((LANES, D_SSM), np.float32)
    for h in range(N_HEADS):
        e[ho + h, h * HEAD_DIM:(h + 1) * HEAD_DIM] = 1.0
    i = np.arange(CHUNK)
    tri = (i[None, :] >= i[:, None]) if reverse else (i[:, None] >= i[None, :])
    return jnp.asarray(e, BF16), jnp.asarray(tri.astype(np.float32), BF16)


def _shift_matrices():
    s = np.zeros(((CONV_K - 1) * CHUNK, CHUNK + 2 * N_META), np.float32)
    i = np.arange(CHUNK)
    for n, k in enumerate(k for k in range(CONV_K) if k != CONV_K // 2):
        s[n * CHUNK + i, N_META + i + k - CONV_K // 2] = 1.0
    return jnp.asarray(s, BF16)


def _ssd_pass(lay, reverse, dt, dtb, arow, fwd_in=(), rev_in=()):
    n = dt.shape[0]
    blk, prev, nxt, flags = lay.ssd_tables(reverse)
    e, tri = _ssd_constants(reverse)
    cur = lambda w: pl.BlockSpec((CHUNK, w), lambda i, bk, pv, nx, fl: (bk[i], 0))
    common = [cur(LANES), _const_spec((1, LANES)), _const_spec((1, LANES)),
              _const_spec((LANES, D_SSM)), _const_spec((CHUNK, CHUNK))]
    state = pltpu.VMEM((SSM_STATE, D_SSM), F32)
    if reverse:
        u, y_f, z, dsk, nw = rev_in
        args = (u, dt, dtb, arow, e, tri, y_f, z, dsk, nw)
        in_specs = [cur(CONV_CH)] + common + [cur(D_SSM), cur(D_SSM), _const_spec((1, D_SSM)),
                                              _const_spec((1, D_SSM))]
        out_specs = cur(D_SSM)
        out_shape = jax.ShapeDtypeStruct((n, D_SSM), BF16)
        scratch = [state, pltpu.VMEM((CHUNK, D_SSM), F32)]
    else:
        xbc, cw = fwd_in
        shift = _shift_matrices()
        args = (xbc, xbc, xbc, shift, cw, dt, dtb, arow, e, tri)
        in_specs = [cur(CONV_CH),
                    pl.BlockSpec((N_META, CONV_CH), lambda i, bk, pv, nx, fl: (pv[i], 0)),
                    pl.BlockSpec((N_META, CONV_CH), lambda i, bk, pv, nx, fl: (nx[i], 0)),
                    _const_spec(shift.shape), _const_spec((8, CONV_CH))] + common
        out_specs = [cur(D_SSM), cur(CONV_CH)]
        out_shape = [jax.ShapeDtypeStruct((n, D_SSM), F32), jax.ShapeDtypeStruct((n, CONV_CH), BF16)]
        scratch = [pltpu.VMEM((CHUNK + 2 * N_META, CONV_CH), BF16), state]
    grid_spec = pltpu.PrefetchScalarGridSpec(
        num_scalar_prefetch=4, grid=(len(blk),), in_specs=in_specs,
        out_specs=out_specs, scratch_shapes=scratch)
    return pl.pallas_call(
        functools.partial(_ssd_kernel, reverse), grid_spec=grid_spec, out_shape=out_shape,
        compiler_params=_cparams(("arbitrary",)),
    )(jnp.asarray(blk), jnp.asarray(prev), jnp.asarray(nxt), jnp.asarray(flags), *args)


def _moe_plan(route_t):
    n = route_t.shape[1]
    flat_e = route_t[0:2].T.astype(jnp.int32).reshape(-1)
    onehot = (flat_e[:, None] == jnp.arange(N_EXPERTS, dtype=jnp.int32)[None, :]).astype(jnp.int32)
    csum = jnp.cumsum(onehot, axis=0)
    counts = csum[-1]
    rank = jnp.sum(csum * onehot, axis=1) - 1
    padded = (counts + TME - 1) // TME * TME
    pends = jnp.cumsum(padded)
    pstarts = pends - padded
    dest = jnp.sum(onehot * pstarts[None, :], axis=1) + rank
    n_rows = _round_up(2 * n + N_EXPERTS * TME, TME)
    n_blocks = n_rows // TME
    block_e = jnp.sum((jnp.arange(n_blocks, dtype=jnp.int32)[:, None] * TME >= pends[None, :]).astype(jnp.int32),
                      axis=1)
    block_e = jnp.minimum(block_e, N_EXPERTS - 1)
    n_used = (pends[-1] // TME).astype(jnp.int32).reshape(1)
    tail = jnp.minimum(pends[-1] + jnp.arange(N_PAD_RANGES - N_EXPERTS + 1, dtype=jnp.int32) * TME, n_rows)
    pad_rows = jnp.concatenate([pstarts + counts, tail[:-1], pends, tail[1:]]).astype(jnp.int32)
    return dest.astype(jnp.int32), block_e, n_used, pad_rows, n_rows


def _row_tile(r):
    return pl.multiple_of(r * ROW_TILE, ROW_TILE)


def _dispatch_kernel(pad_ref, dest_hbm, h_ref, xs_hbm, idx, stage, ztile, sem_i, sem, sem_z):
    @pl.when(pl.program_id(0) == 0)
    def _():
        ztile[...] = jnp.zeros_like(ztile)

        def zero_copy(r):
            return pltpu.make_async_copy(ztile, xs_hbm.at[pl.ds(_row_tile(r), ROW_TILE)], sem_z)

        for e in range(N_PAD_RANGES):
            lo, hi = pad_ref[e], pad_ref[N_PAD_RANGES + e]

            def fill(r, c):
                zero_copy(r).start()
                return c

            def drain_fill(r, c):
                zero_copy(0).wait()
                return c

            lax.fori_loop(lo, hi, fill, 0)
            lax.fori_loop(lo, hi, drain_fill, 0)

    cp = pltpu.make_async_copy(dest_hbm.at[pl.program_id(0)], idx, sem_i)
    cp.start()
    hf = h_ref[...].astype(F32)
    for j in range(ROW_TILE):
        stage[pl.ds(j, TM, stride=ROW_TILE), :] = hf[:, j * LANES:(j + 1) * LANES]
    cp.wait()

    def row_copy(src_row, dst_row):
        return pltpu.make_async_copy(stage.at[pl.ds(_row_tile(src_row), ROW_TILE)],
                                     xs_hbm.at[pl.ds(_row_tile(dst_row), ROW_TILE)], sem)

    def issue(b, c):
        for u in range(DMA_UNROLL):
            r = b * DMA_UNROLL + u
            row_copy(lax.shift_right_logical(r, 1), idx[r]).start(priority=u % 2)
        return c

    lax.fori_loop(0, RB // DMA_UNROLL, issue, 0)

    for _ in range(RB // TM):
        pltpu.make_async_copy(stage, xs_hbm.at[pl.ds(0, TM * ROW_TILE)], sem).wait()


def _dispatch(h, dest, pad_rows, n_rows):
    n = h.shape[0]
    steps = n // TM
    grid_spec = pltpu.PrefetchScalarGridSpec(
        num_scalar_prefetch=1, grid=(steps,),
        in_specs=[pl.BlockSpec(memory_space=pl.ANY), pl.BlockSpec((TM, D_MODEL), lambda i, pad: (i, 0))],
        out_specs=pl.BlockSpec(memory_space=pl.ANY),
        scratch_shapes=[pltpu.SMEM((RB,), jnp.int32), pltpu.VMEM((TM * ROW_TILE, LANES), F32),
                        pltpu.VMEM((ROW_TILE, LANES), F32),
                        pltpu.SemaphoreType.DMA, pltpu.SemaphoreType.DMA, pltpu.SemaphoreType.DMA])
    return pl.pallas_call(
        _dispatch_kernel, grid_spec=grid_spec,
        out_shape=jax.ShapeDtypeStruct((n_rows * ROW_TILE, LANES), F32),
        compiler_params=_cparams(("arbitrary",)),
    )(pad_rows, dest.reshape(steps, RB), h)


def _expert_kernel(be_ref, nu_ref, x_ref, wg_ref, wu_ref, wd_ref, o_ref, xbf, act):
    del be_ref
    used = pl.program_id(0) < nu_ref[0]

    @pl.when(used)
    def _():
        for j in range(ROW_TILE):
            xbf[:, j * LANES:(j + 1) * LANES] = x_ref[pl.ds(j, TME, stride=ROW_TILE), :].astype(BF16)
        x = xbf[...]
        for c in range(D_FF_EXPERT // FF_CHUNK):
            sl = slice(c * FF_CHUNK, (c + 1) * FF_CHUNK)
            g = jnp.dot(x, wg_ref[:, sl], preferred_element_type=F32)
            u = jnp.dot(x, wu_ref[:, sl], preferred_element_type=F32)
            act[:, sl] = (_silu(g) * u).astype(BF16)
        out = jnp.dot(act[...], wd_ref[...], preferred_element_type=F32)
        for j in range(ROW_TILE):
            o_ref[pl.ds(j, TME, stride=ROW_TILE), :] = out[:, j * LANES:(j + 1) * LANES]

    @pl.when(jnp.logical_not(used))
    def _():
        o_ref[...] = jnp.zeros_like(o_ref)


def _experts(xs, block_e, n_used, wg, wu, wd):
    n_blocks = xs.shape[0] // (TME * ROW_TILE)
    rows = pl.BlockSpec((TME * ROW_TILE, LANES), lambda i, be, nu: (i, 0))
    resident = lambda shape: pl.BlockSpec((None,) + shape, lambda i, be, nu: (be[i], 0, 0),
                                          pipeline_mode=pl.Buffered(1))
    grid_spec = pltpu.PrefetchScalarGridSpec(
        num_scalar_prefetch=2, grid=(n_blocks,),
        in_specs=[pl.BlockSpec((TME * ROW_TILE, LANES), lambda i, be, nu: (jnp.minimum(i, nu[0] - 1), 0)),
                  resident((D_MODEL, D_FF_EXPERT)), resident((D_MODEL, D_FF_EXPERT)),
                  resident((D_FF_EXPERT, D_MODEL))],
        out_specs=rows,
        scratch_shapes=[pltpu.VMEM((TME, D_MODEL), BF16), pltpu.VMEM((TME, D_FF_EXPERT), BF16)])
    return pl.pallas_call(
        _expert_kernel, grid_spec=grid_spec,
        out_shape=jax.ShapeDtypeStruct(xs.shape, F32),
        compiler_params=_cparams(("arbitrary",)),
    )(block_e, n_used, xs, wg, wu, wd)


def _combine_kernel(split, dest_hbm, eo_hbm, x_ref, route_ref, nw_ref, out_a, out_b, idx, gbuf, xn, sem_i, sem_g):
    i = pl.program_id(0)
    n = pl.num_programs(0)

    def idx_copy(step, slot):
        return pltpu.make_async_copy(dest_hbm.at[step], idx.at[slot], sem_i.at[slot])

    def row_copy(slot, src_row, k, t):
        return pltpu.make_async_copy(eo_hbm.at[pl.ds(_row_tile(src_row), ROW_TILE)],
                                     gbuf.at[slot, k, pl.ds(_row_tile(t), ROW_TILE)], sem_g.at[slot])

    def issue_rows(slot):
        def issue(b, c):
            for u in range(DMA_UNROLL):
                r = b * DMA_UNROLL + u
                row_copy(slot, idx[slot, r], u % 2, b * (DMA_UNROLL // 2) + u // 2).start(priority=u % 2)
            return c

        lax.fori_loop(0, 2 * TC // DMA_UNROLL, issue, 0)

    @pl.when(i == 0)
    def _():
        idx_copy(0, 0).start()
        idx_copy(0, 0).wait()
        issue_rows(0)

        @pl.when(n > 1)
        def _():
            idx_copy(1, 1).start()

    for slot in range(2):
        @pl.when(jnp.logical_and(i % 2 == slot, i + 1 < n))
        def _(slot=slot):
            idx_copy(0, 1 - slot).wait()
            issue_rows(1 - slot)

            @pl.when(i + 2 < n)
            def _():
                idx_copy(i + 2, slot).start()

    w1 = route_ref[:, 0:1]
    w2 = route_ref[:, 1:2]
    for slot in range(2):
        @pl.when(i % 2 == slot)
        def _(slot=slot):
            for k in range(2):
                pltpu.make_async_copy(eo_hbm.at[pl.ds(0, TC * ROW_TILE)], gbuf.at[slot, k], sem_g.at[slot]).wait()
            for j in range(ROW_TILE):
                sl = slice(j * LANES, (j + 1) * LANES)
                xn[:, sl] = x_ref[:, sl] + (gbuf[slot, 0, pl.ds(j, TC, stride=ROW_TILE), :] * w1
                                            + gbuf[slot, 1, pl.ds(j, TC, stride=ROW_TILE), :] * w2)

    if split is None:
        out_a[...] = xn[...]
        out_b[...] = _rms_rows(xn[...], nw_ref[...]).astype(BF16)
    else:
        tiles_a, tiles_b = split

        @pl.when(i < tiles_a)
        def _():
            out_a[...] = xn[...]

        @pl.when(jnp.logical_and(i >= tiles_a, i < tiles_a + tiles_b))
        def _():
            out_b[...] = xn[...]


def _combine(eo, dest, x, route, nw_next, split_rows=None):
    n = x.shape[0]
    steps = n // TC
    row = lambda w: pl.BlockSpec((TC, w), lambda i: (i, 0))
    if split_rows is None:
        split = None
        out_specs = [row(D_MODEL), row(D_MODEL)]
        out_shape = [jax.ShapeDtypeStruct((n, D_MODEL), F32), jax.ShapeDtypeStruct((n, D_MODEL), BF16)]
    else:
        ta, tb = split = tuple(r // TC for r in split_rows)
        out_specs = [pl.BlockSpec((TC, D_MODEL), lambda i: (jnp.minimum(i, ta - 1), 0)),
                     pl.BlockSpec((TC, D_MODEL), lambda i: (jnp.clip(i - ta, 0, tb - 1), 0))]
        out_shape = [jax.ShapeDtypeStruct((r, D_MODEL), F32) for r in split_rows]
    return pl.pallas_call(
        functools.partial(_combine_kernel, split),
        grid=(steps,),
        in_specs=[pl.BlockSpec(memory_space=pl.ANY), pl.BlockSpec(memory_space=pl.ANY),
                  row(D_MODEL), row(2), _const_spec((1, D_MODEL))],
        out_specs=out_specs, out_shape=out_shape,
        scratch_shapes=[pltpu.SMEM((2, 2 * TC), jnp.int32), pltpu.VMEM((2, 2, TC * ROW_TILE, LANES), F32),
                        pltpu.VMEM((TC, D_MODEL), F32),
                        pltpu.SemaphoreType.DMA((2,)), pltpu.SemaphoreType.DMA((2,))],
        compiler_params=_cparams(("arbitrary",)),
    )(dest.reshape(steps, 2 * TC), eo, x, route, nw_next)


def _trunk(lay, x, h, p, group_rows):
    depth = p['norm1_w'].shape[0]
    row = lambda v: v.reshape(1, -1).astype(F32)
    for l in range(depth):
        w_in = p['w_in'][l]
        wqk = w_in[:, 0:2 * ATT_W].astype(BF16)
        wvzx = w_in[:, 2 * ATT_W:3 * ATT_W + D_SSM + CONV_CH].astype(BF16)
        wdt = jnp.pad(w_in[:, 3 * ATT_W + D_SSM + CONV_CH:], ((0, 0), (0, LANES - 2 * N_HEADS))).astype(BF16)
        nw_qk = jnp.concatenate([jnp.tile(p['q_norm_w'][l].astype(F32), N_HEADS) * (HEAD_DIM ** -0.5),
                                 jnp.tile(p['k_norm_w'][l].astype(F32), N_HEADS)]).reshape(1, -1)
        qk, v, z, xbc, dt_raw = _inproj(h, wqk, wvzx, wdt, nw_qk)

        attn = _attention(lay, qk, v, p['rpb'][l])

        cw = jnp.concatenate([p['conv_w'][l].astype(F32).T, p['conv_b'][l].astype(F32)[None],
                              jnp.zeros((8 - CONV_K - 1, CONV_CH), F32)], axis=0)
        pad = jnp.zeros((LANES - 2 * N_HEADS,), F32)
        dtb = jnp.concatenate([p['dt_bias'][l].astype(F32).reshape(-1), pad]).reshape(1, -1)
        arow = jnp.concatenate([-jnp.exp(p['a_log'][l].astype(F32)).reshape(-1), pad]).reshape(1, -1)
        y_f, u = _ssd_pass(lay, False, dt_raw, dtb, arow, fwd_in=(xbc, cw))
        dsk = jnp.repeat(p['d_skip'][l].astype(F32), HEAD_DIM).reshape(1, -1)
        y = _ssd_pass(lay, True, dt_raw, dtb, arow, rev_in=(u, y_f, z, dsk, row(p['ssm_norm_w'][l])))

        w_out = p['w_out'][l].astype(BF16)
        nw_next = row(p['norm1_w'][l + 1]) if l + 1 < depth else jnp.ones((1, D_MODEL), F32)
        j = l // 2
        if l % 2 == 0:
            x, h2 = _outproj(attn, y, x, w_out, row(p['attn_out_norm_w'][l]), row(p['norm2_w'][l]))
            x, h = _ffn(h2, x, p['ffn_w_gate'][j].astype(BF16), p['ffn_w_up'][j].astype(BF16),
                        p['ffn_w_down'][j].astype(BF16), nw_next)
        else:
            wr = jnp.pad(p['moe_router'][j].astype(F32).T, ((0, ROUTER_ROWS - N_EXPERTS), (0, 0)))
            x, h2, route_t = _outproj(attn, y, x, w_out, row(p['attn_out_norm_w'][l]), row(p['norm2_w'][l]),
                                      router=jnp.concatenate(_split_bf16(wr), axis=0))
            dest, block_e, n_used, pad_rows, n_rows = _moe_plan(route_t)
            xs = _dispatch(h2, dest, pad_rows, n_rows)
            eo = _experts(xs, block_e, n_used, p['moe_w_gate'][j].astype(BF16),
                          p['moe_w_up'][j].astype(BF16), p['moe_w_down'][j].astype(BF16))
            if l + 1 == depth and len(group_rows) == 2:
                return _combine(eo, dest, x, route_t[2:4].T, nw_next, split_rows=group_rows)
            x, h = _combine(eo, dest, x, route_t[2:4].T, nw_next)
    offs = np.concatenate([[0], np.cumsum(group_rows)])
    return [x[offs[g]:offs[g + 1]] for g in range(len(group_rows))]


def kernel(x_prompt, x_sample, meta_tokens, norm1_w, w_in, q_norm_w, k_norm_w, rpb, attn_out_norm_w,
           conv_w, conv_b, dt_bias, a_log, d_skip, ssm_norm_w, w_out, norm2_w,
           ffn_w_gate, ffn_w_up, ffn_w_down, moe_router, moe_w_gate, moe_w_up, moe_w_down):
    p = dict(norm1_w=norm1_w, w_in=w_in, q_norm_w=q_norm_w, k_norm_w=k_norm_w, rpb=rpb,
             attn_out_norm_w=attn_out_norm_w, conv_w=conv_w, conv_b=conv_b, dt_bias=dt_bias, a_log=a_log,
             d_skip=d_skip, ssm_norm_w=ssm_norm_w, w_out=w_out, norm2_w=norm2_w, ffn_w_gate=ffn_w_gate,
             ffn_w_up=ffn_w_up, ffn_w_down=ffn_w_down, moe_router=moe_router, moe_w_gate=moe_w_gate,
             moe_w_up=moe_w_up, moe_w_down=moe_w_down)
    groups = (x_prompt, x_sample)
    lay = _Layout([g.shape[1] for g in groups for _ in range(g.shape[0])])
    meta_block = jnp.concatenate([jnp.zeros((META_OFF, D_MODEL), F32), meta_tokens.astype(F32)], axis=0)
    tail = lay.n_tok - lay.n_grid - CHUNK * lay.n_seq
    tail_rows = jnp.concatenate([jnp.tile(meta_block, (lay.n_seq, 1)), jnp.zeros((tail, D_MODEL), F32)], axis=0)
    x, h = _embed([g.reshape(-1, D_MODEL) for g in groups] + [tail_rows], norm1_w[0])
    flat = _trunk(lay, x, h, p, [g.shape[0] * g.shape[1] for g in groups])
    return tuple(f.reshape(g.shape).astype(g.dtype) for f, g in zip(flat, groups))
```

```python
import functools

import numpy as np
import jax
import jax.numpy as jnp
from jax import lax
from jax.experimental import pallas as pl
from jax.experimental.pallas import tpu as pltpu

F32 = jnp.float32
BF16 = jnp.bfloat16

D_MODEL = 1024
N_META = 16
GRID_W = 64
NA_ROWS = 8
NA_COLS = 16
N_HEADS = 16
HEAD_DIM = 64
N_PAIRS = N_HEADS // 2
UNIT_HEADS = 4
UNIT_W = UNIT_HEADS * HEAD_DIM
N_UNITS = N_HEADS // UNIT_HEADS
ATT_W = N_HEADS * HEAD_DIM
D_SSM = 1024
SSM_GROUPS = 2
SSM_STATE = 128
GROUP_W = D_SSM // SSM_GROUPS
CONV_K = 5
CONV_CH = D_SSM + 2 * SSM_GROUPS * SSM_STATE
D_FF = 2816
N_EXPERTS = 8
D_FF_EXPERT = 3584
RMS_EPS = 1e-6
NEG_INF = -1e30
LOG2E = 1.4426950408889634

LANES = 128
ROW_TILE = D_MODEL // LANES
CHUNK = 128
META_OFF = CHUNK - N_META
TM = 512
BAND_ROWS = 16
BAND = BAND_ROWS * GRID_W
HALO_ROWS = NA_ROWS // 2
BUF_ROWS = BAND_ROWS + NA_ROWS - 1
FF_CHUNK = 256
TME = 512
ROUTER_ROWS = 16
N_PAD_RANGES = 2 * N_EXPERTS + 1
RB = 2 * TM
TC = 512
DMA_UNROLL = 8
VMEM_LIMIT = 56 * 1024 * 1024


def _cparams(sem):
    return pltpu.CompilerParams(dimension_semantics=sem, vmem_limit_bytes=VMEM_LIMIT)


def _round_up(a, b):
    return (a + b - 1) // b * b


def _const_spec(shape):
    nd = len(shape)
    return pl.BlockSpec(shape, lambda *_: (0,) * nd)


class _Layout:
    def __init__(self, seq_lens):
        self.seq_lens = tuple(seq_lens)
        self.n_seq = len(seq_lens)
        self.n_grid = sum(seq_lens)
        self.starts = np.concatenate([[0], np.cumsum(seq_lens)[:-1]]).astype(np.int64)
        self.n_tok = _round_up(self.n_grid + CHUNK * self.n_seq, TM)
        assert all(s % BAND == 0 and s // GRID_W >= NA_ROWS for s in seq_lens)

    def meta_block(self, s):
        return self.n_grid // CHUNK + s

    def attn_tables(self):
        prev, nxt, flags, mblk = [], [], [], []
        for s, (st, ln) in enumerate(zip(self.starts, self.seq_lens)):
            lo, hi = st // BAND, (st + ln) // BAND
            for b in range(lo, hi):
                per = BAND_ROWS // HALO_ROWS
                prev.append(max(b * per - 1, lo * per))
                nxt.append(min((b + 1) * per, hi * per - 1))
                flags.append((1 if b == lo else 0) | (2 if b == hi - 1 else 0))
                mblk.append((self.n_grid + CHUNK * s + META_OFF) // N_META)
        return [np.asarray(a, np.int32) for a in (prev, nxt, flags, mblk)]

    def ssd_tables(self, reverse):
        blk, prev, nxt, flags = [], [], [], []
        for s, (st, ln) in enumerate(zip(self.starts, self.seq_lens)):
            nc = ln // CHUNK
            b0 = st // CHUNK
            meta16 = (self.n_grid + CHUNK * s + META_OFF) // N_META
            steps = []
            steps.append((self.meta_block(s), 0, b0 * 8, 2 | 4))
            for c in range(nc):
                b = b0 + c
                p16 = meta16 if c == 0 else b * 8 - 1
                n16 = (b + 1) * 8 if c < nc - 1 else 0
                steps.append((b, p16, n16, 1 | (2 if c < nc - 1 else 0)))
            if reverse:
                steps = steps[::-1]
            for i, (b, p, n, f) in enumerate(steps):
                blk.append(b); prev.append(p); nxt.append(n)
                flags.append(f | (8 if i == 0 else 0))
        for b in range(self.n_grid // CHUNK + self.n_seq, self.n_tok // CHUNK):
            blk.append(b); prev.append(0); nxt.append(0)
            flags.append(8 | 16)
        return [np.asarray(a, np.int32) for a in (blk, prev, nxt, flags)]


def _rms_rows(x, w):
    ms = jnp.mean(x * x, axis=-1, keepdims=True)
    return x * lax.rsqrt(ms + RMS_EPS) * w


def _embed_kernel(bounds, *refs):
    srcs, (w_ref, x_ref, h_ref) = refs[:len(bounds)], refs[len(bounds):]
    i = pl.program_id(0)
    lo = 0
    for src, hi in zip(srcs, bounds):
        @pl.when(jnp.logical_and(i >= lo, i < hi))
        def _(src=src):
            x = src[...].astype(F32)
            x_ref[...] = x
            h_ref[...] = _rms_rows(x, w_ref[...]).astype(BF16)
        lo = hi


def _embed(parts, w):
    tiles = [a.shape[0] // TM for a in parts]
    assert all(a.shape[0] % TM == 0 for a in parts)
    bounds = tuple(int(b) for b in np.cumsum(tiles))
    starts = (0,) + bounds[:-1]
    n = bounds[-1] * TM
    src_spec = lambda s, t: pl.BlockSpec((TM, D_MODEL), lambda i: (jnp.clip(i - s, 0, t - 1), 0))
    row = pl.BlockSpec((TM, D_MODEL), lambda i: (i, 0))
    return pl.pallas_call(
        functools.partial(_embed_kernel, bounds),
        grid=(bounds[-1],),
        in_specs=[src_spec(s, t) for s, t in zip(starts, tiles)] + [_const_spec((1, D_MODEL))],
        out_specs=[row, row],
        out_shape=[jax.ShapeDtypeStruct((n, D_MODEL), F32), jax.ShapeDtypeStruct((n, D_MODEL), BF16)],
        compiler_params=_cparams(("arbitrary",)),
    )(*parts, w.reshape(1, D_MODEL))


def _inproj_kernel(h_ref, wqk_ref, wvzx_ref, wdt_ref, g_ref, nw_ref,
                   qk_ref, v_ref, z_ref, xbc_ref, dt_ref):
    h = h_ref[...]
    g = g_ref[...]
    n_qk = 2 * ATT_W // FF_CHUNK
    col = lambda c: slice(c * FF_CHUNK, (c + 1) * FF_CHUNK)
    y_next = jnp.dot(h, wqk_ref[:, col(0)], preferred_element_type=F32)
    for c in range(n_qk):
        y = y_next
        if c + 1 < n_qk:
            y_next = jnp.dot(h, wqk_ref[:, col(c + 1)], preferred_element_type=F32)
        ss = jnp.dot((y * y).astype(BF16), g, preferred_element_type=F32)
        inv = lax.rsqrt(ss * (1.0 / HEAD_DIM) + RMS_EPS)
        qk_ref[:, col(c)] = (y * inv * nw_ref[:, col(c)]).astype(BF16)
    for c in range(ATT_W // FF_CHUNK):
        sl = slice(c * FF_CHUNK, (c + 1) * FF_CHUNK)
        v_ref[:, sl] = jnp.dot(h, wvzx_ref[:, sl], preferred_element_type=F32).astype(BF16)
    for c in range(D_SSM // FF_CHUNK):
        sl = slice(c * FF_CHUNK, (c + 1) * FF_CHUNK)
        src = slice(ATT_W + c * FF_CHUNK, ATT_W + (c + 1) * FF_CHUNK)
        z_ref[:, sl] = jnp.dot(h, wvzx_ref[:, src], preferred_element_type=F32).astype(BF16)
    for c in range(CONV_CH // FF_CHUNK):
        sl = slice(c * FF_CHUNK, (c + 1) * FF_CHUNK)
        src = slice(ATT_W + D_SSM + c * FF_CHUNK, ATT_W + D_SSM + (c + 1) * FF_CHUNK)
        xbc_ref[:, sl] = jnp.dot(h, wvzx_ref[:, src], preferred_element_type=F32).astype(BF16)
    dt_ref[...] = jnp.dot(h, wdt_ref[...], preferred_element_type=F32)


def _head_sum_matrix():
    i = np.arange(FF_CHUNK)
    return jnp.asarray((i[:, None] // HEAD_DIM == i[None, :] // HEAD_DIM).astype(np.float32), BF16)


def _inproj(h, wqk, wvzx, wdt, nw):
    n = h.shape[0]
    row = lambda w: pl.BlockSpec((TM, w), lambda i: (i, 0))
    return pl.pallas_call(
        _inproj_kernel,
        grid=(n // TM,),
        in_specs=[row(D_MODEL), _const_spec(wqk.shape), _const_spec(wvzx.shape), _const_spec(wdt.shape),
                  _const_spec((FF_CHUNK, FF_CHUNK)), _const_spec((1, 2 * ATT_W))],
        out_specs=[row(2 * ATT_W), row(ATT_W), row(D_SSM), row(CONV_CH), row(LANES)],
        out_shape=[jax.ShapeDtypeStruct((n, 2 * ATT_W), BF16), jax.ShapeDtypeStruct((n, ATT_W), BF16),
                   jax.ShapeDtypeStruct((n, D_SSM), BF16), jax.ShapeDtypeStruct((n, CONV_CH), BF16),
                   jax.ShapeDtypeStruct((n, LANES), F32)],
        compiler_params=_cparams(("parallel",)),
    )(h, wqk, wvzx, wdt, _head_sum_matrix(), nw)


def _split_bf16(x):
    hi = x.astype(BF16)
    lo = (x - hi.astype(F32)).astype(BF16)
    return hi, lo


def _outproj_kernel(with_router, grid_tiles, attn_ref, attn_tail_ref, y_ref, x_ref, w_ref, aw_ref, nw_ref, *rest):
    if with_router:
        wr_ref, xo_ref, h_ref, route_ref = rest
    else:
        xo_ref, h_ref = rest
    acc = x_ref[...] + jnp.dot(y_ref[...], w_ref[ATT_W:ATT_W + D_SSM, :], preferred_element_type=F32)
    attn = jnp.where(pl.program_id(0) < grid_tiles, attn_ref[...], attn_tail_ref[...])
    a = _rms_rows(attn.astype(F32), aw_ref[...]).astype(BF16)
    acc = acc + jnp.dot(a, w_ref[0:ATT_W, :], preferred_element_type=F32)
    xo_ref[...] = acc
    h2 = _rms_rows(acc, nw_ref[...])
    h_ref[...] = h2.astype(BF16)
    if with_router:
        hh, hl = _split_bf16(h2)
        wr = wr_ref[...]
        lg = _nt_dot(wr, hh)
        logits = lg[0:ROUTER_ROWS] + lg[ROUTER_ROWS:] + _nt_dot(wr[0:ROUTER_ROWS], hl)
        sub = lax.broadcasted_iota(jnp.int32, logits.shape, 0)
        logits = jnp.where(sub < N_EXPERTS, logits, NEG_INF)
        m1 = jnp.max(logits, axis=0, keepdims=True)
        i1 = jnp.min(jnp.where(logits == m1, sub, ROUTER_ROWS), axis=0, keepdims=True)
        rest_l = jnp.where(sub == i1, NEG_INF, logits)
        m2 = jnp.max(rest_l, axis=0, keepdims=True)
        i2 = jnp.min(jnp.where(rest_l == m2, sub, ROUTER_ROWS), axis=0, keepdims=True)
        e = jnp.exp(m2 - m1)
        w1 = 1.0 / (1.0 + e)
        route_ref[...] = jnp.concatenate(
            [i1.astype(F32), i2.astype(F32), w1, e * w1, jnp.zeros((4, logits.shape[1]), F32)], axis=0)


def _outproj(attn, y, x, w_out, aw, nw, router=None):
    n = x.shape[0]
    row = lambda w: pl.BlockSpec((TM, w), lambda i: (i, 0))
    tg, tt = attn[0].shape[0] // TM, attn[1].shape[0] // TM
    in_specs = [pl.BlockSpec((TM, ATT_W), lambda i: (jnp.minimum(i, tg - 1), 0)),
                pl.BlockSpec((TM, ATT_W), lambda i: (jnp.clip(i - tg, 0, tt - 1), 0)),
                row(D_SSM), row(D_MODEL), _const_spec(w_out.shape),
                _const_spec((1, ATT_W)), _const_spec((1, D_MODEL))]
    out_specs = [row(D_MODEL), row(D_MODEL)]
    out_shape = [jax.ShapeDtypeStruct((n, D_MODEL), F32), jax.ShapeDtypeStruct((n, D_MODEL), BF16)]
    args = [attn[0], attn[1], y, x, w_out, aw, nw]
    if router is not None:
        in_specs.append(_const_spec((2 * ROUTER_ROWS, D_MODEL)))
        out_specs.append(pl.BlockSpec((8, TM), lambda i: (0, i)))
        out_shape.append(jax.ShapeDtypeStruct((8, n), F32))
        args.append(router)
    return pl.pallas_call(
        functools.partial(_outproj_kernel, router is not None, tg),
        grid=(n // TM,), in_specs=in_specs, out_specs=out_specs, out_shape=out_shape,
        compiler_params=_cparams(("parallel",)),
    )(*args)


def _silu(x):
    return x * (1.0 / (1.0 + jnp.exp(-x)))


def _ffn_kernel(h_ref, x_ref, wg_ref, wu_ref, wd_ref, nw_ref, xo_ref, ho_ref, act_ref):
    h = h_ref[...]
    for c in range(D_FF // FF_CHUNK):
        sl = slice(c * FF_CHUNK, (c + 1) * FF_CHUNK)
        g = jnp.dot(h, wg_ref[:, sl], preferred_element_type=F32)
        u = jnp.dot(h, wu_ref[:, sl], preferred_element_type=F32)
        act_ref[:, sl] = (_silu(g) * u).astype(BF16)
    xn = x_ref[...] + jnp.dot(act_ref[...], wd_ref[...], preferred_element_type=F32)
    xo_ref[...] = xn
    ho_ref[...] = _rms_rows(xn, nw_ref[...]).astype(BF16)


def _ffn(h, x, wg, wu, wd, nw_next):
    n = x.shape[0]
    row = lambda w: pl.BlockSpec((TM, w), lambda i: (i, 0))
    return pl.pallas_call(
        _ffn_kernel,
        grid=(n // TM,),
        in_specs=[row(D_MODEL), row(D_MODEL), _const_spec(wg.shape), _const_spec(wu.shape),
                  _const_spec(wd.shape), _const_spec((1, D_MODEL))],
        out_specs=[row(D_MODEL), row(D_MODEL)],
        out_shape=[jax.ShapeDtypeStruct((n, D_MODEL), F32), jax.ShapeDtypeStruct((n, D_MODEL), BF16)],
        scratch_shapes=[pltpu.VMEM((TM, D_FF), BF16)],
        compiler_params=_cparams(("parallel",)),
    )(h, x, wg, wu, wd, nw_next)


N_CB = GRID_W // NA_COLS
SPAN_START = (0, 0, 16, 32)
SPAN_W = (32, 48, 48, 32)
META_SLOT = (24, 0, 0, 0)
META_PER_ROW = N_META // 2
KTILE = 16
ATTN_DEPTH = 16


def _attn_bias_tables(rpb):
    i = np.arange(NA_ROWS)
    dr = np.clip(i[None, :] - i[:, None] + NA_ROWS - 1, 0, 2 * NA_ROWS - 2)
    tables = {}
    for j in range(N_CB):
        span = SPAN_W[j]
        qc = NA_COLS * j + np.arange(NA_COLS)
        kc = SPAN_START[j] + np.arange(span)
        st = np.clip(qc - NA_COLS // 2, 0, GRID_W - NA_COLS)
        valid = (kc[None, :] >= st[:, None]) & (kc[None, :] < st[:, None] + NA_COLS)
        dc = np.clip(kc[None, :] - qc[:, None] + NA_COLS - 1, 0, 2 * NA_COLS - 2)
        slot = (np.arange(span) >= META_SLOT[j]) & (np.arange(span) < META_SLOT[j] + META_PER_ROW)
        assert not valid[:, slot].any()
        b = rpb.astype(F32)[:, dr][:, :, :, dc]
        b = jnp.where(valid[None, None, None], b, NEG_INF)
        meta_vis = np.where(i[:, None] < 2, 0.0, NEG_INF) * np.ones((1, span))
        b = jnp.where(slot[None, None, None, None, :], jnp.asarray(meta_vis, F32)[None, None, :, None, :], b)
        b = b.reshape(N_UNITS, UNIT_HEADS, NA_ROWS, NA_ROWS, NA_COLS, span)
        b = jnp.transpose(b, (0, 2, 1, 4, 3, 5)).reshape(N_UNITS, NA_ROWS, UNIT_HEADS * NA_COLS, NA_ROWS * span)
        tables.setdefault(span, []).append(b)
    return [jnp.stack(t, axis=1) for _, t in sorted(tables.items())]


def _pair_queries(q):
    lo = lax.broadcasted_iota(jnp.int32, q.shape, 1) < HEAD_DIM
    zero = jnp.zeros_like(q)
    return jnp.concatenate([jnp.where(lo, q, zero), jnp.where(lo, zero, q)], axis=0)


def _head_queries(q):
    head = lax.broadcasted_iota(jnp.int32, q.shape, 1) // HEAD_DIM
    zero = jnp.zeros_like(q)
    return jnp.concatenate([jnp.where(head == h, q, zero) for h in range(UNIT_HEADS)], axis=0)


def _nt_dot(a, b):
    return lax.dot_general(a, b, (((1,), (1,)), ((), ())), preferred_element_type=F32)


def _attn_kernel(prev_ref, next_ref, flag_ref, mblk_ref,
                 q_ref, kp_ref, kc_ref, kn_ref, vp_ref, vc_ref, vn_ref, km_ref, vm_ref,
                 b32_ref, b48_ref, o_ref, *spans):
    del prev_ref, next_ref, mblk_ref
    flags = flag_ref[pl.program_id(1)]
    is_first = (flags & 1) != 0
    is_last = (flags & 2) != 0
    low = lax.broadcasted_iota(jnp.int32, (KTILE, UNIT_W), 0) < META_PER_ROW

    for refs, bufs, m_ref in (((kp_ref, kc_ref, kn_ref), spans[:N_CB], km_ref),
                              ((vp_ref, vc_ref, vn_ref), spans[N_CB:], vm_ref)):
        m = m_ref[...]
        mf = m.astype(F32)
        m_swapped = jnp.concatenate([mf[META_PER_ROW:], mf[:META_PER_ROW]], axis=0).astype(BF16)
        for rr in range(BUF_ROWS):
            if rr < HALO_ROWS:
                ref, row = refs[0], rr
            elif rr < HALO_ROWS + BAND_ROWS:
                ref, row = refs[1], rr - HALO_ROWS
            else:
                ref, row = refs[2], rr - HALO_ROWS - BAND_ROWS
            for j in range(N_CB):
                span = SPAN_W[j]
                for t in range(span // KTILE):
                    off = row * GRID_W + SPAN_START[j] + t * KTILE
                    tile = ref[off:off + KTILE, :]
                    if t == META_SLOT[j] // KTILE:
                        if META_SLOT[j] % KTILE == 0:
                            tile = jnp.where(low, m if rr % 2 == 0 else m_swapped, tile)
                        else:
                            tile = jnp.where(low, tile, m_swapped if rr % 2 == 0 else m)
                    bufs[j][rr * span + t * KTILE:rr * span + (t + 1) * KTILE, :] = tile

    lane_head = lax.broadcasted_iota(jnp.int32, (NA_COLS, UNIT_W), 1) // HEAD_DIM
    starts = []
    for ri in range(BAND_ROWS):
        ls = jnp.int32(ri)
        ls = jnp.where(is_first, jnp.maximum(ls, HALO_ROWS), ls)
        ls = jnp.where(is_last, jnp.minimum(ls, BAND_ROWS - HALO_ROWS), ls)
        starts.append((ls, HALO_ROWS + ri - ls))

    def scores(u):
        ri, j = divmod(u, N_CB)
        ls, delta = starts[ri]
        span = SPAN_W[j]
        kwin = spans[j][pl.ds(pl.multiple_of(ls * span, KTILE), NA_ROWS * span), :]
        q0 = ri * GRID_W + j * NA_COLS
        bias_ref = b32_ref if span == SPAN_W[0] else b48_ref
        return _nt_dot(_head_queries(q_ref[q0:q0 + NA_COLS, :]), kwin) + bias_ref[SPAN_W[:j].count(span), delta]

    def finish(u, s):
        ri, j = divmod(u, N_CB)
        span = SPAN_W[j]
        vwin = spans[N_CB + j][pl.ds(pl.multiple_of(starts[ri][0] * span, KTILE), NA_ROWS * span), :]
        p = jnp.exp(s - jnp.max(s, axis=-1, keepdims=True))
        l = jnp.sum(p, axis=-1, keepdims=True)
        o = jnp.dot(p.astype(BF16), vwin, preferred_element_type=F32) / l
        q0 = ri * GRID_W + j * NA_COLS
        out = o[:NA_COLS]
        for h in range(1, UNIT_HEADS):
            out = jnp.where(lane_head == h, o[h * NA_COLS:(h + 1) * NA_COLS], out)
        o_ref[q0:q0 + NA_COLS, :] = out.astype(BF16)

    n_units = BAND_ROWS * N_CB
    pending = {}
    for t in range(n_units + ATTN_DEPTH):
        if t < n_units:
            pending[t] = scores(t)
        if t >= ATTN_DEPTH:
            finish(t - ATTN_DEPTH, pending.pop(t - ATTN_DEPTH))


def _attn_meta_kernel(n_seq, qk_ref, v_ref, mbias_ref, o_ref):
    s = pl.program_id(0)
    o_ref[...] = jnp.zeros_like(o_ref)

    @pl.when(s < n_seq)
    def _():
        mbias = mbias_ref[...]
        for p in range(N_PAIRS):
            c = slice(p * LANES, (p + 1) * LANES)
            q = qk_ref[META_OFF:CHUNK, c]
            k = qk_ref[:, ATT_W + p * LANES:ATT_W + (p + 1) * LANES]
            qs = _pair_queries(q)
            sc = _nt_dot(qs, k) + mbias
            m = jnp.max(sc, axis=-1, keepdims=True)
            e = jnp.exp(sc - m)
            l = jnp.sum(e, axis=-1, keepdims=True)
            o = jnp.dot(e.astype(BF16), v_ref[:, c], preferred_element_type=F32) / l
            lo = lax.broadcasted_iota(jnp.int32, (N_META, LANES), 1) < HEAD_DIM
            o_ref[META_OFF:CHUNK, c] = jnp.where(lo, o[0:N_META], o[N_META:]).astype(BF16)


def _attention(lay, qk, v, rpb):
    n = qk.shape[0]
    prev, nxt, flags, mblk = lay.attn_tables()
    n_bands = lay.n_grid // BAND
    b32, b48 = _attn_bias_tables(rpb)
    lane = np.arange(LANES)
    mbias_meta = jnp.asarray(np.where(lane >= META_OFF, 0.0, NEG_INF)[None, :], F32)
    kcol = ATT_W // UNIT_W
    halo = HALO_ROWS * GRID_W
    bias_spec = lambda a: pl.BlockSpec((None,) + a.shape[1:], lambda p, b, *_: (p, 0, 0, 0, 0))
    grid_spec = pltpu.PrefetchScalarGridSpec(
        num_scalar_prefetch=4,
        grid=(N_UNITS, n_bands),
        in_specs=[
            pl.BlockSpec((BAND, UNIT_W), lambda p, b, *_: (b, p)),
            pl.BlockSpec((halo, UNIT_W), lambda p, b, pv, nx, fl, mb: (pv[b], kcol + p)),
            pl.BlockSpec((BAND, UNIT_W), lambda p, b, pv, nx, fl, mb: (b, kcol + p)),
            pl.BlockSpec((halo, UNIT_W), lambda p, b, pv, nx, fl, mb: (nx[b], kcol + p)),
            pl.BlockSpec((halo, UNIT_W), lambda p, b, pv, nx, fl, mb: (pv[b], p)),
            pl.BlockSpec((BAND, UNIT_W), lambda p, b, pv, nx, fl, mb: (b, p)),
            pl.BlockSpec((halo, UNIT_W), lambda p, b, pv, nx, fl, mb: (nx[b], p)),
            pl.BlockSpec((N_META, UNIT_W), lambda p, b, pv, nx, fl, mb: (mb[b], kcol + p)),
            pl.BlockSpec((N_META, UNIT_W), lambda p, b, pv, nx, fl, mb: (mb[b], p)),
            bias_spec(b32), bias_spec(b48),
        ],
        out_specs=pl.BlockSpec((BAND, UNIT_W), lambda p, b, *_: (b, p)),
        scratch_shapes=[pltpu.VMEM((BUF_ROWS * w, UNIT_W), BF16) for w in SPAN_W] * 2,
    )
    attn = pl.pallas_call(
        _attn_kernel, grid_spec=grid_spec,
        out_shape=jax.ShapeDtypeStruct((lay.n_grid, ATT_W), BF16),
        compiler_params=_cparams(("arbitrary", "arbitrary")),
    )(jnp.asarray(prev), jnp.asarray(nxt), jnp.asarray(flags), jnp.asarray(mblk),
      qk, qk, qk, qk, v, v, v, qk, v, b32, b48)
    mb0 = lay.n_grid // CHUNK
    n_tail = (n - lay.n_grid) // CHUNK
    attn_tail = pl.pallas_call(
        functools.partial(_attn_meta_kernel, lay.n_seq),
        grid=(n_tail,),
        in_specs=[pl.BlockSpec((CHUNK, 2 * ATT_W), lambda s: (mb0 + s, 0)),
                  pl.BlockSpec((CHUNK, ATT_W), lambda s: (mb0 + s, 0)),
                  _const_spec((1, LANES))],
        out_specs=pl.BlockSpec((CHUNK, ATT_W), lambda s: (s, 0)),
        out_shape=jax.ShapeDtypeStruct((n - lay.n_grid, ATT_W), BF16),
        compiler_params=_cparams(("arbitrary",)),
    )(qk, v, mbias_meta)
    return attn, attn_tail


def _softplus(x):
    return jnp.maximum(x, 0.0) + jnp.log(1.0 + jnp.exp(-jnp.abs(x)))


def _split3_bf16(x):
    hi = x.astype(BF16)
    r = x - hi.astype(F32)
    mid = r.astype(BF16)
    lo = (r - mid.astype(F32)).astype(BF16)
    return hi, mid, lo


def _expand_heads(w, e):
    hi, lo = _split_bf16(w)
    return jnp.dot(hi, e, preferred_element_type=F32) + jnp.dot(lo, e, preferred_element_type=F32)


def _ssd_kernel(reverse, blk_ref, prev_ref, next_ref, flag_ref, *refs):
    if reverse:
        (u_ref, dt_ref, dtb_ref, arow_ref, e_ref, tri_ref, yf_ref, z_ref, dsk_ref, nw_ref,
         o_ref, state, ybuf) = refs
    else:
        (xbc_ref, xp_ref, xn_ref, shift_ref, cw_ref, dt_ref, dtb_ref, arow_ref, e_ref, tri_ref,
         o_ref, u_ref, xe, state) = refs
        ybuf = o_ref
    del blk_ref, prev_ref, next_ref
    flags = flag_ref[pl.program_id(0)]
    is_meta = (flags & 4) != 0

    @pl.when((flags & 8) != 0)
    def _():
        state[...] = jnp.zeros_like(state)

    if reverse:
        u = u_ref[...].astype(F32)
    else:
        zero = jnp.zeros((N_META, CONV_CH), BF16)
        xe[0:N_META, :] = jnp.where((flags & 1) != 0, xp_ref[...], zero)
        xe[N_META:N_META + CHUNK, :] = xbc_ref[...]
        xe[N_META + CHUNK:, :] = jnp.where((flags & 2) != 0, xn_ref[...], zero)

        @pl.when(is_meta)
        def _():
            xe[0:N_META + META_OFF, :] = jnp.zeros((N_META + META_OFF, CONV_CH), BF16)

        shifted = jnp.dot(shift_ref[...], xe[...], preferred_element_type=F32)
        half = CONV_K // 2
        u = cw_ref[CONV_K:CONV_K + 1, :] + cw_ref[half:half + 1, :] * xe[N_META:N_META + CHUNK, :].astype(F32)
        for n, k in enumerate(k for k in range(CONV_K) if k != half):
            u = u + cw_ref[k:k + 1, :] * shifted[n * CHUNK:(n + 1) * CHUNK, :]
        u = _silu(u)
        u_ref[...] = u.astype(BF16)
    xs = u[:, 0:D_SSM]
    x_bf = xs.astype(BF16)

    rid = lax.broadcasted_iota(jnp.int32, (CHUNK, 1), 0)
    valid = rid >= jnp.where(is_meta, META_OFF, 0)
    dt = jnp.where(valid, _softplus(dt_ref[...] + dtb_ref[...]), 0.0)
    a = dt * arow_ref[...]
    tri = tri_ref[...]
    ah, am, al = _split3_bf16(a)
    acum = (jnp.dot(tri, ah, preferred_element_type=F32) + jnp.dot(tri, am, preferred_element_type=F32)
            + jnp.dot(tri, al, preferred_element_type=F32))
    acum = acum * LOG2E
    acum_t = acum.T
    dt_t = dt.T
    edge = 0 if reverse else CHUNK - 1
    a_tot = acum[edge:edge + 1, :]
    e = e_ref[...]
    w1e = jnp.dot((dt * jnp.exp2(a_tot - acum)).astype(BF16), e, preferred_element_type=F32)
    w2e = jnp.dot(jnp.exp2(acum).astype(BF16), e, preferred_element_type=F32)
    decay_row = _expand_heads(jnp.broadcast_to(jnp.exp2(a_tot), (8, LANES)), e)[0:1]
    xw = (xs * w1e).astype(BF16)

    li = lax.broadcasted_iota(jnp.int32, (CHUNK, CHUNK), 0)
    si = lax.broadcasted_iota(jnp.int32, (CHUNK, CHUNK), 1)
    causal = (si >= li) if reverse else (li >= si)
    lane_lo = si < HEAD_DIM
    ho = N_HEADS if reverse else 0
    heads_per_group = N_HEADS // SSM_GROUPS
    for g in range(SSM_GROUPS):
        gsl = slice(g * GROUP_W, (g + 1) * GROUP_W)
        bg = u[:, D_SSM + g * SSM_STATE:D_SSM + (g + 1) * SSM_STATE]
        cg = u[:, D_SSM + (SSM_GROUPS + g) * SSM_STATE:D_SSM + (SSM_GROUPS + g + 1) * SSM_STATE].astype(BF16)
        cb = _nt_dot(cg, bg.astype(BF16))
        st_old = state[:, gsl]
        y_off = jnp.dot(cg, st_old.astype(BF16), preferred_element_type=F32) * w2e[:, gsl]
        st_new = jnp.dot(bg.T.astype(BF16), xw[:, gsl], preferred_element_type=F32)
        state[:, gsl] = st_old * decay_row[:, gsl] + st_new
        for j in range(heads_per_group // 2):
            col = g * GROUP_W + j * LANES
            xpair = x_bf[:, col:col + LANES]
            zero = jnp.zeros_like(xpair)
            mhs = []
            for hh in range(2):
                hi = ho + g * heads_per_group + 2 * j + hh
                seg = acum[:, hi:hi + 1] - acum_t[hi:hi + 1, :]
                lm = jnp.exp2(jnp.where(causal, seg, NEG_INF))
                mhs.append((cb * lm * dt_t[hi:hi + 1, :]).astype(BF16))
            xsplit = jnp.concatenate([jnp.where(lane_lo, xpair, zero), jnp.where(lane_lo, zero, xpair)], axis=0)
            ybuf[:, col:col + LANES] = y_off[:, j * LANES:(j + 1) * LANES] + jnp.dot(
                jnp.concatenate(mhs, axis=1), xsplit, preferred_element_type=F32)
    if reverse:
        for g in range(SSM_GROUPS):
            gsl = slice(g * GROUP_W, (g + 1) * GROUP_W)
            yg = ybuf[:, gsl] + yf_ref[:, gsl] + dsk_ref[:, gsl] * xs[:, gsl]
            yg = yg * _silu(z_ref[:, gsl].astype(F32))
            ms = jnp.mean(yg * yg, axis=-1, keepdims=True)
            o_ref[:, gsl] = (yg * lax.rsqrt(ms + RMS_EPS) * nw_ref[:, gsl]).astype(BF16)

    @pl.when((flags & 16) != 0)
    def _():
        o_ref[...] = jnp.zeros_like(o_ref)


def _ssd_constants(reverse):
    ho = N_HEADS if reverse else 0
    e = np.zeros((LANES, D_SSM), np.float32)
    for h in range(N_HEADS):
        e[ho + h, h * HEAD_DIM:(h + 1) * HEAD_DIM] = 1.0
    i = np.arange(CHUNK)
    tri = (i[None, :] >= i[:, None]) if reverse else (i[:, None] >= i[None, :])
    return jnp.asarray(e, BF16), jnp.asarray(tri.astype(np.float32), BF16)


def _shift_matrices():
    s = np.zeros(((CONV_K - 1) * CHUNK, CHUNK + 2 * N_META), np.float32)
    i = np.arange(CHUNK)
    for n, k in enumerate(k for k in range(CONV_K) if k != CONV_K // 2):
        s[n * CHUNK + i, N_META + i + k - CONV_K // 2] = 1.0
    return jnp.asarray(s, BF16)


def _ssd_pass(lay, reverse, dt, dtb, arow, fwd_in=(), rev_in=()):
    n = dt.shape[0]
    blk, prev, nxt, flags = lay.ssd_tables(reverse)
    e, tri = _ssd_constants(reverse)
    cur = lambda w: pl.BlockSpec((CHUNK, w), lambda i, bk, pv, nx, fl: (bk[i], 0))
    common = [cur(LANES), _const_spec((1, LANES)), _const_spec((1, LANES)),
              _const_spec((LANES, D_SSM)), _const_spec((CHUNK, CHUNK))]
    state = pltpu.VMEM((SSM_STATE, D_SSM), F32)
    if reverse:
        u, y_f, z, dsk, nw = rev_in
        args = (u, dt, dtb, arow, e, tri, y_f, z, dsk, nw)
        in_specs = [cur(CONV_CH)] + common + [cur(D_SSM), cur(D_SSM), _const_spec((1, D_SSM)),
                                              _const_spec((1, D_SSM))]
        out_specs = cur(D_SSM)
        out_shape = jax.ShapeDtypeStruct((n, D_SSM), BF16)
        scratch = [state, pltpu.VMEM((CHUNK, D_SSM), F32)]
    else:
        xbc, cw = fwd_in
        shift = _shift_matrices()
        args = (xbc, xbc, xbc, shift, cw, dt, dtb, arow, e, tri)
        in_specs = [cur(CONV_CH),
                    pl.BlockSpec((N_META, CONV_CH), lambda i, bk, pv, nx, fl: (pv[i], 0)),
                    pl.BlockSpec((N_META, CONV_CH), lambda i, bk, pv, nx, fl: (nx[i], 0)),
                    _const_spec(shift.shape), _const_spec((8, CONV_CH))] + common
        out_specs = [cur(D_SSM), cur(CONV_CH)]
        out_shape = [jax.ShapeDtypeStruct((n, D_SSM), F32), jax.ShapeDtypeStruct((n, CONV_CH), BF16)]
        scratch = [pltpu.VMEM((CHUNK + 2 * N_META, CONV_CH), BF16), state]
    grid_spec = pltpu.PrefetchScalarGridSpec(
        num_scalar_prefetch=4, grid=(len(blk),), in_specs=in_specs,
        out_specs=out_specs, scratch_shapes=scratch)
    return pl.pallas_call(
        functools.partial(_ssd_kernel, reverse), grid_spec=grid_spec, out_shape=out_shape,
        compiler_params=_cparams(("arbitrary",)),
    )(jnp.asarray(blk), jnp.asarray(prev), jnp.asarray(nxt), jnp.asarray(flags), *args)


def _moe_plan(route_t):
    n = route_t.shape[1]
    flat_e = route_t[0:2].T.astype(jnp.int32).reshape(-1)
    onehot = (flat_e[:, None] == jnp.arange(N_EXPERTS, dtype=jnp.int32)[None, :]).astype(jnp.int32)
    csum = jnp.cumsum(onehot, axis=0)
    counts = csum[-1]
    rank = jnp.sum(csum * onehot, axis=1) - 1
    padded = (counts + TME - 1) // TME * TME
    pends = jnp.cumsum(padded)
    pstarts = pends - padded
    dest = jnp.sum(onehot * pstarts[None, :], axis=1) + rank
    n_rows = _round_up(2 * n + N_EXPERTS * TME, TME)
    n_blocks = n_rows // TME
    block_e = jnp.sum((jnp.arange(n_blocks, dtype=jnp.int32)[:, None] * TME >= pends[None, :]).astype(jnp.int32),
                      axis=1)
    block_e = jnp.minimum(block_e, N_EXPERTS - 1)
    n_used = (pends[-1] // TME).astype(jnp.int32).reshape(1)
    tail = jnp.minimum(pends[-1] + jnp.arange(N_PAD_RANGES - N_EXPERTS + 1, dtype=jnp.int32) * TME, n_rows)
    pad_rows = jnp.concatenate([pstarts + counts, tail[:-1], pends, tail[1:]]).astype(jnp.int32)
    return dest.astype(jnp.int32), block_e, n_used, pad_rows, n_rows


def _row_tile(r):
    return pl.multiple_of(r * ROW_TILE, ROW_TILE)


def _dispatch_kernel(pad_ref, dest_hbm, h_ref, xs_hbm, idx, stage, ztile, sem_i, sem, sem_z):
    @pl.when(pl.program_id(0) == 0)
    def _():
        ztile[...] = jnp.zeros_like(ztile)

        def zero_copy(r):
            return pltpu.make_async_copy(ztile, xs_hbm.at[pl.ds(_row_tile(r), ROW_TILE)], sem_z)

        for e in range(N_PAD_RANGES):
            lo, hi = pad_ref[e], pad_ref[N_PAD_RANGES + e]

            def fill(r, c):
                zero_copy(r).start()
                return c

            def drain_fill(r, c):
                zero_copy(0).wait()
                return c

            lax.fori_loop(lo, hi, fill, 0)
            lax.fori_loop(lo, hi, drain_fill, 0)

    cp = pltpu.make_async_copy(dest_hbm.at[pl.program_id(0)], idx, sem_i)
    cp.start()
    hf = h_ref[...].astype(F32)
    for j in range(ROW_TILE):
        stage[pl.ds(j, TM, stride=ROW_TILE), :] = hf[:, j * LANES:(j + 1) * LANES]
    cp.wait()

    def row_copy(src_row, dst_row):
        return pltpu.make_async_copy(stage.at[pl.ds(_row_tile(src_row), ROW_TILE)],
                                     xs_hbm.at[pl.ds(_row_tile(dst_row), ROW_TILE)], sem)

    def issue(b, c):
        for u in range(DMA_UNROLL):
            r = b * DMA_UNROLL + u
            row_copy(lax.shift_right_logical(r, 1), idx[r]).start(priority=u % 2)
        return c

    lax.fori_loop(0, RB // DMA_UNROLL, issue, 0)

    for _ in range(RB // TM):
        pltpu.make_async_copy(stage, xs_hbm.at[pl.ds(0, TM * ROW_TILE)], sem).wait()


def _dispatch(h, dest, pad_rows, n_rows):
    n = h.shape[0]
    steps = n // TM
    grid_spec = pltpu.PrefetchScalarGridSpec(
        num_scalar_prefetch=1, grid=(steps,),
        in_specs=[pl.BlockSpec(memory_space=pl.ANY), pl.BlockSpec((TM, D_MODEL), lambda i, pad: (i, 0))],
        out_specs=pl.BlockSpec(memory_space=pl.ANY),
        scratch_shapes=[pltpu.SMEM((RB,), jnp.int32), pltpu.VMEM((TM * ROW_TILE, LANES), F32),
                        pltpu.VMEM((ROW_TILE, LANES), F32),
                        pltpu.SemaphoreType.DMA, pltpu.SemaphoreType.DMA, pltpu.SemaphoreType.DMA])
    return pl.pallas_call(
        _dispatch_kernel, grid_spec=grid_spec,
        out_shape=jax.ShapeDtypeStruct((n_rows * ROW_TILE, LANES), F32),
        compiler_params=_cparams(("arbitrary",)),
    )(pad_rows, dest.reshape(steps, RB), h)


def _expert_kernel(be_ref, nu_ref, x_ref, wg_ref, wu_ref, wd_ref, o_ref, xbf, act):
    del be_ref
    used = pl.program_id(0) < nu_ref[0]

    @pl.when(used)
    def _():
        for j in range(ROW_TILE):
            xbf[:, j * LANES:(j + 1) * LANES] = x_ref[pl.ds(j, TME, stride=ROW_TILE), :].astype(BF16)
        x = xbf[...]
        for c in range(D_FF_EXPERT // FF_CHUNK):
            sl = slice(c * FF_CHUNK, (c + 1) * FF_CHUNK)
            g = jnp.dot(x, wg_ref[:, sl], preferred_element_type=F32)
            u = jnp.dot(x, wu_ref[:, sl], preferred_element_type=F32)
            act[:, sl] = (_silu(g) * u).astype(BF16)
        out = jnp.dot(act[...], wd_ref[...], preferred_element_type=F32)
        for j in range(ROW_TILE):
            o_ref[pl.ds(j, TME, stride=ROW_TILE), :] = out[:, j * LANES:(j + 1) * LANES]

    @pl.when(jnp.logical_not(used))
    def _():
        o_ref[...] = jnp.zeros_like(o_ref)


def _experts(xs, block_e, n_used, wg, wu, wd):
    n_blocks = xs.shape[0] // (TME * ROW_TILE)
    rows = pl.BlockSpec((TME * ROW_TILE, LANES), lambda i, be, nu: (i, 0))
    resident = lambda shape: pl.BlockSpec((None,) + shape, lambda i, be, nu: (be[i], 0, 0),
                                          pipeline_mode=pl.Buffered(1))
    grid_spec = pltpu.PrefetchScalarGridSpec(
        num_scalar_prefetch=2, grid=(n_blocks,),
        in_specs=[pl.BlockSpec((TME * ROW_TILE, LANES), lambda i, be, nu: (jnp.minimum(i, nu[0] - 1), 0)),
                  resident((D_MODEL, D_FF_EXPERT)), resident((D_MODEL, D_FF_EXPERT)),
                  resident((D_FF_EXPERT, D_MODEL))],
        out_specs=rows,
        scratch_shapes=[pltpu.VMEM((TME, D_MODEL), BF16), pltpu.VMEM((TME, D_FF_EXPERT), BF16)])
    return pl.pallas_call(
        _expert_kernel, grid_spec=grid_spec,
        out_shape=jax.ShapeDtypeStruct(xs.shape, F32),
        compiler_params=_cparams(("arbitrary",)),
    )(block_e, n_used, xs, wg, wu, wd)


def _combine_kernel(split, dest_hbm, eo_hbm, x_ref, route_ref, nw_ref, out_a, out_b, idx, gbuf, xn, sem_i, sem_g):
    i = pl.program_id(0)
    n = pl.num_programs(0)

    def idx_copy(step, slot):
        return pltpu.make_async_copy(dest_hbm.at[step], idx.at[slot], sem_i.at[slot])

    def row_copy(slot, src_row, k, t):
        return pltpu.make_async_copy(eo_hbm.at[pl.ds(_row_tile(src_row), ROW_TILE)],
                                     gbuf.at[slot, k, pl.ds(_row_tile(t), ROW_TILE)], sem_g.at[slot])

    def issue_rows(slot):
        def issue(b, c):
            for u in range(DMA_UNROLL):
                r = b * DMA_UNROLL + u
                row_copy(slot, idx[slot, r], u % 2, b * (DMA_UNROLL // 2) + u // 2).start(priority=u % 2)
            return c

        lax.fori_loop(0, 2 * TC // DMA_UNROLL, issue, 0)

    @pl.when(i == 0)
    def _():
        idx_copy(0, 0).start()
        idx_copy(0, 0).wait()
        issue_rows(0)

        @pl.when(n > 1)
        def _():
            idx_copy(1, 1).start()

    for slot in range(2):
        @pl.when(jnp.logical_and(i % 2 == slot, i + 1 < n))
        def _(slot=slot):
            idx_copy(0, 1 - slot).wait()
            issue_rows(1 - slot)

            @pl.when(i + 2 < n)
            def _():
                idx_copy(i + 2, slot).start()

    w1 = route_ref[:, 0:1]
    w2 = route_ref[:, 1:2]
    for slot in range(2):
        @pl.when(i % 2 == slot)
        def _(slot=slot):
            for k in range(2):
                pltpu.make_async_copy(eo_hbm.at[pl.ds(0, TC * ROW_TILE)], gbuf.at[slot, k], sem_g.at[slot]).wait()
            for j in range(ROW_TILE):
                sl = slice(j * LANES, (j + 1) * LANES)
                xn[:, sl] = x_ref[:, sl] + (gbuf[slot, 0, pl.ds(j, TC, stride=ROW_TILE), :] * w1
                                            + gbuf[slot, 1, pl.ds(j, TC, stride=ROW_TILE), :] * w2)

    if split is None:
        out_a[...] = xn[...]
        out_b[...] = _rms_rows(xn[...], nw_ref[...]).astype(BF16)
    else:
        tiles_a, tiles_b = split

        @pl.when(i < tiles_a)
        def _():
            out_a[...] = xn[...]

        @pl.when(jnp.logical_and(i >= tiles_a, i < tiles_a + tiles_b))
        def _():
            out_b[...] = xn[...]


def _combine(eo, dest, x, route, nw_next, split_rows=None):
    n = x.shape[0]
    steps = n // TC
    row = lambda w: pl.BlockSpec((TC, w), lambda i: (i, 0))
    if split_rows is None:
        split = None
        out_specs = [row(D_MODEL), row(D_MODEL)]
        out_shape = [jax.ShapeDtypeStruct((n, D_MODEL), F32), jax.ShapeDtypeStruct((n, D_MODEL), BF16)]
    else:
        ta, tb = split = tuple(r // TC for r in split_rows)
        out_specs = [pl.BlockSpec((TC, D_MODEL), lambda i: (jnp.minimum(i, ta - 1), 0)),
                     pl.BlockSpec((TC, D_MODEL), lambda i: (jnp.clip(i - ta, 0, tb - 1), 0))]
        out_shape = [jax.ShapeDtypeStruct((r, D_MODEL), F32) for r in split_rows]
    return pl.pallas_call(
        functools.partial(_combine_kernel, split),
        grid=(steps,),
        in_specs=[pl.BlockSpec(memory_space=pl.ANY), pl.BlockSpec(memory_space=pl.ANY),
                  row(D_MODEL), row(2), _const_spec((1, D_MODEL))],
        out_specs=out_specs, out_shape=out_shape,
        scratch_shapes=[pltpu.SMEM((2, 2 * TC), jnp.int32), pltpu.VMEM((2, 2, TC * ROW_TILE, LANES), F32),
                        pltpu.VMEM((TC, D_MODEL), F32),
                        pltpu.SemaphoreType.DMA((2,)), pltpu.SemaphoreType.DMA((2,))],
        compiler_params=_cparams(("arbitrary",)),
    )(dest.reshape(steps, 2 * TC), eo, x, route, nw_next)


def _trunk(lay, x, h, p, group_rows):
    depth = p['norm1_w'].shape[0]
    row = lambda v: v.reshape(1, -1).astype(F32)
    for l in range(depth):
        w_in = p['w_in'][l]
        wqk = w_in[:, 0:2 * ATT_W].astype(BF16)
        wvzx = w_in[:, 2 * ATT_W:3 * ATT_W + D_SSM + CONV_CH].astype(BF16)
        wdt = jnp.pad(w_in[:, 3 * ATT_W + D_SSM + CONV_CH:], ((0, 0), (0, LANES - 2 * N_HEADS))).astype(BF16)
        nw_qk = jnp.concatenate([jnp.tile(p['q_norm_w'][l].astype(F32), N_HEADS) * (HEAD_DIM ** -0.5),
                                 jnp.tile(p['k_norm_w'][l].astype(F32), N_HEADS)]).reshape(1, -1)
        qk, v, z, xbc, dt_raw = _inproj(h, wqk, wvzx, wdt, nw_qk)

        attn = _attention(lay, qk, v, p['rpb'][l])

        cw = jnp.concatenate([p['conv_w'][l].astype(F32).T, p['conv_b'][l].astype(F32)[None],
                              jnp.zeros((8 - CONV_K - 1, CONV_CH), F32)], axis=0)
        pad = jnp.zeros((LANES - 2 * N_HEADS,), F32)
        dtb = jnp.concatenate([p['dt_bias'][l].astype(F32).reshape(-1), pad]).reshape(1, -1)
        arow = jnp.concatenate([-jnp.exp(p['a_log'][l].astype(F32)).reshape(-1), pad]).reshape(1, -1)
        y_f, u = _ssd_pass(lay, False, dt_raw, dtb, arow, fwd_in=(xbc, cw))
        dsk = jnp.repeat(p['d_skip'][l].astype(F32), HEAD_DIM).reshape(1, -1)
        y = _ssd_pass(lay, True, dt_raw, dtb, arow, rev_in=(u, y_f, z, dsk, row(p['ssm_norm_w'][l])))

        w_out = p['w_out'][l].astype(BF16)
        nw_next = row(p['norm1_w'][l + 1]) if l + 1 < depth else jnp.ones((1, D_MODEL), F32)
        j = l // 2
        if l % 2 == 0:
            x, h2 = _outproj(attn, y, x, w_out, row(p['attn_out_norm_w'][l]), row(p['norm2_w'][l]))
            x, h = _ffn(h2, x, p['ffn_w_gate'][j].astype(BF16), p['ffn_w_up'][j].astype(BF16),
                        p['ffn_w_down'][j].astype(BF16), nw_next)
        else:
            wr = jnp.pad(p['moe_router'][j].astype(F32).T, ((0, ROUTER_ROWS - N_EXPERTS), (0, 0)))
            x, h2, route_t = _outproj(attn, y, x, w_out, row(p['attn_out_norm_w'][l]), row(p['norm2_w'][l]),
                                      router=jnp.concatenate(_split_bf16(wr), axis=0))
            dest, block_e, n_used, pad_rows, n_rows = _moe_plan(route_t)
            xs = _dispatch(h2, dest, pad_rows, n_rows)
            eo = _experts(xs, block_e, n_used, p['moe_w_gate'][j].astype(BF16),
                          p['moe_w_up'][j].astype(BF16), p['moe_w_down'][j].astype(BF16))
            if l + 1 == depth and len(group_rows) == 2:
                return _combine(eo, dest, x, route_t[2:4].T, nw_next, split_rows=group_rows)
            x, h = _combine(eo, dest, x, route_t[2:4].T, nw_next)
    offs = np.concatenate([[0], np.cumsum(group_rows)])
    return [x[offs[g]:offs[g + 1]] for g in range(len(group_rows))]


def kernel(x_prompt, x_sample, meta_tokens, norm1_w, w_in, q_norm_w, k_norm_w, rpb, attn_out_norm_w,
           conv_w, conv_b, dt_bias, a_log, d_skip, ssm_norm_w, w_out, norm2_w,
           ffn_w_gate, ffn_w_up, ffn_w_down, moe_router, moe_w_gate, moe_w_up, moe_w_down):
    p = dict(norm1_w=norm1_w, w_in=w_in, q_norm_w=q_norm_w, k_norm_w=k_norm_w, rpb=rpb,
             attn_out_norm_w=attn_out_norm_w, conv_w=conv_w, conv_b=conv_b, dt_bias=dt_bias, a_log=a_log,
             d_skip=d_skip, ssm_norm_w=ssm_norm_w, w_out=w_out, norm2_w=norm2_w, ffn_w_gate=ffn_w_gate,
             ffn_w_up=ffn_w_up, ffn_w_down=ffn_w_down, moe_router=moe_router, moe_w_gate=moe_w_gate,
             moe_w_up=moe_w_up, moe_w_down=moe_w_down)
    groups = (x_prompt, x_sample)
    lay = _Layout([g.shape[1] for g in groups for _ in range(g.shape[0])])
    meta_block = jnp.concatenate([jnp.zeros((META_OFF, D_MODEL), F32), meta_tokens.astype(F32)], axis=0)
    tail = lay.n_tok - lay.n_grid - CHUNK * lay.n_seq
    tail_rows = jnp.concatenate([jnp.tile(meta_block, (lay.n_seq, 1)), jnp.zeros((tail, D_MODEL), F32)], axis=0)
    x, h = _embed([g.reshape(-1, D_MODEL) for g in groups] + [tail_rows], norm1_w[0])
    flat = _trunk(lay, x, h, p, [g.shape[0] * g.shape[1] for g in groups])
    return tuple(f.reshape(g.shape).astype(g.dtype) for f, g in zip(flat, groups))
```

```python
import functools

import numpy as np
import jax
import jax.numpy as jnp
from jax import lax
from jax.experimental import pallas as pl
from jax.experimental.pallas import tpu as pltpu

F32 = jnp.float32
BF16 = jnp.bfloat16

D_MODEL = 1024
N_META = 16
GRID_W = 64
NA_ROWS = 8
NA_COLS = 16
N_HEADS = 16
HEAD_DIM = 64
N_PAIRS = N_HEADS // 2
UNIT_HEADS = 4
UNIT_W = UNIT_HEADS * HEAD_DIM
N_UNITS = N_HEADS // UNIT_HEADS
ATT_W = N_HEADS * HEAD_DIM
D_SSM = 1024
SSM_GROUPS = 2
SSM_STATE = 128
GROUP_W = D_SSM // SSM_GROUPS
CONV_K = 5
CONV_CH = D_SSM + 2 * SSM_GROUPS * SSM_STATE
D_FF = 2816
N_EXPERTS = 8
D_FF_EXPERT = 3584
RMS_EPS = 1e-6
NEG_INF = -1e30
LOG2E = 1.4426950408889634

LANES = 128
ROW_TILE = D_MODEL // LANES
CHUNK = 128
META_OFF = CHUNK - N_META
TM = 512
BAND_ROWS = 32
BAND = BAND_ROWS * GRID_W
HALO_ROWS = NA_ROWS // 2
BUF_ROWS = BAND_ROWS + NA_ROWS - 1
FF_CHUNK = 256
TME = 512
ROUTER_ROWS = 16
N_PAD_RANGES = 2 * N_EXPERTS + 1
RB = 2 * TM
TC = 512
DMA_UNROLL = 8
VMEM_LIMIT = 56 * 1024 * 1024


def _cparams(sem):
    return pltpu.CompilerParams(dimension_semantics=sem, vmem_limit_bytes=VMEM_LIMIT)


def _round_up(a, b):
    return (a + b - 1) // b * b


def _const_spec(shape):
    nd = len(shape)
    return pl.BlockSpec(shape, lambda *_: (0,) * nd)


class _Layout:
    def __init__(self, seq_lens):
        self.seq_lens = tuple(seq_lens)
        self.n_seq = len(seq_lens)
        self.n_grid = sum(seq_lens)
        self.starts = np.concatenate([[0], np.cumsum(seq_lens)[:-1]]).astype(np.int64)
        self.n_tok = _round_up(self.n_grid + CHUNK * self.n_seq, TM)
        assert all(s % BAND == 0 and s // GRID_W >= NA_ROWS for s in seq_lens)

    def meta_block(self, s):
        return self.n_grid // CHUNK + s

    def attn_tables(self):
        prev, nxt, flags, mblk = [], [], [], []
        for s, (st, ln) in enumerate(zip(self.starts, self.seq_lens)):
            lo, hi = st // BAND, (st + ln) // BAND
            for b in range(lo, hi):
                per = BAND_ROWS // HALO_ROWS
                prev.append(max(b * per - 1, lo * per))
                nxt.append(min((b + 1) * per, hi * per - 1))
                flags.append((1 if b == lo else 0) | (2 if b == hi - 1 else 0))
                mblk.append((self.n_grid + CHUNK * s + META_OFF) // N_META)
        return [np.asarray(a, np.int32) for a in (prev, nxt, flags, mblk)]

    def ssd_tables(self, reverse):
        blk, prev, nxt, flags = [], [], [], []
        for s, (st, ln) in enumerate(zip(self.starts, self.seq_lens)):
            nc = ln // CHUNK
            b0 = st // CHUNK
            meta16 = (self.n_grid + CHUNK * s + META_OFF) // N_META
            steps = []
            steps.append((self.meta_block(s), 0, b0 * 8, 2 | 4))
            for c in range(nc):
                b = b0 + c
                p16 = meta16 if c == 0 else b * 8 - 1
                n16 = (b + 1) * 8 if c < nc - 1 else 0
                steps.append((b, p16, n16, 1 | (2 if c < nc - 1 else 0)))
            if reverse:
                steps = steps[::-1]
            for i, (b, p, n, f) in enumerate(steps):
                blk.append(b); prev.append(p); nxt.append(n)
                flags.append(f | (8 if i == 0 else 0))
        for b in range(self.n_grid // CHUNK + self.n_seq, self.n_tok // CHUNK):
            blk.append(b); prev.append(0); nxt.append(0)
            flags.append(8 | 16)
        return [np.asarray(a, np.int32) for a in (blk, prev, nxt, flags)]


def _rms_rows(x, w):
    ms = jnp.mean(x * x, axis=-1, keepdims=True)
    return x * lax.rsqrt(ms + RMS_EPS) * w


def _embed_kernel(bounds, *refs):
    srcs, (w_ref, x_ref, h_ref) = refs[:len(bounds)], refs[len(bounds):]
    i = pl.program_id(0)
    lo = 0
    for src, hi in zip(srcs, bounds):
        @pl.when(jnp.logical_and(i >= lo, i < hi))
        def _(src=src):
            x = src[...].astype(F32)
            x_ref[...] = x
            h_ref[...] = _rms_rows(x, w_ref[...]).astype(BF16)
        lo = hi


def _embed(parts, w):
    tiles = [a.shape[0] // TM for a in parts]
    assert all(a.shape[0] % TM == 0 for a in parts)
    bounds = tuple(int(b) for b in np.cumsum(tiles))
    starts = (0,) + bounds[:-1]
    n = bounds[-1] * TM
    src_spec = lambda s, t: pl.BlockSpec((TM, D_MODEL), lambda i: (jnp.clip(i - s, 0, t - 1), 0))
    row = pl.BlockSpec((TM, D_MODEL), lambda i: (i, 0))
    return pl.pallas_call(
        functools.partial(_embed_kernel, bounds),
        grid=(bounds[-1],),
        in_specs=[src_spec(s, t) for s, t in zip(starts, tiles)] + [_const_spec((1, D_MODEL))],
        out_specs=[row, row],
        out_shape=[jax.ShapeDtypeStruct((n, D_MODEL), F32), jax.ShapeDtypeStruct((n, D_MODEL), BF16)],
        compiler_params=_cparams(("arbitrary",)),
    )(*parts, w.reshape(1, D_MODEL))


def _inproj_kernel(h_ref, wqk_ref, wvzx_ref, wdt_ref, g_ref, nw_ref,
                   qk_ref, v_ref, z_ref, xbc_ref, dt_ref):
    h = h_ref[...]
    g = g_ref[...]
    n_qk = 2 * ATT_W // FF_CHUNK
    col = lambda c: slice(c * FF_CHUNK, (c + 1) * FF_CHUNK)
    y_next = jnp.dot(h, wqk_ref[:, col(0)], preferred_element_type=F32)
    for c in range(n_qk):
        y = y_next
        if c + 1 < n_qk:
            y_next = jnp.dot(h, wqk_ref[:, col(c + 1)], preferred_element_type=F32)
        ss = jnp.dot((y * y).astype(BF16), g, preferred_element_type=F32)
        inv = lax.rsqrt(ss * (1.0 / HEAD_DIM) + RMS_EPS)
        qk_ref[:, col(c)] = (y * inv * nw_ref[:, col(c)]).astype(BF16)
    for c in range(ATT_W // FF_CHUNK):
        sl = slice(c * FF_CHUNK, (c + 1) * FF_CHUNK)
        v_ref[:, sl] = jnp.dot(h, wvzx_ref[:, sl], preferred_element_type=F32).astype(BF16)
    for c in range(D_SSM // FF_CHUNK):
        sl = slice(c * FF_CHUNK, (c + 1) * FF_CHUNK)
        src = slice(ATT_W + c * FF_CHUNK, ATT_W + (c + 1) * FF_CHUNK)
        z_ref[:, sl] = jnp.dot(h, wvzx_ref[:, src], preferred_element_type=F32).astype(BF16)
    for c in range(CONV_CH // FF_CHUNK):
        sl = slice(c * FF_CHUNK, (c + 1) * FF_CHUNK)
        src = slice(ATT_W + D_SSM + c * FF_CHUNK, ATT_W + D_SSM + (c + 1) * FF_CHUNK)
        xbc_ref[:, sl] = jnp.dot(h, wvzx_ref[:, src], preferred_element_type=F32).astype(BF16)
    dt_ref[...] = jnp.dot(h, wdt_ref[...], preferred_element_type=F32)


def _head_sum_matrix():
    i = np.arange(FF_CHUNK)
    return jnp.asarray((i[:, None] // HEAD_DIM == i[None, :] // HEAD_DIM).astype(np.float32), BF16)


def _inproj(h, wqk, wvzx, wdt, nw):
    n = h.shape[0]
    row = lambda w: pl.BlockSpec((TM, w), lambda i: (i, 0))
    return pl.pallas_call(
        _inproj_kernel,
        grid=(n // TM,),
        in_specs=[row(D_MODEL), _const_spec(wqk.shape), _const_spec(wvzx.shape), _const_spec(wdt.shape),
                  _const_spec((FF_CHUNK, FF_CHUNK)), _const_spec((1, 2 * ATT_W))],
        out_specs=[row(2 * ATT_W), row(ATT_W), row(D_SSM), row(CONV_CH), row(LANES)],
        out_shape=[jax.ShapeDtypeStruct((n, 2 * ATT_W), BF16), jax.ShapeDtypeStruct((n, ATT_W), BF16),
                   jax.ShapeDtypeStruct((n, D_SSM), BF16), jax.ShapeDtypeStruct((n, CONV_CH), BF16),
                   jax.ShapeDtypeStruct((n, LANES), F32)],
        compiler_params=_cparams(("parallel",)),
    )(h, wqk, wvzx, wdt, _head_sum_matrix(), nw)


def _split_bf16(x):
    hi = x.astype(BF16)
    lo = (x - hi.astype(F32)).astype(BF16)
    return hi, lo


def _outproj_kernel(with_router, grid_tiles, attn_ref, attn_tail_ref, y_ref, x_ref, w_ref, aw_ref, nw_ref, *rest):
    if with_router:
        wr_ref, xo_ref, h_ref, route_ref = rest
    else:
        xo_ref, h_ref = rest
    acc = x_ref[...] + jnp.dot(y_ref[...], w_ref[ATT_W:ATT_W + D_SSM, :], preferred_element_type=F32)
    attn = jnp.where(pl.program_id(0) < grid_tiles, attn_ref[...], attn_tail_ref[...])
    a = _rms_rows(attn.astype(F32), aw_ref[...]).astype(BF16)
    acc = acc + jnp.dot(a, w_ref[0:ATT_W, :], preferred_element_type=F32)
    xo_ref[...] = acc
    h2 = _rms_rows(acc, nw_ref[...])
    h_ref[...] = h2.astype(BF16)
    if with_router:
        hh, hl = _split_bf16(h2)
        wr = wr_ref[...]
        lg = _nt_dot(wr, hh)
        logits = lg[0:ROUTER_ROWS] + lg[ROUTER_ROWS:] + _nt_dot(wr[0:ROUTER_ROWS], hl)
        sub = lax.broadcasted_iota(jnp.int32, logits.shape, 0)
        logits = jnp.where(sub < N_EXPERTS, logits, NEG_INF)
        m1 = jnp.max(logits, axis=0, keepdims=True)
        i1 = jnp.min(jnp.where(logits == m1, sub, ROUTER_ROWS), axis=0, keepdims=True)
        rest_l = jnp.where(sub == i1, NEG_INF, logits)
        m2 = jnp.max(rest_l, axis=0, keepdims=True)
        i2 = jnp.min(jnp.where(rest_l == m2, sub, ROUTER_ROWS), axis=0, keepdims=True)
        e = jnp.exp(m2 - m1)
        w1 = 1.0 / (1.0 + e)
        route_ref[...] = jnp.concatenate(
            [i1.astype(F32), i2.astype(F32), w1, e * w1, jnp.zeros((4, logits.shape[1]), F32)], axis=0)


def _outproj(attn, y, x, w_out, aw, nw, router=None):
    n = x.shape[0]
    row = lambda w: pl.BlockSpec((TM, w), lambda i: (i, 0))
    tg, tt = attn[0].shape[0] // TM, attn[1].shape[0] // TM
    in_specs = [pl.BlockSpec((TM, ATT_W), lambda i: (jnp.minimum(i, tg - 1), 0)),
                pl.BlockSpec((TM, ATT_W), lambda i: (jnp.clip(i - tg, 0, tt - 1), 0)),
                row(D_SSM), row(D_MODEL), _const_spec(w_out.shape),
                _const_spec((1, ATT_W)), _const_spec((1, D_MODEL))]
    out_specs = [row(D_MODEL), row(D_MODEL)]
    out_shape = [jax.ShapeDtypeStruct((n, D_MODEL), F32), jax.ShapeDtypeStruct((n, D_MODEL), BF16)]
    args = [attn[0], attn[1], y, x, w_out, aw, nw]
    if router is not None:
        in_specs.append(_const_spec((2 * ROUTER_ROWS, D_MODEL)))
        out_specs.append(pl.BlockSpec((8, TM), lambda i: (0, i)))
        out_shape.append(jax.ShapeDtypeStruct((8, n), F32))
        args.append(router)
    return pl.pallas_call(
        functools.partial(_outproj_kernel, router is not None, tg),
        grid=(n // TM,), in_specs=in_specs, out_specs=out_specs, out_shape=out_shape,
        compiler_params=_cparams(("parallel",)),
    )(*args)


def _silu(x):
    return x * (1.0 / (1.0 + jnp.exp(-x)))


def _ffn_kernel(h_ref, x_ref, wg_ref, wu_ref, wd_ref, nw_ref, xo_ref, ho_ref, act_ref):
    h = h_ref[...]
    for c in range(D_FF // FF_CHUNK):
        sl = slice(c * FF_CHUNK, (c + 1) * FF_CHUNK)
        g = jnp.dot(h, wg_ref[:, sl], preferred_element_type=F32)
        u = jnp.dot(h, wu_ref[:, sl], preferred_element_type=F32)
        act_ref[:, sl] = (_silu(g) * u).astype(BF16)
    xn = x_ref[...] + jnp.dot(act_ref[...], wd_ref[...], preferred_element_type=F32)
    xo_ref[...] = xn
    ho_ref[...] = _rms_rows(xn, nw_ref[...]).astype(BF16)


def _ffn(h, x, wg, wu, wd, nw_next):
    n = x.shape[0]
    row = lambda w: pl.BlockSpec((TM, w), lambda i: (i, 0))
    return pl.pallas_call(
        _ffn_kernel,
        grid=(n // TM,),
        in_specs=[row(D_MODEL), row(D_MODEL), _const_spec(wg.shape), _const_spec(wu.shape),
                  _const_spec(wd.shape), _const_spec((1, D_MODEL))],
        out_specs=[row(D_MODEL), row(D_MODEL)],
        out_shape=[jax.ShapeDtypeStruct((n, D_MODEL), F32), jax.ShapeDtypeStruct((n, D_MODEL), BF16)],
        scratch_shapes=[pltpu.VMEM((TM, D_FF), BF16)],
        compiler_params=_cparams(("parallel",)),
    )(h, x, wg, wu, wd, nw_next)


N_CB = GRID_W // NA_COLS
SPAN_START = (0, 0, 16, 32)
SPAN_W = (32, 48, 48, 32)
META_SLOT = (24, 0, 0, 0)
META_PER_ROW = N_META // 2
KTILE = 16
ATTN_DEPTH = 16


def _attn_bias_tables(rpb):
    i = np.arange(NA_ROWS)
    dr = np.clip(i[None, :] - i[:, None] + NA_ROWS - 1, 0, 2 * NA_ROWS - 2)
    tables = {}
    for j in range(N_CB):
        span = SPAN_W[j]
        qc = NA_COLS * j + np.arange(NA_COLS)
        kc = SPAN_START[j] + np.arange(span)
        st = np.clip(qc - NA_COLS // 2, 0, GRID_W - NA_COLS)
        valid = (kc[None, :] >= st[:, None]) & (kc[None, :] < st[:, None] + NA_COLS)
        dc = np.clip(kc[None, :] - qc[:, None] + NA_COLS - 1, 0, 2 * NA_COLS - 2)
        slot = (np.arange(span) >= META_SLOT[j]) & (np.arange(span) < META_SLOT[j] + META_PER_ROW)
        assert not valid[:, slot].any()
        b = rpb.astype(F32)[:, dr][:, :, :, dc]
        b = jnp.where(valid[None, None, None], b, NEG_INF)
        meta_vis = np.where(i[:, None] < 2, 0.0, NEG_INF) * np.ones((1, span))
        b = jnp.where(slot[None, None, None, None, :], jnp.asarray(meta_vis, F32)[None, None, :, None, :], b)
        b = b.reshape(N_UNITS, UNIT_HEADS, NA_ROWS, NA_ROWS, NA_COLS, span)
        b = jnp.transpose(b, (0, 2, 1, 4, 3, 5)).reshape(N_UNITS, NA_ROWS, UNIT_HEADS * NA_COLS, NA_ROWS * span)
        tables.setdefault(span, []).append(b)
    return [jnp.stack(t, axis=1) for _, t in sorted(tables.items())]


def _pair_queries(q):
    lo = lax.broadcasted_iota(jnp.int32, q.shape, 1) < HEAD_DIM
    zero = jnp.zeros_like(q)
    return jnp.concatenate([jnp.where(lo, q, zero), jnp.where(lo, zero, q)], axis=0)


def _head_queries(q):
    head = lax.broadcasted_iota(jnp.int32, q.shape, 1) // HEAD_DIM
    zero = jnp.zeros_like(q)
    return jnp.concatenate([jnp.where(head == h, q, zero) for h in range(UNIT_HEADS)], axis=0)


def _nt_dot(a, b):
    return lax.dot_general(a, b, (((1,), (1,)), ((), ())), preferred_element_type=F32)


def _attn_kernel(prev_ref, next_ref, flag_ref, mblk_ref,
                 q_ref, kp_ref, kc_ref, kn_ref, vp_ref, vc_ref, vn_ref, km_ref, vm_ref,
                 b32_ref, b48_ref, o_ref, *spans):
    del prev_ref, next_ref, mblk_ref
    flags = flag_ref[pl.program_id(1)]
    is_first = (flags & 1) != 0
    is_last = (flags & 2) != 0
    low = lax.broadcasted_iota(jnp.int32, (KTILE, UNIT_W), 0) < META_PER_ROW

    for refs, bufs, m_ref in (((kp_ref, kc_ref, kn_ref), spans[:N_CB], km_ref),
                              ((vp_ref, vc_ref, vn_ref), spans[N_CB:], vm_ref)):
        m = m_ref[...]
        mf = m.astype(F32)
        m_swapped = jnp.concatenate([mf[META_PER_ROW:], mf[:META_PER_ROW]], axis=0).astype(BF16)
        for rr in range(BUF_ROWS):
            if rr < HALO_ROWS:
                ref, row = refs[0], rr
            elif rr < HALO_ROWS + BAND_ROWS:
                ref, row = refs[1], rr - HALO_ROWS
            else:
                ref, row = refs[2], rr - HALO_ROWS - BAND_ROWS
            for j in range(N_CB):
                span = SPAN_W[j]
                for t in range(span // KTILE):
                    off = row * GRID_W + SPAN_START[j] + t * KTILE
                    tile = ref[off:off + KTILE, :]
                    if t == META_SLOT[j] // KTILE:
                        if META_SLOT[j] % KTILE == 0:
                            tile = jnp.where(low, m if rr % 2 == 0 else m_swapped, tile)
                        else:
                            tile = jnp.where(low, tile, m_swapped if rr % 2 == 0 else m)
                    bufs[j][rr * span + t * KTILE:rr * span + (t + 1) * KTILE, :] = tile

    lane_head = lax.broadcasted_iota(jnp.int32, (NA_COLS, UNIT_W), 1) // HEAD_DIM
    starts = []
    for ri in range(BAND_ROWS):
        ls = jnp.int32(ri)
        ls = jnp.where(is_first, jnp.maximum(ls, HALO_ROWS), ls)
        ls = jnp.where(is_last, jnp.minimum(ls, BAND_ROWS - HALO_ROWS), ls)
        starts.append((ls, HALO_ROWS + ri - ls))

    def scores(u):
        ri, j = divmod(u, N_CB)
        ls, delta = starts[ri]
        span = SPAN_W[j]
        kwin = spans[j][pl.ds(pl.multiple_of(ls * span, KTILE), NA_ROWS * span), :]
        q0 = ri * GRID_W + j * NA_COLS
        bias_ref = b32_ref if span == SPAN_W[0] else b48_ref
        return _nt_dot(_head_queries(q_ref[q0:q0 + NA_COLS, :]), kwin) + bias_ref[SPAN_W[:j].count(span), delta]

    def finish(u, s):
        ri, j = divmod(u, N_CB)
        span = SPAN_W[j]
        vwin = spans[N_CB + j][pl.ds(pl.multiple_of(starts[ri][0] * span, KTILE), NA_ROWS * span), :]
        p = jnp.exp(s - jnp.max(s, axis=-1, keepdims=True))
        l = jnp.sum(p, axis=-1, keepdims=True)
        o = jnp.dot(p.astype(BF16), vwin, preferred_element_type=F32) / l
        q0 = ri * GRID_W + j * NA_COLS
        out = o[:NA_COLS]
        for h in range(1, UNIT_HEADS):
            out = jnp.where(lane_head == h, o[h * NA_COLS:(h + 1) * NA_COLS], out)
        o_ref[q0:q0 + NA_COLS, :] = out.astype(BF16)

    n_units = BAND_ROWS * N_CB
    pending = {}
    for t in range(n_units + ATTN_DEPTH):
        if t < n_units:
            pending[t] = scores(t)
        if t >= ATTN_DEPTH:
            finish(t - ATTN_DEPTH, pending.pop(t - ATTN_DEPTH))


def _attn_meta_kernel(n_seq, qk_ref, v_ref, mbias_ref, o_ref):
    s = pl.program_id(0)
    o_ref[...] = jnp.zeros_like(o_ref)

    @pl.when(s < n_seq)
    def _():
        mbias = mbias_ref[...]
        for p in range(N_PAIRS):
            c = slice(p * LANES, (p + 1) * LANES)
            q = qk_ref[META_OFF:CHUNK, c]
            k = qk_ref[:, ATT_W + p * LANES:ATT_W + (p + 1) * LANES]
            qs = _pair_queries(q)
            sc = _nt_dot(qs, k) + mbias
            m = jnp.max(sc, axis=-1, keepdims=True)
            e = jnp.exp(sc - m)
            l = jnp.sum(e, axis=-1, keepdims=True)
            o = jnp.dot(e.astype(BF16), v_ref[:, c], preferred_element_type=F32) / l
            lo = lax.broadcasted_iota(jnp.int32, (N_META, LANES), 1) < HEAD_DIM
            o_ref[META_OFF:CHUNK, c] = jnp.where(lo, o[0:N_META], o[N_META:]).astype(BF16)


def _attention(lay, qk, v, rpb):
    n = qk.shape[0]
    prev, nxt, flags, mblk = lay.attn_tables()
    n_bands = lay.n_grid // BAND
    b32, b48 = _attn_bias_tables(rpb)
    lane = np.arange(LANES)
    mbias_meta = jnp.asarray(np.where(lane >= META_OFF, 0.0, NEG_INF)[None, :], F32)
    kcol = ATT_W // UNIT_W
    halo = HALO_ROWS * GRID_W
    bias_spec = lambda a: pl.BlockSpec((None,) + a.shape[1:], lambda p, b, *_: (p, 0, 0, 0, 0))
    grid_spec = pltpu.PrefetchScalarGridSpec(
        num_scalar_prefetch=4,
        grid=(N_UNITS, n_bands),
        in_specs=[
            pl.BlockSpec((BAND, UNIT_W), lambda p, b, *_: (b, p)),
            pl.BlockSpec((halo, UNIT_W), lambda p, b, pv, nx, fl, mb: (pv[b], kcol + p)),
            pl.BlockSpec((BAND, UNIT_W), lambda p, b, pv, nx, fl, mb: (b, kcol + p)),
            pl.BlockSpec((halo, UNIT_W), lambda p, b, pv, nx, fl, mb: (nx[b], kcol + p)),
            pl.BlockSpec((halo, UNIT_W), lambda p, b, pv, nx, fl, mb: (pv[b], p)),
            pl.BlockSpec((BAND, UNIT_W), lambda p, b, pv, nx, fl, mb: (b, p)),
            pl.BlockSpec((halo, UNIT_W), lambda p, b, pv, nx, fl, mb: (nx[b], p)),
            pl.BlockSpec((N_META, UNIT_W), lambda p, b, pv, nx, fl, mb: (mb[b], kcol + p)),
            pl.BlockSpec((N_META, UNIT_W), lambda p, b, pv, nx, fl, mb: (mb[b], p)),
            bias_spec(b32), bias_spec(b48),
        ],
        out_specs=pl.BlockSpec((BAND, UNIT_W), lambda p, b, *_: (b, p)),
        scratch_shapes=[pltpu.VMEM((BUF_ROWS * w, UNIT_W), BF16) for w in SPAN_W] * 2,
    )
    attn = pl.pallas_call(
        _attn_kernel, grid_spec=grid_spec,
        out_shape=jax.ShapeDtypeStruct((lay.n_grid, ATT_W), BF16),
        compiler_params=_cparams(("arbitrary", "arbitrary")),
    )(jnp.asarray(prev), jnp.asarray(nxt), jnp.asarray(flags), jnp.asarray(mblk),
      qk, qk, qk, qk, v, v, v, qk, v, b32, b48)
    mb0 = lay.n_grid // CHUNK
    n_tail = (n - lay.n_grid) // CHUNK
    attn_tail = pl.pallas_call(
        functools.partial(_attn_meta_kernel, lay.n_seq),
        grid=(n_tail,),
        in_specs=[pl.BlockSpec((CHUNK, 2 * ATT_W), lambda s: (mb0 + s, 0)),
                  pl.BlockSpec((CHUNK, ATT_W), lambda s: (mb0 + s, 0)),
                  _const_spec((1, LANES))],
        out_specs=pl.BlockSpec((CHUNK, ATT_W), lambda s: (s, 0)),
        out_shape=jax.ShapeDtypeStruct((n - lay.n_grid, ATT_W), BF16),
        compiler_params=_cparams(("arbitrary",)),
    )(qk, v, mbias_meta)
    return attn, attn_tail


def _softplus(x):
    return jnp.maximum(x, 0.0) + jnp.log(1.0 + jnp.exp(-jnp.abs(x)))


def _split3_bf16(x):
    hi = x.astype(BF16)
    r = x - hi.astype(F32)
    mid = r.astype(BF16)
    lo = (r - mid.astype(F32)).astype(BF16)
    return hi, mid, lo


def _expand_heads(w, e):
    hi, lo = _split_bf16(w)
    return jnp.dot(hi, e, preferred_element_type=F32) + jnp.dot(lo, e, preferred_element_type=F32)


def _ssd_kernel(reverse, blk_ref, prev_ref, next_ref, flag_ref, *refs):
    if reverse:
        (u_ref, dt_ref, dtb_ref, arow_ref, e_ref, tri_ref, yf_ref, z_ref, dsk_ref, nw_ref,
         o_ref, state, ybuf) = refs
    else:
        (xbc_ref, xp_ref, xn_ref, shift_ref, cw_ref, dt_ref, dtb_ref, arow_ref, e_ref, tri_ref,
         o_ref, u_ref, xe, state) = refs
        ybuf = o_ref
    del blk_ref, prev_ref, next_ref
    flags = flag_ref[pl.program_id(0)]
    is_meta = (flags & 4) != 0

    @pl.when((flags & 8) != 0)
    def _():
        state[...] = jnp.zeros_like(state)

    if reverse:
        u = u_ref[...].astype(F32)
    else:
        zero = jnp.zeros((N_META, CONV_CH), BF16)
        xe[0:N_META, :] = jnp.where((flags & 1) != 0, xp_ref[...], zero)
        xe[N_META:N_META + CHUNK, :] = xbc_ref[...]
        xe[N_META + CHUNK:, :] = jnp.where((flags & 2) != 0, xn_ref[...], zero)

        @pl.when(is_meta)
        def _():
            xe[0:N_META + META_OFF, :] = jnp.zeros((N_META + META_OFF, CONV_CH), BF16)

        shifted = jnp.dot(shift_ref[...], xe[...], preferred_element_type=F32)
        half = CONV_K // 2
        u = cw_ref[CONV_K:CONV_K + 1, :] + cw_ref[half:half + 1, :] * xe[N_META:N_META + CHUNK, :].astype(F32)
        for n, k in enumerate(k for k in range(CONV_K) if k != half):
            u = u + cw_ref[k:k + 1, :] * shifted[n * CHUNK:(n + 1) * CHUNK, :]
        u = _silu(u)
        u_ref[...] = u.astype(BF16)
    xs = u[:, 0:D_SSM]
    x_bf = xs.astype(BF16)

    rid = lax.broadcasted_iota(jnp.int32, (CHUNK, 1), 0)
    valid = rid >= jnp.where(is_meta, META_OFF, 0)
    dt = jnp.where(valid, _softplus(dt_ref[...] + dtb_ref[...]), 0.0)
    a = dt * arow_ref[...]
    tri = tri_ref[...]
    ah, am, al = _split3_bf16(a)
    acum = (jnp.dot(tri, ah, preferred_element_type=F32) + jnp.dot(tri, am, preferred_element_type=F32)
            + jnp.dot(tri, al, preferred_element_type=F32))
    acum = acum * LOG2E
    acum_t = acum.T
    dt_t = dt.T
    edge = 0 if reverse else CHUNK - 1
    a_tot = acum[edge:edge + 1, :]
    e = e_ref[...]
    w1e = jnp.dot((dt * jnp.exp2(a_tot - acum)).astype(BF16), e, preferred_element_type=F32)
    w2e = jnp.dot(jnp.exp2(acum).astype(BF16), e, preferred_element_type=F32)
    decay_row = _expand_heads(jnp.broadcast_to(jnp.exp2(a_tot), (8, LANES)), e)[0:1]
    xw = (xs * w1e).astype(BF16)

    li = lax.broadcasted_iota(jnp.int32, (CHUNK, CHUNK), 0)
    si = lax.broadcasted_iota(jnp.int32, (CHUNK, CHUNK), 1)
    causal = (si >= li) if reverse else (li >= si)
    lane_lo = si < HEAD_DIM
    ho = N_HEADS if reverse else 0
    heads_per_group = N_HEADS // SSM_GROUPS
    for g in range(SSM_GROUPS):
        gsl = slice(g * GROUP_W, (g + 1) * GROUP_W)
        bg = u[:, D_SSM + g * SSM_STATE:D_SSM + (g + 1) * SSM_STATE]
        cg = u[:, D_SSM + (SSM_GROUPS + g) * SSM_STATE:D_SSM + (SSM_GROUPS + g + 1) * SSM_STATE].astype(BF16)
        cb = _nt_dot(cg, bg.astype(BF16))
        st_old = state[:, gsl]
        y_off = jnp.dot(cg, st_old.astype(BF16), preferred_element_type=F32) * w2e[:, gsl]
        st_new = jnp.dot(bg.T.astype(BF16), xw[:, gsl], preferred_element_type=F32)
        state[:, gsl] = st_old * decay_row[:, gsl] + st_new
        for j in range(heads_per_group // 2):
            col = g * GROUP_W + j * LANES
            xpair = x_bf[:, col:col + LANES]
            zero = jnp.zeros_like(xpair)
            mhs = []
            for hh in range(2):
                hi = ho + g * heads_per_group + 2 * j + hh
                seg = acum[:, hi:hi + 1] - acum_t[hi:hi + 1, :]
                lm = jnp.exp2(jnp.where(causal, seg, NEG_INF))
                mhs.append((cb * lm * dt_t[hi:hi + 1, :]).astype(BF16))
            xsplit = jnp.concatenate([jnp.where(lane_lo, xpair, zero), jnp.where(lane_lo, zero, xpair)], axis=0)
            ybuf[:, col:col + LANES] = y_off[:, j * LANES:(j + 1) * LANES] + jnp.dot(
                jnp.concatenate(mhs, axis=1), xsplit, preferred_element_type=F32)
    if reverse:
        for g in range(SSM_GROUPS):
            gsl = slice(g * GROUP_W, (g + 1) * GROUP_W)
            yg = ybuf[:, gsl] + yf_ref[:, gsl] + dsk_ref[:, gsl] * xs[:, gsl]
            yg = yg * _silu(z_ref[:, gsl].astype(F32))
            ms = jnp.mean(yg * yg, axis=-1, keepdims=True)
            o_ref[:, gsl] = (yg * lax.rsqrt(ms + RMS_EPS) * nw_ref[:, gsl]).astype(BF16)

    @pl.when((flags & 16) != 0)
    def _():
        o_ref[...] = jnp.zeros_like(o_ref)


def _ssd_constants(reverse):
    ho = N_HEADS if reverse else 0
    e = np.zeros((LANES, D_SSM), np.float32)
    for h in range(N_HEADS):
        e[ho + h, h * HEAD_DIM:(h + 1) * HEAD_DIM] = 1.0
    i = np.arange(CHUNK)
    tri = (i[None, :] >= i[:, None]) if reverse else (i[:, None] >= i[None, :])
    return jnp.asarray(e, BF16), jnp.asarray(tri.astype(np.float32), BF16)


def _shift_matrices():
    s = np.zeros(((CONV_K - 1) * CHUNK, CHUNK + 2 * N_META), np.float32)
    i = np.arange(CHUNK)
    for n, k in enumerate(k for k in range(CONV_K) if k != CONV_K // 2):
        s[n * CHUNK + i, N_META + i + k - CONV_K // 2] = 1.0
    return jnp.asarray(s, BF16)


def _ssd_pass(lay, reverse, dt, dtb, arow, fwd_in=(), rev_in=()):
    n = dt.shape[0]
    blk, prev, nxt, flags = lay.ssd_tables(reverse)
    e, tri = _ssd_constants(reverse)
    cur = lambda w: pl.BlockSpec((CHUNK, w), lambda i, bk, pv, nx, fl: (bk[i], 0))
    common = [cur(LANES), _const_spec((1, LANES)), _const_spec((1, LANES)),
              _const_spec((LANES, D_SSM)), _const_spec((CHUNK, CHUNK))]
    state = pltpu.VMEM((SSM_STATE, D_SSM), F32)
    if reverse:
        u, y_f, z, dsk, nw = rev_in
        args = (u, dt, dtb, arow, e, tri, y_f, z, dsk, nw)
        in_specs = [cur(CONV_CH)] + common + [cur(D_SSM), cur(D_SSM), _const_spec((1, D_SSM)),
                                              _const_spec((1, D_SSM))]
        out_specs = cur(D_SSM)
        out_shape = jax.ShapeDtypeStruct((n, D_SSM), BF16)
        scratch = [state, pltpu.VMEM((CHUNK, D_SSM), F32)]
    else:
        xbc, cw = fwd_in
        shift = _shift_matrices()
        args = (xbc, xbc, xbc, shift, cw, dt, dtb, arow, e, tri)
        in_specs = [cur(CONV_CH),
                    pl.BlockSpec((N_META, CONV_CH), lambda i, bk, pv, nx, fl: (pv[i], 0)),
                    pl.BlockSpec((N_META, CONV_CH), lambda i, bk, pv, nx, fl: (nx[i], 0)),
                    _const_spec(shift.shape), _const_spec((8, CONV_CH))] + common
        out_specs = [cur(D_SSM), cur(CONV_CH)]
        out_shape = [jax.ShapeDtypeStruct((n, D_SSM), F32), jax.ShapeDtypeStruct((n, CONV_CH), BF16)]
        scratch = [pltpu.VMEM((CHUNK + 2 * N_META, CONV_CH), BF16), state]
    grid_spec = pltpu.PrefetchScalarGridSpec(
        num_scalar_prefetch=4, grid=(len(blk),), in_specs=in_specs,
        out_specs=out_specs, scratch_shapes=scratch)
    return pl.pallas_call(
        functools.partial(_ssd_kernel, reverse), grid_spec=grid_spec, out_shape=out_shape,
        compiler_params=_cparams(("arbitrary",)),
    )(jnp.asarray(blk), jnp.asarray(prev), jnp.asarray(nxt), jnp.asarray(flags), *args)


def _moe_plan(route_t):
    n = route_t.shape[1]
    flat_e = route_t[0:2].T.astype(jnp.int32).reshape(-1)
    onehot = (flat_e[:, None] == jnp.arange(N_EXPERTS, dtype=jnp.int32)[None, :]).astype(jnp.int32)
    csum = jnp.cumsum(onehot, axis=0)
    counts = csum[-1]
    rank = jnp.sum(csum * onehot, axis=1) - 1
    padded = (counts + TME - 1) // TME * TME
    pends = jnp.cumsum(padded)
    pstarts = pends - padded
    dest = jnp.sum(onehot * pstarts[None, :], axis=1) + rank
    n_rows = _round_up(2 * n + N_EXPERTS * TME, TME)
    n_blocks = n_rows // TME
    block_e = jnp.sum((jnp.arange(n_blocks, dtype=jnp.int32)[:, None] * TME >= pends[None, :]).astype(jnp.int32),
                      axis=1)
    block_e = jnp.minimum(block_e, N_EXPERTS - 1)
    n_used = (pends[-1] // TME).astype(jnp.int32).reshape(1)
    tail = jnp.minimum(pends[-1] + jnp.arange(N_PAD_RANGES - N_EXPERTS + 1, dtype=jnp.int32) * TME, n_rows)
    pad_rows = jnp.concatenate([pstarts + counts, tail[:-1], pends, tail[1:]]).astype(jnp.int32)
    return dest.astype(jnp.int32), block_e, n_used, pad_rows, n_rows


def _row_tile(r):
    return pl.multiple_of(r * ROW_TILE, ROW_TILE)


def _dispatch_kernel(pad_ref, dest_hbm, h_ref, xs_hbm, idx, stage, ztile, sem_i, sem, sem_z):
    @pl.when(pl.program_id(0) == 0)
    def _():
        ztile[...] = jnp.zeros_like(ztile)

        def zero_copy(r):
            return pltpu.make_async_copy(ztile, xs_hbm.at[pl.ds(_row_tile(r), ROW_TILE)], sem_z)

        for e in range(N_PAD_RANGES):
            lo, hi = pad_ref[e], pad_ref[N_PAD_RANGES + e]

            def fill(r, c):
                zero_copy(r).start()
                return c

            def drain_fill(r, c):
                zero_copy(0).wait()
                return c

            lax.fori_loop(lo, hi, fill, 0)
            lax.fori_loop(lo, hi, drain_fill, 0)

    cp = pltpu.make_async_copy(dest_hbm.at[pl.program_id(0)], idx, sem_i)
    cp.start()
    hf = h_ref[...].astype(F32)
    for j in range(ROW_TILE):
        stage[pl.ds(j, TM, stride=ROW_TILE), :] = hf[:, j * LANES:(j + 1) * LANES]
    cp.wait()

    def row_copy(src_row, dst_row):
        return pltpu.make_async_copy(stage.at[pl.ds(_row_tile(src_row), ROW_TILE)],
                                     xs_hbm.at[pl.ds(_row_tile(dst_row), ROW_TILE)], sem)

    def issue(b, c):
        for u in range(DMA_UNROLL):
            r = b * DMA_UNROLL + u
            row_copy(lax.shift_right_logical(r, 1), idx[r]).start(priority=u % 2)
        return c

    lax.fori_loop(0, RB // DMA_UNROLL, issue, 0)

    for _ in range(RB // TM):
        pltpu.make_async_copy(stage, xs_hbm.at[pl.ds(0, TM * ROW_TILE)], sem).wait()


def _dispatch(h, dest, pad_rows, n_rows):
    n = h.shape[0]
    steps = n // TM
    grid_spec = pltpu.PrefetchScalarGridSpec(
        num_scalar_prefetch=1, grid=(steps,),
        in_specs=[pl.BlockSpec(memory_space=pl.ANY), pl.BlockSpec((TM, D_MODEL), lambda i, pad: (i, 0))],
        out_specs=pl.BlockSpec(memory_space=pl.ANY),
        scratch_shapes=[pltpu.SMEM((RB,), jnp.int32), pltpu.VMEM((TM * ROW_TILE, LANES), F32),
                        pltpu.VMEM((ROW_TILE, LANES), F32),
                        pltpu.SemaphoreType.DMA, pltpu.SemaphoreType.DMA, pltpu.SemaphoreType.DMA])
    return pl.pallas_call(
        _dispatch_kernel, grid_spec=grid_spec,
        out_shape=jax.ShapeDtypeStruct((n_rows * ROW_TILE, LANES), F32),
        compiler_params=_cparams(("arbitrary",)),
    )(pad_rows, dest.reshape(steps, RB), h)


def _expert_kernel(be_ref, nu_ref, x_ref, wg_ref, wu_ref, wd_ref, o_ref, xbf, act):
    del be_ref
    used = pl.program_id(0) < nu_ref[0]

    @pl.when(used)
    def _():
        for j in range(ROW_TILE):
            xbf[:, j * LANES:(j + 1) * LANES] = x_ref[pl.ds(j, TME, stride=ROW_TILE), :].astype(BF16)
        x = xbf[...]
        for c in range(D_FF_EXPERT // FF_CHUNK):
            sl = slice(c * FF_CHUNK, (c + 1) * FF_CHUNK)
            g = jnp.dot(x, wg_ref[:, sl], preferred_element_type=F32)
            u = jnp.dot(x, wu_ref[:, sl], preferred_element_type=F32)
            act[:, sl] = (_silu(g) * u).astype(BF16)
        out = jnp.dot(act[...], wd_ref[...], preferred_element_type=F32)
        for j in range(ROW_TILE):
            o_ref[pl.ds(j, TME, stride=ROW_TILE), :] = out[:, j * LANES:(j + 1) * LANES]

    @pl.when(jnp.logical_not(used))
    def _():
        o_ref[...] = jnp.zeros_like(o_ref)


def _experts(xs, block_e, n_used, wg, wu, wd):
    n_blocks = xs.shape[0] // (TME * ROW_TILE)
    rows = pl.BlockSpec((TME * ROW_TILE, LANES), lambda i, be, nu: (i, 0))
    resident = lambda shape: pl.BlockSpec((None,) + shape, lambda i, be, nu: (be[i], 0, 0),
                                          pipeline_mode=pl.Buffered(1))
    grid_spec = pltpu.PrefetchScalarGridSpec(
        num_scalar_prefetch=2, grid=(n_blocks,),
        in_specs=[pl.BlockSpec((TME * ROW_TILE, LANES), lambda i, be, nu: (jnp.minimum(i, nu[0] - 1), 0)),
                  resident((D_MODEL, D_FF_EXPERT)), resident((D_MODEL, D_FF_EXPERT)),
                  resident((D_FF_EXPERT, D_MODEL))],
        out_specs=rows,
        scratch_shapes=[pltpu.VMEM((TME, D_MODEL), BF16), pltpu.VMEM((TME, D_FF_EXPERT), BF16)])
    return pl.pallas_call(
        _expert_kernel, grid_spec=grid_spec,
        out_shape=jax.ShapeDtypeStruct(xs.shape, F32),
        compiler_params=_cparams(("arbitrary",)),
    )(block_e, n_used, xs, wg, wu, wd)


def _combine_kernel(split, dest_hbm, eo_hbm, x_ref, route_ref, nw_ref, out_a, out_b, idx, gbuf, xn, sem_i, sem_g):
    i = pl.program_id(0)
    n = pl.num_programs(0)

    def idx_copy(step, slot):
        return pltpu.make_async_copy(dest_hbm.at[step], idx.at[slot], sem_i.at[slot])

    def row_copy(slot, src_row, k, t):
        return pltpu.make_async_copy(eo_hbm.at[pl.ds(_row_tile(src_row), ROW_TILE)],
                                     gbuf.at[slot, k, pl.ds(_row_tile(t), ROW_TILE)], sem_g.at[slot])

    def issue_rows(slot):
        def issue(b, c):
            for u in range(DMA_UNROLL):
                r = b * DMA_UNROLL + u
                row_copy(slot, idx[slot, r], u % 2, b * (DMA_UNROLL // 2) + u // 2).start(priority=u % 2)
            return c

        lax.fori_loop(0, 2 * TC // DMA_UNROLL, issue, 0)

    @pl.when(i == 0)
    def _():
        idx_copy(0, 0).start()
        idx_copy(0, 0).wait()
        issue_rows(0)

        @pl.when(n > 1)
        def _():
            idx_copy(1, 1).start()

    for slot in range(2):
        @pl.when(jnp.logical_and(i % 2 == slot, i + 1 < n))
        def _(slot=slot):
            idx_copy(0, 1 - slot).wait()
            issue_rows(1 - slot)

            @pl.when(i + 2 < n)
            def _():
                idx_copy(i + 2, slot).start()

    w1 = route_ref[:, 0:1]
    w2 = route_ref[:, 1:2]
    for slot in range(2):
        @pl.when(i % 2 == slot)
        def _(slot=slot):
            for k in range(2):
                pltpu.make_async_copy(eo_hbm.at[pl.ds(0, TC * ROW_TILE)], gbuf.at[slot, k], sem_g.at[slot]).wait()
            for j in range(ROW_TILE):
                sl = slice(j * LANES, (j + 1) * LANES)
                xn[:, sl] = x_ref[:, sl] + (gbuf[slot, 0, pl.ds(j, TC, stride=ROW_TILE), :] * w1
                                            + gbuf[slot, 1, pl.ds(j, TC, stride=ROW_TILE), :] * w2)

    if split is None:
        out_a[...] = xn[...]
        out_b[...] = _rms_rows(xn[...], nw_ref[...]).astype(BF16)
    else:
        tiles_a, tiles_b = split

        @pl.when(i < tiles_a)
        def _():
            out_a[...] = xn[...]

        @pl.when(jnp.logical_and(i >= tiles_a, i < tiles_a + tiles_b))
        def _():
            out_b[...] = xn[...]


def _combine(eo, dest, x, route, nw_next, split_rows=None):
    n = x.shape[0]
    steps = n // TC
    row = lambda w: pl.BlockSpec((TC, w), lambda i: (i, 0))
    if split_rows is None:
        split = None
        out_specs = [row(D_MODEL), row(D_MODEL)]
        out_shape = [jax.ShapeDtypeStruct((n, D_MODEL), F32), jax.ShapeDtypeStruct((n, D_MODEL), BF16)]
    else:
        ta, tb = split = tuple(r // TC for r in split_rows)
        out_specs = [pl.BlockSpec((TC, D_MODEL), lambda i: (jnp.minimum(i, ta - 1), 0)),
                     pl.BlockSpec((TC, D_MODEL), lambda i: (jnp.clip(i - ta, 0, tb - 1), 0))]
        out_shape = [jax.ShapeDtypeStruct((r, D_MODEL), F32) for r in split_rows]
    return pl.pallas_call(
        functools.partial(_combine_kernel, split),
        grid=(steps,),
        in_specs=[pl.BlockSpec(memory_space=pl.ANY), pl.BlockSpec(memory_space=pl.ANY),
                  row(D_MODEL), row(2), _const_spec((1, D_MODEL))],
        out_specs=out_specs, out_shape=out_shape,
        scratch_shapes=[pltpu.SMEM((2, 2 * TC), jnp.int32), pltpu.VMEM((2, 2, TC * ROW_TILE, LANES), F32),
                        pltpu.VMEM((TC, D_MODEL), F32),
                        pltpu.SemaphoreType.DMA((2,)), pltpu.SemaphoreType.DMA((2,))],
        compiler_params=_cparams(("arbitrary",)),
    )(dest.reshape(steps, 2 * TC), eo, x, route, nw_next)


def _trunk(lay, x, h, p, group_rows):
    depth = p['norm1_w'].shape[0]
    row = lambda v: v.reshape(1, -1).astype(F32)
    for l in range(depth):
        w_in = p['w_in'][l]
        wqk = w_in[:, 0:2 * ATT_W].astype(BF16)
        wvzx = w_in[:, 2 * ATT_W:3 * ATT_W + D_SSM + CONV_CH].astype(BF16)
        wdt = jnp.pad(w_in[:, 3 * ATT_W + D_SSM + CONV_CH:], ((0, 0), (0, LANES - 2 * N_HEADS))).astype(BF16)
        nw_qk = jnp.concatenate([jnp.tile(p['q_norm_w'][l].astype(F32), N_HEADS) * (HEAD_DIM ** -0.5),
                                 jnp.tile(p['k_norm_w'][l].astype(F32), N_HEADS)]).reshape(1, -1)
        qk, v, z, xbc, dt_raw = _inproj(h, wqk, wvzx, wdt, nw_qk)

        attn = _attention(lay, qk, v, p['rpb'][l])

        cw = jnp.concatenate([p['conv_w'][l].astype(F32).T, p['conv_b'][l].astype(F32)[None],
                              jnp.zeros((8 - CONV_K - 1, CONV_CH), F32)], axis=0)
        pad = jnp.zeros((LANES - 2 * N_HEADS,), F32)
        dtb = jnp.concatenate([p['dt_bias'][l].astype(F32).reshape(-1), pad]).reshape(1, -1)
        arow = jnp.concatenate([-jnp.exp(p['a_log'][l].astype(F32)).reshape(-1), pad]).reshape(1, -1)
        y_f, u = _ssd_pass(lay, False, dt_raw, dtb, arow, fwd_in=(xbc, cw))
        dsk = jnp.repeat(p['d_skip'][l].astype(F32), HEAD_DIM).reshape(1, -1)
        y = _ssd_pass(lay, True, dt_raw, dtb, arow, rev_in=(u, y_f, z, dsk, row(p['ssm_norm_w'][l])))

        w_out = p['w_out'][l].astype(BF16)
        nw_next = row(p['norm1_w'][l + 1]) if l + 1 < depth else jnp.ones((1, D_MODEL), F32)
        j = l // 2
        if l % 2 == 0:
            x, h2 = _outproj(attn, y, x, w_out, row(p['attn_out_norm_w'][l]), row(p['norm2_w'][l]))
            x, h = _ffn(h2, x, p['ffn_w_gate'][j].astype(BF16), p['ffn_w_up'][j].astype(BF16),
                        p['ffn_w_down'][j].astype(BF16), nw_next)
        else:
            wr = jnp.pad(p['moe_router'][j].astype(F32).T, ((0, ROUTER_ROWS - N_EXPERTS), (0, 0)))
            x, h2, route_t = _outproj(attn, y, x, w_out, row(p['attn_out_norm_w'][l]), row(p['norm2_w'][l]),
                                      router=jnp.concatenate(_split_bf16(wr), axis=0))
            dest, block_e, n_used, pad_rows, n_rows = _moe_plan(route_t)
            xs = _dispatch(h2, dest, pad_rows, n_rows)
            eo = _experts(xs, block_e, n_used, p['moe_w_gate'][j].astype(BF16),
                          p['moe_w_up'][j].astype(BF16), p['moe_w_down'][j].astype(BF16))
            if l + 1 == depth and len(group_rows) == 2:
                return _combine(eo, dest, x, route_t[2:4].T, nw_next, split_rows=group_rows)
            x, h = _combine(eo, dest, x, route_t[2:4].T, nw_next)
    offs = np.concatenate([[0], np.cumsum(group_rows)])
    return [x[offs[g]:offs[g + 1]] for g in range(len(group_rows))]


def kernel(x_prompt, x_sample, meta_tokens, norm1_w, w_in, q_norm_w, k_norm_w, rpb, attn_out_norm_w,
           conv_w, conv_b, dt_bias, a_log, d_skip, ssm_norm_w, w_out, norm2_w,
           ffn_w_gate, ffn_w_up, ffn_w_down, moe_router, moe_w_gate, moe_w_up, moe_w_down):
    p = dict(norm1_w=norm1_w, w_in=w_in, q_norm_w=q_norm_w, k_norm_w=k_norm_w, rpb=rpb,
             attn_out_norm_w=attn_out_norm_w, conv_w=conv_w, conv_b=conv_b, dt_bias=dt_bias, a_log=a_log,
             d_skip=d_skip, ssm_norm_w=ssm_norm_w, w_out=w_out, norm2_w=norm2_w, ffn_w_gate=ffn_w_gate,
             ffn_w_up=ffn_w_up, ffn_w_down=ffn_w_down, moe_router=moe_router, moe_w_gate=moe_w_gate,
             moe_w_up=moe_w_up, moe_w_down=moe_w_down)
    groups = (x_prompt, x_sample)
    lay = _Layout([g.shape[1] for g in groups for _ in range(g.shape[0])])
    meta_block = jnp.concatenate([jnp.zeros((META_OFF, D_MODEL), F32), meta_tokens.astype(F32)], axis=0)
    tail = lay.n_tok - lay.n_grid - CHUNK * lay.n_seq
    tail_rows = jnp.concatenate([jnp.tile(meta_block, (lay.n_seq, 1)), jnp.zeros((tail, D_MODEL), F32)], axis=0)
    x, h = _embed([g.reshape(-1, D_MODEL) for g in groups] + [tail_rows], norm1_w[0])
    flat = _trunk(lay, x, h, p, [g.shape[0] * g.shape[1] for g in groups])
    return tuple(f.reshape(g.shape).astype(g.dtype) for f, g in zip(flat, groups))
```

```python
import functools

import numpy as np
import jax
import jax.numpy as jnp
from jax import lax
from jax.experimental import pallas as pl
from jax.experimental.pallas import tpu as pltpu

F32 = jnp.float32
BF16 = jnp.bfloat16

D_MODEL = 1024
N_META = 16
GRID_W = 64
NA_ROWS = 8
NA_COLS = 16
N_HEADS = 16
HEAD_DIM = 64
N_PAIRS = N_HEADS // 2
UNIT_HEADS = 4
UNIT_W = UNIT_HEADS * HEAD_DIM
N_UNITS = N_HEADS // UNIT_HEADS
ATT_W = N_HEADS * HEAD_DIM
D_SSM = 1024
SSM_GROUPS = 2
SSM_STATE = 128
GROUP_W = D_SSM // SSM_GROUPS
CONV_K = 5
CONV_CH = D_SSM + 2 * SSM_GROUPS * SSM_STATE
D_FF = 2816
N_EXPERTS = 8
D_FF_EXPERT = 3584
RMS_EPS = 1e-6
NEG_INF = -1e30
LOG2E = 1.4426950408889634

LANES = 128
ROW_TILE = D_MODEL // LANES
CHUNK = 128
META_OFF = CHUNK - N_META
TM = 512
BAND_ROWS = 64
BAND = BAND_ROWS * GRID_W
HALO_ROWS = NA_ROWS // 2
BUF_ROWS = BAND_ROWS + NA_ROWS - 1
FF_CHUNK = 256
TME = 512
ROUTER_ROWS = 16
N_PAD_RANGES = 2 * N_EXPERTS + 1
RB = 2 * TM
TC = 512
DMA_UNROLL = 8
VMEM_LIMIT = 56 * 1024 * 1024


def _cparams(sem):
    return pltpu.CompilerParams(dimension_semantics=sem, vmem_limit_bytes=VMEM_LIMIT)


def _round_up(a, b):
    return (a + b - 1) // b * b


def _const_spec(shape):
    nd = len(shape)
    return pl.BlockSpec(shape, lambda *_: (0,) * nd)


class _Layout:
    def __init__(self, seq_lens):
        self.seq_lens = tuple(seq_lens)
        self.n_seq = len(seq_lens)
        self.n_grid = sum(seq_lens)
        self.starts = np.concatenate([[0], np.cumsum(seq_lens)[:-1]]).astype(np.int64)
        self.n_tok = _round_up(self.n_grid + CHUNK * self.n_seq, TM)
        assert all(s % BAND == 0 and s // GRID_W >= NA_ROWS for s in seq_lens)

    def meta_block(self, s):
        return self.n_grid // CHUNK + s

    def attn_tables(self):
        prev, nxt, flags, mblk = [], [], [], []
        for s, (st, ln) in enumerate(zip(self.starts, self.seq_lens)):
            lo, hi = st // BAND, (st + ln) // BAND
            for b in range(lo, hi):
                per = BAND_ROWS // HALO_ROWS
                prev.append(max(b * per - 1, lo * per))
                nxt.append(min((b + 1) * per, hi * per - 1))
                flags.append((1 if b == lo else 0) | (2 if b == hi - 1 else 0))
                mblk.append((self.n_grid + CHUNK * s + META_OFF) // N_META)
        return [np.asarray(a, np.int32) for a in (prev, nxt, flags, mblk)]

    def ssd_tables(self, reverse):
        blk, prev, nxt, flags = [], [], [], []
        for s, (st, ln) in enumerate(zip(self.starts, self.seq_lens)):
            nc = ln // CHUNK
            b0 = st // CHUNK
            meta16 = (self.n_grid + CHUNK * s + META_OFF) // N_META
            steps = []
            steps.append((self.meta_block(s), 0, b0 * 8, 2 | 4))
            for c in range(nc):
                b = b0 + c
                p16 = meta16 if c == 0 else b * 8 - 1
                n16 = (b + 1) * 8 if c < nc - 1 else 0
                steps.append((b, p16, n16, 1 | (2 if c < nc - 1 else 0)))
            if reverse:
                steps = steps[::-1]
            for i, (b, p, n, f) in enumerate(steps):
                blk.append(b); prev.append(p); nxt.append(n)
                flags.append(f | (8 if i == 0 else 0))
        for b in range(self.n_grid // CHUNK + self.n_seq, self.n_tok // CHUNK):
            blk.append(b); prev.append(0); nxt.append(0)
            flags.append(8 | 16)
        return [np.asarray(a, np.int32) for a in (blk, prev, nxt, flags)]


def _rms_rows(x, w):
    ms = jnp.mean(x * x, axis=-1, keepdims=True)
    return x * lax.rsqrt(ms + RMS_EPS) * w


def _embed_kernel(bounds, *refs):
    srcs, (w_ref, x_ref, h_ref) = refs[:len(bounds)], refs[len(bounds):]
    i = pl.program_id(0)
    lo = 0
    for src, hi in zip(srcs, bounds):
        @pl.when(jnp.logical_and(i >= lo, i < hi))
        def _(src=src):
            x = src[...].astype(F32)
            x_ref[...] = x
            h_ref[...] = _rms_rows(x, w_ref[...]).astype(BF16)
        lo = hi


def _embed(parts, w):
    tiles = [a.shape[0] // TM for a in parts]
    assert all(a.shape[0] % TM == 0 for a in parts)
    bounds = tuple(int(b) for b in np.cumsum(tiles))
    starts = (0,) + bounds[:-1]
    n = bounds[-1] * TM
    src_spec = lambda s, t: pl.BlockSpec((TM, D_MODEL), lambda i: (jnp.clip(i - s, 0, t - 1), 0))
    row = pl.BlockSpec((TM, D_MODEL), lambda i: (i, 0))
    return pl.pallas_call(
        functools.partial(_embed_kernel, bounds),
        grid=(bounds[-1],),
        in_specs=[src_spec(s, t) for s, t in zip(starts, tiles)] + [_const_spec((1, D_MODEL))],
        out_specs=[row, row],
        out_shape=[jax.ShapeDtypeStruct((n, D_MODEL), F32), jax.ShapeDtypeStruct((n, D_MODEL), BF16)],
        compiler_params=_cparams(("arbitrary",)),
    )(*parts, w.reshape(1, D_MODEL))


def _inproj_kernel(h_ref, wqk_ref, wvzx_ref, wdt_ref, g_ref, nw_ref,
                   qk_ref, v_ref, z_ref, xbc_ref, dt_ref):
    h = h_ref[...]
    g = g_ref[...]
    n_qk = 2 * ATT_W // FF_CHUNK
    col = lambda c: slice(c * FF_CHUNK, (c + 1) * FF_CHUNK)
    y_next = jnp.dot(h, wqk_ref[:, col(0)], preferred_element_type=F32)
    for c in range(n_qk):
        y = y_next
        if c + 1 < n_qk:
            y_next = jnp.dot(h, wqk_ref[:, col(c + 1)], preferred_element_type=F32)
        ss = jnp.dot((y * y).astype(BF16), g, preferred_element_type=F32)
        inv = lax.rsqrt(ss * (1.0 / HEAD_DIM) + RMS_EPS)
        qk_ref[:, col(c)] = (y * inv * nw_ref[:, col(c)]).astype(BF16)
    for c in range(ATT_W // FF_CHUNK):
        sl = slice(c * FF_CHUNK, (c + 1) * FF_CHUNK)
        v_ref[:, sl] = jnp.dot(h, wvzx_ref[:, sl], preferred_element_type=F32).astype(BF16)
    for c in range(D_SSM // FF_CHUNK):
        sl = slice(c * FF_CHUNK, (c + 1) * FF_CHUNK)
        src = slice(ATT_W + c * FF_CHUNK, ATT_W + (c + 1) * FF_CHUNK)
        z_ref[:, sl] = jnp.dot(h, wvzx_ref[:, src], preferred_element_type=F32).astype(BF16)
    for c in range(CONV_CH // FF_CHUNK):
        sl = slice(c * FF_CHUNK, (c + 1) * FF_CHUNK)
        src = slice(ATT_W + D_SSM + c * FF_CHUNK, ATT_W + D_SSM + (c + 1) * FF_CHUNK)
        xbc_ref[:, sl] = jnp.dot(h, wvzx_ref[:, src], preferred_element_type=F32).astype(BF16)
    dt_ref[...] = jnp.dot(h, wdt_ref[...], preferred_element_type=F32)


def _head_sum_matrix():
    i = np.arange(FF_CHUNK)
    return jnp.asarray((i[:, None] // HEAD_DIM == i[None, :] // HEAD_DIM).astype(np.float32), BF16)


def _inproj(h, wqk, wvzx, wdt, nw):
    n = h.shape[0]
    row = lambda w: pl.BlockSpec((TM, w), lambda i: (i, 0))
    return pl.pallas_call(
        _inproj_kernel,
        grid=(n // TM,),
        in_specs=[row(D_MODEL), _const_spec(wqk.shape), _const_spec(wvzx.shape), _const_spec(wdt.shape),
                  _const_spec((FF_CHUNK, FF_CHUNK)), _const_spec((1, 2 * ATT_W))],
        out_specs=[row(2 * ATT_W), row(ATT_W), row(D_SSM), row(CONV_CH), row(LANES)],
        out_shape=[jax.ShapeDtypeStruct((n, 2 * ATT_W), BF16), jax.ShapeDtypeStruct((n, ATT_W), BF16),
                   jax.ShapeDtypeStruct((n, D_SSM), BF16), jax.ShapeDtypeStruct((n, CONV_CH), BF16),
                   jax.ShapeDtypeStruct((n, LANES), F32)],
        compiler_params=_cparams(("parallel",)),
    )(h, wqk, wvzx, wdt, _head_sum_matrix(), nw)


def _split_bf16(x):
    hi = x.astype(BF16)
    lo = (x - hi.astype(F32)).astype(BF16)
    return hi, lo


def _outproj_kernel(with_router, grid_tiles, attn_ref, attn_tail_ref, y_ref, x_ref, w_ref, aw_ref, nw_ref, *rest):
    if with_router:
        wr_ref, xo_ref, h_ref, route_ref = rest
    else:
        xo_ref, h_ref = rest
    acc = x_ref[...] + jnp.dot(y_ref[...], w_ref[ATT_W:ATT_W + D_SSM, :], preferred_element_type=F32)
    attn = jnp.where(pl.program_id(0) < grid_tiles, attn_ref[...], attn_tail_ref[...])
    a = _rms_rows(attn.astype(F32), aw_ref[...]).astype(BF16)
    acc = acc + jnp.dot(a, w_ref[0:ATT_W, :], preferred_element_type=F32)
    xo_ref[...] = acc
    h2 = _rms_rows(acc, nw_ref[...])
    h_ref[...] = h2.astype(BF16)
    if with_router:
        hh, hl = _split_bf16(h2)
        wr = wr_ref[...]
        lg = _nt_dot(wr, hh)
        logits = lg[0:ROUTER_ROWS] + lg[ROUTER_ROWS:] + _nt_dot(wr[0:ROUTER_ROWS], hl)
        sub = lax.broadcasted_iota(jnp.int32, logits.shape, 0)
        logits = jnp.where(sub < N_EXPERTS, logits, NEG_INF)
        m1 = jnp.max(logits, axis=0, keepdims=True)
        i1 = jnp.min(jnp.where(logits == m1, sub, ROUTER_ROWS), axis=0, keepdims=True)
        rest_l = jnp.where(sub == i1, NEG_INF, logits)
        m2 = jnp.max(rest_l, axis=0, keepdims=True)
        i2 = jnp.min(jnp.where(rest_l == m2, sub, ROUTER_ROWS), axis=0, keepdims=True)
        e = jnp.exp(m2 - m1)
        w1 = 1.0 / (1.0 + e)
        route_ref[...] = jnp.concatenate(
            [i1.astype(F32), i2.astype(F32), w1, e * w1, jnp.zeros((4, logits.shape[1]), F32)], axis=0)


def _outproj(attn, y, x, w_out, aw, nw, router=None):
    n = x.shape[0]
    row = lambda w: pl.BlockSpec((TM, w), lambda i: (i, 0))
    tg, tt = attn[0].shape[0] // TM, attn[1].shape[0] // TM
    in_specs = [pl.BlockSpec((TM, ATT_W), lambda i: (jnp.minimum(i, tg - 1), 0)),
                pl.BlockSpec((TM, ATT_W), lambda i: (jnp.clip(i - tg, 0, tt - 1), 0)),
                row(D_SSM), row(D_MODEL), _const_spec(w_out.shape),
                _const_spec((1, ATT_W)), _const_spec((1, D_MODEL))]
    out_specs = [row(D_MODEL), row(D_MODEL)]
    out_shape = [jax.ShapeDtypeStruct((n, D_MODEL), F32), jax.ShapeDtypeStruct((n, D_MODEL), BF16)]
    args = [attn[0], attn[1], y, x, w_out, aw, nw]
    if router is not None:
        in_specs.append(_const_spec((2 * ROUTER_ROWS, D_MODEL)))
        out_specs.append(pl.BlockSpec((8, TM), lambda i: (0, i)))
        out_shape.append(jax.ShapeDtypeStruct((8, n), F32))
        args.append(router)
    return pl.pallas_call(
        functools.partial(_outproj_kernel, router is not None, tg),
        grid=(n // TM,), in_specs=in_specs, out_specs=out_specs, out_shape=out_shape,
        compiler_params=_cparams(("parallel",)),
    )(*args)


def _silu(x):
    return x * (1.0 / (1.0 + jnp.exp(-x)))


def _ffn_kernel(h_ref, x_ref, wg_ref, wu_ref, wd_ref, nw_ref, xo_ref, ho_ref, act_ref):
    h = h_ref[...]
    for c in range(D_FF // FF_CHUNK):
        sl = slice(c * FF_CHUNK, (c + 1) * FF_CHUNK)
        g = jnp.dot(h, wg_ref[:, sl], preferred_element_type=F32)
        u = jnp.dot(h, wu_ref[:, sl], preferred_element_type=F32)
        act_ref[:, sl] = (_silu(g) * u).astype(BF16)
    xn = x_ref[...] + jnp.dot(act_ref[...], wd_ref[...], preferred_element_type=F32)
    xo_ref[...] = xn
    ho_ref[...] = _rms_rows(xn, nw_ref[...]).astype(BF16)


def _ffn(h, x, wg, wu, wd, nw_next):
    n = x.shape[0]
    row = lambda w: pl.BlockSpec((TM, w), lambda i: (i, 0))
    return pl.pallas_call(
        _ffn_kernel,
        grid=(n // TM,),
        in_specs=[row(D_MODEL), row(D_MODEL), _const_spec(wg.shape), _const_spec(wu.shape),
                  _const_spec(wd.shape), _const_spec((1, D_MODEL))],
        out_specs=[row(D_MODEL), row(D_MODEL)],
        out_shape=[jax.ShapeDtypeStruct((n, D_MODEL), F32), jax.ShapeDtypeStruct((n, D_MODEL), BF16)],
        scratch_shapes=[pltpu.VMEM((TM, D_FF), BF16)],
        compiler_params=_cparams(("parallel",)),
    )(h, x, wg, wu, wd, nw_next)


N_CB = GRID_W // NA_COLS
SPAN_START = (0, 0, 16, 32)
SPAN_W = (32, 48, 48, 32)
META_SLOT = (24, 0, 0, 0)
META_PER_ROW = N_META // 2
KTILE = 16
ATTN_DEPTH = 16


def _attn_bias_tables(rpb):
    i = np.arange(NA_ROWS)
    dr = np.clip(i[None, :] - i[:, None] + NA_ROWS - 1, 0, 2 * NA_ROWS - 2)
    tables = {}
    for j in range(N_CB):
        span = SPAN_W[j]
        qc = NA_COLS * j + np.arange(NA_COLS)
        kc = SPAN_START[j] + np.arange(span)
        st = np.clip(qc - NA_COLS // 2, 0, GRID_W - NA_COLS)
        valid = (kc[None, :] >= st[:, None]) & (kc[None, :] < st[:, None] + NA_COLS)
        dc = np.clip(kc[None, :] - qc[:, None] + NA_COLS - 1, 0, 2 * NA_COLS - 2)
        slot = (np.arange(span) >= META_SLOT[j]) & (np.arange(span) < META_SLOT[j] + META_PER_ROW)
        assert not valid[:, slot].any()
        b = rpb.astype(F32)[:, dr][:, :, :, dc]
        b = jnp.where(valid[None, None, None], b, NEG_INF)
        meta_vis = np.where(i[:, None] < 2, 0.0, NEG_INF) * np.ones((1, span))
        b = jnp.where(slot[None, None, None, None, :], jnp.asarray(meta_vis, F32)[None, None, :, None, :], b)
        b = b.reshape(N_UNITS, UNIT_HEADS, NA_ROWS, NA_ROWS, NA_COLS, span)
        b = jnp.transpose(b, (0, 2, 1, 4, 3, 5)).reshape(N_UNITS, NA_ROWS, UNIT_HEADS * NA_COLS, NA_ROWS * span)
        tables.setdefault(span, []).append(b)
    return [jnp.stack(t, axis=1) for _, t in sorted(tables.items())]


def _pair_queries(q):
    lo = lax.broadcasted_iota(jnp.int32, q.shape, 1) < HEAD_DIM
    zero = jnp.zeros_like(q)
    return jnp.concatenate([jnp.where(lo, q, zero), jnp.where(lo, zero, q)], axis=0)


def _head_queries(q):
    head = lax.broadcasted_iota(jnp.int32, q.shape, 1) // HEAD_DIM
    zero = jnp.zeros_like(q)
    return jnp.concatenate([jnp.where(head == h, q, zero) for h in range(UNIT_HEADS)], axis=0)


def _nt_dot(a, b):
    return lax.dot_general(a, b, (((1,), (1,)), ((), ())), preferred_element_type=F32)


def _attn_kernel(prev_ref, next_ref, flag_ref, mblk_ref,
                 q_ref, kp_ref, kc_ref, kn_ref, vp_ref, vc_ref, vn_ref, km_ref, vm_ref,
                 b32_ref, b48_ref, o_ref, *spans):
    del prev_ref, next_ref, mblk_ref
    flags = flag_ref[pl.program_id(1)]
    is_first = (flags & 1) != 0
    is_last = (flags & 2) != 0
    low = lax.broadcasted_iota(jnp.int32, (KTILE, UNIT_W), 0) < META_PER_ROW

    for refs, bufs, m_ref in (((kp_ref, kc_ref, kn_ref), spans[:N_CB], km_ref),
                              ((vp_ref, vc_ref, vn_ref), spans[N_CB:], vm_ref)):
        m = m_ref[...]
        mf = m.astype(F32)
        m_swapped = jnp.concatenate([mf[META_PER_ROW:], mf[:META_PER_ROW]], axis=0).astype(BF16)
        for rr in range(BUF_ROWS):
            if rr < HALO_ROWS:
                ref, row = refs[0], rr
            elif rr < HALO_ROWS + BAND_ROWS:
                ref, row = refs[1], rr - HALO_ROWS
            else:
                ref, row = refs[2], rr - HALO_ROWS - BAND_ROWS
            for j in range(N_CB):
                span = SPAN_W[j]
                for t in range(span // KTILE):
                    off = row * GRID_W + SPAN_START[j] + t * KTILE
                    tile = ref[off:off + KTILE, :]
                    if t == META_SLOT[j] // KTILE:
                        if META_SLOT[j] % KTILE == 0:
                            tile = jnp.where(low, m if rr % 2 == 0 else m_swapped, tile)
                        else:
                            tile = jnp.where(low, tile, m_swapped if rr % 2 == 0 else m)
                    bufs[j][rr * span + t * KTILE:rr * span + (t + 1) * KTILE, :] = tile

    lane_head = lax.broadcasted_iota(jnp.int32, (NA_COLS, UNIT_W), 1) // HEAD_DIM
    starts = []
    for ri in range(BAND_ROWS):
        ls = jnp.int32(ri)
        ls = jnp.where(is_first, jnp.maximum(ls, HALO_ROWS), ls)
        ls = jnp.where(is_last, jnp.minimum(ls, BAND_ROWS - HALO_ROWS), ls)
        starts.append((ls, HALO_ROWS + ri - ls))

    def scores(u):
        ri, j = divmod(u, N_CB)
        ls, delta = starts[ri]
        span = SPAN_W[j]
        kwin = spans[j][pl.ds(pl.multiple_of(ls * span, KTILE), NA_ROWS * span), :]
        q0 = ri * GRID_W + j * NA_COLS
        bias_ref = b32_ref if span == SPAN_W[0] else b48_ref
        return _nt_dot(_head_queries(q_ref[q0:q0 + NA_COLS, :]), kwin) + bias_ref[SPAN_W[:j].count(span), delta]

    def finish(u, s):
        ri, j = divmod(u, N_CB)
        span = SPAN_W[j]
        vwin = spans[N_CB + j][pl.ds(pl.multiple_of(starts[ri][0] * span, KTILE), NA_ROWS * span), :]
        p = jnp.exp(s - jnp.max(s, axis=-1, keepdims=True))
        l = jnp.sum(p, axis=-1, keepdims=True)
        o = jnp.dot(p.astype(BF16), vwin, preferred_element_type=F32) / l
        q0 = ri * GRID_W + j * NA_COLS
        out = o[:NA_COLS]
        for h in range(1, UNIT_HEADS):
            out = jnp.where(lane_head == h, o[h * NA_COLS:(h + 1) * NA_COLS], out)
        o_ref[q0:q0 + NA_COLS, :] = out.astype(BF16)

    n_units = BAND_ROWS * N_CB
    pending = {}
    for t in range(n_units + ATTN_DEPTH):
        if t < n_units:
            pending[t] = scores(t)
        if t >= ATTN_DEPTH:
            finish(t - ATTN_DEPTH, pending.pop(t - ATTN_DEPTH))


def _attn_meta_kernel(n_seq, qk_ref, v_ref, mbias_ref, o_ref):
    s = pl.program_id(0)
    o_ref[...] = jnp.zeros_like(o_ref)

    @pl.when(s < n_seq)
    def _():
        mbias = mbias_ref[...]
        for p in range(N_PAIRS):
            c = slice(p * LANES, (p + 1) * LANES)
            q = qk_ref[META_OFF:CHUNK, c]
            k = qk_ref[:, ATT_W + p * LANES:ATT_W + (p + 1) * LANES]
            qs = _pair_queries(q)
            sc = _nt_dot(qs, k) + mbias
            m = jnp.max(sc, axis=-1, keepdims=True)
            e = jnp.exp(sc - m)
            l = jnp.sum(e, axis=-1, keepdims=True)
            o = jnp.dot(e.astype(BF16), v_ref[:, c], preferred_element_type=F32) / l
            lo = lax.broadcasted_iota(jnp.int32, (N_META, LANES), 1) < HEAD_DIM
            o_ref[META_OFF:CHUNK, c] = jnp.where(lo, o[0:N_META], o[N_META:]).astype(BF16)


def _attention(lay, qk, v, rpb):
    n = qk.shape[0]
    prev, nxt, flags, mblk = lay.attn_tables()
    n_bands = lay.n_grid // BAND
    b32, b48 = _attn_bias_tables(rpb)
    lane = np.arange(LANES)
    mbias_meta = jnp.asarray(np.where(lane >= META_OFF, 0.0, NEG_INF)[None, :], F32)
    kcol = ATT_W // UNIT_W
    halo = HALO_ROWS * GRID_W
    bias_spec = lambda a: pl.BlockSpec((None,) + a.shape[1:], lambda p, b, *_: (p, 0, 0, 0, 0))
    grid_spec = pltpu.PrefetchScalarGridSpec(
        num_scalar_prefetch=4,
        grid=(N_UNITS, n_bands),
        in_specs=[
            pl.BlockSpec((BAND, UNIT_W), lambda p, b, *_: (b, p)),
            pl.BlockSpec((halo, UNIT_W), lambda p, b, pv, nx, fl, mb: (pv[b], kcol + p)),
            pl.BlockSpec((BAND, UNIT_W), lambda p, b, pv, nx, fl, mb: (b, kcol + p)),
            pl.BlockSpec((halo, UNIT_W), lambda p, b, pv, nx, fl, mb: (nx[b], kcol + p)),
            pl.BlockSpec((halo, UNIT_W), lambda p, b, pv, nx, fl, mb: (pv[b], p)),
            pl.BlockSpec((BAND, UNIT_W), lambda p, b, pv, nx, fl, mb: (b, p)),
            pl.BlockSpec((halo, UNIT_W), lambda p, b, pv, nx, fl, mb: (nx[b], p)),
            pl.BlockSpec((N_META, UNIT_W), lambda p, b, pv, nx, fl, mb: (mb[b], kcol + p)),
            pl.BlockSpec((N_META, UNIT_W), lambda p, b, pv, nx, fl, mb: (mb[b], p)),
            bias_spec(b32), bias_spec(b48),
        ],
        out_specs=pl.BlockSpec((BAND, UNIT_W), lambda p, b, *_: (b, p)),
        scratch_shapes=[pltpu.VMEM((BUF_ROWS * w, UNIT_W), BF16) for w in SPAN_W] * 2,
    )
    attn = pl.pallas_call(
        _attn_kernel, grid_spec=grid_spec,
        out_shape=jax.ShapeDtypeStruct((lay.n_grid, ATT_W), BF16),
        compiler_params=_cparams(("arbitrary", "arbitrary")),
    )(jnp.asarray(prev), jnp.asarray(nxt), jnp.asarray(flags), jnp.asarray(mblk),
      qk, qk, qk, qk, v, v, v, qk, v, b32, b48)
    mb0 = lay.n_grid // CHUNK
    n_tail = (n - lay.n_grid) // CHUNK
    attn_tail = pl.pallas_call(
        functools.partial(_attn_meta_kernel, lay.n_seq),
        grid=(n_tail,),
        in_specs=[pl.BlockSpec((CHUNK, 2 * ATT_W), lambda s: (mb0 + s, 0)),
                  pl.BlockSpec((CHUNK, ATT_W), lambda s: (mb0 + s, 0)),
                  _const_spec((1, LANES))],
        out_specs=pl.BlockSpec((CHUNK, ATT_W), lambda s: (s, 0)),
        out_shape=jax.ShapeDtypeStruct((n - lay.n_grid, ATT_W), BF16),
        compiler_params=_cparams(("arbitrary",)),
    )(qk, v, mbias_meta)
    return attn, attn_tail


def _softplus(x):
    return jnp.maximum(x, 0.0) + jnp.log(1.0 + jnp.exp(-jnp.abs(x)))


def _split3_bf16(x):
    hi = x.astype(BF16)
    r = x - hi.astype(F32)
    mid = r.astype(BF16)
    lo = (r - mid.astype(F32)).astype(BF16)
    return hi, mid, lo


def _expand_heads(w, e):
    hi, lo = _split_bf16(w)
    return jnp.dot(hi, e, preferred_element_type=F32) + jnp.dot(lo, e, preferred_element_type=F32)


def _ssd_kernel(reverse, blk_ref, prev_ref, next_ref, flag_ref, *refs):
    if reverse:
        (u_ref, dt_ref, dtb_ref, arow_ref, e_ref, tri_ref, yf_ref, z_ref, dsk_ref, nw_ref,
         o_ref, state, ybuf) = refs
    else:
        (xbc_ref, xp_ref, xn_ref, shift_ref, cw_ref, dt_ref, dtb_ref, arow_ref, e_ref, tri_ref,
         o_ref, u_ref, xe, state) = refs
        ybuf = o_ref
    del blk_ref, prev_ref, next_ref
    flags = flag_ref[pl.program_id(0)]
    is_meta = (flags & 4) != 0

    @pl.when((flags & 8) != 0)
    def _():
        state[...] = jnp.zeros_like(state)

    if reverse:
        u = u_ref[...].astype(F32)
    else:
        zero = jnp.zeros((N_META, CONV_CH), BF16)
        xe[0:N_META, :] = jnp.where((flags & 1) != 0, xp_ref[...], zero)
        xe[N_META:N_META + CHUNK, :] = xbc_ref[...]
        xe[N_META + CHUNK:, :] = jnp.where((flags & 2) != 0, xn_ref[...], zero)

        @pl.when(is_meta)
        def _():
            xe[0:N_META + META_OFF, :] = jnp.zeros((N_META + META_OFF, CONV_CH), BF16)

        shifted = jnp.dot(shift_ref[...], xe[...], preferred_element_type=F32)
        half = CONV_K // 2
        u = cw_ref[CONV_K:CONV_K + 1, :] + cw_ref[half:half + 1, :] * xe[N_META:N_META + CHUNK, :].astype(F32)
        for n, k in enumerate(k for k in range(CONV_K) if k != half):
            u = u + cw_ref[k:k + 1, :] * shifted[n * CHUNK:(n + 1) * CHUNK, :]
        u = _silu(u)
        u_ref[...] = u.astype(BF16)
    xs = u[:, 0:D_SSM]
    x_bf = xs.astype(BF16)

    rid = lax.broadcasted_iota(jnp.int32, (CHUNK, 1), 0)
    valid = rid >= jnp.where(is_meta, META_OFF, 0)
    dt = jnp.where(valid, _softplus(dt_ref[...] + dtb_ref[...]), 0.0)
    a = dt * arow_ref[...]
    tri = tri_ref[...]
    ah, am, al = _split3_bf16(a)
    acum = (jnp.dot(tri, ah, preferred_element_type=F32) + jnp.dot(tri, am, preferred_element_type=F32)
            + jnp.dot(tri, al, preferred_element_type=F32))
    acum = acum * LOG2E
    acum_t = acum.T
    dt_t = dt.T
    edge = 0 if reverse else CHUNK - 1
    a_tot = acum[edge:edge + 1, :]
    e = e_ref[...]
    w1e = jnp.dot((dt * jnp.exp2(a_tot - acum)).astype(BF16), e, preferred_element_type=F32)
    w2e = jnp.dot(jnp.exp2(acum).astype(BF16), e, preferred_element_type=F32)
    decay_row = _expand_heads(jnp.broadcast_to(jnp.exp2(a_tot), (8, LANES)), e)[0:1]
    xw = (xs * w1e).astype(BF16)

    li = lax.broadcasted_iota(jnp.int32, (CHUNK, CHUNK), 0)
    si = lax.broadcasted_iota(jnp.int32, (CHUNK, CHUNK), 1)
    causal = (si >= li) if reverse else (li >= si)
    lane_lo = si < HEAD_DIM
    ho = N_HEADS if reverse else 0
    heads_per_group = N_HEADS // SSM_GROUPS
    for g in range(SSM_GROUPS):
        gsl = slice(g * GROUP_W, (g + 1) * GROUP_W)
        bg = u[:, D_SSM + g * SSM_STATE:D_SSM + (g + 1) * SSM_STATE]
        cg = u[:, D_SSM + (SSM_GROUPS + g) * SSM_STATE:D_SSM + (SSM_GROUPS + g + 1) * SSM_STATE].astype(BF16)
        cb = _nt_dot(cg, bg.astype(BF16))
        st_old = state[:, gsl]
        y_off = jnp.dot(cg, st_old.astype(BF16), preferred_element_type=F32) * w2e[:, gsl]
        st_new = jnp.dot(bg.T.astype(BF16), xw[:, gsl], preferred_element_type=F32)
        state[:, gsl] = st_old * decay_row[:, gsl] + st_new
        for j in range(heads_per_group // 2):
            col = g * GROUP_W + j * LANES
            xpair = x_bf[:, col:col + LANES]
            zero = jnp.zeros_like(xpair)
            mhs = []
            for hh in range(2):
                hi = ho + g * heads_per_group + 2 * j + hh
                seg = acum[:, hi:hi + 1] - acum_t[hi:hi + 1, :]
                lm = jnp.exp2(jnp.where(causal, seg, NEG_INF))
                mhs.append((cb * lm * dt_t[hi:hi + 1, :]).astype(BF16))
            xsplit = jnp.concatenate([jnp.where(lane_lo, xpair, zero), jnp.where(lane_lo, zero, xpair)], axis=0)
            ybuf[:, col:col + LANES] = y_off[:, j * LANES:(j + 1) * LANES] + jnp.dot(
                jnp.concatenate(mhs, axis=1), xsplit, preferred_element_type=F32)
    if reverse:
        for g in range(SSM_GROUPS):
            gsl = slice(g * GROUP_W, (g + 1) * GROUP_W)
            yg = ybuf[:, gsl] + yf_ref[:, gsl] + dsk_ref[:, gsl] * xs[:, gsl]
            yg = yg * _silu(z_ref[:, gsl].astype(F32))
            ms = jnp.mean(yg * yg, axis=-1, keepdims=True)
            o_ref[:, gsl] = (yg * lax.rsqrt(ms + RMS_EPS) * nw_ref[:, gsl]).astype(BF16)

    @pl.when((flags & 16) != 0)
    def _():
        o_ref[...] = jnp.zeros_like(o_ref)


def _ssd_constants(reverse):
    ho = N_HEADS if reverse else 0
    e = np.zeros((LANES, D_SSM), np.float32)
    for h in range(N_HEADS):
        e[ho + h, h * HEAD_DIM:(h + 1) * HEAD_DIM] = 1.0
    i = np.arange(CHUNK)
    tri = (i[None, :] >= i[:, None]) if reverse else (i[:, None] >= i[None, :])
    return jnp.asarray(e, BF16), jnp.asarray(tri.astype(np.float32), BF16)


def _shift_matrices():
    s = np.zeros(((CONV_K - 1) * CHUNK, CHUNK + 2 * N_META), np.float32)
    i = np.arange(CHUNK)
    for n, k in enumerate(k for k in range(CONV_K) if k != CONV_K // 2):
        s[n * CHUNK + i, N_META + i + k - CONV_K // 2] = 1.0
    return jnp.asarray(s, BF16)


def _ssd_pass(lay, reverse, dt, dtb, arow, fwd_in=(), rev_in=()):
    n = dt.shape[0]
    blk, prev, nxt, flags = lay.ssd_tables(reverse)
    e, tri = _ssd_constants(reverse)
    cur = lambda w: pl.BlockSpec((CHUNK, w), lambda i, bk, pv, nx, fl: (bk[i], 0))
    common = [cur(LANES), _const_spec((1, LANES)), _const_spec((1, LANES)),
              _const_spec((LANES, D_SSM)), _const_spec((CHUNK, CHUNK))]
    state = pltpu.VMEM((SSM_STATE, D_SSM), F32)
    if reverse:
        u, y_f, z, dsk, nw = rev_in
        args = (u, dt, dtb, arow, e, tri, y_f, z, dsk, nw)
        in_specs = [cur(CONV_CH)] + common + [cur(D_SSM), cur(D_SSM), _const_spec((1, D_SSM)),
                                              _const_spec((1, D_SSM))]
        out_specs = cur(D_SSM)
        out_shape = jax.ShapeDtypeStruct((n, D_SSM), BF16)
        scratch = [state, pltpu.VMEM((CHUNK, D_SSM), F32)]
    else:
        xbc, cw = fwd_in
        shift = _shift_matrices()
        args = (xbc, xbc, xbc, shift, cw, dt, dtb, arow, e, tri)
        in_specs = [cur(CONV_CH),
                    pl.BlockSpec((N_META, CONV_CH), lambda i, bk, pv, nx, fl: (pv[i], 0)),
                    pl.BlockSpec((N_META, CONV_CH), lambda i, bk, pv, nx, fl: (nx[i], 0)),
                    _const_spec(shift.shape), _const_spec((8, CONV_CH))] + common
        out_specs = [cur(D_SSM), cur(CONV_CH)]
        out_shape = [jax.ShapeDtypeStruct((n, D_SSM), F32), jax.ShapeDtypeStruct((n, CONV_CH), BF16)]
        scratch = [pltpu.VMEM((CHUNK + 2 * N_META, CONV_CH), BF16), state]
    grid_spec = pltpu.PrefetchScalarGridSpec(
        num_scalar_prefetch=4, grid=(len(blk),), in_specs=in_specs,
        out_specs=out_specs, scratch_shapes=scratch)
    return pl.pallas_call(
        functools.partial(_ssd_kernel, reverse), grid_spec=grid_spec, out_shape=out_shape,
        compiler_params=_cparams(("arbitrary",)),
    )(jnp.asarray(blk), jnp.asarray(prev), jnp.asarray(nxt), jnp.asarray(flags), *args)


def _moe_plan(route_t):
    n = route_t.shape[1]
    flat_e = route_t[0:2].T.astype(jnp.int32).reshape(-1)
    onehot = (flat_e[:, None] == jnp.arange(N_EXPERTS, dtype=jnp.int32)[None, :]).astype(jnp.int32)
    csum = jnp.cumsum(onehot, axis=0)
    counts = csum[-1]
    rank = jnp.sum(csum * onehot, axis=1) - 1
    padded = (counts + TME - 1) // TME * TME
    pends = jnp.cumsum(padded)
    pstarts = pends - padded
    dest = jnp.sum(onehot * pstarts[None, :], axis=1) + rank
    n_rows = _round_up(2 * n + N_EXPERTS * TME, TME)
    n_blocks = n_rows // TME
    block_e = jnp.sum((jnp.arange(n_blocks, dtype=jnp.int32)[:, None] * TME >= pends[None, :]).astype(jnp.int32),
                      axis=1)
    block_e = jnp.minimum(block_e, N_EXPERTS - 1)
    n_used = (pends[-1] // TME).astype(jnp.int32).reshape(1)
    tail = jnp.minimum(pends[-1] + jnp.arange(N_PAD_RANGES - N_EXPERTS + 1, dtype=jnp.int32) * TME, n_rows)
    pad_rows = jnp.concatenate([pstarts + counts, tail[:-1], pends, tail[1:]]).astype(jnp.int32)
    return dest.astype(jnp.int32), block_e, n_used, pad_rows, n_rows


def _row_tile(r):
    return pl.multiple_of(r * ROW_TILE, ROW_TILE)


def _dispatch_kernel(pad_ref, dest_hbm, h_ref, xs_hbm, idx, stage, ztile, sem_i, sem, sem_z):
    @pl.when(pl.program_id(0) == 0)
    def _():
        ztile[...] = jnp.zeros_like(ztile)

        def zero_copy(r):
            return pltpu.make_async_copy(ztile, xs_hbm.at[pl.ds(_row_tile(r), ROW_TILE)], sem_z)

        for e in range(N_PAD_RANGES):
            lo, hi = pad_ref[e], pad_ref[N_PAD_RANGES + e]

            def fill(r, c):
                zero_copy(r).start()
                return c

            def drain_fill(r, c):
                zero_copy(0).wait()
                return c

            lax.fori_loop(lo, hi, fill, 0)
            lax.fori_loop(lo, hi, drain_fill, 0)

    cp = pltpu.make_async_copy(dest_hbm.at[pl.program_id(0)], idx, sem_i)
    cp.start()
    hf = h_ref[...].astype(F32)
    for j in range(ROW_TILE):
        stage[pl.ds(j, TM, stride=ROW_TILE), :] = hf[:, j * LANES:(j + 1) * LANES]
    cp.wait()

    def row_copy(src_row, dst_row):
        return pltpu.make_async_copy(stage.at[pl.ds(_row_tile(src_row), ROW_TILE)],
                                     xs_hbm.at[pl.ds(_row_tile(dst_row), ROW_TILE)], sem)

    def issue(b, c):
        for u in range(DMA_UNROLL):
            r = b * DMA_UNROLL + u
            row_copy(lax.shift_right_logical(r, 1), idx[r]).start(priority=u % 2)
        return c

    lax.fori_loop(0, RB // DMA_UNROLL, issue, 0)

    for _ in range(RB // TM):
        pltpu.make_async_copy(stage, xs_hbm.at[pl.ds(0, TM * ROW_TILE)], sem).wait()


def _dispatch(h, dest, pad_rows, n_rows):
    n = h.shape[0]
    steps = n // TM
    grid_spec = pltpu.PrefetchScalarGridSpec(
        num_scalar_prefetch=1, grid=(steps,),
        in_specs=[pl.BlockSpec(memory_space=pl.ANY), pl.BlockSpec((TM, D_MODEL), lambda i, pad: (i, 0))],
        out_specs=pl.BlockSpec(memory_space=pl.ANY),
        scratch_shapes=[pltpu.SMEM((RB,), jnp.int32), pltpu.VMEM((TM * ROW_TILE, LANES), F32),
                        pltpu.VMEM((ROW_TILE, LANES), F32),
                        pltpu.SemaphoreType.DMA, pltpu.SemaphoreType.DMA, pltpu.SemaphoreType.DMA])
    return pl.pallas_call(
        _dispatch_kernel, grid_spec=grid_spec,
        out_shape=jax.ShapeDtypeStruct((n_rows * ROW_TILE, LANES), F32),
        compiler_params=_cparams(("arbitrary",)),
    )(pad_rows, dest.reshape(steps, RB), h)


def _expert_kernel(be_ref, nu_ref, x_ref, wg_ref, wu_ref, wd_ref, o_ref, xbf, act):
    del be_ref
    used = pl.program_id(0) < nu_ref[0]

    @pl.when(used)
    def _():
        for j in range(ROW_TILE):
            xbf[:, j * LANES:(j + 1) * LANES] = x_ref[pl.ds(j, TME, stride=ROW_TILE), :].astype(BF16)
        x = xbf[...]
        for c in range(D_FF_EXPERT // FF_CHUNK):
            sl = slice(c * FF_CHUNK, (c + 1) * FF_CHUNK)
            g = jnp.dot(x, wg_ref[:, sl], preferred_element_type=F32)
            u = jnp.dot(x, wu_ref[:, sl], preferred_element_type=F32)
            act[:, sl] = (_silu(g) * u).astype(BF16)
        out = jnp.dot(act[...], wd_ref[...], preferred_element_type=F32)
        for j in range(ROW_TILE):
            o_ref[pl.ds(j, TME, stride=ROW_TILE), :] = out[:, j * LANES:(j + 1) * LANES]

    @pl.when(jnp.logical_not(used))
    def _():
        o_ref[...] = jnp.zeros_like(o_ref)


def _experts(xs, block_e, n_used, wg, wu, wd):
    n_blocks = xs.shape[0] // (TME * ROW_TILE)
    rows = pl.BlockSpec((TME * ROW_TILE, LANES), lambda i, be, nu: (i, 0))
    resident = lambda shape: pl.BlockSpec((None,) + shape, lambda i, be, nu: (be[i], 0, 0),
                                          pipeline_mode=pl.Buffered(1))
    grid_spec = pltpu.PrefetchScalarGridSpec(
        num_scalar_prefetch=2, grid=(n_blocks,),
        in_specs=[pl.BlockSpec((TME * ROW_TILE, LANES), lambda i, be, nu: (jnp.minimum(i, nu[0] - 1), 0)),
                  resident((D_MODEL, D_FF_EXPERT)), resident((D_MODEL, D_FF_EXPERT)),
                  resident((D_FF_EXPERT, D_MODEL))],
        out_specs=rows,
        scratch_shapes=[pltpu.VMEM((TME, D_MODEL), BF16), pltpu.VMEM((TME, D_FF_EXPERT), BF16)])
    return pl.pallas_call(
        _expert_kernel, grid_spec=grid_spec,
        out_shape=jax.ShapeDtypeStruct(xs.shape, F32),
        compiler_params=_cparams(("arbitrary",)),
    )(block_e, n_used, xs, wg, wu, wd)


def _combine_kernel(split, dest_hbm, eo_hbm, x_ref, route_ref, nw_ref, out_a, out_b, idx, gbuf, xn, sem_i, sem_g):
    i = pl.program_id(0)
    n = pl.num_programs(0)

    def idx_copy(step, slot):
        return pltpu.make_async_copy(dest_hbm.at[step], idx.at[slot], sem_i.at[slot])

    def row_copy(slot, src_row, k, t):
        return pltpu.make_async_copy(eo_hbm.at[pl.ds(_row_tile(src_row), ROW_TILE)],
                                     gbuf.at[slot, k, pl.ds(_row_tile(t), ROW_TILE)], sem_g.at[slot])

    def issue_rows(slot):
        def issue(b, c):
            for u in range(DMA_UNROLL):
                r = b * DMA_UNROLL + u
                row_copy(slot, idx[slot, r], u % 2, b * (DMA_UNROLL // 2) + u // 2).start(priority=u % 2)
            return c

        lax.fori_loop(0, 2 * TC // DMA_UNROLL, issue, 0)

    @pl.when(i == 0)
    def _():
        idx_copy(0, 0).start()
        idx_copy(0, 0).wait()
        issue_rows(0)

        @pl.when(n > 1)
        def _():
            idx_copy(1, 1).start()

    for slot in range(2):
        @pl.when(jnp.logical_and(i % 2 == slot, i + 1 < n))
        def _(slot=slot):
            idx_copy(0, 1 - slot).wait()
            issue_rows(1 - slot)

            @pl.when(i + 2 < n)
            def _():
                idx_copy(i + 2, slot).start()

    w1 = route_ref[:, 0:1]
    w2 = route_ref[:, 1:2]
    for slot in range(2):
        @pl.when(i % 2 == slot)
        def _(slot=slot):
            for k in range(2):
                pltpu.make_async_copy(eo_hbm.at[pl.ds(0, TC * ROW_TILE)], gbuf.at[slot, k], sem_g.at[slot]).wait()
            for j in range(ROW_TILE):
                sl = slice(j * LANES, (j + 1) * LANES)
                xn[:, sl] = x_ref[:, sl] + (gbuf[slot, 0, pl.ds(j, TC, stride=ROW_TILE), :] * w1
                                            + gbuf[slot, 1, pl.ds(j, TC, stride=ROW_TILE), :] * w2)

    if split is None:
        out_a[...] = xn[...]
        out_b[...] = _rms_rows(xn[...], nw_ref[...]).astype(BF16)
    else:
        tiles_a, tiles_b = split

        @pl.when(i < tiles_a)
        def _():
            out_a[...] = xn[...]

        @pl.when(jnp.logical_and(i >= tiles_a, i < tiles_a + tiles_b))
        def _():
            out_b[...] = xn[...]


def _combine(eo, dest, x, route, nw_next, split_rows=None):
    n = x.shape[0]
    steps = n // TC
    row = lambda w: pl.BlockSpec((TC, w), lambda i: (i, 0))
    if split_rows is None:
        split = None
        out_specs = [row(D_MODEL), row(D_MODEL)]
        out_shape = [jax.ShapeDtypeStruct((n, D_MODEL), F32), jax.ShapeDtypeStruct((n, D_MODEL), BF16)]
    else:
        ta, tb = split = tuple(r // TC for r in split_rows)
        out_specs = [pl.BlockSpec((TC, D_MODEL), lambda i: (jnp.minimum(i, ta - 1), 0)),
                     pl.BlockSpec((TC, D_MODEL), lambda i: (jnp.clip(i - ta, 0, tb - 1), 0))]
        out_shape = [jax.ShapeDtypeStruct((r, D_MODEL), F32) for r in split_rows]
    return pl.pallas_call(
        functools.partial(_combine_kernel, split),
        grid=(steps,),
        in_specs=[pl.BlockSpec(memory_space=pl.ANY), pl.BlockSpec(memory_space=pl.ANY),
                  row(D_MODEL), row(2), _const_spec((1, D_MODEL))],
        out_specs=out_specs, out_shape=out_shape,
        scratch_shapes=[pltpu.SMEM((2, 2 * TC), jnp.int32), pltpu.VMEM((2, 2, TC * ROW_TILE, LANES), F32),
                        pltpu.VMEM((TC, D_MODEL), F32),
                        pltpu.SemaphoreType.DMA((2,)), pltpu.SemaphoreType.DMA((2,))],
        compiler_params=_cparams(("arbitrary",)),
    )(dest.reshape(steps, 2 * TC), eo, x, route, nw_next)


def _trunk(lay, x, h, p, group_rows):
    depth = p['norm1_w'].shape[0]
    row = lambda v: v.reshape(1, -1).astype(F32)
    for l in range(depth):
        w_in = p['w_in'][l]
        wqk = w_in[:, 0:2 * ATT_W].astype(BF16)
        wvzx = w_in[:, 2 * ATT_W:3 * ATT_W + D_SSM + CONV_CH].astype(BF16)
        wdt = jnp.pad(w_in[:, 3 * ATT_W + D_SSM + CONV_CH:], ((0, 0), (0, LANES - 2 * N_HEADS))).astype(BF16)
        nw_qk = jnp.concatenate([jnp.tile(p['q_norm_w'][l].astype(F32), N_HEADS) * (HEAD_DIM ** -0.5),
                                 jnp.tile(p['k_norm_w'][l].astype(F32), N_HEADS)]).reshape(1, -1)
        qk, v, z, xbc, dt_raw = _inproj(h, wqk, wvzx, wdt, nw_qk)

        attn = _attention(lay, qk, v, p['rpb'][l])

        cw = jnp.concatenate([p['conv_w'][l].astype(F32).T, p['conv_b'][l].astype(F32)[None],
                              jnp.zeros((8 - CONV_K - 1, CONV_CH), F32)], axis=0)
        pad = jnp.zeros((LANES - 2 * N_HEADS,), F32)
        dtb = jnp.concatenate([p['dt_bias'][l].astype(F32).reshape(-1), pad]).reshape(1, -1)
        arow = jnp.concatenate([-jnp.exp(p['a_log'][l].astype(F32)).reshape(-1), pad]).reshape(1, -1)
        y_f, u = _ssd_pass(lay, False, dt_raw, dtb, arow, fwd_in=(xbc, cw))
        dsk = jnp.repeat(p['d_skip'][l].astype(F32), HEAD_DIM).reshape(1, -1)
        y = _ssd_pass(lay, True, dt_raw, dtb, arow, rev_in=(u, y_f, z, dsk, row(p['ssm_norm_w'][l])))

        w_out = p['w_out'][l].astype(BF16)
        nw_next = row(p['norm1_w'][l + 1]) if l + 1 < depth else jnp.ones((1, D_MODEL), F32)
        j = l // 2
        if l % 2 == 0:
            x, h2 = _outproj(attn, y, x, w_out, row(p['attn_out_norm_w'][l]), row(p['norm2_w'][l]))
            x, h = _ffn(h2, x, p['ffn_w_gate'][j].astype(BF16), p['ffn_w_up'][j].astype(BF16),
                        p['ffn_w_down'][j].astype(BF16), nw_next)
        else:
            wr = jnp.pad(p['moe_router'][j].astype(F32).T, ((0, ROUTER_ROWS - N_EXPERTS), (0, 0)))
            x, h2, route_t = _outproj(attn, y, x, w_out, row(p['attn_out_norm_w'][l]), row(p['norm2_w'][l]),
                                      router=jnp.concatenate(_split_bf16(wr), axis=0))
            dest, block_e, n_used, pad_rows, n_rows = _moe_plan(route_t)
            xs = _dispatch(h2, dest, pad_rows, n_rows)
            eo = _experts(xs, block_e, n_used, p['moe_w_gate'][j].astype(BF16),
                          p['moe_w_up'][j].astype(BF16), p['moe_w_down'][j].astype(BF16))
            if l + 1 == depth and len(group_rows) == 2:
                return _combine(eo, dest, x, route_t[2:4].T, nw_next, split_rows=group_rows)
            x, h = _combine(eo, dest, x, route_t[2:4].T, nw_next)
    offs = np.concatenate([[0], np.cumsum(group_rows)])
    return [x[offs[g]:offs[g + 1]] for g in range(len(group_rows))]


def kernel(x_prompt, x_sample, meta_tokens, norm1_w, w_in, q_norm_w, k_norm_w, rpb, attn_out_norm_w,
           conv_w, conv_b, dt_bias, a_log, d_skip, ssm_norm_w, w_out, norm2_w,
           ffn_w_gate, ffn_w_up, ffn_w_down, moe_router, moe_w_gate, moe_w_up, moe_w_down):
    p = dict(norm1_w=norm1_w, w_in=w_in, q_norm_w=q_norm_w, k_norm_w=k_norm_w, rpb=rpb,
             attn_out_norm_w=attn_out_norm_w, conv_w=conv_w, conv_b=conv_b, dt_bias=dt_bias, a_log=a_log,
             d_skip=d_skip, ssm_norm_w=ssm_norm_w, w_out=w_out, norm2_w=norm2_w, ffn_w_gate=ffn_w_gate,
             ffn_w_up=ffn_w_up, ffn_w_down=ffn_w_down, moe_router=moe_router, moe_w_gate=moe_w_gate,
             moe_w_up=moe_w_up, moe_w_down=moe_w_down)
    groups = (x_prompt, x_sample)
    lay = _Layout([g.shape[1] for g in groups for _ in range(g.shape[0])])
    meta_block = jnp.concatenate([jnp.zeros((META_OFF, D_MODEL), F32), meta_tokens.astype(F32)], axis=0)
    tail = lay.n_tok - lay.n_grid - CHUNK * lay.n_seq
    tail_rows = jnp.concatenate([jnp.tile(meta_block, (lay.n_seq, 1)), jnp.zeros((tail, D_MODEL), F32)], axis=0)
    x, h = _embed([g.reshape(-1, D_MODEL) for g in groups] + [tail_rows], norm1_w[0])
    flat = _trunk(lay, x, h, p, [g.shape[0] * g.shape[1] for g in groups])
    return tuple(f.reshape(g.shape).astype(g.dtype) for f, g in zip(flat, groups))
```

```python
import functools

import numpy as np
import jax
import jax.numpy as jnp
from jax import lax
from jax.experimental import pallas as pl
from jax.experimental.pallas import tpu as pltpu

F32 = jnp.float32
BF16 = jnp.bfloat16

D_MODEL = 1024
N_META = 16
GRID_W = 64
NA_ROWS = 8
NA_COLS = 16
N_HEADS = 16
HEAD_DIM = 64
N_PAIRS = N_HEADS // 2
UNIT_HEADS = 4
UNIT_W = UNIT_HEADS * HEAD_DIM
N_UNITS = N_HEADS // UNIT_HEADS
ATT_W = N_HEADS * HEAD_DIM
D_SSM = 1024
SSM_GROUPS = 2
SSM_STATE = 128
GROUP_W = D_SSM // SSM_GROUPS
CONV_K = 5
CONV_CH = D_SSM + 2 * SSM_GROUPS * SSM_STATE
D_FF = 2816
N_EXPERTS = 8
D_FF_EXPERT = 3584
RMS_EPS = 1e-6
NEG_INF = -1e30
LOG2E = 1.4426950408889634

LANES = 128
ROW_TILE = D_MODEL // LANES
CHUNK = 128
META_OFF = CHUNK - N_META
TM = 512
BAND_ROWS = 64
BAND = BAND_ROWS * GRID_W
HALO_ROWS = NA_ROWS // 2
BUF_ROWS = BAND_ROWS + NA_ROWS - 1
FF_CHUNK = 256
TME = 512
ROUTER_ROWS = 16
N_PAD_RANGES = 2 * N_EXPERTS + 1
RB = 2 * TM
TC = 512
DMA_UNROLL = 8
VMEM_LIMIT = 56 * 1024 * 1024


def _cparams(sem):
    return pltpu.CompilerParams(dimension_semantics=sem, vmem_limit_bytes=VMEM_LIMIT)


def _round_up(a, b):
    return (a + b - 1) // b * b


def _const_spec(shape):
    nd = len(shape)
    return pl.BlockSpec(shape, lambda *_: (0,) * nd)


class _Layout:
    def __init__(self, seq_lens):
        self.seq_lens = tuple(seq_lens)
        self.n_seq = len(seq_lens)
        self.n_grid = sum(seq_lens)
        self.starts = np.concatenate([[0], np.cumsum(seq_lens)[:-1]]).astype(np.int64)
        self.n_tok = _round_up(self.n_grid + CHUNK * self.n_seq, TM)
        assert all(s % BAND == 0 and s // GRID_W >= NA_ROWS for s in seq_lens)

    def meta_block(self, s):
        return self.n_grid // CHUNK + s

    def attn_tables(self):
        prev, nxt, flags, mblk = [], [], [], []
        for s, (st, ln) in enumerate(zip(self.starts, self.seq_lens)):
            lo, hi = st // BAND, (st + ln) // BAND
            for b in range(lo, hi):
                per = BAND_ROWS // HALO_ROWS
                prev.append(max(b * per - 1, lo * per))
                nxt.append(min((b + 1) * per, hi * per - 1))
                flags.append((1 if b == lo else 0) | (2 if b == hi - 1 else 0))
                mblk.append((self.n_grid + CHUNK * s + META_OFF) // N_META)
        return [np.asarray(a, np.int32) for a in (prev, nxt, flags, mblk)]

    def ssd_tables(self, reverse):
        blk, prev, nxt, flags = [], [], [], []
        for s, (st, ln) in enumerate(zip(self.starts, self.seq_lens)):
            nc = ln // CHUNK
            b0 = st // CHUNK
            meta16 = (self.n_grid + CHUNK * s + META_OFF) // N_META
            steps = []
            steps.append((self.meta_block(s), 0, b0 * 8, 2 | 4))
            for c in range(nc):
                b = b0 + c
                p16 = meta16 if c == 0 else b * 8 - 1
                n16 = (b + 1) * 8 if c < nc - 1 else 0
                steps.append((b, p16, n16, 1 | (2 if c < nc - 1 else 0)))
            if reverse:
                steps = steps[::-1]
            for i, (b, p, n, f) in enumerate(steps):
                blk.append(b); prev.append(p); nxt.append(n)
                flags.append(f | (8 if i == 0 else 0))
        for b in range(self.n_grid // CHUNK + self.n_seq, self.n_tok // CHUNK):
            blk.append(b); prev.append(0); nxt.append(0)
            flags.append(8 | 16)
        return [np.asarray(a, np.int32) for a in (blk, prev, nxt, flags)]


def _rms_rows(x, w):
    ms = jnp.mean(x * x, axis=-1, keepdims=True)
    return x * lax.rsqrt(ms + RMS_EPS) * w


def _embed_kernel(bounds, *refs):
    srcs, (w_ref, x_ref, h_ref) = refs[:len(bounds)], refs[len(bounds):]
    i = pl.program_id(0)
    lo = 0
    for src, hi in zip(srcs, bounds):
        @pl.when(jnp.logical_and(i >= lo, i < hi))
        def _(src=src):
            x = src[...].astype(F32)
            x_ref[...] = x
            h_ref[...] = _rms_rows(x, w_ref[...]).astype(BF16)
        lo = hi


def _embed(parts, w):
    tiles = [a.shape[0] // TM for a in parts]
    assert all(a.shape[0] % TM == 0 for a in parts)
    bounds = tuple(int(b) for b in np.cumsum(tiles))
    starts = (0,) + bounds[:-1]
    n = bounds[-1] * TM
    src_spec = lambda s, t: pl.BlockSpec((TM, D_MODEL), lambda i: (jnp.clip(i - s, 0, t - 1), 0))
    row = pl.BlockSpec((TM, D_MODEL), lambda i: (i, 0))
    return pl.pallas_call(
        functools.partial(_embed_kernel, bounds),
        grid=(bounds[-1],),
        in_specs=[src_spec(s, t) for s, t in zip(starts, tiles)] + [_const_spec((1, D_MODEL))],
        out_specs=[row, row],
        out_shape=[jax.ShapeDtypeStruct((n, D_MODEL), F32), jax.ShapeDtypeStruct((n, D_MODEL), BF16)],
        compiler_params=_cparams(("arbitrary",)),
    )(*parts, w.reshape(1, D_MODEL))


def _inproj_kernel(h_ref, wqk_ref, wvzx_ref, wdt_ref, g_ref, nw_ref,
                   qk_ref, v_ref, z_ref, xbc_ref, dt_ref):
    h = h_ref[...]
    g = g_ref[...]
    n_qk = 2 * ATT_W // FF_CHUNK
    col = lambda c: slice(c * FF_CHUNK, (c + 1) * FF_CHUNK)
    y_next = jnp.dot(h, wqk_ref[:, col(0)], preferred_element_type=F32)
    for c in range(n_qk):
        y = y_next
        if c + 1 < n_qk:
            y_next = jnp.dot(h, wqk_ref[:, col(c + 1)], preferred_element_type=F32)
        ss = jnp.dot((y * y).astype(BF16), g, preferred_element_type=F32)
        inv = lax.rsqrt(ss * (1.0 / HEAD_DIM) + RMS_EPS)
        qk_ref[:, col(c)] = (y * inv * nw_ref[:, col(c)]).astype(BF16)
    for c in range(ATT_W // FF_CHUNK):
        sl = slice(c * FF_CHUNK, (c + 1) * FF_CHUNK)
        v_ref[:, sl] = jnp.dot(h, wvzx_ref[:, sl], preferred_element_type=F32).astype(BF16)
    for c in range(D_SSM // FF_CHUNK):
        sl = slice(c * FF_CHUNK, (c + 1) * FF_CHUNK)
        src = slice(ATT_W + c * FF_CHUNK, ATT_W + (c + 1) * FF_CHUNK)
        z_ref[:, sl] = jnp.dot(h, wvzx_ref[:, src], preferred_element_type=F32).astype(BF16)
    for c in range(CONV_CH // FF_CHUNK):
        sl = slice(c * FF_CHUNK, (c + 1) * FF_CHUNK)
        src = slice(ATT_W + D_SSM + c * FF_CHUNK, ATT_W + D_SSM + (c + 1) * FF_CHUNK)
        xbc_ref[:, sl] = jnp.dot(h, wvzx_ref[:, src], preferred_element_type=F32).astype(BF16)
    dt_ref[...] = jnp.dot(h, wdt_ref[...], preferred_element_type=F32)


def _head_sum_matrix():
    i = np.arange(FF_CHUNK)
    return jnp.asarray((i[:, None] // HEAD_DIM == i[None, :] // HEAD_DIM).astype(np.float32), BF16)


def _inproj(h, wqk, wvzx, wdt, nw):
    n = h.shape[0]
    row = lambda w: pl.BlockSpec((TM, w), lambda i: (i, 0))
    return pl.pallas_call(
        _inproj_kernel,
        grid=(n // TM,),
        in_specs=[row(D_MODEL), _const_spec(wqk.shape), _const_spec(wvzx.shape), _const_spec(wdt.shape),
                  _const_spec((FF_CHUNK, FF_CHUNK)), _const_spec((1, 2 * ATT_W))],
        out_specs=[row(2 * ATT_W), row(ATT_W), row(D_SSM), row(CONV_CH), row(LANES)],
        out_shape=[jax.ShapeDtypeStruct((n, 2 * ATT_W), BF16), jax.ShapeDtypeStruct((n, ATT_W), BF16),
                   jax.ShapeDtypeStruct((n, D_SSM), BF16), jax.ShapeDtypeStruct((n, CONV_CH), BF16),
                   jax.ShapeDtypeStruct((n, LANES), F32)],
        compiler_params=_cparams(("parallel",)),
    )(h, wqk, wvzx, wdt, _head_sum_matrix(), nw)


def _split_bf16(x):
    hi = x.astype(BF16)
    lo = (x - hi.astype(F32)).astype(BF16)
    return hi, lo


def _outproj_kernel(with_router, grid_tiles, attn_ref, attn_tail_ref, y_ref, x_ref, w_ref, aw_ref, nw_ref, *rest):
    if with_router:
        wr_ref, xo_ref, h_ref, route_ref = rest
    else:
        xo_ref, h_ref = rest
    acc = x_ref[...] + jnp.dot(y_ref[...], w_ref[ATT_W:ATT_W + D_SSM, :], preferred_element_type=F32)
    attn = jnp.where(pl.program_id(0) < grid_tiles, attn_ref[...], attn_tail_ref[...])
    a = _rms_rows(attn.astype(F32), aw_ref[...]).astype(BF16)
    acc = acc + jnp.dot(a, w_ref[0:ATT_W, :], preferred_element_type=F32)
    xo_ref[...] = acc
    h2 = _rms_rows(acc, nw_ref[...])
    h_ref[...] = h2.astype(BF16)
    if with_router:
        hh, hl = _split_bf16(h2)
        wr = wr_ref[...]
        lg = _nt_dot(wr, hh)
        logits = lg[0:ROUTER_ROWS] + lg[ROUTER_ROWS:] + _nt_dot(wr[0:ROUTER_ROWS], hl)
        sub = lax.broadcasted_iota(jnp.int32, logits.shape, 0)
        logits = jnp.where(sub < N_EXPERTS, logits, NEG_INF)
        m1 = jnp.max(logits, axis=0, keepdims=True)
        i1 = jnp.min(jnp.where(logits == m1, sub, ROUTER_ROWS), axis=0, keepdims=True)
        rest_l = jnp.where(sub == i1, NEG_INF, logits)
        m2 = jnp.max(rest_l, axis=0, keepdims=True)
        i2 = jnp.min(jnp.where(rest_l == m2, sub, ROUTER_ROWS), axis=0, keepdims=True)
        e = jnp.exp(m2 - m1)
        w1 = 1.0 / (1.0 + e)
        route_ref[...] = jnp.concatenate(
            [i1.astype(F32), i2.astype(F32), w1, e * w1, jnp.zeros((4, logits.shape[1]), F32)], axis=0)


def _outproj(attn, y, x, w_out, aw, nw, router=None):
    n = x.shape[0]
    row = lambda w: pl.BlockSpec((TM, w), lambda i: (i, 0))
    tg, tt = attn[0].shape[0] // TM, attn[1].shape[0] // TM
    in_specs = [pl.BlockSpec((TM, ATT_W), lambda i: (jnp.minimum(i, tg - 1), 0)),
                pl.BlockSpec((TM, ATT_W), lambda i: (jnp.clip(i - tg, 0, tt - 1), 0)),
                row(D_SSM), row(D_MODEL), _const_spec(w_out.shape),
                _const_spec((1, ATT_W)), _const_spec((1, D_MODEL))]
    out_specs = [row(D_MODEL), row(D_MODEL)]
    out_shape = [jax.ShapeDtypeStruct((n, D_MODEL), F32), jax.ShapeDtypeStruct((n, D_MODEL), BF16)]
    args = [attn[0], attn[1], y, x, w_out, aw, nw]
    if router is not None:
        in_specs.append(_const_spec((2 * ROUTER_ROWS, D_MODEL)))
        out_specs.append(pl.BlockSpec((8, TM), lambda i: (0, i)))
        out_shape.append(jax.ShapeDtypeStruct((8, n), F32))
        args.append(router)
    return pl.pallas_call(
        functools.partial(_outproj_kernel, router is not None, tg),
        grid=(n // TM,), in_specs=in_specs, out_specs=out_specs, out_shape=out_shape,
        compiler_params=_cparams(("parallel",)),
    )(*args)


def _silu(x):
    return x * (1.0 / (1.0 + jnp.exp(-x)))


def _ffn_kernel(h_ref, x_ref, wg_ref, wu_ref, wd_ref, nw_ref, xo_ref, ho_ref, act_ref):
    h = h_ref[...]
    for c in range(D_FF // FF_CHUNK):
        sl = slice(c * FF_CHUNK, (c + 1) * FF_CHUNK)
        g = jnp.dot(h, wg_ref[:, sl], preferred_element_type=F32)
        u = jnp.dot(h, wu_ref[:, sl], preferred_element_type=F32)
        act_ref[:, sl] = (_silu(g) * u).astype(BF16)
    xn = x_ref[...] + jnp.dot(act_ref[...], wd_ref[...], preferred_element_type=F32)
    xo_ref[...] = xn
    ho_ref[...] = _rms_rows(xn, nw_ref[...]).astype(BF16)


def _ffn(h, x, wg, wu, wd, nw_next):
    n = x.shape[0]
    row = lambda w: pl.BlockSpec((TM, w), lambda i: (i, 0))
    return pl.pallas_call(
        _ffn_kernel,
        grid=(n // TM,),
        in_specs=[row(D_MODEL), row(D_MODEL), _const_spec(wg.shape), _const_spec(wu.shape),
                  _const_spec(wd.shape), _const_spec((1, D_MODEL))],
        out_specs=[row(D_MODEL), row(D_MODEL)],
        out_shape=[jax.ShapeDtypeStruct((n, D_MODEL), F32), jax.ShapeDtypeStruct((n, D_MODEL), BF16)],
        scratch_shapes=[pltpu.VMEM((TM, D_FF), BF16)],
        compiler_params=_cparams(("parallel",)),
    )(h, x, wg, wu, wd, nw_next)


N_CB = GRID_W // NA_COLS
SPAN_START = (0, 0, 16, 32)
SPAN_W = (32, 48, 48, 32)
META_SLOT = (24, 0, 0, 0)
META_PER_ROW = N_META // 2
KTILE = 16
ATTN_DEPTH = 16


def _attn_bias_tables(rpb):
    i = np.arange(NA_ROWS)
    dr = np.clip(i[None, :] - i[:, None] + NA_ROWS - 1, 0, 2 * NA_ROWS - 2)
    tables = {}
    for j in range(N_CB):
        span = SPAN_W[j]
        qc = NA_COLS * j + np.arange(NA_COLS)
        kc = SPAN_START[j] + np.arange(span)
        st = np.clip(qc - NA_COLS // 2, 0, GRID_W - NA_COLS)
        valid = (kc[None, :] >= st[:, None]) & (kc[None, :] < st[:, None] + NA_COLS)
        dc = np.clip(kc[None, :] - qc[:, None] + NA_COLS - 1, 0, 2 * NA_COLS - 2)
        slot = (np.arange(span) >= META_SLOT[j]) & (np.arange(span) < META_SLOT[j] + META_PER_ROW)
        assert not valid[:, slot].any()
        b = rpb.astype(F32)[:, dr][:, :, :, dc]
        b = jnp.where(valid[None, None, None], b, NEG_INF)
        meta_vis = np.where(i[:, None] < 2, 0.0, NEG_INF) * np.ones((1, span))
        b = jnp.where(slot[None, None, None, None, :], jnp.asarray(meta_vis, F32)[None, None, :, None, :], b)
        b = b.reshape(N_UNITS, UNIT_HEADS, NA_ROWS, NA_ROWS, NA_COLS, span)
        b = jnp.transpose(b, (0, 2, 1, 4, 3, 5)).reshape(N_UNITS, NA_ROWS, UNIT_HEADS * NA_COLS, NA_ROWS * span)
        tables.setdefault(span, []).append(b)
    return [jnp.stack(t, axis=1) for _, t in sorted(tables.items())]


def _pair_queries(q):
    lo = lax.broadcasted_iota(jnp.int32, q.shape, 1) < HEAD_DIM
    zero = jnp.zeros_like(q)
    return jnp.concatenate([jnp.where(lo, q, zero), jnp.where(lo, zero, q)], axis=0)


def _head_queries(q):
    head = lax.broadcasted_iota(jnp.int32, q.shape, 1) // HEAD_DIM
    zero = jnp.zeros_like(q)
    return jnp.concatenate([jnp.where(head == h, q, zero) for h in range(UNIT_HEADS)], axis=0)


def _nt_dot(a, b):
    return lax.dot_general(a, b, (((1,), (1,)), ((), ())), preferred_element_type=F32)


def _attn_kernel(prev_ref, next_ref, flag_ref, mblk_ref,
                 q_ref, kp_ref, kc_ref, kn_ref, vp_ref, vc_ref, vn_ref, km_ref, vm_ref,
                 b32_ref, b48_ref, o_ref, *spans):
    del prev_ref, next_ref, mblk_ref
    flags = flag_ref[pl.program_id(1)]
    is_first = (flags & 1) != 0
    is_last = (flags & 2) != 0
    low = lax.broadcasted_iota(jnp.int32, (KTILE, UNIT_W), 0) < META_PER_ROW

    for refs, bufs, m_ref in (((kp_ref, kc_ref, kn_ref), spans[:N_CB], km_ref),
                              ((vp_ref, vc_ref, vn_ref), spans[N_CB:], vm_ref)):
        m = m_ref[...]
        mf = m.astype(F32)
        m_swapped = jnp.concatenate([mf[META_PER_ROW:], mf[:META_PER_ROW]], axis=0).astype(BF16)
        for rr in range(BUF_ROWS):
            if rr < HALO_ROWS:
                ref, row = refs[0], rr
            elif rr < HALO_ROWS + BAND_ROWS:
                ref, row = refs[1], rr - HALO_ROWS
            else:
                ref, row = refs[2], rr - HALO_ROWS - BAND_ROWS
            for j in range(N_CB):
                span = SPAN_W[j]
                for t in range(span // KTILE):
                    off = row * GRID_W + SPAN_START[j] + t * KTILE
                    tile = ref[off:off + KTILE, :]
                    if t == META_SLOT[j] // KTILE:
                        if META_SLOT[j] % KTILE == 0:
                            tile = jnp.where(low, m if rr % 2 == 0 else m_swapped, tile)
                        else:
                            tile = jnp.where(low, tile, m_swapped if rr % 2 == 0 else m)
                    bufs[j][rr * span + t * KTILE:rr * span + (t + 1) * KTILE, :] = tile

    lane_head = lax.broadcasted_iota(jnp.int32, (NA_COLS, UNIT_W), 1) // HEAD_DIM
    starts = []
    for ri in range(BAND_ROWS):
        ls = jnp.int32(ri)
        ls = jnp.where(is_first, jnp.maximum(ls, HALO_ROWS), ls)
        ls = jnp.where(is_last, jnp.minimum(ls, BAND_ROWS - HALO_ROWS), ls)
        starts.append((ls, HALO_ROWS + ri - ls))

    def scores(u):
        ri, j = divmod(u, N_CB)
        ls, delta = starts[ri]
        span = SPAN_W[j]
        kwin = spans[j][pl.ds(pl.multiple_of(ls * span, KTILE), NA_ROWS * span), :]
        q0 = ri * GRID_W + j * NA_COLS
        bias_ref = b32_ref if span == SPAN_W[0] else b48_ref
        return _nt_dot(_head_queries(q_ref[q0:q0 + NA_COLS, :]), kwin) + bias_ref[SPAN_W[:j].count(span), delta]

    def finish(u, s):
        ri, j = divmod(u, N_CB)
        span = SPAN_W[j]
        vwin = spans[N_CB + j][pl.ds(pl.multiple_of(starts[ri][0] * span, KTILE), NA_ROWS * span), :]
        p = jnp.exp(s - jnp.max(s, axis=-1, keepdims=True))
        l = jnp.sum(p, axis=-1, keepdims=True)
        o = jnp.dot(p.astype(BF16), vwin, preferred_element_type=F32) / l
        q0 = ri * GRID_W + j * NA_COLS
        out = o[:NA_COLS]
        for h in range(1, UNIT_HEADS):
            out = jnp.where(lane_head == h, o[h * NA_COLS:(h + 1) * NA_COLS], out)
        o_ref[q0:q0 + NA_COLS, :] = out.astype(BF16)

    n_units = BAND_ROWS * N_CB
    pending = {}
    for t in range(n_units + ATTN_DEPTH):
        if t < n_units:
            pending[t] = scores(t)
        if t >= ATTN_DEPTH:
            finish(t - ATTN_DEPTH, pending.pop(t - ATTN_DEPTH))


def _attn_meta_kernel(n_seq, qk_ref, v_ref, mbias_ref, o_ref):
    s = pl.program_id(0)
    o_ref[...] = jnp.zeros_like(o_ref)

    @pl.when(s < n_seq)
    def _():
        mbias = mbias_ref[...]
        for p in range(N_PAIRS):
            c = slice(p * LANES, (p + 1) * LANES)
            q = qk_ref[META_OFF:CHUNK, c]
            k = qk_ref[:, ATT_W + p * LANES:ATT_W + (p + 1) * LANES]
            qs = _pair_queries(q)
            sc = _nt_dot(qs, k) + mbias
            m = jnp.max(sc, axis=-1, keepdims=True)
            e = jnp.exp(sc - m)
            l = jnp.sum(e, axis=-1, keepdims=True)
            o = jnp.dot(e.astype(BF16), v_ref[:, c], preferred_element_type=F32) / l
            lo = lax.broadcasted_iota(jnp.int32, (N_META, LANES), 1) < HEAD_DIM
            o_ref[META_OFF:CHUNK, c] = jnp.where(lo, o[0:N_META], o[N_META:]).astype(BF16)


def _attention(lay, qk, v, rpb):
    n = qk.shape[0]
    prev, nxt, flags, mblk = lay.attn_tables()
    n_bands = lay.n_grid // BAND
    b32, b48 = _attn_bias_tables(rpb)
    lane = np.arange(LANES)
    mbias_meta = jnp.asarray(np.where(lane >= META_OFF, 0.0, NEG_INF)[None, :], F32)
    kcol = ATT_W // UNIT_W
    halo = HALO_ROWS * GRID_W
    bias_spec = lambda a: pl.BlockSpec((None,) + a.shape[1:], lambda p, b, *_: (p, 0, 0, 0, 0))
    grid_spec = pltpu.PrefetchScalarGridSpec(
        num_scalar_prefetch=4,
        grid=(N_UNITS, n_bands),
        in_specs=[
            pl.BlockSpec((BAND, UNIT_W), lambda p, b, *_: (b, p)),
            pl.BlockSpec((halo, UNIT_W), lambda p, b, pv, nx, fl, mb: (pv[b], kcol + p)),
            pl.BlockSpec((BAND, UNIT_W), lambda p, b, pv, nx, fl, mb: (b, kcol + p)),
            pl.BlockSpec((halo, UNIT_W), lambda p, b, pv, nx, fl, mb: (nx[b], kcol + p)),
            pl.BlockSpec((halo, UNIT_W), lambda p, b, pv, nx, fl, mb: (pv[b], p)),
            pl.BlockSpec((BAND, UNIT_W), lambda p, b, pv, nx, fl, mb: (b, p)),
            pl.BlockSpec((halo, UNIT_W), lambda p, b, pv, nx, fl, mb: (nx[b], p)),
            pl.BlockSpec((N_META, UNIT_W), lambda p, b, pv, nx, fl, mb: (mb[b], kcol + p)),
            pl.BlockSpec((N_META, UNIT_W), lambda p, b, pv, nx, fl, mb: (mb[b], p)),
            bias_spec(b32), bias_spec(b48),
        ],
        out_specs=pl.BlockSpec((BAND, UNIT_W), lambda p, b, *_: (b, p)),
        scratch_shapes=[pltpu.VMEM((BUF_ROWS * w, UNIT_W), BF16) for w in SPAN_W] * 2,
    )
    attn = pl.pallas_call(
        _attn_kernel, grid_spec=grid_spec,
        out_shape=jax.ShapeDtypeStruct((lay.n_grid, ATT_W), BF16),
        compiler_params=_cparams(("arbitrary", "arbitrary")),
    )(jnp.asarray(prev), jnp.asarray(nxt), jnp.asarray(flags), jnp.asarray(mblk),
      qk, qk, qk, qk, v, v, v, qk, v, b32, b48)
    mb0 = lay.n_grid // CHUNK
    n_tail = (n - lay.n_grid) // CHUNK
    attn_tail = pl.pallas_call(
        functools.partial(_attn_meta_kernel, lay.n_seq),
        grid=(n_tail,),
        in_specs=[pl.BlockSpec((CHUNK, 2 * ATT_W), lambda s: (mb0 + s, 0)),
                  pl.BlockSpec((CHUNK, ATT_W), lambda s: (mb0 + s, 0)),
                  _const_spec((1, LANES))],
        out_specs=pl.BlockSpec((CHUNK, ATT_W), lambda s: (s, 0)),
        out_shape=jax.ShapeDtypeStruct((n - lay.n_grid, ATT_W), BF16),
        compiler_params=_cparams(("arbitrary",)),
    )(qk, v, mbias_meta)
    return attn, attn_tail


def _softplus(x):
    return jnp.maximum(x, 0.0) + jnp.log(1.0 + jnp.exp(-jnp.abs(x)))


def _split3_bf16(x):
    hi = x.astype(BF16)
    r = x - hi.astype(F32)
    mid = r.astype(BF16)
    lo = (r - mid.astype(F32)).astype(BF16)
    return hi, mid, lo


def _expand_heads(w, e):
    hi, lo = _split_bf16(w)
    return jnp.dot(hi, e, preferred_element_type=F32) + jnp.dot(lo, e, preferred_element_type=F32)


def _ssd_kernel(reverse, blk_ref, prev_ref, next_ref, flag_ref, *refs):
    if reverse:
        (u_ref, dt_ref, dtb_ref, arow_ref, e_ref, tri_ref, yf_ref, z_ref, dsk_ref, nw_ref,
         o_ref, state, ybuf) = refs
    else:
        (xbc_ref, xp_ref, xn_ref, shift_ref, cw_ref, dt_ref, dtb_ref, arow_ref, e_ref, tri_ref,
         o_ref, u_ref, xe, state) = refs
        ybuf = o_ref
    del blk_ref, prev_ref, next_ref
    flags = flag_ref[pl.program_id(0)]
    is_meta = (flags & 4) != 0

    @pl.when((flags & 8) != 0)
    def _():
        state[...] = jnp.zeros_like(state)

    if reverse:
        u = u_ref[...].astype(F32)
    else:
        zero = jnp.zeros((N_META, CONV_CH), BF16)
        xe[0:N_META, :] = jnp.where((flags & 1) != 0, xp_ref[...], zero)
        xe[N_META:N_META + CHUNK, :] = xbc_ref[...]
        xe[N_META + CHUNK:, :] = jnp.where((flags & 2) != 0, xn_ref[...], zero)

        @pl.when(is_meta)
        def _():
            xe[0:N_META + META_OFF, :] = jnp.zeros((N_META + META_OFF, CONV_CH), BF16)

        half = CONV_K // 2
        parts = []
        for c in range(CONV_CH // FF_CHUNK):
            sl = slice(c * FF_CHUNK, (c + 1) * FF_CHUNK)
            xc = xe[:, sl]
            uc = cw_ref[CONV_K:CONV_K + 1, sl] + cw_ref[half:half + 1, sl] * xc[N_META:N_META + CHUNK].astype(F32)
            for n, k in enumerate(k for k in range(CONV_K) if k != half):
                uc = uc + cw_ref[k:k + 1, sl] * jnp.dot(shift_ref[n * CHUNK:(n + 1) * CHUNK, :], xc,
                                                        preferred_element_type=F32)
            parts.append(_silu(uc))
        u = jnp.concatenate(parts, axis=1)
        u_ref[...] = u.astype(BF16)
    xs = u[:, 0:D_SSM]
    x_bf = xs.astype(BF16)

    rid = lax.broadcasted_iota(jnp.int32, (CHUNK, 1), 0)
    valid = rid >= jnp.where(is_meta, META_OFF, 0)
    dt = jnp.where(valid, _softplus(dt_ref[...] + dtb_ref[...]), 0.0)
    a = dt * arow_ref[...]
    tri = tri_ref[...]
    ah, am, al = _split3_bf16(a)
    acum = (jnp.dot(tri, ah, preferred_element_type=F32) + jnp.dot(tri, am, preferred_element_type=F32)
            + jnp.dot(tri, al, preferred_element_type=F32))
    acum = acum * LOG2E
    acum_t = acum.T
    dt_t = dt.T
    edge = 0 if reverse else CHUNK - 1
    a_tot = acum[edge:edge + 1, :]
    e = e_ref[...]
    w1e = jnp.dot((dt * jnp.exp2(a_tot - acum)).astype(BF16), e, preferred_element_type=F32)
    w2e = jnp.dot(jnp.exp2(acum).astype(BF16), e, preferred_element_type=F32)
    decay_row = _expand_heads(jnp.broadcast_to(jnp.exp2(a_tot), (8, LANES)), e)[0:1]
    xw = (xs * w1e).astype(BF16)

    li = lax.broadcasted_iota(jnp.int32, (CHUNK, CHUNK), 0)
    si = lax.broadcasted_iota(jnp.int32, (CHUNK, CHUNK), 1)
    causal = (si >= li) if reverse else (li >= si)
    lane_lo = si < HEAD_DIM
    ho = N_HEADS if reverse else 0
    heads_per_group = N_HEADS // SSM_GROUPS
    for g in range(SSM_GROUPS):
        gsl = slice(g * GROUP_W, (g + 1) * GROUP_W)
        bg = u[:, D_SSM + g * SSM_STATE:D_SSM + (g + 1) * SSM_STATE]
        cg = u[:, D_SSM + (SSM_GROUPS + g) * SSM_STATE:D_SSM + (SSM_GROUPS + g + 1) * SSM_STATE].astype(BF16)
        cb = _nt_dot(cg, bg.astype(BF16))
        st_old = state[:, gsl]
        y_off = jnp.dot(cg, st_old.astype(BF16), preferred_element_type=F32) * w2e[:, gsl]
        st_new = jnp.dot(bg.T.astype(BF16), xw[:, gsl], preferred_element_type=F32)
        state[:, gsl] = st_old * decay_row[:, gsl] + st_new
        for j in range(heads_per_group // 2):
            col = g * GROUP_W + j * LANES
            xpair = x_bf[:, col:col + LANES]
            zero = jnp.zeros_like(xpair)
            mhs = []
            for hh in range(2):
                hi = ho + g * heads_per_group + 2 * j + hh
                seg = acum[:, hi:hi + 1] - acum_t[hi:hi + 1, :]
                lm = jnp.exp2(jnp.where(causal, seg, NEG_INF))
                mhs.append((cb * lm * dt_t[hi:hi + 1, :]).astype(BF16))
            xsplit = jnp.concatenate([jnp.where(lane_lo, xpair, zero), jnp.where(lane_lo, zero, xpair)], axis=0)
            ybuf[:, col:col + LANES] = y_off[:, j * LANES:(j + 1) * LANES] + jnp.dot(
                jnp.concatenate(mhs, axis=1), xsplit, preferred_element_type=F32)
    if reverse:
        for g in range(SSM_GROUPS):
            gsl = slice(g * GROUP_W, (g + 1) * GROUP_W)
            yg = ybuf[:, gsl] + yf_ref[:, gsl] + dsk_ref[:, gsl] * xs[:, gsl]
            yg = yg * _silu(z_ref[:, gsl].astype(F32))
            ms = jnp.mean(yg * yg, axis=-1, keepdims=True)
            o_ref[:, gsl] = (yg * lax.rsqrt(ms + RMS_EPS) * nw_ref[:, gsl]).astype(BF16)

    @pl.when((flags & 16) != 0)
    def _():
        o_ref[...] = jnp.zeros_like(o_ref)


def _ssd_constants(reverse):
    ho = N_HEADS if reverse else 0
    e = np.zeros((LANES, D_SSM), np.float32)
    for h in range(N_HEADS):
        e[ho + h, h * HEAD_DIM:(h + 1) * HEAD_DIM] = 1.0
    i = np.arange(CHUNK)
    tri = (i[None, :] >= i[:, None]) if reverse else (i[:, None] >= i[None, :])
    return jnp.asarray(e, BF16), jnp.asarray(tri.astype(np.float32), BF16)


def _shift_matrices():
    s = np.zeros(((CONV_K - 1) * CHUNK, CHUNK + 2 * N_META), np.float32)
    i = np.arange(CHUNK)
    for n, k in enumerate(k for k in range(CONV_K) if k != CONV_K // 2):
        s[n * CHUNK + i, N_META + i + k - CONV_K // 2] = 1.0
    return jnp.asarray(s, BF16)


def _ssd_pass(lay, reverse, dt, dtb, arow, fwd_in=(), rev_in=()):
    n = dt.shape[0]
    blk, prev, nxt, flags = lay.ssd_tables(reverse)
    e, tri = _ssd_constants(reverse)
    cur = lambda w: pl.BlockSpec((CHUNK, w), lambda i, bk, pv, nx, fl: (bk[i], 0))
    common = [cur(LANES), _const_spec((1, LANES)), _const_spec((1, LANES)),
              _const_spec((LANES, D_SSM)), _const_spec((CHUNK, CHUNK))]
    state = pltpu.VMEM((SSM_STATE, D_SSM), F32)
    if reverse:
        u, y_f, z, dsk, nw = rev_in
        args = (u, dt, dtb, arow, e, tri, y_f, z, dsk, nw)
        in_specs = [cur(CONV_CH)] + common + [cur(D_SSM), cur(D_SSM), _const_spec((1, D_SSM)),
                                              _const_spec((1, D_SSM))]
        out_specs = cur(D_SSM)
        out_shape = jax.ShapeDtypeStruct((n, D_SSM), BF16)
        scratch = [state, pltpu.VMEM((CHUNK, D_SSM), F32)]
    else:
        xbc, cw = fwd_in
        shift = _shift_matrices()
        args = (xbc, xbc, xbc, shift, cw, dt, dtb, arow, e, tri)
        in_specs = [cur(CONV_CH),
                    pl.BlockSpec((N_META, CONV_CH), lambda i, bk, pv, nx, fl: (pv[i], 0)),
                    pl.BlockSpec((N_META, CONV_CH), lambda i, bk, pv, nx, fl: (nx[i], 0)),
                    _const_spec(shift.shape), _const_spec((8, CONV_CH))] + common
        out_specs = [cur(D_SSM), cur(CONV_CH)]
        out_shape = [jax.ShapeDtypeStruct((n, D_SSM), F32), jax.ShapeDtypeStruct((n, CONV_CH), BF16)]
        scratch = [pltpu.VMEM((CHUNK + 2 * N_META, CONV_CH), BF16), state]
    grid_spec = pltpu.PrefetchScalarGridSpec(
        num_scalar_prefetch=4, grid=(len(blk),), in_specs=in_specs,
        out_specs=out_specs, scratch_shapes=scratch)
    return pl.pallas_call(
        functools.partial(_ssd_kernel, reverse), grid_spec=grid_spec, out_shape=out_shape,
        compiler_params=_cparams(("arbitrary",)),
    )(jnp.asarray(blk), jnp.asarray(prev), jnp.asarray(nxt), jnp.asarray(flags), *args)


def _moe_plan(route_t):
    n = route_t.shape[1]
    flat_e = route_t[0:2].T.astype(jnp.int32).reshape(-1)
    onehot = (flat_e[:, None] == jnp.arange(N_EXPERTS, dtype=jnp.int32)[None, :]).astype(jnp.int32)
    csum = jnp.cumsum(onehot, axis=0)
    counts = csum[-1]
    rank = jnp.sum(csum * onehot, axis=1) - 1
    padded = (counts + TME - 1) // TME * TME
    pends = jnp.cumsum(padded)
    pstarts = pends - padded
    dest = jnp.sum(onehot * pstarts[None, :], axis=1) + rank
    n_rows = _round_up(2 * n + N_EXPERTS * TME, TME)
    n_blocks = n_rows // TME
    block_e = jnp.sum((jnp.arange(n_blocks, dtype=jnp.int32)[:, None] * TME >= pends[None, :]).astype(jnp.int32),
                      axis=1)
    block_e = jnp.minimum(block_e, N_EXPERTS - 1)
    n_used = (pends[-1] // TME).astype(jnp.int32).reshape(1)
    tail = jnp.minimum(pends[-1] + jnp.arange(N_PAD_RANGES - N_EXPERTS + 1, dtype=jnp.int32) * TME, n_rows)
    pad_rows = jnp.concatenate([pstarts + counts, tail[:-1], pends, tail[1:]]).astype(jnp.int32)
    return dest.astype(jnp.int32), block_e, n_used, pad_rows, n_rows


def _row_tile(r):
    return pl.multiple_of(r * ROW_TILE, ROW_TILE)


def _dispatch_kernel(pad_ref, dest_hbm, h_ref, xs_hbm, idx, stage, ztile, sem_i, sem, sem_z):
    @pl.when(pl.program_id(0) == 0)
    def _():
        ztile[...] = jnp.zeros_like(ztile)

        def zero_copy(r):
            return pltpu.make_async_copy(ztile, xs_hbm.at[pl.ds(_row_tile(r), ROW_TILE)], sem_z)

        for e in range(N_PAD_RANGES):
            lo, hi = pad_ref[e], pad_ref[N_PAD_RANGES + e]

            def fill(r, c):
                zero_copy(r).start()
                return c

            def drain_fill(r, c):
                zero_copy(0).wait()
                return c

            lax.fori_loop(lo, hi, fill, 0)
            lax.fori_loop(lo, hi, drain_fill, 0)

    cp = pltpu.make_async_copy(dest_hbm.at[pl.program_id(0)], idx, sem_i)
    cp.start()
    hf = h_ref[...].astype(F32)
    for j in range(ROW_TILE):
        stage[pl.ds(j, TM, stride=ROW_TILE), :] = hf[:, j * LANES:(j + 1) * LANES]
    cp.wait()

    def row_copy(src_row, dst_row):
        return pltpu.make_async_copy(stage.at[pl.ds(_row_tile(src_row), ROW_TILE)],
                                     xs_hbm.at[pl.ds(_row_tile(dst_row), ROW_TILE)], sem)

    def issue(b, c):
        for u in range(DMA_UNROLL):
            r = b * DMA_UNROLL + u
            row_copy(lax.shift_right_logical(r, 1), idx[r]).start(priority=u % 2)
        return c

    lax.fori_loop(0, RB // DMA_UNROLL, issue, 0)

    for _ in range(RB // TM):
        pltpu.make_async_copy(stage, xs_hbm.at[pl.ds(0, TM * ROW_TILE)], sem).wait()


def _dispatch(h, dest, pad_rows, n_rows):
    n = h.shape[0]
    steps = n // TM
    grid_spec = pltpu.PrefetchScalarGridSpec(
        num_scalar_prefetch=1, grid=(steps,),
        in_specs=[pl.BlockSpec(memory_space=pl.ANY), pl.BlockSpec((TM, D_MODEL), lambda i, pad: (i, 0))],
        out_specs=pl.BlockSpec(memory_space=pl.ANY),
        scratch_shapes=[pltpu.SMEM((RB,), jnp.int32), pltpu.VMEM((TM * ROW_TILE, LANES), F32),
                        pltpu.VMEM((ROW_TILE, LANES), F32),
                        pltpu.SemaphoreType.DMA, pltpu.SemaphoreType.DMA, pltpu.SemaphoreType.DMA])
    return pl.pallas_call(
        _dispatch_kernel, grid_spec=grid_spec,
        out_shape=jax.ShapeDtypeStruct((n_rows * ROW_TILE, LANES), F32),
        compiler_params=_cparams(("arbitrary",)),
    )(pad_rows, dest.reshape(steps, RB), h)


def _expert_kernel(be_ref, nu_ref, x_ref, wg_ref, wu_ref, wd_ref, o_ref, xbf, act):
    del be_ref
    used = pl.program_id(0) < nu_ref[0]

    @pl.when(used)
    def _():
        for j in range(ROW_TILE):
            xbf[:, j * LANES:(j + 1) * LANES] = x_ref[pl.ds(j, TME, stride=ROW_TILE), :].astype(BF16)
        x = xbf[...]
        for c in range(D_FF_EXPERT // FF_CHUNK):
            sl = slice(c * FF_CHUNK, (c + 1) * FF_CHUNK)
            g = jnp.dot(x, wg_ref[:, sl], preferred_element_type=F32)
            u = jnp.dot(x, wu_ref[:, sl], preferred_element_type=F32)
            act[:, sl] = (_silu(g) * u).astype(BF16)
        out = jnp.dot(act[...], wd_ref[...], preferred_element_type=F32)
        for j in range(ROW_TILE):
            o_ref[pl.ds(j, TME, stride=ROW_TILE), :] = out[:, j * LANES:(j + 1) * LANES]

    @pl.when(jnp.logical_not(used))
    def _():
        o_ref[...] = jnp.zeros_like(o_ref)


def _experts(xs, block_e, n_used, wg, wu, wd):
    n_blocks = xs.shape[0] // (TME * ROW_TILE)
    rows = pl.BlockSpec((TME * ROW_TILE, LANES), lambda i, be, nu: (i, 0))
    resident = lambda shape: pl.BlockSpec((None,) + shape, lambda i, be, nu: (be[i], 0, 0),
                                          pipeline_mode=pl.Buffered(1))
    grid_spec = pltpu.PrefetchScalarGridSpec(
        num_scalar_prefetch=2, grid=(n_blocks,),
        in_specs=[pl.BlockSpec((TME * ROW_TILE, LANES), lambda i, be, nu: (jnp.minimum(i, nu[0] - 1), 0)),
                  resident((D_MODEL, D_FF_EXPERT)), resident((D_MODEL, D_FF_EXPERT)),
                  resident((D_FF_EXPERT, D_MODEL))],
        out_specs=rows,
        scratch_shapes=[pltpu.VMEM((TME, D_MODEL), BF16), pltpu.VMEM((TME, D_FF_EXPERT), BF16)])
    return pl.pallas_call(
        _expert_kernel, grid_spec=grid_spec,
        out_shape=jax.ShapeDtypeStruct(xs.shape, F32),
        compiler_params=_cparams(("arbitrary",)),
    )(block_e, n_used, xs, wg, wu, wd)


def _combine_kernel(split, dest_hbm, eo_hbm, x_ref, route_ref, nw_ref, out_a, out_b, idx, gbuf, xn, sem_i, sem_g):
    i = pl.program_id(0)
    n = pl.num_programs(0)

    def idx_copy(step, slot):
        return pltpu.make_async_copy(dest_hbm.at[step], idx.at[slot], sem_i.at[slot])

    def row_copy(slot, src_row, k, t):
        return pltpu.make_async_copy(eo_hbm.at[pl.ds(_row_tile(src_row), ROW_TILE)],
                                     gbuf.at[slot, k, pl.ds(_row_tile(t), ROW_TILE)], sem_g.at[slot])

    def issue_rows(slot):
        def issue(b, c):
            for u in range(DMA_UNROLL):
                r = b * DMA_UNROLL + u
                row_copy(slot, idx[slot, r], u % 2, b * (DMA_UNROLL // 2) + u // 2).start(priority=u % 2)
            return c

        lax.fori_loop(0, 2 * TC // DMA_UNROLL, issue, 0)

    @pl.when(i == 0)
    def _():
        idx_copy(0, 0).start()
        idx_copy(0, 0).wait()
        issue_rows(0)

        @pl.when(n > 1)
        def _():
            idx_copy(1, 1).start()

    for slot in range(2):
        @pl.when(jnp.logical_and(i % 2 == slot, i + 1 < n))
        def _(slot=slot):
            idx_copy(0, 1 - slot).wait()
            issue_rows(1 - slot)

            @pl.when(i + 2 < n)
            def _():
                idx_copy(i + 2, slot).start()

    w1 = route_ref[:, 0:1]
    w2 = route_ref[:, 1:2]
    for slot in range(2):
        @pl.when(i % 2 == slot)
        def _(slot=slot):
            for k in range(2):
                pltpu.make_async_copy(eo_hbm.at[pl.ds(0, TC * ROW_TILE)], gbuf.at[slot, k], sem_g.at[slot]).wait()
            for j in range(ROW_TILE):
                sl = slice(j * LANES, (j + 1) * LANES)
                xn[:, sl] = x_ref[:, sl] + (gbuf[slot, 0, pl.ds(j, TC, stride=ROW_TILE), :] * w1
                                            + gbuf[slot, 1, pl.ds(j, TC, stride=ROW_TILE), :] * w2)

    if split is None:
        out_a[...] = xn[...]
        out_b[...] = _rms_rows(xn[...], nw_ref[...]).astype(BF16)
    else:
        tiles_a, tiles_b = split

        @pl.when(i < tiles_a)
        def _():
            out_a[...] = xn[...]

        @pl.when(jnp.logical_and(i >= tiles_a, i < tiles_a + tiles_b))
        def _():
            out_b[...] = xn[...]


def _combine(eo, dest, x, route, nw_next, split_rows=None):
    n = x.shape[0]
    steps = n // TC
    row = lambda w: pl.BlockSpec((TC, w), lambda i: (i, 0))
    if split_rows is None:
        split = None
        out_specs = [row(D_MODEL), row(D_MODEL)]
        out_shape = [jax.ShapeDtypeStruct((n, D_MODEL), F32), jax.ShapeDtypeStruct((n, D_MODEL), BF16)]
    else:
        ta, tb = split = tuple(r // TC for r in split_rows)
        out_specs = [pl.BlockSpec((TC, D_MODEL), lambda i: (jnp.minimum(i, ta - 1), 0)),
                     pl.BlockSpec((TC, D_MODEL), lambda i: (jnp.clip(i - ta, 0, tb - 1), 0))]
        out_shape = [jax.ShapeDtypeStruct((r, D_MODEL), F32) for r in split_rows]
    return pl.pallas_call(
        functools.partial(_combine_kernel, split),
        grid=(steps,),
        in_specs=[pl.BlockSpec(memory_space=pl.ANY), pl.BlockSpec(memory_space=pl.ANY),
                  row(D_MODEL), row(2), _const_spec((1, D_MODEL))],
        out_specs=out_specs, out_shape=out_shape,
        scratch_shapes=[pltpu.SMEM((2, 2 * TC), jnp.int32), pltpu.VMEM((2, 2, TC * ROW_TILE, LANES), F32),
                        pltpu.VMEM((TC, D_MODEL), F32),
                        pltpu.SemaphoreType.DMA((2,)), pltpu.SemaphoreType.DMA((2,))],
        compiler_params=_cparams(("arbitrary",)),
    )(dest.reshape(steps, 2 * TC), eo, x, route, nw_next)


def _trunk(lay, x, h, p, group_rows):
    depth = p['norm1_w'].shape[0]
    row = lambda v: v.reshape(1, -1).astype(F32)
    for l in range(depth):
        w_in = p['w_in'][l]
        wqk = w_in[:, 0:2 * ATT_W].astype(BF16)
        wvzx = w_in[:, 2 * ATT_W:3 * ATT_W + D_SSM + CONV_CH].astype(BF16)
        wdt = jnp.pad(w_in[:, 3 * ATT_W + D_SSM + CONV_CH:], ((0, 0), (0, LANES - 2 * N_HEADS))).astype(BF16)
        nw_qk = jnp.concatenate([jnp.tile(p['q_norm_w'][l].astype(F32), N_HEADS) * (HEAD_DIM ** -0.5),
                                 jnp.tile(p['k_norm_w'][l].astype(F32), N_HEADS)]).reshape(1, -1)
        qk, v, z, xbc, dt_raw = _inproj(h, wqk, wvzx, wdt, nw_qk)

        attn = _attention(lay, qk, v, p['rpb'][l])

        cw = jnp.concatenate([p['conv_w'][l].astype(F32).T, p['conv_b'][l].astype(F32)[None],
                              jnp.zeros((8 - CONV_K - 1, CONV_CH), F32)], axis=0)
        pad = jnp.zeros((LANES - 2 * N_HEADS,), F32)
        dtb = jnp.concatenate([p['dt_bias'][l].astype(F32).reshape(-1), pad]).reshape(1, -1)
        arow = jnp.concatenate([-jnp.exp(p['a_log'][l].astype(F32)).reshape(-1), pad]).reshape(1, -1)
        y_f, u = _ssd_pass(lay, False, dt_raw, dtb, arow, fwd_in=(xbc, cw))
        dsk = jnp.repeat(p['d_skip'][l].astype(F32), HEAD_DIM).reshape(1, -1)
        y = _ssd_pass(lay, True, dt_raw, dtb, arow, rev_in=(u, y_f, z, dsk, row(p['ssm_norm_w'][l])))

        w_out = p['w_out'][l].astype(BF16)
        nw_next = row(p['norm1_w'][l + 1]) if l + 1 < depth else jnp.ones((1, D_MODEL), F32)
        j = l // 2
        if l % 2 == 0:
            x, h2 = _outproj(attn, y, x, w_out, row(p['attn_out_norm_w'][l]), row(p['norm2_w'][l]))
            x, h = _ffn(h2, x, p['ffn_w_gate'][j].astype(BF16), p['ffn_w_up'][j].astype(BF16),
                        p['ffn_w_down'][j].astype(BF16), nw_next)
        else:
            wr = jnp.pad(p['moe_router'][j].astype(F32).T, ((0, ROUTER_ROWS - N_EXPERTS), (0, 0)))
            x, h2, route_t = _outproj(attn, y, x, w_out, row(p['attn_out_norm_w'][l]), row(p['norm2_w'][l]),
                                      router=jnp.concatenate(_split_bf16(wr), axis=0))
            dest, block_e, n_used, pad_rows, n_rows = _moe_plan(route_t)
            xs = _dispatch(h2, dest, pad_rows, n_rows)
            eo = _experts(xs, block_e, n_used, p['moe_w_gate'][j].astype(BF16),
                          p['moe_w_up'][j].astype(BF16), p['moe_w_down'][j].astype(BF16))
            if l + 1 == depth and len(group_rows) == 2:
                return _combine(eo, dest, x, route_t[2:4].T, nw_next, split_rows=group_rows)
            x, h = _combine(eo, dest, x, route_t[2:4].T, nw_next)
    offs = np.concatenate([[0], np.cumsum(group_rows)])
    return [x[offs[g]:offs[g + 1]] for g in range(len(group_rows))]


def kernel(x_prompt, x_sample, meta_tokens, norm1_w, w_in, q_norm_w, k_norm_w, rpb, attn_out_norm_w,
           conv_w, conv_b, dt_bias, a_log, d_skip, ssm_norm_w, w_out, norm2_w,
           ffn_w_gate, ffn_w_up, ffn_w_down, moe_router, moe_w_gate, moe_w_up, moe_w_down):
    p = dict(norm1_w=norm1_w, w_in=w_in, q_norm_w=q_norm_w, k_norm_w=k_norm_w, rpb=rpb,
             attn_out_norm_w=attn_out_norm_w, conv_w=conv_w, conv_b=conv_b, dt_bias=dt_bias, a_log=a_log,
             d_skip=d_skip, ssm_norm_w=ssm_norm_w, w_out=w_out, norm2_w=norm2_w, ffn_w_gate=ffn_w_gate,
             ffn_w_up=ffn_w_up, ffn_w_down=ffn_w_down, moe_router=moe_router, moe_w_gate=moe_w_gate,
             moe_w_up=moe_w_up, moe_w_down=moe_w_down)
    groups = (x_prompt, x_sample)
    lay = _Layout([g.shape[1] for g in groups for _ in range(g.shape[0])])
    meta_block = jnp.concatenate([jnp.zeros((META_OFF, D_MODEL), F32), meta_tokens.astype(F32)], axis=0)
    tail = lay.n_tok - lay.n_grid - CHUNK * lay.n_seq
    tail_rows = jnp.concatenate([jnp.tile(meta_block, (lay.n_seq, 1)), jnp.zeros((tail, D_MODEL), F32)], axis=0)
    x, h = _embed([g.reshape(-1, D_MODEL) for g in groups] + [tail_rows], norm1_w[0])
    flat = _trunk(lay, x, h, p, [g.shape[0] * g.shape[1] for g in groups])
    return tuple(f.reshape(g.shape).astype(g.dtype) for f, g in zip(flat, groups))
```
